```python
import jax, jax.numpy as jnp
from jax import lax
import numpy as np

D_MODEL = 1024
BATCH = 8
SEQ = 4096
DEPTH = 1

HEAD_DIM = 64
ROT_DIM = HEAD_DIM // 4
ROPE_THETA = 500000.0
ATTN_SCALE = HEAD_DIM ** -0.5
QBLOCK = 128

A_HEADS = 12
A_WIDTH = A_HEADS * HEAD_DIM
IDX_HEADS = 8
IDX_DIM = 64
IDX_SCALE = (IDX_HEADS ** -0.5) * (IDX_DIM ** -0.5)
TOPK_MAX = 256

B_GROUPS = ((128, 1), (512, 4), (2048, 16))
B_HEADS_PER_GROUP = 4
B_HEADS = B_HEADS_PER_GROUP * len(B_GROUPS)
B_WIDTH = B_HEADS * HEAD_DIM
B_OUT_WIDTH = B_HEADS_PER_GROUP * HEAD_DIM

FFN_HIDDEN = -(-8 * D_MODEL // (3 * 256)) * 256

DN_ALPHA = (2 * DEPTH) ** 0.25
DN_BETA = (8 * DEPTH) ** -0.25
LN_EPS = 1e-5
NEG = -1e30

IN_SPLITS = (A_WIDTH, A_WIDTH, A_WIDTH, B_WIDTH, B_WIDTH, B_WIDTH,
             IDX_HEADS * IDX_DIM, IDX_DIM, IDX_HEADS, 2 * D_MODEL)
N_IN = sum(IN_SPLITS)

kernel_name = 'hybrid_dsa_dilated_gated_deepnorm'


def layer_norm(x, g, b):
    xf = x.astype(jnp.float32)
    mu = xf.mean(-1, keepdims=True)
    var = jnp.square(xf - mu).mean(-1, keepdims=True)
    y = (xf - mu) * lax.rsqrt(var + LN_EPS)
    return (y * g.astype(jnp.float32) + b.astype(jnp.float32)).astype(x.dtype)


def rope_tables(positions, dtype):
    inv_freq = ROPE_THETA ** (-jnp.arange(0, ROT_DIM, 2, dtype=jnp.float32) / ROT_DIM)
    ang = positions.astype(jnp.float32)[..., None] * inv_freq
    return (jnp.cos(ang).astype(dtype)[:, :, None, :],
            jnp.sin(ang).astype(dtype)[:, :, None, :])


def partial_rope(x, cos, sin):
    r = cos.shape[-1]
    x1, x2, rest = x[..., :r], x[..., r:2 * r], x[..., 2 * r:]
    return jnp.concatenate([x1 * cos - x2 * sin, x2 * cos + x1 * sin, rest], axis=-1)


def to_blocks(a):
    b, s = a.shape[:2]
    a = a.reshape((b, s // QBLOCK, QBLOCK) + a.shape[2:])
    return jnp.moveaxis(a, 1, 0)


def from_blocks(a):
    a = jnp.moveaxis(a, 0, 1)
    return a.reshape((a.shape[0], -1) + a.shape[3:])


def dsa_attention(q, k, v, q_idx, k_idx, w_idx):
    seq = q.shape[1]
    k_top = min(TOPK_MAX, seq // 4)
    nblk = seq // QBLOCK
    key_pos = jnp.arange(seq)
    k_idx_f = k_idx.astype(jnp.float32)

    def block(args):
        qb, qib, wb, start = args
        t = start + jnp.arange(QBLOCK)
        s = jnp.einsum('bqhd,bsd->bqhs', qib.astype(jnp.float32), k_idx_f)
        scores = jnp.einsum('bqh,bqhs->bqs', wb.astype(jnp.float32) * IDX_SCALE, jax.nn.relu(s))
        causal = key_pos[None, :] <= t[:, None]
        scores = jnp.where(causal[None], scores, NEG)
        _, sel = lax.top_k(scores, k_top)
        valid = sel <= t[None, :, None]
        gather = jax.vmap(lambda kv, ii: kv[ii])
        k_sel = gather(k, sel)
        v_sel = gather(v, sel)
        logits = jnp.einsum('bqhd,bqkhd->bhqk', qb, k_sel).astype(jnp.float32) * ATTN_SCALE
        logits = jnp.where(valid[:, None], logits, NEG)
        p = jax.nn.softmax(logits, axis=-1).astype(v.dtype)
        return jnp.einsum('bhqk,bqkhd->bqhd', p, v_sel)

    starts = jnp.arange(nblk, dtype=jnp.int32) * QBLOCK
    out = lax.map(block, (to_blocks(q), to_blocks(q_idx), to_blocks(w_idx), starts))
    return from_blocks(out)


def dilated_attention(q, k, v):
    seq = q.shape[1]
    nblk = seq // QBLOCK
    hs = [slice(g * B_HEADS_PER_GROUP, (g + 1) * B_HEADS_PER_GROUP) for g in range(len(B_GROUPS))]
    k_groups = [k[:, :, h] for h in hs]
    v_groups = [v[:, :, h] for h in hs]

    def block(args):
        qb, start = args
        t = start + jnp.arange(QBLOCK)
        outs, lses = [], []
        for g, (window, dil) in enumerate(B_GROUPS):
            offs = dil * jnp.arange(window // dil + 1)
            pos = t[:, None] - offs[None, :]
            valid = pos >= 0
            pos = jnp.maximum(pos, 0)
            k_g = k_groups[g][:, pos]
            v_g = v_groups[g][:, pos]
            logits = jnp.einsum('bqhd,bqjhd->bhqj', qb[:, :, hs[g]], k_g).astype(jnp.float32) * ATTN_SCALE
            logits = jnp.where(valid[None, None], logits, NEG)
            lse = jax.nn.logsumexp(logits, axis=-1)
            p = jnp.exp(logits - lse[..., None]).astype(v.dtype)
            outs.append(jnp.einsum('bhqj,bqjhd->bqhd', p, v_g))
            lses.append(lse)
        alpha = jax.nn.softmax(jnp.stack(lses, axis=0), axis=0)
        alpha = jnp.swapaxes(alpha, 2, 3)[..., None].astype(v.dtype)
        return jnp.sum(alpha * jnp.stack(outs, axis=0), axis=0)

    starts = jnp.arange(nblk, dtype=jnp.int32) * QBLOCK
    out = lax.map(block, (to_blocks(q), starts))
    return from_blocks(out)


def hybrid_layer(x, cos, sin, w_in, b_gate, w_branch_a, w_branch_b, w_out, ln1_g, ln1_b,
                 w_ffn_gate, w_ffn_up, w_ffn_down, ln2_g, ln2_b):
    bsz, seq, _ = x.shape
    points = np.cumsum(IN_SPLITS)[:-1].tolist()
    qa, ka, va, qb, kb, vb, qi, ki, wi, gate_pre = jnp.split(x @ w_in, points, axis=-1)
    heads = lambda a, h: a.reshape(bsz, seq, h, -1)
    qa = partial_rope(heads(qa, A_HEADS), cos, sin)
    ka = partial_rope(heads(ka, A_HEADS), cos, sin)
    qi = partial_rope(heads(qi, IDX_HEADS), cos, sin)
    ki = partial_rope(ki.reshape(bsz, seq, 1, IDX_DIM), cos, sin)[:, :, 0]
    o_a = dsa_attention(qa, ka, heads(va, A_HEADS), qi, ki, wi).reshape(bsz, seq, A_WIDTH)
    qb = partial_rope(heads(qb, B_HEADS), cos, sin)
    kb = partial_rope(heads(kb, B_HEADS), cos, sin)
    o_b = dilated_attention(qb, kb, heads(vb, B_HEADS)).reshape(bsz, seq, B_OUT_WIDTH)
    g_a, g_b = jnp.split(jax.nn.sigmoid(gate_pre + b_gate), 2, axis=-1)
    mixed = (g_a * (o_a @ w_branch_a) + g_b * (o_b @ w_branch_b)) @ w_out
    x = layer_norm(DN_ALPHA * x + mixed, ln1_g, ln1_b)
    h = jax.nn.silu(x @ w_ffn_gate) * (x @ w_ffn_up)
    return layer_norm(DN_ALPHA * x + h @ w_ffn_down, ln2_g, ln2_b)


def setup_inputs(seed: int = 0) -> dict:
    key = jax.random.key(seed)
    ks = jax.random.split(key, 16)
    nrm = lambda k, shape, scale: jax.random.normal(k, shape, jnp.float32) * scale
    return {
        'x': nrm(ks[0], (BATCH, SEQ, D_MODEL), 1.0),
        'positions': jnp.broadcast_to(jnp.arange(SEQ, dtype=jnp.int32), (BATCH, SEQ)),
        'w_in': nrm(ks[1], (DEPTH, D_MODEL, N_IN), D_MODEL ** -0.5),
        'b_gate': nrm(ks[2], (DEPTH, 2 * D_MODEL), 0.02),
        'w_branch_a': nrm(ks[3], (DEPTH, A_WIDTH, D_MODEL), A_WIDTH ** -0.5),
        'w_branch_b': nrm(ks[4], (DEPTH, B_OUT_WIDTH, D_MODEL), B_OUT_WIDTH ** -0.5),
        'w_out': nrm(ks[5], (DEPTH, D_MODEL, D_MODEL), DN_BETA * D_MODEL ** -0.5),
        'ln1_g': 1.0 + nrm(ks[6], (DEPTH, D_MODEL), 0.02),
        'ln1_b': nrm(ks[7], (DEPTH, D_MODEL), 0.02),
        'w_ffn_gate': nrm(ks[8], (DEPTH, D_MODEL, FFN_HIDDEN), D_MODEL ** -0.5),
        'w_ffn_up': nrm(ks[9], (DEPTH, D_MODEL, FFN_HIDDEN), D_MODEL ** -0.5),
        'w_ffn_down': nrm(ks[10], (DEPTH, FFN_HIDDEN, D_MODEL), DN_BETA * FFN_HIDDEN ** -0.5),
        'ln2_g': 1.0 + nrm(ks[11], (DEPTH, D_MODEL), 0.02),
        'ln2_b': nrm(ks[12], (DEPTH, D_MODEL), 0.02),
    }


def reference(x, positions, w_in, b_gate, w_branch_a, w_branch_b, w_out, ln1_g, ln1_b,
              w_ffn_gate, w_ffn_up, w_ffn_down, ln2_g, ln2_b):
    cos, sin = rope_tables(positions, x.dtype)
    for layer in range(DEPTH):
        x = hybrid_layer(x, cos, sin, w_in[layer], b_gate[layer], w_branch_a[layer],
                         w_branch_b[layer], w_out[layer], ln1_g[layer], ln1_b[layer],
                         w_ffn_gate[layer], w_ffn_up[layer], w_ffn_down[layer],
                         ln2_g[layer], ln2_b[layer])
    return x
```

```python
import functools

import jax
import jax.numpy as jnp
from jax import lax
from jax.experimental import pallas as pl
from jax.experimental.pallas import tpu as pltpu

F32 = jnp.float32
BF16 = jnp.bfloat16

D_MODEL = 1024
HEAD_DIM = 64
ROT_HALF = 8
ROPE_THETA = 500000.0
ATTN_SCALE = HEAD_DIM ** -0.5
A_HEADS = 12
A_WIDTH = A_HEADS * HEAD_DIM
IDX_HEADS = 8
IDX_DIM = 64
IDX_SCALE = (IDX_HEADS ** -0.5) * (IDX_DIM ** -0.5)
TOPK_MAX = 256
B_DILATIONS = (1, 4, 16)
B_WINDOW_STEPS = 128
B_HEADS_PER_GROUP = 4
B_WIDTH = 3 * B_HEADS_PER_GROUP * HEAD_DIM
B_OUT_WIDTH = B_HEADS_PER_GROUP * HEAD_DIM
FFN_HIDDEN = 2816
LN_EPS = 1e-5
NEG = -1e30

LANES = 128
VMEM_LIMIT = 56 * 1024 * 1024
TM = 512
TQ = 256
TK = 256
TB = 256
FFN_CHUNK = 256
ROPE_TILE = 2048

_N_KA, _N_QB, _N_KB, _N_VB = 0, 768, 1536, 2304
_N_KI, _N_GATE, _N_NAT = 3072, 3200, 5248
_T_QA, _T_VA, _T_QI, _T_WI, _T_ROWS = 0, 768, 1536, 2048, 2064


def _params(sem):
    return pltpu.CompilerParams(dimension_semantics=sem, vmem_limit_bytes=VMEM_LIMIT)


def _rope_kernel(pc_ref, pr_ref, fr_ref, fc_ref, c_ref, s_ref, ct_ref, st_ref):
    ang = pc_ref[...].astype(F32) * fr_ref[...]
    d = lax.broadcasted_iota(jnp.int32, ang.shape, 1) & (HEAD_DIM - 1)
    cos, sin = jnp.cos(ang), jnp.sin(ang)
    c_ref[...] = jnp.where(d < 2 * ROT_HALF, cos, 1.0)
    s_ref[...] = jnp.where(d < ROT_HALF, -sin, jnp.where(d < 2 * ROT_HALF, sin, 0.0))
    ang_t = fc_ref[...] * pr_ref[...].astype(F32)
    ct_ref[...] = jnp.cos(ang_t)
    st_ref[...] = jnp.sin(ang_t)


def _rope_tables(positions):
    m = positions.size
    inv_freq = ROPE_THETA ** (-jnp.arange(0, 2 * ROT_HALF, 2, dtype=F32) / (2 * ROT_HALF))
    f_row = jnp.tile(inv_freq, LANES // ROT_HALF).reshape(1, LANES)
    f_col = inv_freq.reshape(ROT_HALF, 1)
    t = min(ROPE_TILE, m)
    return pl.pallas_call(
        _rope_kernel,
        grid=(m // t,),
        in_specs=[pl.BlockSpec((t, 1), lambda i: (i, 0)), pl.BlockSpec((1, t), lambda i: (0, i)),
                  pl.BlockSpec((1, LANES), lambda i: (0, 0)), pl.BlockSpec((ROT_HALF, 1), lambda i: (0, 0))],
        out_specs=[pl.BlockSpec((t, LANES), lambda i: (i, 0)), pl.BlockSpec((t, LANES), lambda i: (i, 0)),
                   pl.BlockSpec((ROT_HALF, t), lambda i: (0, i)), pl.BlockSpec((ROT_HALF, t), lambda i: (0, i))],
        out_shape=[jax.ShapeDtypeStruct((m, LANES), F32), jax.ShapeDtypeStruct((m, LANES), F32),
                   jax.ShapeDtypeStruct((ROT_HALF, m), F32), jax.ShapeDtypeStruct((ROT_HALF, m), F32)],
        compiler_params=_params(("parallel",)),
        name="rope_tables",
    )(positions.reshape(m, 1), positions.reshape(1, m), f_row, f_col)


def _proj_nat_kernel(x_ref, c_ref, s_ref, w_ref, ka_ref, qb_ref, kb_ref, vb_ref, ki_ref, gate_ref):
    xb = x_ref[0].astype(BF16)
    cos, sin = c_ref[...], s_ref[...]
    lane = lax.broadcasted_iota(jnp.int32, cos.shape, 1)
    first = (lane & (HEAD_DIM - 1)) < ROT_HALF

    def rope(y):
        partner = jnp.where(first, pltpu.roll(y, LANES - ROT_HALF, 1), pltpu.roll(y, ROT_HALF, 1))
        return y * cos + partner * sin

    def proj(lo, hi):
        return jnp.dot(xb, w_ref[:, lo:hi], preferred_element_type=F32)

    blocks = A_WIDTH // LANES
    y = proj(_N_KA, _N_KA + A_WIDTH)
    for p in range(blocks):
        ka_ref[0, p] = rope(y[:, p * LANES:(p + 1) * LANES]).astype(BF16)
    y = proj(_N_QB, _N_QB + B_WIDTH)
    for p in range(blocks):
        qb_ref[0, :, p * LANES:(p + 1) * LANES] = (rope(y[:, p * LANES:(p + 1) * LANES]) * ATTN_SCALE).astype(BF16)
    y = proj(_N_KB, _N_KB + B_WIDTH)
    for p in range(blocks):
        kb_ref[0, :, p * LANES:(p + 1) * LANES] = rope(y[:, p * LANES:(p + 1) * LANES]).astype(BF16)
    vb_ref[0] = proj(_N_VB, _N_VB + B_WIDTH).astype(BF16)
    r = rope(proj(_N_KI, _N_KI + LANES))
    hi = r.astype(BF16).astype(F32)
    hl = jnp.where(lane < IDX_DIM, hi, r - hi).astype(BF16)
    ki_ref[0, :, 0:LANES] = hl
    ki_ref[0, :, LANES:2 * LANES] = hl
    half = D_MODEL
    gate_ref[0, :, 0:half] = proj(_N_GATE, _N_GATE + half)
    gate_ref[0, :, half:2 * half] = proj(_N_GATE + half, _N_GATE + 2 * half)


def _proj_nat(x, cos, sin, w_nat):
    b, s, d = x.shape
    nt = s // TM
    tok = lambda width: pl.BlockSpec((1, TM, width), lambda i, j: (i, j, 0))
    return pl.pallas_call(
        _proj_nat_kernel,
        grid=(b, nt),
        in_specs=[tok(d),
                  pl.BlockSpec((TM, LANES), lambda i, j: (i * nt + j, 0)),
                  pl.BlockSpec((TM, LANES), lambda i, j: (i * nt + j, 0)),
                  pl.BlockSpec((d, _N_NAT), lambda i, j: (0, 0))],
        out_specs=[pl.BlockSpec((1, A_WIDTH // LANES, TM, LANES), lambda i, j: (i, 0, j, 0)),
                   tok(B_WIDTH), tok(B_WIDTH), tok(B_WIDTH), tok(2 * LANES), tok(2 * D_MODEL)],
        out_shape=[jax.ShapeDtypeStruct((b, A_WIDTH // LANES, s, LANES), BF16),
                   jax.ShapeDtypeStruct((b, s, B_WIDTH), BF16),
                   jax.ShapeDtypeStruct((b, s, B_WIDTH), BF16),
                   jax.ShapeDtypeStruct((b, s, B_WIDTH), BF16),
                   jax.ShapeDtypeStruct((b, s, 2 * LANES), BF16),
                   jax.ShapeDtypeStruct((b, s, 2 * D_MODEL), F32)],
        compiler_params=_params(("parallel", "parallel")),
        name="proj_token_major",
    )(x, cos, sin, w_nat)


def _proj_t_kernel(x_ref, ct_ref, st_ref, w_ref, qt_ref, vt_ref, qi_ref, wt_ref):
    xb = x_ref[0].astype(BF16)
    cos, sin = ct_ref[...], st_ref[...]

    def proj(lo, hi):
        return lax.dot_general(w_ref[lo:hi, :], xb, (((1,), (1,)), ((), ())), preferred_element_type=F32)

    def rope_head(y):
        x1, x2 = y[0:ROT_HALF], y[ROT_HALF:2 * ROT_HALF]
        return jnp.concatenate([x1 * cos - x2 * sin, x2 * cos + x1 * sin, y[2 * ROT_HALF:]], axis=0)

    y = proj(_T_QA, _T_QA + A_WIDTH)
    for h in range(A_HEADS):
        r = rope_head(y[h * HEAD_DIM:(h + 1) * HEAD_DIM])
        qt_ref[0, h * HEAD_DIM:(h + 1) * HEAD_DIM, :] = (r * ATTN_SCALE).astype(BF16)
    y = proj(_T_VA, _T_VA + A_WIDTH)
    for h in range(A_HEADS):
        for c in range(TM // TK):
            vt_ref[0, h, c] = y[h * HEAD_DIM:(h + 1) * HEAD_DIM, c * TK:(c + 1) * TK].astype(BF16)
    y = proj(_T_QI, _T_QI + IDX_HEADS * IDX_DIM)
    for h in range(IDX_HEADS):
        r = rope_head(y[h * IDX_DIM:(h + 1) * IDX_DIM])
        hi = r.astype(BF16)
        lo = (r - hi.astype(F32)).astype(BF16)
        base = 4 * h * IDX_DIM
        qi_ref[0, base:base + IDX_DIM, :] = hi
        qi_ref[0, base + IDX_DIM:base + 2 * IDX_DIM, :] = hi
        qi_ref[0, base + 2 * IDX_DIM:base + 3 * IDX_DIM, :] = lo
        qi_ref[0, base + 3 * IDX_DIM:base + 4 * IDX_DIM, :] = lo
    wt_ref[0] = proj(_T_WI, _T_ROWS)[0:IDX_HEADS] * IDX_SCALE


def _proj_t(x, cos_t, sin_t, w_t):
    b, s, d = x.shape
    nt = s // TM
    return pl.pallas_call(
        _proj_t_kernel,
        grid=(b, nt),
        in_specs=[pl.BlockSpec((1, TM, d), lambda i, j: (i, j, 0)),
                  pl.BlockSpec((ROT_HALF, TM), lambda i, j: (0, i * nt + j)),
                  pl.BlockSpec((ROT_HALF, TM), lambda i, j: (0, i * nt + j)),
                  pl.BlockSpec((_T_ROWS, d), lambda i, j: (0, 0))],
        out_specs=[pl.BlockSpec((1, A_WIDTH, TM), lambda i, j: (i, 0, j)),
                   pl.BlockSpec((1, A_HEADS, TM // TK, HEAD_DIM, TK), lambda i, j: (i, 0, j, 0, 0)),
                   pl.BlockSpec((1, 4 * IDX_HEADS * IDX_DIM, TM), lambda i, j: (i, 0, j)),
                   pl.BlockSpec((1, IDX_HEADS, TM), lambda i, j: (i, 0, j))],
        out_shape=[jax.ShapeDtypeStruct((b, A_WIDTH, s), BF16),
                   jax.ShapeDtypeStruct((b, A_HEADS, s // TK, HEAD_DIM, TK), BF16),
                   jax.ShapeDtypeStruct((b, 4 * IDX_HEADS * IDX_DIM, s), BF16),
                   jax.ShapeDtypeStruct((b, IDX_HEADS, s), F32)],
        compiler_params=_params(("parallel", "parallel")),
        name="proj_feature_major",
    )(x, cos_t, sin_t, w_t)


def _dsa_kernel(qt_ref, qi_ref, wt_ref, k_ref, vt_ref, ki_ref, o_ref,
                sc_ref, qpad_ref, ot_ref, st_ref):
    j = pl.program_id(1)
    nkt = j + 1
    n_key_bits = max((k_ref.shape[2] - 1).bit_length(), 1)

    for h in range(A_HEADS):
        off = (h % 2) * HEAD_DIM
        qpad_ref[h] = jnp.zeros((2 * HEAD_DIM, TQ), BF16)
        qpad_ref[h, off:off + HEAD_DIM, :] = qt_ref[0, h * HEAD_DIM:(h + 1) * HEAD_DIM, :]

    qpos = j * TQ + lax.broadcasted_iota(jnp.int32, (1, TQ), 1)
    row_iota = lax.broadcasted_iota(jnp.int32, (TK, TQ), 0)
    w = wt_ref[0]

    def score_tile(kt, carry):
        mn, mx = carry
        ki = ki_ref[0, pl.ds(pl.multiple_of(kt * TK, TK), TK), :]
        acc = jnp.zeros((TK, TQ), F32)
        for h in range(IDX_HEADS):
            s = jnp.dot(ki, qi_ref[0, 4 * h * IDX_DIM:4 * (h + 1) * IDX_DIM, :], preferred_element_type=F32)
            acc = acc + w[h:h + 1, :] * jnp.maximum(s, 0.0)
        causal = (kt * TK + row_iota) <= qpos
        sc_ref[kt] = jnp.where(causal, acc, -jnp.inf)
        lo_t = jnp.where(causal, acc, jnp.inf).reshape(TK // 8, 8, TQ).min(axis=0)
        hi_t = jnp.where(causal, acc, -jnp.inf).reshape(TK // 8, 8, TQ).max(axis=0)
        return jnp.minimum(mn, lo_t), jnp.maximum(mx, hi_t)

    mn8, mx8 = lax.fori_loop(0, nkt, score_tile,
                             (jnp.full((8, TQ), jnp.inf, F32), jnp.full((8, TQ), -jnp.inf, F32)))
    mn = mn8.min(axis=0, keepdims=True)
    mx = mx8.max(axis=0, keepdims=True)

    def count(pred):
        def body(kt, cnt):
            return cnt + pred(sc_ref[kt], kt).reshape(TK // 8, 8, TQ).sum(axis=0)
        return lax.fori_loop(0, nkt, body, jnp.zeros((8, TQ), F32)).sum(axis=0, keepdims=True)

    def count_ge(cand):
        return count(lambda t, kt: jnp.where(t >= cand, 1.0, 0.0))

    n_causal = (qpos + 1).astype(F32)
    k_q = jnp.minimum(n_causal, float(TOPK_MAX))
    c_mx = count_ge(mx)
    at_max = (n_causal > k_q) & (c_mx >= k_q)
    st_ref[0:1, :] = jnp.where(at_max, mx, mn)
    st_ref[1:2, :] = mx
    st_ref[2:3, :] = jnp.where(at_max, c_mx, n_causal)
    st_ref[3:4, :] = jnp.where(at_max, 0.0, c_mx)
    act0 = jnp.where((n_causal > k_q) & (c_mx < k_q), 1.0, 0.0)
    st_ref[4:5, :] = act0

    def bisect(go):
        lo, hi, c_lo, c_hi = st_ref[0:1, :], st_ref[1:2, :], st_ref[2:3, :], st_ref[3:4, :]
        act = st_ref[4:5, :] > 0.0
        mid = 0.5 * lo + 0.5 * hi
        live = act & (mid > lo) & (mid < hi)
        c = count_ge(mid)
        up = live & (c >= k_q)
        dn = live & (c < k_q)
        c_lo = jnp.where(up, c, c_lo)
        st_ref[0:1, :] = jnp.where(up, mid, lo)
        st_ref[1:2, :] = jnp.where(dn, mid, hi)
        st_ref[2:3, :] = c_lo
        st_ref[3:4, :] = jnp.where(dn, c, c_hi)
        act_new = jnp.where(live & (c_lo > k_q), 1.0, 0.0)
        st_ref[4:5, :] = act_new
        return (jnp.max(act_new) > 0.0).astype(jnp.int32)

    lax.while_loop(lambda go: go > 0, bisect, (jnp.max(act0) > 0.0).astype(jnp.int32))

    lo = st_ref[0:1, :]
    tie = st_ref[2:3, :] > k_q

    @pl.when(jnp.max(jnp.where(tie, 1.0, 0.0)) > 0.0)
    def _():
        need = k_q - st_ref[3:4, :]

        def eq_upto(jmid):
            return count(lambda t, kt: jnp.where(t == lo, jnp.where(kt * TK + row_iota <= jmid, 1.0, 0.0), 0.0))

        def step(_, carry):
            jlo, jhi = carry
            jmid = (jlo + jhi) >> 1
            ok = eq_upto(jmid) >= need
            return jnp.where(ok, jlo, jmid), jnp.where(ok, jmid, jhi)

        _, jcut = lax.fori_loop(0, n_key_bits + 1, step,
                                (jnp.full((1, TQ), -1, jnp.int32), jnp.full((1, TQ), nkt * TK - 1, jnp.int32)))
        jcut = jnp.where(tie, jcut, nkt * TK)

        def drop(kt, _):
            t = sc_ref[kt]
            sc_ref[kt] = jnp.where(t == lo, jnp.where(kt * TK + row_iota > jcut, -jnp.inf, t), t)
            return 0

        lax.fori_loop(0, nkt, drop, 0)

    def to_bias(kt, _):
        sc_ref[kt] = jnp.where(sc_ref[kt] >= lo, 0.0, NEG)
        return 0

    lax.fori_loop(0, nkt, to_bias, 0)

    def head(h, _):
        qh = qpad_ref[h]
        pair = h >> 1

        def kv_tile(kt, carry):
            m, l, acc = carry
            kb = k_ref[0, pair, pl.ds(pl.multiple_of(kt * TK, TK), TK), :]
            s = jnp.dot(kb, qh, preferred_element_type=F32) + sc_ref[kt]
            m_new = jnp.maximum(m, s.reshape(TK // 8, 8, TQ).max(axis=0).max(axis=0, keepdims=True))
            p = jnp.exp(s - m_new)
            corr = jnp.exp(m - m_new)
            l = l * corr + p.reshape(TK // 8, 8, TQ).sum(axis=0).sum(axis=0, keepdims=True)
            acc = acc * corr + jnp.dot(vt_ref[0, h, kt], p.astype(BF16), preferred_element_type=F32)
            return m_new, l, acc

        m, l, acc = lax.fori_loop(0, nkt, kv_tile,
                                  (jnp.full((1, TQ), NEG, F32), jnp.zeros((1, TQ), F32),
                                   jnp.zeros((HEAD_DIM, TQ), F32)))
        ot_ref[h] = acc / l
        return 0

    lax.fori_loop(0, A_HEADS, head, 0)

    for p in range(A_HEADS // 2):
        both = ot_ref[2 * p:2 * p + 2].reshape(2 * HEAD_DIM, TQ)
        o_ref[0, :, p * LANES:(p + 1) * LANES] = both.T.astype(BF16)


def _dsa_attention(qt, qi4t, wt, k6, vt5, ki4):
    b, _, s = qt.shape
    nq = s // TQ
    return pl.pallas_call(
        _dsa_kernel,
        grid=(b, nq),
        in_specs=[pl.BlockSpec((1, A_WIDTH, TQ), lambda i, j: (i, 0, j)),
                  pl.BlockSpec((1, 4 * IDX_HEADS * IDX_DIM, TQ), lambda i, j: (i, 0, j)),
                  pl.BlockSpec((1, IDX_HEADS, TQ), lambda i, j: (i, 0, j)),
                  pl.BlockSpec((1, A_WIDTH // LANES, s, LANES), lambda i, j: (i, 0, 0, 0)),
                  pl.BlockSpec((1, A_HEADS, s // TK, HEAD_DIM, TK), lambda i, j: (i, 0, 0, 0, 0)),
                  pl.BlockSpec((1, s, 4 * IDX_DIM), lambda i, j: (i, 0, 0))],
        out_specs=pl.BlockSpec((1, TQ, A_WIDTH), lambda i, j: (i, j, 0)),
        out_shape=jax.ShapeDtypeStruct((b, s, A_WIDTH), BF16),
        scratch_shapes=[pltpu.VMEM((s // TK, TK, TQ), F32),
                        pltpu.VMEM((A_HEADS, 2 * HEAD_DIM, TQ), BF16),
                        pltpu.VMEM((A_HEADS, HEAD_DIM, TQ), F32),
                        pltpu.VMEM((8, TQ), F32)],
        compiler_params=_params(("parallel", "arbitrary")),
        name="dsa_attention",
    )(qt, qi4t, wt, k6, vt5, ki4)


def _window_kernel(q_ref, kp_ref, kc_ref, vp_ref, vc_ref, o_ref, lse_ref):
    i = pl.program_id(2)
    r = lax.broadcasted_iota(jnp.int32, (TB, 2 * TB), 0)
    c = lax.broadcasted_iota(jnp.int32, (TB, 2 * TB), 1)
    dist = r + TB - c
    first_col = jnp.where(i > 0, 0, TB)
    bias = jnp.where(dist >= 0, jnp.where(dist <= B_WINDOW_STEPS, jnp.where(c >= first_col, 0.0, NEG), NEG), NEG)
    lane = lax.broadcasted_iota(jnp.int32, (TB, LANES), 1)
    left = lane < HEAD_DIM
    for p in range(B_HEADS_PER_GROUP // 2):
        cols = slice(p * LANES, (p + 1) * LANES)
        q2 = q_ref[0, :, cols]
        k2 = jnp.concatenate([kp_ref[0, :, cols], kc_ref[0, :, cols]], axis=0)
        v2 = jnp.concatenate([vp_ref[0, :, cols], vc_ref[0, :, cols]], axis=0)
        outs, lses = [], []
        for side in (left, ~left):
            qh = jnp.where(side, q2, jnp.zeros_like(q2))
            s = lax.dot_general(qh, k2, (((1,), (1,)), ((), ())), preferred_element_type=F32) + bias
            m = s.max(axis=-1, keepdims=True)
            e = jnp.exp(s - m)
            l = e.sum(axis=-1, keepdims=True)
            outs.append(jnp.dot(e.astype(BF16), v2, preferred_element_type=F32) / l)
            lses.append(jnp.broadcast_to(m + jnp.log(l), (TB, LANES)))
        o_ref[0, :, cols] = jnp.where(left, outs[0], outs[1])
        lse_ref[0, :, cols] = jnp.where(left, lses[0], lses[1])


def _window_attention(qb, kb, vb, group):
    b, s, _ = qb.shape
    d = B_DILATIONS[group]
    n = s // d
    nt = n // TB
    view = lambda a: a.reshape(b, n, d * a.shape[-1])
    blocks_per_token = B_WIDTH // B_OUT_WIDTH
    cur = pl.BlockSpec((1, TB, B_OUT_WIDTH), lambda bi, ri, ti: (bi, ti, blocks_per_token * ri + group))
    prev = pl.BlockSpec((1, TB, B_OUT_WIDTH),
                        lambda bi, ri, ti: (bi, jnp.maximum(ti - 1, 0), blocks_per_token * ri + group))
    out = pl.BlockSpec((1, TB, B_OUT_WIDTH), lambda bi, ri, ti: (bi, ti, ri))
    o, lse = pl.pallas_call(
        _window_kernel,
        grid=(b, d, nt),
        in_specs=[cur, prev, cur, prev, cur],
        out_specs=[out, out],
        out_shape=[jax.ShapeDtypeStruct((b, n, d * B_OUT_WIDTH), F32)] * 2,
        compiler_params=_params(("parallel", "parallel", "arbitrary")),
        name=f"window_attention_d{d}",
    )(view(qb), view(kb), view(kb), view(vb), view(vb))
    return o.reshape(b * s, B_OUT_WIDTH), lse.reshape(b * s, B_OUT_WIDTH)


def _layer_norm(y, g, b):
    mu = y.mean(axis=-1, keepdims=True)
    yc = y - mu
    var = (yc * yc).mean(axis=-1, keepdims=True)
    return yc * lax.rsqrt(var + LN_EPS) * g + b


def _merge_kernel(alpha, oa_ref, o0_ref, o1_ref, o2_ref, l0_ref, l1_ref, l2_ref, gate_ref, x_ref,
                  bg_ref, wa_ref, wb_ref, wo_ref, g_ref, b_ref, y_ref):
    l0, l1, l2 = l0_ref[...], l1_ref[...], l2_ref[...]
    lm = jnp.maximum(jnp.maximum(l0, l1), l2)
    e0, e1, e2 = jnp.exp(l0 - lm), jnp.exp(l1 - lm), jnp.exp(l2 - lm)
    ob = (e0 * o0_ref[...] + e1 * o1_ref[...] + e2 * o2_ref[...]) / (e0 + e1 + e2)
    pa = jnp.dot(oa_ref[...], wa_ref[...], preferred_element_type=F32)
    pb = jnp.dot(ob.astype(BF16), wb_ref[...], preferred_element_type=F32)
    gate = 1.0 / (1.0 + jnp.exp(-(gate_ref[...] + bg_ref[...])))
    merged = gate[:, :D_MODEL] * pa + gate[:, D_MODEL:] * pb
    mixed = jnp.dot(merged.astype(BF16), wo_ref[...], preferred_element_type=F32)
    y_ref[...] = _layer_norm(alpha * x_ref[...] + mixed, g_ref[...], b_ref[...])


def _merge(alpha, oa, obs, lses, gate, x2, b_gate, wa, wb, wo, g, bb):
    m = x2.shape[0]
    tok = lambda width: pl.BlockSpec((TM, width), lambda i: (i, 0))
    full = lambda a: pl.BlockSpec(a.shape, lambda i: (0, 0))
    return pl.pallas_call(
        functools.partial(_merge_kernel, alpha),
        grid=(m // TM,),
        in_specs=[tok(A_WIDTH)] + [tok(B_OUT_WIDTH)] * 6 + [tok(2 * D_MODEL), tok(D_MODEL),
                  full(b_gate), full(wa), full(wb), full(wo), full(g), full(bb)],
        out_specs=tok(D_MODEL),
        out_shape=jax.ShapeDtypeStruct((m, D_MODEL), F32),
        compiler_params=_params(("parallel",)),
        name="merge_outproj_norm",
    )(oa, *obs, *lses, gate, x2, b_gate, wa, wb, wo, g, bb)


def _ffn_kernel(alpha, x_ref, wg_ref, wu_ref, wd_ref, g_ref, b_ref, y_ref):
    x = x_ref[...]
    xb = x.astype(BF16)
    acc = jnp.zeros((TM, D_MODEL), F32)
    for c in range(FFN_HIDDEN // FFN_CHUNK):
        cols = slice(c * FFN_CHUNK, (c + 1) * FFN_CHUNK)
        gate = jnp.dot(xb, wg_ref[:, cols], preferred_element_type=F32)
        up = jnp.dot(xb, wu_ref[:, cols], preferred_element_type=F32)
        h = gate / (1.0 + jnp.exp(-gate)) * up
        acc = acc + jnp.dot(h.astype(BF16), wd_ref[cols, :], preferred_element_type=F32)
    y_ref[...] = _layer_norm(alpha * x + acc, g_ref[...], b_ref[...])


def _ffn(alpha, x2, wg, wu, wd, g, bb):
    m = x2.shape[0]
    tok = pl.BlockSpec((TM, D_MODEL), lambda i: (i, 0))
    full = lambda a: pl.BlockSpec(a.shape, lambda i: (0, 0))
    return pl.pallas_call(
        functools.partial(_ffn_kernel, alpha),
        grid=(m // TM,),
        in_specs=[tok, full(wg), full(wu), full(wd), full(g), full(bb)],
        out_specs=tok,
        out_shape=jax.ShapeDtypeStruct((m, D_MODEL), F32),
        compiler_params=_params(("parallel",)),
        name="swiglu_norm",
    )(x2, wg, wu, wd, g, bb)


def _split_w_in(w):
    a, bw, hi = A_WIDTH, B_WIDTH, IDX_HEADS * IDX_DIM
    o = 0
    qa, o = w[:, o:o + a], o + a
    ka, o = w[:, o:o + a], o + a
    va, o = w[:, o:o + a], o + a
    qb, o = w[:, o:o + bw], o + bw
    kb, o = w[:, o:o + bw], o + bw
    vb, o = w[:, o:o + bw], o + bw
    qi, o = w[:, o:o + hi], o + hi
    ki, o = w[:, o:o + IDX_DIM], o + IDX_DIM
    wi, o = w[:, o:o + IDX_HEADS], o + IDX_HEADS
    gate = w[:, o:]
    w_nat = jnp.concatenate([ka, qb, kb, vb, ki, ki, gate], axis=1).astype(BF16)
    pad = jnp.zeros((w.shape[0], _T_ROWS - _T_WI - IDX_HEADS), w.dtype)
    w_t = jnp.concatenate([qa, va, qi, wi, pad], axis=1).T.astype(BF16)
    return w_nat, w_t


def kernel(x, positions, w_in, b_gate, w_branch_a, w_branch_b, w_out, ln1_g, ln1_b,
           w_ffn_gate, w_ffn_up, w_ffn_down, ln2_g, ln2_b):
    b, s, d = x.shape
    depth = w_in.shape[0]
    assert d == D_MODEL and s % (max(B_DILATIONS) * TB) == 0 and s % TM == 0
    alpha = (2 * depth) ** 0.25
    cos, sin, cos_t, sin_t = _rope_tables(positions)
    row = lambda v: v.reshape(1, -1)
    for layer in range(depth):
        w_nat, w_t = _split_w_in(w_in[layer])
        ka6, qb, kb, vb, ki4, gate = _proj_nat(x, cos, sin, w_nat)
        qt, vt5, qi4t, wt = _proj_t(x, cos_t, sin_t, w_t)
        oa = _dsa_attention(qt, qi4t, wt, ka6, vt5, ki4)
        groups = [_window_attention(qb, kb, vb, g) for g in range(len(B_DILATIONS))]
        x1 = _merge(alpha, oa.reshape(b * s, A_WIDTH), [o for o, _ in groups], [l for _, l in groups],
                    gate.reshape(b * s, 2 * D_MODEL), x.reshape(b * s, d), row(b_gate[layer]),
                    w_branch_a[layer].astype(BF16), w_branch_b[layer].astype(BF16),
                    w_out[layer].astype(BF16), row(ln1_g[layer]), row(ln1_b[layer]))
        x2 = _ffn(alpha, x1, w_ffn_gate[layer].astype(BF16), w_ffn_up[layer].astype(BF16),
                  w_ffn_down[layer].astype(BF16), row(ln2_g[layer]), row(ln2_b[layer]))
        x = x2.reshape(b, s, d)
    return x
```

```python
import functools

import jax
import jax.numpy as jnp
from jax import lax
from jax.experimental import pallas as pl
from jax.experimental.pallas import tpu as pltpu

F32 = jnp.float32
BF16 = jnp.bfloat16

D_MODEL = 1024
HEAD_DIM = 64
ROT_HALF = 8
ROPE_THETA = 500000.0
ATTN_SCALE = HEAD_DIM ** -0.5
A_HEADS = 12
A_WIDTH = A_HEADS * HEAD_DIM
IDX_HEADS = 8
IDX_DIM = 64
IDX_SCALE = (IDX_HEADS ** -0.5) * (IDX_DIM ** -0.5)
TOPK_MAX = 256
B_DILATIONS = (1, 4, 16)
B_WINDOW_STEPS = 128
B_HEADS_PER_GROUP = 4
B_WIDTH = 3 * B_HEADS_PER_GROUP * HEAD_DIM
B_OUT_WIDTH = B_HEADS_PER_GROUP * HEAD_DIM
FFN_HIDDEN = 2816
LN_EPS = 1e-5
NEG = -1e30

LANES = 128
VMEM_LIMIT = 56 * 1024 * 1024
TM = 512
TQ = 256
TK = 256
TB = 256
FFN_CHUNK = 256
ROPE_TILE = 2048

_N_KA, _N_QB, _N_KB, _N_VB = 0, 768, 1536, 2304
_N_KI, _N_GATE, _N_NAT = 3072, 3200, 5248
_T_QA, _T_VA, _T_QI, _T_WI, _T_ROWS = 0, 768, 1536, 2048, 2064


def _params(sem):
    return pltpu.CompilerParams(dimension_semantics=sem, vmem_limit_bytes=VMEM_LIMIT)


def _rope_kernel(pc_ref, pr_ref, fr_ref, fc_ref, c_ref, s_ref, ct_ref, st_ref):
    ang = pc_ref[...].astype(F32) * fr_ref[...]
    d = lax.broadcasted_iota(jnp.int32, ang.shape, 1) & (HEAD_DIM - 1)
    cos, sin = jnp.cos(ang), jnp.sin(ang)
    c_ref[...] = jnp.where(d < 2 * ROT_HALF, cos, 1.0)
    s_ref[...] = jnp.where(d < ROT_HALF, -sin, jnp.where(d < 2 * ROT_HALF, sin, 0.0))
    ang_t = fc_ref[...] * pr_ref[...].astype(F32)
    ct_ref[...] = jnp.cos(ang_t)
    st_ref[...] = jnp.sin(ang_t)


def _rope_tables(positions):
    m = positions.size
    inv_freq = ROPE_THETA ** (-jnp.arange(0, 2 * ROT_HALF, 2, dtype=F32) / (2 * ROT_HALF))
    f_row = jnp.tile(inv_freq, LANES // ROT_HALF).reshape(1, LANES)
    f_col = inv_freq.reshape(ROT_HALF, 1)
    t = min(ROPE_TILE, m)
    return pl.pallas_call(
        _rope_kernel,
        grid=(m // t,),
        in_specs=[pl.BlockSpec((t, 1), lambda i: (i, 0)), pl.BlockSpec((1, t), lambda i: (0, i)),
                  pl.BlockSpec((1, LANES), lambda i: (0, 0)), pl.BlockSpec((ROT_HALF, 1), lambda i: (0, 0))],
        out_specs=[pl.BlockSpec((t, LANES), lambda i: (i, 0)), pl.BlockSpec((t, LANES), lambda i: (i, 0)),
                   pl.BlockSpec((ROT_HALF, t), lambda i: (0, i)), pl.BlockSpec((ROT_HALF, t), lambda i: (0, i))],
        out_shape=[jax.ShapeDtypeStruct((m, LANES), F32), jax.ShapeDtypeStruct((m, LANES), F32),
                   jax.ShapeDtypeStruct((ROT_HALF, m), F32), jax.ShapeDtypeStruct((ROT_HALF, m), F32)],
        compiler_params=_params(("parallel",)),
        name="rope_tables",
    )(positions.reshape(m, 1), positions.reshape(1, m), f_row, f_col)


def _proj_nat_kernel(x_ref, c_ref, s_ref, w_ref, ka_ref, qb_ref, kb_ref, vb_ref, ki_ref, gate_ref):
    xb = x_ref[0].astype(BF16)
    cos, sin = c_ref[...], s_ref[...]
    lane = lax.broadcasted_iota(jnp.int32, cos.shape, 1)
    first = (lane & (HEAD_DIM - 1)) < ROT_HALF

    def rope(y):
        partner = jnp.where(first, pltpu.roll(y, LANES - ROT_HALF, 1), pltpu.roll(y, ROT_HALF, 1))
        return y * cos + partner * sin

    def proj(lo, hi):
        return jnp.dot(xb, w_ref[:, lo:hi], preferred_element_type=F32)

    blocks = A_WIDTH // LANES
    y = proj(_N_KA, _N_KA + A_WIDTH)
    for p in range(blocks):
        ka_ref[0, p] = rope(y[:, p * LANES:(p + 1) * LANES]).astype(BF16)
    y = proj(_N_QB, _N_QB + B_WIDTH)
    for p in range(blocks):
        qb_ref[0, :, p * LANES:(p + 1) * LANES] = (rope(y[:, p * LANES:(p + 1) * LANES]) * ATTN_SCALE).astype(BF16)
    y = proj(_N_KB, _N_KB + B_WIDTH)
    for p in range(blocks):
        kb_ref[0, :, p * LANES:(p + 1) * LANES] = rope(y[:, p * LANES:(p + 1) * LANES]).astype(BF16)
    vb_ref[0] = proj(_N_VB, _N_VB + B_WIDTH).astype(BF16)
    r = rope(proj(_N_KI, _N_KI + LANES))
    hi = r.astype(BF16).astype(F32)
    hl = jnp.where(lane < IDX_DIM, hi, r - hi).astype(BF16)
    ki_ref[0, :, 0:LANES] = hl
    ki_ref[0, :, LANES:2 * LANES] = hl
    half = D_MODEL
    gate_ref[0, :, 0:half] = proj(_N_GATE, _N_GATE + half)
    gate_ref[0, :, half:2 * half] = proj(_N_GATE + half, _N_GATE + 2 * half)


def _proj_nat(x, cos, sin, w_nat):
    b, s, d = x.shape
    nt = s // TM
    tok = lambda width: pl.BlockSpec((1, TM, width), lambda i, j: (i, j, 0))
    return pl.pallas_call(
        _proj_nat_kernel,
        grid=(b, nt),
        in_specs=[tok(d),
                  pl.BlockSpec((TM, LANES), lambda i, j: (i * nt + j, 0)),
                  pl.BlockSpec((TM, LANES), lambda i, j: (i * nt + j, 0)),
                  pl.BlockSpec((d, _N_NAT), lambda i, j: (0, 0))],
        out_specs=[pl.BlockSpec((1, A_WIDTH // LANES, TM, LANES), lambda i, j: (i, 0, j, 0)),
                   tok(B_WIDTH), tok(B_WIDTH), tok(B_WIDTH), tok(2 * LANES), tok(2 * D_MODEL)],
        out_shape=[jax.ShapeDtypeStruct((b, A_WIDTH // LANES, s, LANES), BF16),
                   jax.ShapeDtypeStruct((b, s, B_WIDTH), BF16),
                   jax.ShapeDtypeStruct((b, s, B_WIDTH), BF16),
                   jax.ShapeDtypeStruct((b, s, B_WIDTH), BF16),
                   jax.ShapeDtypeStruct((b, s, 2 * LANES), BF16),
                   jax.ShapeDtypeStruct((b, s, 2 * D_MODEL), F32)],
        compiler_params=_params(("parallel", "parallel")),
        name="proj_token_major",
    )(x, cos, sin, w_nat)


def _proj_t_kernel(x_ref, ct_ref, st_ref, w_ref, qt_ref, vt_ref, qi_ref, wt_ref):
    xb = x_ref[0].astype(BF16)
    cos, sin = ct_ref[...], st_ref[...]

    def proj(lo, hi):
        return lax.dot_general(w_ref[lo:hi, :], xb, (((1,), (1,)), ((), ())), preferred_element_type=F32)

    def rope_head(y):
        x1, x2 = y[0:ROT_HALF], y[ROT_HALF:2 * ROT_HALF]
        return jnp.concatenate([x1 * cos - x2 * sin, x2 * cos + x1 * sin, y[2 * ROT_HALF:]], axis=0)

    y = proj(_T_QA, _T_QA + A_WIDTH)
    for h in range(A_HEADS):
        r = rope_head(y[h * HEAD_DIM:(h + 1) * HEAD_DIM])
        qt_ref[0, h * HEAD_DIM:(h + 1) * HEAD_DIM, :] = (r * ATTN_SCALE).astype(BF16)
    y = proj(_T_VA, _T_VA + A_WIDTH)
    for h in range(A_HEADS):
        for c in range(TM // TK):
            vt_ref[0, h, c] = y[h * HEAD_DIM:(h + 1) * HEAD_DIM, c * TK:(c + 1) * TK].astype(BF16)
    y = proj(_T_QI, _T_QI + IDX_HEADS * IDX_DIM)
    for h in range(IDX_HEADS):
        r = rope_head(y[h * IDX_DIM:(h + 1) * IDX_DIM])
        hi = r.astype(BF16)
        lo = (r - hi.astype(F32)).astype(BF16)
        base = 4 * h * IDX_DIM
        qi_ref[0, base:base + IDX_DIM, :] = hi
        qi_ref[0, base + IDX_DIM:base + 2 * IDX_DIM, :] = hi
        qi_ref[0, base + 2 * IDX_DIM:base + 3 * IDX_DIM, :] = lo
        qi_ref[0, base + 3 * IDX_DIM:base + 4 * IDX_DIM, :] = lo
    wt_ref[0] = proj(_T_WI, _T_ROWS)[0:IDX_HEADS] * IDX_SCALE


def _proj_t(x, cos_t, sin_t, w_t):
    b, s, d = x.shape
    nt = s // TM
    return pl.pallas_call(
        _proj_t_kernel,
        grid=(b, nt),
        in_specs=[pl.BlockSpec((1, TM, d), lambda i, j: (i, j, 0)),
                  pl.BlockSpec((ROT_HALF, TM), lambda i, j: (0, i * nt + j)),
                  pl.BlockSpec((ROT_HALF, TM), lambda i, j: (0, i * nt + j)),
                  pl.BlockSpec((_T_ROWS, d), lambda i, j: (0, 0))],
        out_specs=[pl.BlockSpec((1, A_WIDTH, TM), lambda i, j: (i, 0, j)),
                   pl.BlockSpec((1, A_HEADS, TM // TK, HEAD_DIM, TK), lambda i, j: (i, 0, j, 0, 0)),
                   pl.BlockSpec((1, 4 * IDX_HEADS * IDX_DIM, TM), lambda i, j: (i, 0, j)),
                   pl.BlockSpec((1, IDX_HEADS, TM), lambda i, j: (i, 0, j))],
        out_shape=[jax.ShapeDtypeStruct((b, A_WIDTH, s), BF16),
                   jax.ShapeDtypeStruct((b, A_HEADS, s // TK, HEAD_DIM, TK), BF16),
                   jax.ShapeDtypeStruct((b, 4 * IDX_HEADS * IDX_DIM, s), BF16),
                   jax.ShapeDtypeStruct((b, IDX_HEADS, s), F32)],
        compiler_params=_params(("parallel", "parallel")),
        name="proj_feature_major",
    )(x, cos_t, sin_t, w_t)


def _dsa_kernel(qt_ref, qi_ref, wt_ref, k_ref, vt_ref, ki_ref, o_ref,
                sc_ref, qpad_ref, ot_ref, m_ref, l_ref, st_ref, sa_ref, sb_ref):
    j = pl.program_id(1)
    nkt = j + 1
    n_key_bits = max((k_ref.shape[2] - 1).bit_length(), 1)

    for h in range(A_HEADS):
        off = (h % 2) * HEAD_DIM
        qpad_ref[h] = jnp.zeros((2 * HEAD_DIM, TQ), BF16)
        qpad_ref[h, off:off + HEAD_DIM, :] = qt_ref[0, h * HEAD_DIM:(h + 1) * HEAD_DIM, :]

    qpos = j * TQ + lax.broadcasted_iota(jnp.int32, (1, TQ), 1)
    row_iota = lax.broadcasted_iota(jnp.int32, (TK, TQ), 0)
    w = wt_ref[0]

    def score_tile(kt, carry):
        mn, mx = carry
        ki = ki_ref[0, pl.ds(pl.multiple_of(kt * TK, TK), TK), :]
        acc = jnp.zeros((TK, TQ), F32)
        for h in range(IDX_HEADS):
            s = jnp.dot(ki, qi_ref[0, 4 * h * IDX_DIM:4 * (h + 1) * IDX_DIM, :], preferred_element_type=F32)
            acc = acc + w[h:h + 1, :] * jnp.maximum(s, 0.0)
        causal = (kt * TK + row_iota) <= qpos
        sc_ref[kt] = jnp.where(causal, acc, -jnp.inf)
        lo_t = jnp.where(causal, acc, jnp.inf).reshape(TK // 8, 8, TQ).min(axis=0)
        hi_t = jnp.where(causal, acc, -jnp.inf).reshape(TK // 8, 8, TQ).max(axis=0)
        return jnp.minimum(mn, lo_t), jnp.maximum(mx, hi_t)

    mn8, mx8 = lax.fori_loop(0, nkt, score_tile,
                             (jnp.full((8, TQ), jnp.inf, F32), jnp.full((8, TQ), -jnp.inf, F32)))
    mn = mn8.min(axis=0, keepdims=True)
    mx = mx8.max(axis=0, keepdims=True)

    def count(pred):
        def body(kt, cnt):
            return cnt + pred(sc_ref[kt], kt).reshape(TK // 8, 8, TQ).sum(axis=0)
        return lax.fori_loop(0, nkt, body, jnp.zeros((8, TQ), F32)).sum(axis=0, keepdims=True)

    def count_ge(cand):
        return count(lambda t, kt: jnp.where(t >= cand, 1.0, 0.0))

    n_causal = (qpos + 1).astype(F32)
    k_q = jnp.minimum(n_causal, float(TOPK_MAX))
    c_mx = count_ge(mx)
    at_max = (n_causal > k_q) & (c_mx >= k_q)
    st_ref[0:1, :] = jnp.where(at_max, mx, mn)
    st_ref[1:2, :] = mx
    st_ref[2:3, :] = jnp.where(at_max, c_mx, n_causal)
    st_ref[3:4, :] = jnp.where(at_max, 0.0, c_mx)
    act0 = jnp.where((n_causal > k_q) & (c_mx < k_q), 1.0, 0.0)
    st_ref[4:5, :] = act0

    def bisect(go):
        lo, hi, c_lo, c_hi = st_ref[0:1, :], st_ref[1:2, :], st_ref[2:3, :], st_ref[3:4, :]
        act = st_ref[4:5, :] > 0.0
        mid = 0.5 * lo + 0.5 * hi
        live = act & (mid > lo) & (mid < hi)
        c = count_ge(mid)
        up = live & (c >= k_q)
        dn = live & (c < k_q)
        c_lo = jnp.where(up, c, c_lo)
        st_ref[0:1, :] = jnp.where(up, mid, lo)
        st_ref[1:2, :] = jnp.where(dn, mid, hi)
        st_ref[2:3, :] = c_lo
        st_ref[3:4, :] = jnp.where(dn, c, c_hi)
        act_new = jnp.where(live & (c_lo > k_q), 1.0, 0.0)
        st_ref[4:5, :] = act_new
        return (jnp.max(act_new) > 0.0).astype(jnp.int32)

    lax.while_loop(lambda go: go > 0, bisect, (jnp.max(act0) > 0.0).astype(jnp.int32))

    lo = st_ref[0:1, :]
    tie = st_ref[2:3, :] > k_q

    @pl.when(jnp.max(jnp.where(tie, 1.0, 0.0)) > 0.0)
    def _():
        need = k_q - st_ref[3:4, :]

        def eq_upto(jmid):
            return count(lambda t, kt: jnp.where(t == lo, jnp.where(kt * TK + row_iota <= jmid, 1.0, 0.0), 0.0))

        def step(_, carry):
            jlo, jhi = carry
            jmid = (jlo + jhi) >> 1
            ok = eq_upto(jmid) >= need
            return jnp.where(ok, jlo, jmid), jnp.where(ok, jmid, jhi)

        _, jcut = lax.fori_loop(0, n_key_bits + 1, step,
                                (jnp.full((1, TQ), -1, jnp.int32), jnp.full((1, TQ), nkt * TK - 1, jnp.int32)))
        jcut = jnp.where(tie, jcut, nkt * TK)

        def drop(kt, _):
            t = sc_ref[kt]
            sc_ref[kt] = jnp.where(t == lo, jnp.where(kt * TK + row_iota > jcut, -jnp.inf, t), t)
            return 0

        lax.fori_loop(0, nkt, drop, 0)

    def to_bias(kt, _):
        sc_ref[kt] = jnp.where(sc_ref[kt] >= lo, 0.0, NEG)
        return 0

    lax.fori_loop(0, nkt, to_bias, 0)

    m_ref[...] = jnp.full(m_ref.shape, NEG, F32)
    l_ref[...] = jnp.zeros(l_ref.shape, F32)
    ot_ref[...] = jnp.zeros(ot_ref.shape, F32)

    def logits(h, kt, bias):
        rows = pl.ds(pl.multiple_of(kt * TK, TK), TK)
        return jnp.dot(k_ref[0, h // 2, rows, :], qpad_ref[h], preferred_element_type=F32) + bias

    def half_step(kt, cur_ref, nxt_ref):
        kn = jnp.minimum(kt + 1, nkt - 1)
        bias_n = sc_ref[kn]
        for h in range(A_HEADS):
            s = cur_ref[h]
            nxt_ref[h] = logits(h, kn, bias_n)
            m = m_ref[h]
            m_new = jnp.maximum(m, s.reshape(TK // 8, 8, TQ).max(axis=0).max(axis=0, keepdims=True))
            p = jnp.exp(s - m_new)
            corr = jnp.exp(m - m_new)
            m_ref[h] = m_new
            l_ref[h] = l_ref[h] * corr + p.reshape(TK // 8, 8, TQ).sum(axis=0).sum(axis=0, keepdims=True)
            ot_ref[h] = ot_ref[h] * corr + jnp.dot(vt_ref[0, h, kt], p.astype(BF16), preferred_element_type=F32)

    bias0 = sc_ref[0]
    for h in range(A_HEADS):
        sa_ref[h] = logits(h, 0, bias0)

    def kv_pair(i, _):
        half_step(2 * i, sa_ref, sb_ref)

        @pl.when(2 * i + 1 < nkt)
        def _():
            half_step(2 * i + 1, sb_ref, sa_ref)

        return 0

    lax.fori_loop(0, (nkt + 1) // 2, kv_pair, 0)

    for p in range(A_HEADS // 2):
        both = jnp.concatenate([ot_ref[2 * p] / l_ref[2 * p], ot_ref[2 * p + 1] / l_ref[2 * p + 1]], axis=0)
        o_ref[0, :, p * LANES:(p + 1) * LANES] = both.T.astype(BF16)


def _dsa_attention(qt, qi4t, wt, k6, vt5, ki4):
    b, _, s = qt.shape
    nq = s // TQ
    return pl.pallas_call(
        _dsa_kernel,
        grid=(b, nq),
        in_specs=[pl.BlockSpec((1, A_WIDTH, TQ), lambda i, j: (i, 0, j)),
                  pl.BlockSpec((1, 4 * IDX_HEADS * IDX_DIM, TQ), lambda i, j: (i, 0, j)),
                  pl.BlockSpec((1, IDX_HEADS, TQ), lambda i, j: (i, 0, j)),
                  pl.BlockSpec((1, A_WIDTH // LANES, s, LANES), lambda i, j: (i, 0, 0, 0)),
                  pl.BlockSpec((1, A_HEADS, s // TK, HEAD_DIM, TK), lambda i, j: (i, 0, 0, 0, 0)),
                  pl.BlockSpec((1, s, 4 * IDX_DIM), lambda i, j: (i, 0, 0))],
        out_specs=pl.BlockSpec((1, TQ, A_WIDTH), lambda i, j: (i, j, 0)),
        out_shape=jax.ShapeDtypeStruct((b, s, A_WIDTH), BF16),
        scratch_shapes=[pltpu.VMEM((s // TK, TK, TQ), F32),
                        pltpu.VMEM((A_HEADS, 2 * HEAD_DIM, TQ), BF16),
                        pltpu.VMEM((A_HEADS, HEAD_DIM, TQ), F32),
                        pltpu.VMEM((A_HEADS, 1, TQ), F32),
                        pltpu.VMEM((A_HEADS, 1, TQ), F32),
                        pltpu.VMEM((8, TQ), F32),
                        pltpu.VMEM((A_HEADS, TK, TQ), F32),
                        pltpu.VMEM((A_HEADS, TK, TQ), F32)],
        compiler_params=_params(("parallel", "arbitrary")),
        name="dsa_attention",
    )(qt, qi4t, wt, k6, vt5, ki4)


def _window_kernel(q_ref, kp_ref, kc_ref, vp_ref, vc_ref, o_ref, lse_ref):
    i = pl.program_id(2)
    r = lax.broadcasted_iota(jnp.int32, (TB, 2 * TB), 0)
    c = lax.broadcasted_iota(jnp.int32, (TB, 2 * TB), 1)
    dist = r + TB - c
    first_col = jnp.where(i > 0, 0, TB)
    bias = jnp.where(dist >= 0, jnp.where(dist <= B_WINDOW_STEPS, jnp.where(c >= first_col, 0.0, NEG), NEG), NEG)
    lane = lax.broadcasted_iota(jnp.int32, (TB, LANES), 1)
    left = lane < HEAD_DIM
    for p in range(B_HEADS_PER_GROUP // 2):
        cols = slice(p * LANES, (p + 1) * LANES)
        q2 = q_ref[0, :, cols]
        k2 = jnp.concatenate([kp_ref[0, :, cols], kc_ref[0, :, cols]], axis=0)
        v2 = jnp.concatenate([vp_ref[0, :, cols], vc_ref[0, :, cols]], axis=0)
        outs, lses = [], []
        for side in (left, ~left):
            qh = jnp.where(side, q2, jnp.zeros_like(q2))
            s = lax.dot_general(qh, k2, (((1,), (1,)), ((), ())), preferred_element_type=F32) + bias
            m = s.max(axis=-1, keepdims=True)
            e = jnp.exp(s - m)
            l = e.sum(axis=-1, keepdims=True)
            outs.append(jnp.dot(e.astype(BF16), v2, preferred_element_type=F32) / l)
            lses.append(jnp.broadcast_to(m + jnp.log(l), (TB, LANES)))
        o_ref[0, :, cols] = jnp.where(left, outs[0], outs[1])
        lse_ref[0, :, cols] = jnp.where(left, lses[0], lses[1])


def _window_attention(qb, kb, vb, group):
    b, s, _ = qb.shape
    d = B_DILATIONS[group]
    n = s // d
    nt = n // TB
    view = lambda a: a.reshape(b, n, d * a.shape[-1])
    blocks_per_token = B_WIDTH // B_OUT_WIDTH
    cur = pl.BlockSpec((1, TB, B_OUT_WIDTH), lambda bi, ri, ti: (bi, ti, blocks_per_token * ri + group))
    prev = pl.BlockSpec((1, TB, B_OUT_WIDTH),
                        lambda bi, ri, ti: (bi, jnp.maximum(ti - 1, 0), blocks_per_token * ri + group))
    out = pl.BlockSpec((1, TB, B_OUT_WIDTH), lambda bi, ri, ti: (bi, ti, ri))
    o, lse = pl.pallas_call(
        _window_kernel,
        grid=(b, d, nt),
        in_specs=[cur, prev, cur, prev, cur],
        out_specs=[out, out],
        out_shape=[jax.ShapeDtypeStruct((b, n, d * B_OUT_WIDTH), F32)] * 2,
        compiler_params=_params(("parallel", "parallel", "arbitrary")),
        name=f"window_attention_d{d}",
    )(view(qb), view(kb), view(kb), view(vb), view(vb))
    return o.reshape(b * s, B_OUT_WIDTH), lse.reshape(b * s, B_OUT_WIDTH)


def _layer_norm(y, g, b):
    mu = y.mean(axis=-1, keepdims=True)
    yc = y - mu
    var = (yc * yc).mean(axis=-1, keepdims=True)
    return yc * lax.rsqrt(var + LN_EPS) * g + b


def _merge_kernel(alpha, oa_ref, o0_ref, o1_ref, o2_ref, l0_ref, l1_ref, l2_ref, gate_ref, x_ref,
                  bg_ref, wa_ref, wb_ref, wo_ref, g_ref, b_ref, y_ref):
    l0, l1, l2 = l0_ref[...], l1_ref[...], l2_ref[...]
    lm = jnp.maximum(jnp.maximum(l0, l1), l2)
    e0, e1, e2 = jnp.exp(l0 - lm), jnp.exp(l1 - lm), jnp.exp(l2 - lm)
    ob = (e0 * o0_ref[...] + e1 * o1_ref[...] + e2 * o2_ref[...]) / (e0 + e1 + e2)
    pa = jnp.dot(oa_ref[...], wa_ref[...], preferred_element_type=F32)
    pb = jnp.dot(ob.astype(BF16), wb_ref[...], preferred_element_type=F32)
    gate = 1.0 / (1.0 + jnp.exp(-(gate_ref[...] + bg_ref[...])))
    merged = gate[:, :D_MODEL] * pa + gate[:, D_MODEL:] * pb
    mixed = jnp.dot(merged.astype(BF16), wo_ref[...], preferred_element_type=F32)
    y_ref[...] = _layer_norm(alpha * x_ref[...] + mixed, g_ref[...], b_ref[...])


def _merge(alpha, oa, obs, lses, gate, x2, b_gate, wa, wb, wo, g, bb):
    m = x2.shape[0]
    tok = lambda width: pl.BlockSpec((TM, width), lambda i: (i, 0))
    full = lambda a: pl.BlockSpec(a.shape, lambda i: (0, 0))
    return pl.pallas_call(
        functools.partial(_merge_kernel, alpha),
        grid=(m // TM,),
        in_specs=[tok(A_WIDTH)] + [tok(B_OUT_WIDTH)] * 6 + [tok(2 * D_MODEL), tok(D_MODEL),
                  full(b_gate), full(wa), full(wb), full(wo), full(g), full(bb)],
        out_specs=tok(D_MODEL),
        out_shape=jax.ShapeDtypeStruct((m, D_MODEL), F32),
        compiler_params=_params(("parallel",)),
        name="merge_outproj_norm",
    )(oa, *obs, *lses, gate, x2, b_gate, wa, wb, wo, g, bb)


def _ffn_kernel(alpha, x_ref, wg_ref, wu_ref, wd_ref, g_ref, b_ref, y_ref):
    x = x_ref[...]
    xb = x.astype(BF16)
    acc = jnp.zeros((TM, D_MODEL), F32)
    for c in range(FFN_HIDDEN // FFN_CHUNK):
        cols = slice(c * FFN_CHUNK, (c + 1) * FFN_CHUNK)
        gate = jnp.dot(xb, wg_ref[:, cols], preferred_element_type=F32)
        up = jnp.dot(xb, wu_ref[:, cols], preferred_element_type=F32)
        h = gate / (1.0 + jnp.exp(-gate)) * up
        acc = acc + jnp.dot(h.astype(BF16), wd_ref[cols, :], preferred_element_type=F32)
    y_ref[...] = _layer_norm(alpha * x + acc, g_ref[...], b_ref[...])


def _ffn(alpha, x2, wg, wu, wd, g, bb):
    m = x2.shape[0]
    tok = pl.BlockSpec((TM, D_MODEL), lambda i: (i, 0))
    full = lambda a: pl.BlockSpec(a.shape, lambda i: (0, 0))
    return pl.pallas_call(
        functools.partial(_ffn_kernel, alpha),
        grid=(m // TM,),
        in_specs=[tok, full(wg), full(wu), full(wd), full(g), full(bb)],
        out_specs=tok,
        out_shape=jax.ShapeDtypeStruct((m, D_MODEL), F32),
        compiler_params=_params(("parallel",)),
        name="swiglu_norm",
    )(x2, wg, wu, wd, g, bb)


def _split_w_in(w):
    a, bw, hi = A_WIDTH, B_WIDTH, IDX_HEADS * IDX_DIM
    o = 0
    qa, o = w[:, o:o + a], o + a
    ka, o = w[:, o:o + a], o + a
    va, o = w[:, o:o + a], o + a
    qb, o = w[:, o:o + bw], o + bw
    kb, o = w[:, o:o + bw], o + bw
    vb, o = w[:, o:o + bw], o + bw
    qi, o = w[:, o:o + hi], o + hi
    ki, o = w[:, o:o + IDX_DIM], o + IDX_DIM
    wi, o = w[:, o:o + IDX_HEADS], o + IDX_HEADS
    gate = w[:, o:]
    w_nat = jnp.concatenate([ka, qb, kb, vb, ki, ki, gate], axis=1).astype(BF16)
    pad = jnp.zeros((w.shape[0], _T_ROWS - _T_WI - IDX_HEADS), w.dtype)
    w_t = jnp.concatenate([qa, va, qi, wi, pad], axis=1).T.astype(BF16)
    return w_nat, w_t


def kernel(x, positions, w_in, b_gate, w_branch_a, w_branch_b, w_out, ln1_g, ln1_b,
           w_ffn_gate, w_ffn_up, w_ffn_down, ln2_g, ln2_b):
    b, s, d = x.shape
    depth = w_in.shape[0]
    assert d == D_MODEL and s % (max(B_DILATIONS) * TB) == 0 and s % TM == 0
    alpha = (2 * depth) ** 0.25
    cos, sin, cos_t, sin_t = _rope_tables(positions)
    row = lambda v: v.reshape(1, -1)
    for layer in range(depth):
        w_nat, w_t = _split_w_in(w_in[layer])
        ka6, qb, kb, vb, ki4, gate = _proj_nat(x, cos, sin, w_nat)
        qt, vt5, qi4t, wt = _proj_t(x, cos_t, sin_t, w_t)
        oa = _dsa_attention(qt, qi4t, wt, ka6, vt5, ki4)
        groups = [_window_attention(qb, kb, vb, g) for g in range(len(B_DILATIONS))]
        x1 = _merge(alpha, oa.reshape(b * s, A_WIDTH), [o for o, _ in groups], [l for _, l in groups],
                    gate.reshape(b * s, 2 * D_MODEL), x.reshape(b * s, d), row(b_gate[layer]),
                    w_branch_a[layer].astype(BF16), w_branch_b[layer].astype(BF16),
                    w_out[layer].astype(BF16), row(ln1_g[layer]), row(ln1_b[layer]))
        x2 = _ffn(alpha, x1, w_ffn_gate[layer].astype(BF16), w_ffn_up[layer].astype(BF16),
                  w_ffn_down[layer].astype(BF16), row(ln2_g[layer]), row(ln2_b[layer]))
        x = x2.reshape(b, s, d)
    return x
```

```python
import functools

import jax
import jax.numpy as jnp
from jax import lax
from jax.experimental import pallas as pl
from jax.experimental.pallas import tpu as pltpu

F32 = jnp.float32
BF16 = jnp.bfloat16

D_MODEL = 1024
HEAD_DIM = 64
ROT_HALF = 8
ROPE_THETA = 500000.0
ATTN_SCALE = HEAD_DIM ** -0.5
A_HEADS = 12
A_WIDTH = A_HEADS * HEAD_DIM
IDX_HEADS = 8
IDX_DIM = 64
IDX_SCALE = (IDX_HEADS ** -0.5) * (IDX_DIM ** -0.5)
TOPK_MAX = 256
B_DILATIONS = (1, 4, 16)
B_WINDOW_STEPS = 128
B_HEADS_PER_GROUP = 4
B_WIDTH = 3 * B_HEADS_PER_GROUP * HEAD_DIM
B_OUT_WIDTH = B_HEADS_PER_GROUP * HEAD_DIM
FFN_HIDDEN = 2816
LN_EPS = 1e-5
NEG = -1e30

LANES = 128
VMEM_LIMIT = 56 * 1024 * 1024
TM = 512
TQ = 256
TK = 256
TB = 256
FFN_CHUNK = 256
BISECT_STEPS_PER_CHECK = 2
ROPE_TILE = 2048

_N_KA, _N_QB, _N_KB, _N_VB = 0, 768, 1536, 2304
_N_KI, _N_GATE, _N_NAT = 3072, 3200, 5248
_T_QA, _T_VA, _T_QI, _T_WI, _T_ROWS = 0, 768, 1536, 2048, 2064


def _params(sem):
    return pltpu.CompilerParams(dimension_semantics=sem, vmem_limit_bytes=VMEM_LIMIT)


def _rope_kernel(pc_ref, pr_ref, fr_ref, fc_ref, c_ref, s_ref, ct_ref, st_ref):
    ang = pc_ref[...].astype(F32) * fr_ref[...]
    d = lax.broadcasted_iota(jnp.int32, ang.shape, 1) & (HEAD_DIM - 1)
    cos, sin = jnp.cos(ang), jnp.sin(ang)
    c_ref[...] = jnp.where(d < 2 * ROT_HALF, cos, 1.0)
    s_ref[...] = jnp.where(d < ROT_HALF, -sin, jnp.where(d < 2 * ROT_HALF, sin, 0.0))
    ang_t = fc_ref[...] * pr_ref[...].astype(F32)
    ct_ref[...] = jnp.cos(ang_t)
    st_ref[...] = jnp.sin(ang_t)


def _rope_tables(positions):
    m = positions.size
    inv_freq = ROPE_THETA ** (-jnp.arange(0, 2 * ROT_HALF, 2, dtype=F32) / (2 * ROT_HALF))
    f_row = jnp.tile(inv_freq, LANES // ROT_HALF).reshape(1, LANES)
    f_col = inv_freq.reshape(ROT_HALF, 1)
    t = min(ROPE_TILE, m)
    return pl.pallas_call(
        _rope_kernel,
        grid=(m // t,),
        in_specs=[pl.BlockSpec((t, 1), lambda i: (i, 0)), pl.BlockSpec((1, t), lambda i: (0, i)),
                  pl.BlockSpec((1, LANES), lambda i: (0, 0)), pl.BlockSpec((ROT_HALF, 1), lambda i: (0, 0))],
        out_specs=[pl.BlockSpec((t, LANES), lambda i: (i, 0)), pl.BlockSpec((t, LANES), lambda i: (i, 0)),
                   pl.BlockSpec((ROT_HALF, t), lambda i: (0, i)), pl.BlockSpec((ROT_HALF, t), lambda i: (0, i))],
        out_shape=[jax.ShapeDtypeStruct((m, LANES), F32), jax.ShapeDtypeStruct((m, LANES), F32),
                   jax.ShapeDtypeStruct((ROT_HALF, m), F32), jax.ShapeDtypeStruct((ROT_HALF, m), F32)],
        compiler_params=_params(("parallel",)),
        name="rope_tables",
    )(positions.reshape(m, 1), positions.reshape(1, m), f_row, f_col)


def _proj_nat_kernel(x_ref, c_ref, s_ref, w_ref, ka_ref, ki_ref, gate_ref, *rest):
    streams, y_ref = rest[:-1], rest[-1]
    xb = x_ref[0].astype(BF16)
    cos, sin = c_ref[...], s_ref[...]
    lane = lax.broadcasted_iota(jnp.int32, cos.shape, 1)
    first = (lane & (HEAD_DIM - 1)) < ROT_HALF

    def rope(y):
        partner = jnp.where(first, pltpu.roll(y, LANES - ROT_HALF, 1), pltpu.roll(y, ROT_HALF, 1))
        return y * cos + partner * sin

    def proj(lo, hi):
        return jnp.dot(xb, w_ref[:, lo:hi], preferred_element_type=F32)

    blocks = A_WIDTH // LANES
    y = proj(_N_KA, _N_KA + A_WIDTH)
    for p in range(blocks):
        ka_ref[0, p] = rope(y[:, p * LANES:(p + 1) * LANES]).astype(BF16)

    def scatter_streams(which):
        per_group = B_OUT_WIDTH // LANES
        for g, d in enumerate(B_DILATIONS):
            out = streams[3 * g + which]
            for r in range(d):
                for p in range(per_group):
                    rows = y_ref[g * per_group + p, pl.ds(r, TM // d, stride=d), :]
                    out[0, r, :, p * LANES:(p + 1) * LANES] = rows.astype(BF16)

    y = proj(_N_QB, _N_QB + B_WIDTH)
    for p in range(blocks):
        y_ref[p] = rope(y[:, p * LANES:(p + 1) * LANES]) * ATTN_SCALE
    scatter_streams(0)
    y = proj(_N_KB, _N_KB + B_WIDTH)
    for p in range(blocks):
        y_ref[p] = rope(y[:, p * LANES:(p + 1) * LANES])
    scatter_streams(1)
    y = proj(_N_VB, _N_VB + B_WIDTH)
    for p in range(blocks):
        y_ref[p] = y[:, p * LANES:(p + 1) * LANES]
    scatter_streams(2)
    r = rope(proj(_N_KI, _N_KI + LANES))
    hi = r.astype(BF16).astype(F32)
    hl = jnp.where(lane < IDX_DIM, hi, r - hi).astype(BF16)
    ki_ref[0, :, 0:LANES] = hl
    ki_ref[0, :, LANES:2 * LANES] = hl
    half = D_MODEL
    gate_ref[0, :, 0:half] = proj(_N_GATE, _N_GATE + half)
    gate_ref[0, :, half:2 * half] = proj(_N_GATE + half, _N_GATE + 2 * half)


def _proj_nat(x, cos, sin, w_nat):
    b, s, d = x.shape
    nt = s // TM
    tok = lambda width: pl.BlockSpec((1, TM, width), lambda i, j: (i, j, 0))
    stream_specs, stream_shapes = [], []
    for dil in B_DILATIONS:
        for _ in range(3):
            stream_specs.append(pl.BlockSpec((1, dil, TM // dil, B_OUT_WIDTH), lambda i, j: (i, 0, j, 0)))
            stream_shapes.append(jax.ShapeDtypeStruct((b, dil, s // dil, B_OUT_WIDTH), BF16))
    return pl.pallas_call(
        _proj_nat_kernel,
        grid=(b, nt),
        in_specs=[tok(d),
                  pl.BlockSpec((TM, LANES), lambda i, j: (i * nt + j, 0)),
                  pl.BlockSpec((TM, LANES), lambda i, j: (i * nt + j, 0)),
                  pl.BlockSpec((d, _N_NAT), lambda i, j: (0, 0))],
        out_specs=[pl.BlockSpec((1, A_WIDTH // LANES, TM, LANES), lambda i, j: (i, 0, j, 0)),
                   tok(2 * LANES), tok(2 * D_MODEL)] + stream_specs,
        out_shape=[jax.ShapeDtypeStruct((b, A_WIDTH // LANES, s, LANES), BF16),
                   jax.ShapeDtypeStruct((b, s, 2 * LANES), BF16),
                   jax.ShapeDtypeStruct((b, s, 2 * D_MODEL), F32)] + stream_shapes,
        scratch_shapes=[pltpu.VMEM((B_WIDTH // LANES, TM, LANES), F32)],
        compiler_params=_params(("parallel", "parallel")),
        name="proj_token_major",
    )(x, cos, sin, w_nat)


def _proj_t_kernel(x_ref, ct_ref, st_ref, w_ref, qt_ref, vt_ref, qi_ref, wt_ref):
    xb = x_ref[0].astype(BF16)
    cos, sin = ct_ref[...], st_ref[...]

    def proj(lo, hi):
        return lax.dot_general(w_ref[lo:hi, :], xb, (((1,), (1,)), ((), ())), preferred_element_type=F32)

    def rope_head(y):
        x1, x2 = y[0:ROT_HALF], y[ROT_HALF:2 * ROT_HALF]
        return jnp.concatenate([x1 * cos - x2 * sin, x2 * cos + x1 * sin, y[2 * ROT_HALF:]], axis=0)

    y = proj(_T_QA, _T_QA + A_WIDTH)
    for h in range(A_HEADS):
        r = rope_head(y[h * HEAD_DIM:(h + 1) * HEAD_DIM])
        qt_ref[0, h * HEAD_DIM:(h + 1) * HEAD_DIM, :] = (r * ATTN_SCALE).astype(BF16)
    y = proj(_T_VA, _T_VA + A_WIDTH)
    for h in range(A_HEADS):
        for c in range(TM // TK):
            vt_ref[0, h, c] = y[h * HEAD_DIM:(h + 1) * HEAD_DIM, c * TK:(c + 1) * TK].astype(BF16)
    y = proj(_T_QI, _T_QI + IDX_HEADS * IDX_DIM)
    for h in range(IDX_HEADS):
        r = rope_head(y[h * IDX_DIM:(h + 1) * IDX_DIM])
        hi = r.astype(BF16)
        lo = (r - hi.astype(F32)).astype(BF16)
        base = 4 * h * IDX_DIM
        qi_ref[0, base:base + IDX_DIM, :] = hi
        qi_ref[0, base + IDX_DIM:base + 2 * IDX_DIM, :] = hi
        qi_ref[0, base + 2 * IDX_DIM:base + 3 * IDX_DIM, :] = lo
        qi_ref[0, base + 3 * IDX_DIM:base + 4 * IDX_DIM, :] = lo
    wt_ref[0] = proj(_T_WI, _T_ROWS)[0:IDX_HEADS] * IDX_SCALE


def _proj_t(x, cos_t, sin_t, w_t):
    b, s, d = x.shape
    nt = s // TM
    return pl.pallas_call(
        _proj_t_kernel,
        grid=(b, nt),
        in_specs=[pl.BlockSpec((1, TM, d), lambda i, j: (i, j, 0)),
                  pl.BlockSpec((ROT_HALF, TM), lambda i, j: (0, i * nt + j)),
                  pl.BlockSpec((ROT_HALF, TM), lambda i, j: (0, i * nt + j)),
                  pl.BlockSpec((_T_ROWS, d), lambda i, j: (0, 0))],
        out_specs=[pl.BlockSpec((1, A_WIDTH, TM), lambda i, j: (i, 0, j)),
                   pl.BlockSpec((1, A_HEADS, TM // TK, HEAD_DIM, TK), lambda i, j: (i, 0, j, 0, 0)),
                   pl.BlockSpec((1, 4 * IDX_HEADS * IDX_DIM, TM), lambda i, j: (i, 0, j)),
                   pl.BlockSpec((1, IDX_HEADS, TM), lambda i, j: (i, 0, j))],
        out_shape=[jax.ShapeDtypeStruct((b, A_WIDTH, s), BF16),
                   jax.ShapeDtypeStruct((b, A_HEADS, s // TK, HEAD_DIM, TK), BF16),
                   jax.ShapeDtypeStruct((b, 4 * IDX_HEADS * IDX_DIM, s), BF16),
                   jax.ShapeDtypeStruct((b, IDX_HEADS, s), F32)],
        compiler_params=_params(("parallel", "parallel")),
        name="proj_feature_major",
    )(x, cos_t, sin_t, w_t)


def _dsa_kernel(qt_ref, qi_ref, wt_ref, k_ref, vt_ref, ki_ref, o_ref,
                sc_ref, qpad_ref, ot_ref, m_ref, l_ref, st_ref, sa_ref, sb_ref):
    j = pl.program_id(1)
    nkt = j + 1

    for h in range(A_HEADS):
        off = (h % 2) * HEAD_DIM
        qpad_ref[h] = jnp.zeros((2 * HEAD_DIM, TQ), BF16)
        qpad_ref[h, off:off + HEAD_DIM, :] = qt_ref[0, h * HEAD_DIM:(h + 1) * HEAD_DIM, :]

    qpos = j * TQ + lax.broadcasted_iota(jnp.int32, (1, TQ), 1)
    row_iota = lax.broadcasted_iota(jnp.int32, (TK, TQ), 0)
    w = wt_ref[0]

    def score_tile(kt, carry):
        mn, mx = carry
        ki = ki_ref[0, pl.ds(pl.multiple_of(kt * TK, TK), TK), :]
        acc = jnp.zeros((TK, TQ), F32)
        for h in range(IDX_HEADS):
            s = jnp.dot(ki, qi_ref[0, 4 * h * IDX_DIM:4 * (h + 1) * IDX_DIM, :], preferred_element_type=F32)
            acc = acc + w[h:h + 1, :] * jnp.maximum(s, 0.0)
        causal = (kt * TK + row_iota) <= qpos
        sc_ref[kt] = jnp.where(causal, acc, -jnp.inf)
        lo_t = jnp.where(causal, acc, jnp.inf).reshape(TK // 8, 8, TQ).min(axis=0)
        hi_t = jnp.where(causal, acc, -jnp.inf).reshape(TK // 8, 8, TQ).max(axis=0)
        return jnp.minimum(mn, lo_t), jnp.maximum(mx, hi_t)

    mn8, mx8 = lax.fori_loop(0, nkt, score_tile,
                             (jnp.full((8, TQ), jnp.inf, F32), jnp.full((8, TQ), -jnp.inf, F32)))
    mn = mn8.min(axis=0, keepdims=True)
    mx = mx8.max(axis=0, keepdims=True)

    sc_ref[nkt] = jnp.full((TK, TQ), -jnp.inf, F32)

    def count(preds):
        def part(p, kt):
            return p(sc_ref[kt]).reshape(TK // 8, 8, TQ).sum(axis=0)

        def body(i, cnts):
            return tuple(c + part(p, 2 * i) + part(p, 2 * i + 1) for c, p in zip(cnts, preds))

        cnts = lax.fori_loop(0, (nkt + 1) // 2, body, tuple(jnp.zeros((8, TQ), F32) for _ in preds))
        return [c.sum(axis=0, keepdims=True) for c in cnts]

    def ge(cand):
        return lambda t: jnp.where(t >= cand, 1.0, 0.0)

    n_causal = (qpos + 1).astype(F32)
    k_q = jnp.minimum(n_causal, float(TOPK_MAX))
    c_mx, c_pos, c_nn = count([ge(mx), lambda t: jnp.where(t > 0.0, 1.0, 0.0), ge(0.0)])
    select = n_causal > k_q
    at_max = select & (c_mx >= k_q)
    search = select & (c_mx < k_q)
    at_zero = search & (c_pos < k_q) & (c_nn >= k_q)
    above = search & (c_pos >= k_q)
    below = search & (c_nn < k_q)
    lo0 = jnp.where(at_max, mx, jnp.where(at_zero | above, 0.0, mn))
    c_lo0 = jnp.where(at_max, c_mx, jnp.where(at_zero | above, c_nn, n_causal))
    st_ref[0:1, :] = lo0
    st_ref[1:2, :] = jnp.where(below, 0.0, mx)
    st_ref[2:3, :] = c_lo0
    st_ref[3:4, :] = jnp.where(at_max, 0.0, jnp.where(at_zero, c_pos, jnp.where(below, c_nn, c_mx)))
    act0 = jnp.where((above | below) & (c_lo0 > k_q), 1.0, 0.0)
    st_ref[4:5, :] = act0

    def bisect_once():
        lo, hi, c_lo, c_hi = st_ref[0:1, :], st_ref[1:2, :], st_ref[2:3, :], st_ref[3:4, :]
        act = st_ref[4:5, :] > 0.0
        mid = 0.5 * lo + 0.5 * hi
        live = act & (mid > lo) & (mid < hi)
        c, = count([ge(mid)])
        up = live & (c >= k_q)
        dn = live & (c < k_q)
        c_lo = jnp.where(up, c, c_lo)
        st_ref[0:1, :] = jnp.where(up, mid, lo)
        st_ref[1:2, :] = jnp.where(dn, mid, hi)
        st_ref[2:3, :] = c_lo
        st_ref[3:4, :] = jnp.where(dn, c, c_hi)
        act_new = jnp.where(live & (c_lo > k_q), 1.0, 0.0)
        st_ref[4:5, :] = act_new
        return act_new

    def bisect(go):
        for _ in range(BISECT_STEPS_PER_CHECK - 1):
            bisect_once()
        return (jnp.max(bisect_once()) > 0.0).astype(jnp.int32)

    lax.while_loop(lambda go: go > 0, bisect, (jnp.max(act0) > 0.0).astype(jnp.int32))

    lo = st_ref[0:1, :]
    tie = st_ref[2:3, :] > k_q

    @pl.when(jnp.max(jnp.where(tie, 1.0, 0.0)) > 0.0)
    def _():
        need = jnp.where(tie, k_q - st_ref[3:4, :], jnp.inf)
        tri = jnp.where(lax.broadcasted_iota(jnp.int32, (TK, TK), 0) >= lax.broadcasted_iota(jnp.int32, (TK, TK), 1),
                        1.0, 0.0).astype(BF16)

        def drop(kt, seen):
            t = sc_ref[kt]
            eq = jnp.where(t == lo, 1.0, 0.0).astype(BF16)
            rank = jnp.dot(tri, eq, preferred_element_type=F32) + seen
            sc_ref[kt] = jnp.where(t == lo, jnp.where(rank > need, -jnp.inf, t), t)
            return rank[TK - 1:TK, :]

        lax.fori_loop(0, nkt, drop, jnp.zeros((1, TQ), F32))

    def to_bias(kt, _):
        sc_ref[kt] = jnp.where(sc_ref[kt] >= lo, 0.0, NEG)
        return 0

    lax.fori_loop(0, nkt, to_bias, 0)

    m_ref[...] = jnp.full(m_ref.shape, NEG, F32)
    l_ref[...] = jnp.zeros(l_ref.shape, F32)
    ot_ref[...] = jnp.zeros(ot_ref.shape, F32)

    def logits(h, kt, bias):
        rows = pl.ds(pl.multiple_of(kt * TK, TK), TK)
        return jnp.dot(k_ref[0, h // 2, rows, :], qpad_ref[h], preferred_element_type=F32) + bias

    def half_step(kt, cur_ref, nxt_ref):
        kn = jnp.minimum(kt + 1, nkt - 1)
        bias_n = sc_ref[kn]
        for h in range(A_HEADS):
            s = cur_ref[h]
            nxt_ref[h] = logits(h, kn, bias_n)
            m = m_ref[h]
            m_new = jnp.maximum(m, s.reshape(TK // 8, 8, TQ).max(axis=0).max(axis=0, keepdims=True))
            p = jnp.exp(s - m_new)
            corr = jnp.exp(m - m_new)
            m_ref[h] = m_new
            l_ref[h] = l_ref[h] * corr + p.reshape(TK // 8, 8, TQ).sum(axis=0).sum(axis=0, keepdims=True)
            ot_ref[h] = ot_ref[h] * corr + jnp.dot(vt_ref[0, h, kt], p.astype(BF16), preferred_element_type=F32)

    bias0 = sc_ref[0]
    for h in range(A_HEADS):
        sa_ref[h] = logits(h, 0, bias0)

    def kv_pair(i, _):
        half_step(2 * i, sa_ref, sb_ref)

        @pl.when(2 * i + 1 < nkt)
        def _():
            half_step(2 * i + 1, sb_ref, sa_ref)

        return 0

    lax.fori_loop(0, (nkt + 1) // 2, kv_pair, 0)

    for p in range(A_HEADS // 2):
        both = jnp.concatenate([ot_ref[2 * p] / l_ref[2 * p], ot_ref[2 * p + 1] / l_ref[2 * p + 1]], axis=0)
        o_ref[0, :, p * LANES:(p + 1) * LANES] = both.T.astype(BF16)


def _dsa_attention(qt, qi4t, wt, k6, vt5, ki4):
    b, _, s = qt.shape
    nq = s // TQ
    return pl.pallas_call(
        _dsa_kernel,
        grid=(b, nq),
        in_specs=[pl.BlockSpec((1, A_WIDTH, TQ), lambda i, j: (i, 0, j)),
                  pl.BlockSpec((1, 4 * IDX_HEADS * IDX_DIM, TQ), lambda i, j: (i, 0, j)),
                  pl.BlockSpec((1, IDX_HEADS, TQ), lambda i, j: (i, 0, j)),
                  pl.BlockSpec((1, A_WIDTH // LANES, s, LANES), lambda i, j: (i, 0, 0, 0)),
                  pl.BlockSpec((1, A_HEADS, s // TK, HEAD_DIM, TK), lambda i, j: (i, 0, 0, 0, 0)),
                  pl.BlockSpec((1, s, 4 * IDX_DIM), lambda i, j: (i, 0, 0))],
        out_specs=pl.BlockSpec((1, TQ, A_WIDTH), lambda i, j: (i, j, 0)),
        out_shape=jax.ShapeDtypeStruct((b, s, A_WIDTH), BF16),
        scratch_shapes=[pltpu.VMEM((s // TK + 1, TK, TQ), F32),
                        pltpu.VMEM((A_HEADS, 2 * HEAD_DIM, TQ), BF16),
                        pltpu.VMEM((A_HEADS, HEAD_DIM, TQ), F32),
                        pltpu.VMEM((A_HEADS, 1, TQ), F32),
                        pltpu.VMEM((A_HEADS, 1, TQ), F32),
                        pltpu.VMEM((8, TQ), F32),
                        pltpu.VMEM((A_HEADS, TK, TQ), F32),
                        pltpu.VMEM((A_HEADS, TK, TQ), F32)],
        compiler_params=_params(("parallel", "arbitrary")),
        name="dsa_attention",
    )(qt, qi4t, wt, k6, vt5, ki4)


def _window_kernel(q_ref, kp_ref, kc_ref, vp_ref, vc_ref, o_ref, lse_ref):
    i = pl.program_id(2)
    r = lax.broadcasted_iota(jnp.int32, (TB, 2 * TB), 0)
    c = lax.broadcasted_iota(jnp.int32, (TB, 2 * TB), 1)
    dist = r + TB - c
    first_col = jnp.where(i > 0, 0, TB)
    bias = jnp.where(dist >= 0, jnp.where(dist <= B_WINDOW_STEPS, jnp.where(c >= first_col, 0.0, NEG), NEG), NEG)
    lane = lax.broadcasted_iota(jnp.int32, (TB, LANES), 1)
    left = lane < HEAD_DIM
    for p in range(B_HEADS_PER_GROUP // 2):
        cols = slice(p * LANES, (p + 1) * LANES)
        q2 = q_ref[0, 0, :, cols]
        k2 = jnp.concatenate([kp_ref[0, 0, :, cols], kc_ref[0, 0, :, cols]], axis=0)
        v2 = jnp.concatenate([vp_ref[0, 0, :, cols], vc_ref[0, 0, :, cols]], axis=0)
        outs, lses = [], []
        for side in (left, ~left):
            qh = jnp.where(side, q2, jnp.zeros_like(q2))
            s = lax.dot_general(qh, k2, (((1,), (1,)), ((), ())), preferred_element_type=F32) + bias
            m = s.max(axis=-1, keepdims=True)
            e = jnp.exp(s - m)
            l = e.sum(axis=-1, keepdims=True)
            outs.append(jnp.dot(e.astype(BF16), v2, preferred_element_type=F32) / l)
            lses.append(jnp.broadcast_to(m + jnp.log(l), (TB, LANES)))
        o_ref[0, 0, :, cols] = jnp.where(left, outs[0], outs[1])
        lse_ref[0, 0, :, cols] = jnp.where(left, lses[0], lses[1])


def _window_attention(q, k, v):
    b, d, n, _ = q.shape
    cur = pl.BlockSpec((1, 1, TB, B_OUT_WIDTH), lambda bi, ri, ti: (bi, ri, ti, 0))
    prev = pl.BlockSpec((1, 1, TB, B_OUT_WIDTH), lambda bi, ri, ti: (bi, ri, jnp.maximum(ti - 1, 0), 0))
    return pl.pallas_call(
        _window_kernel,
        grid=(b, d, n // TB),
        in_specs=[cur, prev, cur, prev, cur],
        out_specs=[cur, cur],
        out_shape=[jax.ShapeDtypeStruct((b, d, n, B_OUT_WIDTH), F32)] * 2,
        compiler_params=_params(("parallel", "parallel", "arbitrary")),
        name=f"window_attention_d{d}",
    )(q, k, k, v, v)


def _layer_norm(y, g, b):
    mu = y.mean(axis=-1, keepdims=True)
    yc = y - mu
    var = (yc * yc).mean(axis=-1, keepdims=True)
    return yc * lax.rsqrt(var + LN_EPS) * g + b


def _merge_kernel(alpha, oa_ref, o0_ref, o1_ref, o2_ref, l0_ref, l1_ref, l2_ref, gate_ref, x_ref,
                  bg_ref, wa_ref, wb_ref, wo_ref, g_ref, b_ref, y_ref, tok_ref):
    def token_order(ref):
        d = ref.shape[1]
        if d == 1:
            return ref[0, 0]
        for r in range(d):
            for p in range(B_OUT_WIDTH // LANES):
                tok_ref[p, pl.ds(r, TM // d, stride=d), :] = ref[0, r, :, p * LANES:(p + 1) * LANES]
        return jnp.concatenate([tok_ref[p] for p in range(B_OUT_WIDTH // LANES)], axis=1)

    l0, l1, l2 = token_order(l0_ref), token_order(l1_ref), token_order(l2_ref)
    lm = jnp.maximum(jnp.maximum(l0, l1), l2)
    e0, e1, e2 = jnp.exp(l0 - lm), jnp.exp(l1 - lm), jnp.exp(l2 - lm)
    ob = (e0 * token_order(o0_ref) + e1 * token_order(o1_ref) + e2 * token_order(o2_ref)) / (e0 + e1 + e2)
    pa = jnp.dot(oa_ref[0], wa_ref[...], preferred_element_type=F32)
    pb = jnp.dot(ob.astype(BF16), wb_ref[...], preferred_element_type=F32)
    gate = 1.0 / (1.0 + jnp.exp(-(gate_ref[0] + bg_ref[...])))
    merged = gate[:, :D_MODEL] * pa + gate[:, D_MODEL:] * pb
    mixed = jnp.dot(merged.astype(BF16), wo_ref[...], preferred_element_type=F32)
    y_ref[0] = _layer_norm(alpha * x_ref[0] + mixed, g_ref[...], b_ref[...])


def _merge(alpha, oa, obs, lses, gate, x, b_gate, wa, wb, wo, g, bb):
    b, s, _ = x.shape
    tok = lambda width: pl.BlockSpec((1, TM, width), lambda i, j: (i, j, 0))
    full = lambda a: pl.BlockSpec(a.shape, lambda i, j: (0, 0))
    streams = [pl.BlockSpec((1, a.shape[1], TM // a.shape[1], B_OUT_WIDTH), lambda i, j: (i, 0, j, 0))
               for a in list(obs) + list(lses)]
    return pl.pallas_call(
        functools.partial(_merge_kernel, alpha),
        grid=(b, s // TM),
        in_specs=[tok(A_WIDTH)] + streams + [tok(2 * D_MODEL), tok(D_MODEL),
                  full(b_gate), full(wa), full(wb), full(wo), full(g), full(bb)],
        out_specs=tok(D_MODEL),
        out_shape=jax.ShapeDtypeStruct((b, s, D_MODEL), F32),
        scratch_shapes=[pltpu.VMEM((B_OUT_WIDTH // LANES, TM, LANES), F32)],
        compiler_params=_params(("parallel", "parallel")),
        name="merge_outproj_norm",
    )(oa, *obs, *lses, gate, x, b_gate, wa, wb, wo, g, bb)


def _ffn_kernel(alpha, x_ref, wg_ref, wu_ref, wd_ref, g_ref, b_ref, y_ref):
    x = x_ref[...]
    xb = x.astype(BF16)
    acc = jnp.zeros((TM, D_MODEL), F32)
    for c in range(FFN_HIDDEN // FFN_CHUNK):
        cols = slice(c * FFN_CHUNK, (c + 1) * FFN_CHUNK)
        gate = jnp.dot(xb, wg_ref[:, cols], preferred_element_type=F32)
        up = jnp.dot(xb, wu_ref[:, cols], preferred_element_type=F32)
        h = gate / (1.0 + jnp.exp(-gate)) * up
        acc = acc + jnp.dot(h.astype(BF16), wd_ref[cols, :], preferred_element_type=F32)
    y_ref[...] = _layer_norm(alpha * x + acc, g_ref[...], b_ref[...])


def _ffn(alpha, x2, wg, wu, wd, g, bb):
    m = x2.shape[0]
    tok = pl.BlockSpec((TM, D_MODEL), lambda i: (i, 0))
    full = lambda a: pl.BlockSpec(a.shape, lambda i: (0, 0))
    return pl.pallas_call(
        functools.partial(_ffn_kernel, alpha),
        grid=(m // TM,),
        in_specs=[tok, full(wg), full(wu), full(wd), full(g), full(bb)],
        out_specs=tok,
        out_shape=jax.ShapeDtypeStruct((m, D_MODEL), F32),
        compiler_params=_params(("parallel",)),
        name="swiglu_norm",
    )(x2, wg, wu, wd, g, bb)


def _split_w_in(w):
    a, bw, hi = A_WIDTH, B_WIDTH, IDX_HEADS * IDX_DIM
    o = 0
    qa, o = w[:, o:o + a], o + a
    ka, o = w[:, o:o + a], o + a
    va, o = w[:, o:o + a], o + a
    qb, o = w[:, o:o + bw], o + bw
    kb, o = w[:, o:o + bw], o + bw
    vb, o = w[:, o:o + bw], o + bw
    qi, o = w[:, o:o + hi], o + hi
    ki, o = w[:, o:o + IDX_DIM], o + IDX_DIM
    wi, o = w[:, o:o + IDX_HEADS], o + IDX_HEADS
    gate = w[:, o:]
    w_nat = jnp.concatenate([ka, qb, kb, vb, ki, ki, gate], axis=1).astype(BF16)
    pad = jnp.zeros((w.shape[0], _T_ROWS - _T_WI - IDX_HEADS), w.dtype)
    w_t = jnp.concatenate([qa, va, qi, wi, pad], axis=1).T.astype(BF16)
    return w_nat, w_t


def kernel(x, positions, w_in, b_gate, w_branch_a, w_branch_b, w_out, ln1_g, ln1_b,
           w_ffn_gate, w_ffn_up, w_ffn_down, ln2_g, ln2_b):
    b, s, d = x.shape
    depth = w_in.shape[0]
    assert d == D_MODEL and s % (max(B_DILATIONS) * TB) == 0 and s % TM == 0
    alpha = (2 * depth) ** 0.25
    cos, sin, cos_t, sin_t = _rope_tables(positions)
    row = lambda v: v.reshape(1, -1)
    for layer in range(depth):
        w_nat, w_t = _split_w_in(w_in[layer])
        ka6, ki4, gate, *streams = _proj_nat(x, cos, sin, w_nat)
        qt, vt5, qi4t, wt = _proj_t(x, cos_t, sin_t, w_t)
        oa = _dsa_attention(qt, qi4t, wt, ka6, vt5, ki4)
        groups = [_window_attention(*streams[3 * g:3 * g + 3]) for g in range(len(B_DILATIONS))]
        x1 = _merge(alpha, oa, [o for o, _ in groups], [l for _, l in groups], gate, x, row(b_gate[layer]),
                    w_branch_a[layer].astype(BF16), w_branch_b[layer].astype(BF16),
                    w_out[layer].astype(BF16), row(ln1_g[layer]), row(ln1_b[layer]))
        x2 = _ffn(alpha, x1.reshape(b * s, d), w_ffn_gate[layer].astype(BF16), w_ffn_up[layer].astype(BF16),
                  w_ffn_down[layer].astype(BF16), row(ln2_g[layer]), row(ln2_b[layer]))
        x = x2.reshape(b, s, d)
    return x
```

```python
import functools

import jax
import jax.numpy as jnp
from jax import lax
from jax.experimental import pallas as pl
from jax.experimental.pallas import tpu as pltpu

F32 = jnp.float32
BF16 = jnp.bfloat16

D_MODEL = 1024
HEAD_DIM = 64
ROT_HALF = 8
ROPE_THETA = 500000.0
ATTN_SCALE = HEAD_DIM ** -0.5
LOG2E = 1.4426950408889634
V_ROWS = HEAD_DIM + 16
A_HEADS = 12
A_WIDTH = A_HEADS * HEAD_DIM
IDX_HEADS = 8
IDX_DIM = 64
IDX_SCALE = (IDX_HEADS ** -0.5) * (IDX_DIM ** -0.5)
TOPK_MAX = 256
B_DILATIONS = (1, 4, 16)
B_WINDOW_STEPS = 128
B_HEADS_PER_GROUP = 4
B_WIDTH = 3 * B_HEADS_PER_GROUP * HEAD_DIM
B_OUT_WIDTH = B_HEADS_PER_GROUP * HEAD_DIM
FFN_HIDDEN = 2816
LN_EPS = 1e-5
NEG = -1e30

LANES = 128
VMEM_LIMIT = 56 * 1024 * 1024
TM = 512
TQ = 256
TK = 256
TB = 256
FFN_CHUNK = 256
BISECT_STEPS_PER_CHECK = 2
ROPE_TILE = 2048

_N_KA, _N_QB, _N_KB, _N_VB = 0, 768, 1536, 2304
_N_KI, _N_NAT = 3072, 3200
_T_QA, _T_VA, _T_QI, _T_WI, _T_ROWS = 0, 768, 1536, 2048, 2064


def _params(sem):
    return pltpu.CompilerParams(dimension_semantics=sem, vmem_limit_bytes=VMEM_LIMIT)


def _rope_kernel(pc_ref, pr_ref, fr_ref, fc_ref, c_ref, s_ref, ct_ref, st_ref):
    ang = pc_ref[...].astype(F32) * fr_ref[...]
    d = lax.broadcasted_iota(jnp.int32, ang.shape, 1) & (HEAD_DIM - 1)
    cos, sin = jnp.cos(ang), jnp.sin(ang)
    c_ref[...] = jnp.where(d < 2 * ROT_HALF, cos, 1.0)
    s_ref[...] = jnp.where(d < ROT_HALF, -sin, jnp.where(d < 2 * ROT_HALF, sin, 0.0))
    ang_t = fc_ref[...] * pr_ref[...].astype(F32)
    ct_ref[...] = jnp.cos(ang_t)
    st_ref[...] = jnp.sin(ang_t)


def _rope_tables(positions):
    m = positions.size
    inv_freq = ROPE_THETA ** (-jnp.arange(0, 2 * ROT_HALF, 2, dtype=F32) / (2 * ROT_HALF))
    f_row = jnp.tile(inv_freq, LANES // ROT_HALF).reshape(1, LANES)
    f_col = inv_freq.reshape(ROT_HALF, 1)
    t = min(ROPE_TILE, m)
    return pl.pallas_call(
        _rope_kernel,
        grid=(m // t,),
        in_specs=[pl.BlockSpec((t, 1), lambda i: (i, 0)), pl.BlockSpec((1, t), lambda i: (0, i)),
                  pl.BlockSpec((1, LANES), lambda i: (0, 0)), pl.BlockSpec((ROT_HALF, 1), lambda i: (0, 0))],
        out_specs=[pl.BlockSpec((t, LANES), lambda i: (i, 0)), pl.BlockSpec((t, LANES), lambda i: (i, 0)),
                   pl.BlockSpec((ROT_HALF, t), lambda i: (0, i)), pl.BlockSpec((ROT_HALF, t), lambda i: (0, i))],
        out_shape=[jax.ShapeDtypeStruct((m, LANES), F32), jax.ShapeDtypeStruct((m, LANES), F32),
                   jax.ShapeDtypeStruct((ROT_HALF, m), F32), jax.ShapeDtypeStruct((ROT_HALF, m), F32)],
        compiler_params=_params(("parallel",)),
        name="rope_tables",
    )(positions.reshape(m, 1), positions.reshape(1, m), f_row, f_col)


def _proj_nat_kernel(x_ref, c_ref, s_ref, w_ref, ka_ref, ki_ref, *rest):
    streams, y_ref = rest[:-1], rest[-1]
    xb = x_ref[0].astype(BF16)
    cos, sin = c_ref[...], s_ref[...]
    lane = lax.broadcasted_iota(jnp.int32, cos.shape, 1)
    first = (lane & (HEAD_DIM - 1)) < ROT_HALF

    def rope(y):
        partner = jnp.where(first, pltpu.roll(y, LANES - ROT_HALF, 1), pltpu.roll(y, ROT_HALF, 1))
        return y * cos + partner * sin

    def proj(lo, hi):
        return jnp.dot(xb, w_ref[:, lo:hi], preferred_element_type=F32)

    blocks = A_WIDTH // LANES
    y = proj(_N_KA, _N_KA + A_WIDTH)
    for p in range(blocks):
        ka_ref[0, p] = rope(y[:, p * LANES:(p + 1) * LANES]).astype(BF16)

    def scatter_streams(which):
        per_group = B_OUT_WIDTH // LANES
        for g, d in enumerate(B_DILATIONS):
            out = streams[3 * g + which]
            for r in range(d):
                for p in range(per_group):
                    rows = y_ref[g * per_group + p, pl.ds(r, TM // d, stride=d), :]
                    out[0, r, :, p * LANES:(p + 1) * LANES] = rows.astype(BF16)

    y = proj(_N_QB, _N_QB + B_WIDTH)
    for p in range(blocks):
        y_ref[p] = rope(y[:, p * LANES:(p + 1) * LANES]) * ATTN_SCALE
    scatter_streams(0)
    y = proj(_N_KB, _N_KB + B_WIDTH)
    for p in range(blocks):
        y_ref[p] = rope(y[:, p * LANES:(p + 1) * LANES])
    scatter_streams(1)
    y = proj(_N_VB, _N_VB + B_WIDTH)
    for p in range(blocks):
        y_ref[p] = y[:, p * LANES:(p + 1) * LANES]
    scatter_streams(2)
    r = rope(proj(_N_KI, _N_KI + LANES))
    hi = r.astype(BF16).astype(F32)
    hl = jnp.where(lane < IDX_DIM, hi, r - hi).astype(BF16)
    ki_ref[0, :, 0:LANES] = hl
    ki_ref[0, :, LANES:2 * LANES] = hl


def _proj_nat(x, cos, sin, w_nat):
    b, s, d = x.shape
    nt = s // TM
    tok = lambda width: pl.BlockSpec((1, TM, width), lambda i, j: (i, j, 0))
    stream_specs, stream_shapes = [], []
    for dil in B_DILATIONS:
        for _ in range(3):
            stream_specs.append(pl.BlockSpec((1, dil, TM // dil, B_OUT_WIDTH), lambda i, j: (i, 0, j, 0)))
            stream_shapes.append(jax.ShapeDtypeStruct((b, dil, s // dil, B_OUT_WIDTH), BF16))
    return pl.pallas_call(
        _proj_nat_kernel,
        grid=(b, nt),
        in_specs=[tok(d),
                  pl.BlockSpec((TM, LANES), lambda i, j: (i * nt + j, 0)),
                  pl.BlockSpec((TM, LANES), lambda i, j: (i * nt + j, 0)),
                  pl.BlockSpec((d, _N_NAT), lambda i, j: (0, 0))],
        out_specs=[pl.BlockSpec((1, A_WIDTH // LANES, TM, LANES), lambda i, j: (i, 0, j, 0)),
                   tok(2 * LANES)] + stream_specs,
        out_shape=[jax.ShapeDtypeStruct((b, A_WIDTH // LANES, s, LANES), BF16),
                   jax.ShapeDtypeStruct((b, s, 2 * LANES), BF16)] + stream_shapes,
        scratch_shapes=[pltpu.VMEM((B_WIDTH // LANES, TM, LANES), F32)],
        compiler_params=_params(("parallel", "parallel")),
        name="proj_token_major",
    )(x, cos, sin, w_nat)


def _proj_t_kernel(x_ref, ct_ref, st_ref, w_ref, qt_ref, vt_ref, qi_ref, wt_ref):
    xb = x_ref[0].astype(BF16)
    cos, sin = ct_ref[...], st_ref[...]

    def proj(lo, hi):
        return lax.dot_general(w_ref[lo:hi, :], xb, (((1,), (1,)), ((), ())), preferred_element_type=F32)

    def rope_head(y):
        x1, x2 = y[0:ROT_HALF], y[ROT_HALF:2 * ROT_HALF]
        return jnp.concatenate([x1 * cos - x2 * sin, x2 * cos + x1 * sin, y[2 * ROT_HALF:]], axis=0)

    y = proj(_T_QA, _T_QA + A_WIDTH)
    for h in range(A_HEADS):
        r = rope_head(y[h * HEAD_DIM:(h + 1) * HEAD_DIM])
        qt_ref[0, h * HEAD_DIM:(h + 1) * HEAD_DIM, :] = (r * (ATTN_SCALE * LOG2E)).astype(BF16)
    y = proj(_T_VA, _T_VA + A_WIDTH)
    for h in range(A_HEADS):
        for c in range(TM // TK):
            vt_ref[0, h, c, 0:HEAD_DIM, :] = y[h * HEAD_DIM:(h + 1) * HEAD_DIM, c * TK:(c + 1) * TK].astype(BF16)
            vt_ref[0, h, c, HEAD_DIM:V_ROWS, :] = jnp.ones((V_ROWS - HEAD_DIM, TK), BF16)
    y = proj(_T_QI, _T_QI + IDX_HEADS * IDX_DIM)
    for h in range(IDX_HEADS):
        r = rope_head(y[h * IDX_DIM:(h + 1) * IDX_DIM])
        hi = r.astype(BF16)
        lo = (r - hi.astype(F32)).astype(BF16)
        base = 4 * h * IDX_DIM
        qi_ref[0, base:base + IDX_DIM, :] = hi
        qi_ref[0, base + IDX_DIM:base + 2 * IDX_DIM, :] = hi
        qi_ref[0, base + 2 * IDX_DIM:base + 3 * IDX_DIM, :] = lo
        qi_ref[0, base + 3 * IDX_DIM:base + 4 * IDX_DIM, :] = lo
    wt_ref[0] = proj(_T_WI, _T_ROWS)[0:IDX_HEADS] * IDX_SCALE


def _proj_t(x, cos_t, sin_t, w_t):
    b, s, d = x.shape
    nt = s // TM
    return pl.pallas_call(
        _proj_t_kernel,
        grid=(b, nt),
        in_specs=[pl.BlockSpec((1, TM, d), lambda i, j: (i, j, 0)),
                  pl.BlockSpec((ROT_HALF, TM), lambda i, j: (0, i * nt + j)),
                  pl.BlockSpec((ROT_HALF, TM), lambda i, j: (0, i * nt + j)),
                  pl.BlockSpec((_T_ROWS, d), lambda i, j: (0, 0))],
        out_specs=[pl.BlockSpec((1, A_WIDTH, TM), lambda i, j: (i, 0, j)),
                   pl.BlockSpec((1, A_HEADS, TM // TK, V_ROWS, TK), lambda i, j: (i, 0, j, 0, 0)),
                   pl.BlockSpec((1, 4 * IDX_HEADS * IDX_DIM, TM), lambda i, j: (i, 0, j)),
                   pl.BlockSpec((1, IDX_HEADS, TM), lambda i, j: (i, 0, j))],
        out_shape=[jax.ShapeDtypeStruct((b, A_WIDTH, s), BF16),
                   jax.ShapeDtypeStruct((b, A_HEADS, s // TK, V_ROWS, TK), BF16),
                   jax.ShapeDtypeStruct((b, 4 * IDX_HEADS * IDX_DIM, s), BF16),
                   jax.ShapeDtypeStruct((b, IDX_HEADS, s), F32)],
        compiler_params=_params(("parallel", "parallel")),
        name="proj_feature_major",
    )(x, cos_t, sin_t, w_t)


def _dsa_kernel(qt_ref, qi_ref, wt_ref, k_ref, vt_ref, ki_ref, o_ref,
                sc_ref, qpad_ref, ot_ref, m_ref, st_ref, sa_ref, sb_ref):
    j = pl.program_id(1)
    nkt = j + 1

    for h in range(A_HEADS):
        off = (h % 2) * HEAD_DIM
        qpad_ref[h] = jnp.zeros((2 * HEAD_DIM, TQ), BF16)
        qpad_ref[h, off:off + HEAD_DIM, :] = qt_ref[0, h * HEAD_DIM:(h + 1) * HEAD_DIM, :]

    qpos = j * TQ + lax.broadcasted_iota(jnp.int32, (1, TQ), 1)
    row_iota = lax.broadcasted_iota(jnp.int32, (TK, TQ), 0)
    w = wt_ref[0]

    def score_tile(kt, carry):
        mn, mx = carry
        ki = ki_ref[0, pl.ds(pl.multiple_of(kt * TK, TK), TK), :]
        acc = jnp.zeros((TK, TQ), F32)
        for h in range(IDX_HEADS):
            s = jnp.dot(ki, qi_ref[0, 4 * h * IDX_DIM:4 * (h + 1) * IDX_DIM, :], preferred_element_type=F32)
            acc = acc + w[h:h + 1, :] * jnp.maximum(s, 0.0)
        causal = (kt * TK + row_iota) <= qpos
        sc_ref[kt] = jnp.where(causal, acc, -jnp.inf)
        lo_t = jnp.where(causal, acc, jnp.inf).reshape(TK // 8, 8, TQ).min(axis=0)
        hi_t = jnp.where(causal, acc, -jnp.inf).reshape(TK // 8, 8, TQ).max(axis=0)
        return jnp.minimum(mn, lo_t), jnp.maximum(mx, hi_t)

    mn8, mx8 = lax.fori_loop(0, nkt, score_tile,
                             (jnp.full((8, TQ), jnp.inf, F32), jnp.full((8, TQ), -jnp.inf, F32)))
    mn = mn8.min(axis=0, keepdims=True)
    mx = mx8.max(axis=0, keepdims=True)

    sc_ref[nkt] = jnp.full((TK, TQ), -jnp.inf, F32)

    def count(preds):
        def part(p, kt):
            return p(sc_ref[kt]).reshape(TK // 8, 8, TQ).sum(axis=0)

        def body(i, cnts):
            return tuple(c + part(p, 2 * i) + part(p, 2 * i + 1) for c, p in zip(cnts, preds))

        cnts = lax.fori_loop(0, (nkt + 1) // 2, body, tuple(jnp.zeros((8, TQ), F32) for _ in preds))
        return [c.sum(axis=0, keepdims=True) for c in cnts]

    def ge(cand):
        return lambda t: jnp.where(t >= cand, 1.0, 0.0)

    n_causal = (qpos + 1).astype(F32)
    k_q = jnp.minimum(n_causal, float(TOPK_MAX))
    c_mx, c_pos, c_nn = count([ge(mx), lambda t: jnp.where(t > 0.0, 1.0, 0.0), ge(0.0)])
    select = n_causal > k_q
    at_max = select & (c_mx >= k_q)
    search = select & (c_mx < k_q)
    at_zero = search & (c_pos < k_q) & (c_nn >= k_q)
    above = search & (c_pos >= k_q)
    below = search & (c_nn < k_q)
    lo0 = jnp.where(at_max, mx, jnp.where(at_zero | above, 0.0, mn))
    c_lo0 = jnp.where(at_max, c_mx, jnp.where(at_zero | above, c_nn, n_causal))
    st_ref[0:1, :] = lo0
    st_ref[1:2, :] = jnp.where(below, 0.0, mx)
    st_ref[2:3, :] = c_lo0
    st_ref[3:4, :] = jnp.where(at_max, 0.0, jnp.where(at_zero, c_pos, jnp.where(below, c_nn, c_mx)))
    act0 = jnp.where((above | below) & (c_lo0 > k_q), 1.0, 0.0)
    st_ref[4:5, :] = act0

    def bisect_once():
        lo, hi, c_lo, c_hi = st_ref[0:1, :], st_ref[1:2, :], st_ref[2:3, :], st_ref[3:4, :]
        act = st_ref[4:5, :] > 0.0
        mid = 0.5 * lo + 0.5 * hi
        live = act & (mid > lo) & (mid < hi)
        c, = count([ge(mid)])
        up = live & (c >= k_q)
        dn = live & (c < k_q)
        c_lo = jnp.where(up, c, c_lo)
        st_ref[0:1, :] = jnp.where(up, mid, lo)
        st_ref[1:2, :] = jnp.where(dn, mid, hi)
        st_ref[2:3, :] = c_lo
        st_ref[3:4, :] = jnp.where(dn, c, c_hi)
        act_new = jnp.where(live & (c_lo > k_q), 1.0, 0.0)
        st_ref[4:5, :] = act_new
        return act_new

    def bisect(go):
        for _ in range(BISECT_STEPS_PER_CHECK - 1):
            bisect_once()
        return (jnp.max(bisect_once()) > 0.0).astype(jnp.int32)

    lax.while_loop(lambda go: go > 0, bisect, (jnp.max(act0) > 0.0).astype(jnp.int32))

    lo = st_ref[0:1, :]
    tie = st_ref[2:3, :] > k_q

    @pl.when(jnp.max(jnp.where(tie, 1.0, 0.0)) > 0.0)
    def _():
        need = jnp.where(tie, k_q - st_ref[3:4, :], jnp.inf)
        tri = jnp.where(lax.broadcasted_iota(jnp.int32, (TK, TK), 0) >= lax.broadcasted_iota(jnp.int32, (TK, TK), 1),
                        1.0, 0.0).astype(BF16)

        def drop(kt, seen):
            t = sc_ref[kt]
            eq = jnp.where(t == lo, 1.0, 0.0).astype(BF16)
            rank = jnp.dot(tri, eq, preferred_element_type=F32) + seen
            sc_ref[kt] = jnp.where(t == lo, jnp.where(rank > need, -jnp.inf, t), t)
            return rank[TK - 1:TK, :]

        lax.fori_loop(0, nkt, drop, jnp.zeros((1, TQ), F32))

    def to_bias(kt, _):
        sc_ref[kt] = jnp.where(sc_ref[kt] >= lo, 0.0, NEG)
        return 0

    lax.fori_loop(0, nkt, to_bias, 0)

    m_ref[...] = jnp.full(m_ref.shape, NEG, F32)
    ot_ref[...] = jnp.zeros(ot_ref.shape, F32)

    def logits(h, kt, bias):
        rows = pl.ds(pl.multiple_of(kt * TK, TK), TK)
        return jnp.dot(k_ref[0, h // 2, rows, :], qpad_ref[h], preferred_element_type=F32) + bias

    def half_step(kt, cur_ref, nxt_ref):
        kn = jnp.minimum(kt + 1, nkt - 1)
        bias_n = sc_ref[kn]
        for h in range(A_HEADS):
            s = cur_ref[h]
            nxt_ref[h] = logits(h, kn, bias_n)
            m = m_ref[h]
            m_new = jnp.maximum(m, s.reshape(TK // 8, 8, TQ).max(axis=0).max(axis=0, keepdims=True))
            p = jnp.exp2(s - m_new)
            corr = jnp.exp2(m - m_new)
            m_ref[h] = m_new
            ot_ref[h] = ot_ref[h] * corr + jnp.dot(vt_ref[0, h, kt], p.astype(BF16), preferred_element_type=F32)

    bias0 = sc_ref[0]
    for h in range(A_HEADS):
        sa_ref[h] = logits(h, 0, bias0)

    def kv_pair(i, _):
        half_step(2 * i, sa_ref, sb_ref)

        @pl.when(2 * i + 1 < nkt)
        def _():
            half_step(2 * i + 1, sb_ref, sa_ref)

        return 0

    lax.fori_loop(0, (nkt + 1) // 2, kv_pair, 0)

    for p in range(A_HEADS // 2):
        both = jnp.concatenate([ot_ref[h, 0:HEAD_DIM, :] / ot_ref[h, HEAD_DIM:HEAD_DIM + 1, :]
                                for h in (2 * p, 2 * p + 1)], axis=0)
        o_ref[0, :, p * LANES:(p + 1) * LANES] = both.T.astype(BF16)


def _dsa_attention(qt, qi4t, wt, k6, vt5, ki4):
    b, _, s = qt.shape
    nq = s // TQ
    return pl.pallas_call(
        _dsa_kernel,
        grid=(b, nq),
        in_specs=[pl.BlockSpec((1, A_WIDTH, TQ), lambda i, j: (i, 0, j)),
                  pl.BlockSpec((1, 4 * IDX_HEADS * IDX_DIM, TQ), lambda i, j: (i, 0, j)),
                  pl.BlockSpec((1, IDX_HEADS, TQ), lambda i, j: (i, 0, j)),
                  pl.BlockSpec((1, A_WIDTH // LANES, s, LANES), lambda i, j: (i, 0, 0, 0)),
                  pl.BlockSpec((1, A_HEADS, s // TK, V_ROWS, TK), lambda i, j: (i, 0, 0, 0, 0)),
                  pl.BlockSpec((1, s, 4 * IDX_DIM), lambda i, j: (i, 0, 0))],
        out_specs=pl.BlockSpec((1, TQ, A_WIDTH), lambda i, j: (i, j, 0)),
        out_shape=jax.ShapeDtypeStruct((b, s, A_WIDTH), BF16),
        scratch_shapes=[pltpu.VMEM((s // TK + 1, TK, TQ), F32),
                        pltpu.VMEM((A_HEADS, 2 * HEAD_DIM, TQ), BF16),
                        pltpu.VMEM((A_HEADS, V_ROWS, TQ), F32),
                        pltpu.VMEM((A_HEADS, 1, TQ), F32),
                        pltpu.VMEM((8, TQ), F32),
                        pltpu.VMEM((A_HEADS, TK, TQ), F32),
                        pltpu.VMEM((A_HEADS, TK, TQ), F32)],
        compiler_params=_params(("parallel", "arbitrary")),
        name="dsa_attention",
    )(qt, qi4t, wt, k6, vt5, ki4)


def _window_kernel(q_ref, kp_ref, kc_ref, vp_ref, vc_ref, o_ref, lse_ref):
    i = pl.program_id(2)
    hb = B_WINDOW_STEPS
    r = lax.broadcasted_iota(jnp.int32, (hb, 2 * hb), 0)
    c = lax.broadcasted_iota(jnp.int32, (hb, 2 * hb), 1)
    dist = r + hb - c
    band = jnp.where(dist >= 0, jnp.where(dist <= B_WINDOW_STEPS, 0.0, NEG), NEG)
    first_col = jnp.where(i > 0, 0, hb)
    biases = (jnp.where(c >= first_col, band, NEG), band)
    lane = lax.broadcasted_iota(jnp.int32, (hb, LANES), 1)
    left = lane < HEAD_DIM
    for p in range(B_HEADS_PER_GROUP // 2):
        cols = slice(p * LANES, (p + 1) * LANES)
        q2, kc, vc = q_ref[0, 0, :, cols], kc_ref[0, 0, :, cols], vc_ref[0, 0, :, cols]
        windows = ((jnp.concatenate([kp_ref[0, 0, :, cols], kc[:hb]], axis=0),
                    jnp.concatenate([vp_ref[0, 0, :, cols], vc[:hb]], axis=0)), (kc, vc))
        for half, ((kw, vw), bias) in enumerate(zip(windows, biases)):
            rows = slice(half * hb, (half + 1) * hb)
            outs, lses = [], []
            for side in (left, ~left):
                qh = jnp.where(side, q2[rows], jnp.zeros((hb, LANES), BF16))
                s = lax.dot_general(qh, kw, (((1,), (1,)), ((), ())), preferred_element_type=F32) + bias
                m = s.max(axis=-1, keepdims=True)
                e = jnp.exp(s - m)
                l = e.sum(axis=-1, keepdims=True)
                outs.append(jnp.dot(e.astype(BF16), vw, preferred_element_type=F32) / l)
                lses.append(jnp.broadcast_to(m + jnp.log(l), (hb, LANES)))
            o_ref[0, 0, rows, cols] = jnp.where(left, outs[0], outs[1])
            lse_ref[0, 0, rows, cols] = jnp.where(left, lses[0], lses[1])


def _window_attention(q, k, v):
    b, d, n, _ = q.shape
    assert TB == 2 * B_WINDOW_STEPS
    cur = pl.BlockSpec((1, 1, TB, B_OUT_WIDTH), lambda bi, ri, ti: (bi, ri, ti, 0))
    prev = pl.BlockSpec((1, 1, TB // 2, B_OUT_WIDTH), lambda bi, ri, ti: (bi, ri, jnp.maximum(2 * ti - 1, 0), 0))
    return pl.pallas_call(
        _window_kernel,
        grid=(b, d, n // TB),
        in_specs=[cur, prev, cur, prev, cur],
        out_specs=[cur, cur],
        out_shape=[jax.ShapeDtypeStruct((b, d, n, B_OUT_WIDTH), F32)] * 2,
        compiler_params=_params(("parallel", "parallel", "arbitrary")),
        name=f"window_attention_d{d}",
    )(q, k, k, v, v)


def _layer_norm(y, g, b):
    mu = y.mean(axis=-1, keepdims=True)
    yc = y - mu
    var = (yc * yc).mean(axis=-1, keepdims=True)
    return yc * lax.rsqrt(var + LN_EPS) * g + b


def _merge_kernel(alpha, oa_ref, o0_ref, o1_ref, o2_ref, l0_ref, l1_ref, l2_ref, x_ref,
                  wg_ref, bg_ref, wa_ref, wb_ref, wo_ref, g_ref, b_ref, y_ref, tok_ref):
    def token_order(ref):
        d = ref.shape[1]
        if d == 1:
            return ref[0, 0]
        for r in range(d):
            for p in range(B_OUT_WIDTH // LANES):
                tok_ref[p, pl.ds(r, TM // d, stride=d), :] = ref[0, r, :, p * LANES:(p + 1) * LANES]
        return jnp.concatenate([tok_ref[p] for p in range(B_OUT_WIDTH // LANES)], axis=1)

    l0, l1, l2 = token_order(l0_ref), token_order(l1_ref), token_order(l2_ref)
    lm = jnp.maximum(jnp.maximum(l0, l1), l2)
    e0, e1, e2 = jnp.exp(l0 - lm), jnp.exp(l1 - lm), jnp.exp(l2 - lm)
    ob = (e0 * token_order(o0_ref) + e1 * token_order(o1_ref) + e2 * token_order(o2_ref)) / (e0 + e1 + e2)
    pa = jnp.dot(oa_ref[0], wa_ref[...], preferred_element_type=F32)
    pb = jnp.dot(ob.astype(BF16), wb_ref[...], preferred_element_type=F32)
    x = x_ref[0]
    xb = x.astype(BF16)

    def gate(lo, hi):
        z = jnp.dot(xb, wg_ref[:, lo:hi], preferred_element_type=F32) + bg_ref[:, lo:hi]
        return 1.0 / (1.0 + jnp.exp(-z))

    merged = gate(0, D_MODEL) * pa + gate(D_MODEL, 2 * D_MODEL) * pb
    mixed = jnp.dot(merged.astype(BF16), wo_ref[...], preferred_element_type=F32)
    y_ref[0] = _layer_norm(alpha * x + mixed, g_ref[...], b_ref[...])


def _merge(alpha, oa, obs, lses, x, w_gate, b_gate, wa, wb, wo, g, bb):
    b, s, _ = x.shape
    tok = lambda width: pl.BlockSpec((1, TM, width), lambda i, j: (i, j, 0))
    full = lambda a: pl.BlockSpec(a.shape, lambda i, j: (0, 0))
    streams = [pl.BlockSpec((1, a.shape[1], TM // a.shape[1], B_OUT_WIDTH), lambda i, j: (i, 0, j, 0))
               for a in list(obs) + list(lses)]
    return pl.pallas_call(
        functools.partial(_merge_kernel, alpha),
        grid=(b, s // TM),
        in_specs=[tok(A_WIDTH)] + streams + [tok(D_MODEL), full(w_gate),
                  full(b_gate), full(wa), full(wb), full(wo), full(g), full(bb)],
        out_specs=tok(D_MODEL),
        out_shape=jax.ShapeDtypeStruct((b, s, D_MODEL), F32),
        scratch_shapes=[pltpu.VMEM((B_OUT_WIDTH // LANES, TM, LANES), F32)],
        compiler_params=_params(("parallel", "parallel")),
        name="merge_outproj_norm",
    )(oa, *obs, *lses, x, w_gate, b_gate, wa, wb, wo, g, bb)


def _ffn_kernel(alpha, x_ref, wg_ref, wu_ref, wd_ref, g_ref, b_ref, y_ref):
    x = x_ref[...]
    xb = x.astype(BF16)
    acc = jnp.zeros((TM, D_MODEL), F32)
    for c in range(FFN_HIDDEN // FFN_CHUNK):
        cols = slice(c * FFN_CHUNK, (c + 1) * FFN_CHUNK)
        gate = jnp.dot(xb, wg_ref[:, cols], preferred_element_type=F32)
        up = jnp.dot(xb, wu_ref[:, cols], preferred_element_type=F32)
        h = gate / (1.0 + jnp.exp(-gate)) * up
        acc = acc + jnp.dot(h.astype(BF16), wd_ref[cols, :], preferred_element_type=F32)
    y_ref[...] = _layer_norm(alpha * x + acc, g_ref[...], b_ref[...])


def _ffn(alpha, x2, wg, wu, wd, g, bb):
    m = x2.shape[0]
    tok = pl.BlockSpec((TM, D_MODEL), lambda i: (i, 0))
    full = lambda a: pl.BlockSpec(a.shape, lambda i: (0, 0))
    return pl.pallas_call(
        functools.partial(_ffn_kernel, alpha),
        grid=(m // TM,),
        in_specs=[tok, full(wg), full(wu), full(wd), full(g), full(bb)],
        out_specs=tok,
        out_shape=jax.ShapeDtypeStruct((m, D_MODEL), F32),
        compiler_params=_params(("parallel",)),
        name="swiglu_norm",
    )(x2, wg, wu, wd, g, bb)


def _split_w_in(w):
    a, bw, hi = A_WIDTH, B_WIDTH, IDX_HEADS * IDX_DIM
    o = 0
    qa, o = w[:, o:o + a], o + a
    ka, o = w[:, o:o + a], o + a
    va, o = w[:, o:o + a], o + a
    qb, o = w[:, o:o + bw], o + bw
    kb, o = w[:, o:o + bw], o + bw
    vb, o = w[:, o:o + bw], o + bw
    qi, o = w[:, o:o + hi], o + hi
    ki, o = w[:, o:o + IDX_DIM], o + IDX_DIM
    wi, o = w[:, o:o + IDX_HEADS], o + IDX_HEADS
    w_gate = w[:, o:].astype(BF16)
    w_nat = jnp.concatenate([ka, qb, kb, vb, ki, ki], axis=1).astype(BF16)
    pad = jnp.zeros((w.shape[0], _T_ROWS - _T_WI - IDX_HEADS), w.dtype)
    w_t = jnp.concatenate([qa, va, qi, wi, pad], axis=1).T.astype(BF16)
    return w_nat, w_t, w_gate


def kernel(x, positions, w_in, b_gate, w_branch_a, w_branch_b, w_out, ln1_g, ln1_b,
           w_ffn_gate, w_ffn_up, w_ffn_down, ln2_g, ln2_b):
    b, s, d = x.shape
    depth = w_in.shape[0]
    assert d == D_MODEL and s % (max(B_DILATIONS) * TB) == 0 and s % TM == 0
    alpha = (2 * depth) ** 0.25
    cos, sin, cos_t, sin_t = _rope_tables(positions)
    row = lambda v: v.reshape(1, -1)
    for layer in range(depth):
        w_nat, w_t, w_gate = _split_w_in(w_in[layer])
        ka6, ki4, *streams = _proj_nat(x, cos, sin, w_nat)
        qt, vt5, qi4t, wt = _proj_t(x, cos_t, sin_t, w_t)
        oa = _dsa_attention(qt, qi4t, wt, ka6, vt5, ki4)
        groups = [_window_attention(*streams[3 * g:3 * g + 3]) for g in range(len(B_DILATIONS))]
        x1 = _merge(alpha, oa, [o for o, _ in groups], [l for _, l in groups], x, w_gate, row(b_gate[layer]),
                    w_branch_a[layer].astype(BF16), w_branch_b[layer].astype(BF16),
                    w_out[layer].astype(BF16), row(ln1_g[layer]), row(ln1_b[layer]))
        x2 = _ffn(alpha, x1.reshape(b * s, d), w_ffn_gate[layer].astype(BF16), w_ffn_up[layer].astype(BF16),
                  w_ffn_down[layer].astype(BF16), row(ln2_g[layer]), row(ln2_b[layer]))
        x = x2.reshape(b, s, d)
    return x
```

```python
import functools

import jax
import jax.numpy as jnp
from jax import lax
from jax.experimental import pallas as pl
from jax.experimental.pallas import tpu as pltpu

F32 = jnp.float32
BF16 = jnp.bfloat16

D_MODEL = 1024
HEAD_DIM = 64
ROT_HALF = 8
ROPE_THETA = 500000.0
ATTN_SCALE = HEAD_DIM ** -0.5
LOG2E = 1.4426950408889634
V_ROWS = HEAD_DIM + 16
A_HEADS = 12
A_WIDTH = A_HEADS * HEAD_DIM
IDX_HEADS = 8
IDX_DIM = 64
IDX_SCALE = (IDX_HEADS ** -0.5) * (IDX_DIM ** -0.5)
TOPK_MAX = 256
B_DILATIONS = (1, 4, 16)
B_WINDOW_STEPS = 128
B_HEADS_PER_GROUP = 4
B_WIDTH = 3 * B_HEADS_PER_GROUP * HEAD_DIM
B_OUT_WIDTH = B_HEADS_PER_GROUP * HEAD_DIM
FFN_HIDDEN = 2816
LN_EPS = 1e-5
NEG = -1e30
MASK_NEG = -(2.0 ** 100)

LANES = 128
VMEM_LIMIT = 56 * 1024 * 1024
TM = 512
TQ = 256
TK = 256
TB = 256
FFN_CHUNK = 256
BISECT_STEPS_PER_CHECK = 2
ROPE_TILE = 2048

_N_KA, _N_QB, _N_KB, _N_VB = 0, 768, 1536, 2304
_N_KI, _N_NAT = 3072, 3200
_T_QA, _T_VA, _T_QI, _T_WI, _T_ROWS = 0, 768, 1536, 2048, 2064


def _params(sem):
    return pltpu.CompilerParams(dimension_semantics=sem, vmem_limit_bytes=VMEM_LIMIT)


def _rope_kernel(pc_ref, pr_ref, fr_ref, fc_ref, c_ref, s_ref, ct_ref, st_ref):
    ang = pc_ref[...].astype(F32) * fr_ref[...]
    d = lax.broadcasted_iota(jnp.int32, ang.shape, 1) & (HEAD_DIM - 1)
    cos, sin = jnp.cos(ang), jnp.sin(ang)
    c_ref[...] = jnp.where(d < 2 * ROT_HALF, cos, 1.0)
    s_ref[...] = jnp.where(d < ROT_HALF, -sin, jnp.where(d < 2 * ROT_HALF, sin, 0.0))
    ang_t = fc_ref[...] * pr_ref[...].astype(F32)
    ct_ref[...] = jnp.cos(ang_t)
    st_ref[...] = jnp.sin(ang_t)


def _rope_tables(positions):
    m = positions.size
    inv_freq = ROPE_THETA ** (-jnp.arange(0, 2 * ROT_HALF, 2, dtype=F32) / (2 * ROT_HALF))
    f_row = jnp.tile(inv_freq, LANES // ROT_HALF).reshape(1, LANES)
    f_col = inv_freq.reshape(ROT_HALF, 1)
    t = min(ROPE_TILE, m)
    return pl.pallas_call(
        _rope_kernel,
        grid=(m // t,),
        in_specs=[pl.BlockSpec((t, 1), lambda i: (i, 0)), pl.BlockSpec((1, t), lambda i: (0, i)),
                  pl.BlockSpec((1, LANES), lambda i: (0, 0)), pl.BlockSpec((ROT_HALF, 1), lambda i: (0, 0))],
        out_specs=[pl.BlockSpec((t, LANES), lambda i: (i, 0)), pl.BlockSpec((t, LANES), lambda i: (i, 0)),
                   pl.BlockSpec((ROT_HALF, t), lambda i: (0, i)), pl.BlockSpec((ROT_HALF, t), lambda i: (0, i))],
        out_shape=[jax.ShapeDtypeStruct((m, LANES), F32), jax.ShapeDtypeStruct((m, LANES), F32),
                   jax.ShapeDtypeStruct((ROT_HALF, m), F32), jax.ShapeDtypeStruct((ROT_HALF, m), F32)],
        compiler_params=_params(("parallel",)),
        name="rope_tables",
    )(positions.reshape(m, 1), positions.reshape(1, m), f_row, f_col)


def _proj_nat_kernel(x_ref, c_ref, s_ref, w_ref, ka_ref, ki_ref, *rest):
    streams, y_ref = rest[:-1], rest[-1]
    xb = x_ref[0].astype(BF16)
    cos, sin = c_ref[...], s_ref[...]
    lane = lax.broadcasted_iota(jnp.int32, cos.shape, 1)
    first = (lane & (HEAD_DIM - 1)) < ROT_HALF

    def rope(y):
        partner = jnp.where(first, pltpu.roll(y, LANES - ROT_HALF, 1), pltpu.roll(y, ROT_HALF, 1))
        return y * cos + partner * sin

    def proj(lo, hi):
        return jnp.dot(xb, w_ref[:, lo:hi], preferred_element_type=F32)

    blocks = A_WIDTH // LANES
    y = proj(_N_KA, _N_KA + A_WIDTH)
    for p in range(blocks):
        ka_ref[0, p] = rope(y[:, p * LANES:(p + 1) * LANES]).astype(BF16)

    def scatter_streams(which):
        per_group = B_OUT_WIDTH // LANES
        for g, d in enumerate(B_DILATIONS):
            out = streams[3 * g + which]
            for r in range(d):
                for p in range(per_group):
                    rows = y_ref[g * per_group + p, pl.ds(r, TM // d, stride=d), :]
                    out[0, r, :, p * LANES:(p + 1) * LANES] = rows.astype(BF16)

    y = proj(_N_QB, _N_QB + B_WIDTH)
    for p in range(blocks):
        y_ref[p] = rope(y[:, p * LANES:(p + 1) * LANES]) * ATTN_SCALE
    scatter_streams(0)
    y = proj(_N_KB, _N_KB + B_WIDTH)
    for p in range(blocks):
        y_ref[p] = rope(y[:, p * LANES:(p + 1) * LANES])
    scatter_streams(1)
    y = proj(_N_VB, _N_VB + B_WIDTH)
    for p in range(blocks):
        y_ref[p] = y[:, p * LANES:(p + 1) * LANES]
    scatter_streams(2)
    r = rope(proj(_N_KI, _N_KI + LANES))
    hi = r.astype(BF16).astype(F32)
    hl = jnp.where(lane < IDX_DIM, hi, r - hi).astype(BF16)
    ki_ref[0, :, 0:LANES] = hl
    ki_ref[0, :, LANES:2 * LANES] = hl


def _proj_nat(x, cos, sin, w_nat):
    b, s, d = x.shape
    nt = s // TM
    tok = lambda width: pl.BlockSpec((1, TM, width), lambda i, j: (i, j, 0))
    stream_specs, stream_shapes = [], []
    for dil in B_DILATIONS:
        for _ in range(3):
            stream_specs.append(pl.BlockSpec((1, dil, TM // dil, B_OUT_WIDTH), lambda i, j: (i, 0, j, 0)))
            stream_shapes.append(jax.ShapeDtypeStruct((b, dil, s // dil, B_OUT_WIDTH), BF16))
    return pl.pallas_call(
        _proj_nat_kernel,
        grid=(b, nt),
        in_specs=[tok(d),
                  pl.BlockSpec((TM, LANES), lambda i, j: (i * nt + j, 0)),
                  pl.BlockSpec((TM, LANES), lambda i, j: (i * nt + j, 0)),
                  pl.BlockSpec((d, _N_NAT), lambda i, j: (0, 0))],
        out_specs=[pl.BlockSpec((1, A_WIDTH // LANES, TM, LANES), lambda i, j: (i, 0, j, 0)),
                   tok(2 * LANES)] + stream_specs,
        out_shape=[jax.ShapeDtypeStruct((b, A_WIDTH // LANES, s, LANES), BF16),
                   jax.ShapeDtypeStruct((b, s, 2 * LANES), BF16)] + stream_shapes,
        scratch_shapes=[pltpu.VMEM((B_WIDTH // LANES, TM, LANES), F32)],
        compiler_params=_params(("parallel", "parallel")),
        name="proj_token_major",
    )(x, cos, sin, w_nat)


def _proj_t_kernel(x_ref, ct_ref, st_ref, w_ref, qt_ref, vt_ref, qi_ref, wt_ref):
    xb = x_ref[0].astype(BF16)
    cos, sin = ct_ref[...], st_ref[...]

    def proj(lo, hi):
        return lax.dot_general(w_ref[lo:hi, :], xb, (((1,), (1,)), ((), ())), preferred_element_type=F32)

    def rope_head(y):
        x1, x2 = y[0:ROT_HALF], y[ROT_HALF:2 * ROT_HALF]
        return jnp.concatenate([x1 * cos - x2 * sin, x2 * cos + x1 * sin, y[2 * ROT_HALF:]], axis=0)

    y = proj(_T_QA, _T_QA + A_WIDTH)
    for h in range(A_HEADS):
        r = rope_head(y[h * HEAD_DIM:(h + 1) * HEAD_DIM])
        qt_ref[0, h * HEAD_DIM:(h + 1) * HEAD_DIM, :] = (r * (ATTN_SCALE * LOG2E)).astype(BF16)
    y = proj(_T_VA, _T_VA + A_WIDTH)
    for h in range(A_HEADS):
        for c in range(TM // TK):
            vt_ref[0, h, c, 0:HEAD_DIM, :] = y[h * HEAD_DIM:(h + 1) * HEAD_DIM, c * TK:(c + 1) * TK].astype(BF16)
            vt_ref[0, h, c, HEAD_DIM:V_ROWS, :] = jnp.ones((V_ROWS - HEAD_DIM, TK), BF16)
    y = proj(_T_QI, _T_QI + IDX_HEADS * IDX_DIM)
    for h in range(IDX_HEADS):
        r = rope_head(y[h * IDX_DIM:(h + 1) * IDX_DIM])
        hi = r.astype(BF16)
        lo = (r - hi.astype(F32)).astype(BF16)
        base = 4 * h * IDX_DIM
        qi_ref[0, base:base + IDX_DIM, :] = hi
        qi_ref[0, base + IDX_DIM:base + 2 * IDX_DIM, :] = hi
        qi_ref[0, base + 2 * IDX_DIM:base + 3 * IDX_DIM, :] = lo
        qi_ref[0, base + 3 * IDX_DIM:base + 4 * IDX_DIM, :] = lo
    wt_ref[0] = proj(_T_WI, _T_ROWS)[0:IDX_HEADS] * IDX_SCALE


def _proj_t(x, cos_t, sin_t, w_t):
    b, s, d = x.shape
    nt = s // TM
    return pl.pallas_call(
        _proj_t_kernel,
        grid=(b, nt),
        in_specs=[pl.BlockSpec((1, TM, d), lambda i, j: (i, j, 0)),
                  pl.BlockSpec((ROT_HALF, TM), lambda i, j: (0, i * nt + j)),
                  pl.BlockSpec((ROT_HALF, TM), lambda i, j: (0, i * nt + j)),
                  pl.BlockSpec((_T_ROWS, d), lambda i, j: (0, 0))],
        out_specs=[pl.BlockSpec((1, A_WIDTH, TM), lambda i, j: (i, 0, j)),
                   pl.BlockSpec((1, A_HEADS, TM // TK, V_ROWS, TK), lambda i, j: (i, 0, j, 0, 0)),
                   pl.BlockSpec((1, 4 * IDX_HEADS * IDX_DIM, TM), lambda i, j: (i, 0, j)),
                   pl.BlockSpec((1, IDX_HEADS, TM), lambda i, j: (i, 0, j))],
        out_shape=[jax.ShapeDtypeStruct((b, A_WIDTH, s), BF16),
                   jax.ShapeDtypeStruct((b, A_HEADS, s // TK, V_ROWS, TK), BF16),
                   jax.ShapeDtypeStruct((b, 4 * IDX_HEADS * IDX_DIM, s), BF16),
                   jax.ShapeDtypeStruct((b, IDX_HEADS, s), F32)],
        compiler_params=_params(("parallel", "parallel")),
        name="proj_feature_major",
    )(x, cos_t, sin_t, w_t)


def _dsa_kernel(qt_ref, qi_ref, wt_ref, k_ref, vt_ref, ki_ref, o_ref,
                sc_ref, qpad_ref, ot_ref, m_ref, st_ref, bias_ref, sa_ref, sb_ref):
    j = pl.program_id(1)
    nkt = j + 1

    for h in range(A_HEADS):
        off = (h % 2) * HEAD_DIM
        qpad_ref[h] = jnp.zeros((2 * HEAD_DIM, TQ), BF16)
        qpad_ref[h, off:off + HEAD_DIM, :] = qt_ref[0, h * HEAD_DIM:(h + 1) * HEAD_DIM, :]

    qpos = j * TQ + lax.broadcasted_iota(jnp.int32, (1, TQ), 1)
    row_iota = lax.broadcasted_iota(jnp.int32, (TK, TQ), 0)
    w = wt_ref[0]

    def fold(t):
        return t.reshape(TK // 8, 8, TQ)

    def score_tile(kt, weight, carry):
        mn, mx, pos, nn = carry
        ki = ki_ref[0, pl.ds(pl.multiple_of(kt * TK, TK), TK), :]
        acc = jnp.zeros((TK, TQ), F32)
        for h in range(IDX_HEADS):
            s = jnp.dot(ki, qi_ref[0, 4 * h * IDX_DIM:4 * (h + 1) * IDX_DIM, :], preferred_element_type=F32)
            acc = acc + w[h:h + 1, :] * jnp.maximum(s, 0.0)
        causal = (kt * TK + row_iota) <= qpos
        val = jnp.where(causal, acc, -jnp.inf)
        sc_ref[kt] = val
        return (jnp.minimum(mn, fold(jnp.where(causal, acc, jnp.inf)).min(axis=0)),
                jnp.maximum(mx, fold(val).max(axis=0)),
                pos + weight * fold(jnp.where(val > 0.0, 1.0, 0.0)).sum(axis=0),
                nn + weight * fold(jnp.where(val >= 0.0, 1.0, 0.0)).sum(axis=0))

    def score_pair(i, carry):
        carry = score_tile(2 * i, 1.0, carry)
        second = jnp.minimum(2 * i + 1, nkt - 1)
        return score_tile(second, jnp.where(2 * i + 1 < nkt, 1.0, 0.0), carry)

    zeros8 = jnp.zeros((8, TQ), F32)
    mn8, mx8, pos8, nn8 = lax.fori_loop(0, (nkt + 1) // 2, score_pair,
                                        (jnp.full((8, TQ), jnp.inf, F32), jnp.full((8, TQ), -jnp.inf, F32),
                                         zeros8, zeros8))
    mn = mn8.min(axis=0, keepdims=True)
    mx = mx8.max(axis=0, keepdims=True)
    c_pos = pos8.sum(axis=0, keepdims=True)
    c_nn = nn8.sum(axis=0, keepdims=True)

    sc_ref[nkt] = jnp.full((TK, TQ), -jnp.inf, F32)

    def count(preds):
        def part(p, kt):
            return p(sc_ref[kt]).reshape(TK // 8, 8, TQ).sum(axis=0)

        def body(i, cnts):
            return tuple(c + part(p, 2 * i) + part(p, 2 * i + 1) for c, p in zip(cnts, preds))

        cnts = lax.fori_loop(0, (nkt + 1) // 2, body, tuple(jnp.zeros((8, TQ), F32) for _ in preds))
        return [c.sum(axis=0, keepdims=True) for c in cnts]

    def ge(cand):
        return lambda t: jnp.where(t >= cand, 1.0, 0.0)

    n_causal = (qpos + 1).astype(F32)
    k_q = jnp.minimum(n_causal, float(TOPK_MAX))
    c_mx, = count([ge(mx)])
    select = n_causal > k_q
    at_max = select & (c_mx >= k_q)
    search = select & (c_mx < k_q)
    at_zero = search & (c_pos < k_q) & (c_nn >= k_q)
    above = search & (c_pos >= k_q)
    below = search & (c_nn < k_q)
    lo0 = jnp.where(at_max, mx, jnp.where(at_zero | above, 0.0, mn))
    c_lo0 = jnp.where(at_max, c_mx, jnp.where(at_zero | above, c_nn, n_causal))
    st_ref[0:1, :] = lo0
    st_ref[1:2, :] = jnp.where(below, 0.0, mx)
    st_ref[2:3, :] = c_lo0
    st_ref[3:4, :] = jnp.where(at_max, 0.0, jnp.where(at_zero, c_pos, jnp.where(below, c_nn, c_mx)))
    act0 = jnp.where((above | below) & (c_lo0 > k_q), 1.0, 0.0)
    st_ref[4:5, :] = act0

    def bisect_once():
        lo, hi, c_lo, c_hi = st_ref[0:1, :], st_ref[1:2, :], st_ref[2:3, :], st_ref[3:4, :]
        act = st_ref[4:5, :] > 0.0
        mid = 0.5 * lo + 0.5 * hi
        live = act & (mid > lo) & (mid < hi)
        c, = count([ge(mid)])
        up = live & (c >= k_q)
        dn = live & (c < k_q)
        c_lo = jnp.where(up, c, c_lo)
        st_ref[0:1, :] = jnp.where(up, mid, lo)
        st_ref[1:2, :] = jnp.where(dn, mid, hi)
        st_ref[2:3, :] = c_lo
        st_ref[3:4, :] = jnp.where(dn, c, c_hi)
        act_new = jnp.where(live & (c_lo > k_q), 1.0, 0.0)
        st_ref[4:5, :] = act_new
        return act_new

    def bisect(go):
        for _ in range(BISECT_STEPS_PER_CHECK - 1):
            bisect_once()
        return (jnp.max(bisect_once()) > 0.0).astype(jnp.int32)

    lax.while_loop(lambda go: go > 0, bisect, (jnp.max(act0) > 0.0).astype(jnp.int32))

    lo = st_ref[0:1, :]
    tie = st_ref[2:3, :] > k_q

    @pl.when(jnp.max(jnp.where(tie, 1.0, 0.0)) > 0.0)
    def _():
        need = jnp.where(tie, k_q - st_ref[3:4, :], jnp.inf)
        tri = jnp.where(lax.broadcasted_iota(jnp.int32, (TK, TK), 0) >= lax.broadcasted_iota(jnp.int32, (TK, TK), 1),
                        1.0, 0.0).astype(BF16)

        def drop(kt, seen):
            t = sc_ref[kt]
            eq = jnp.where(t == lo, 1.0, 0.0)
            rank = jnp.dot(tri, eq.astype(BF16), preferred_element_type=F32) + seen
            sc_ref[kt] = jnp.where(t == lo, jnp.where(rank > need, -jnp.inf, t), t)
            return seen + fold(eq).sum(axis=0).sum(axis=0, keepdims=True)

        def drop_pair(i, seen):
            return drop(2 * i + 1, drop(2 * i, seen))

        lax.fori_loop(0, (nkt + 1) // 2, drop_pair, jnp.zeros((1, TQ), F32))

    def to_bias(kt, _):
        bias_ref[kt] = jnp.where(sc_ref[kt] >= lo, 0.0, MASK_NEG).astype(BF16)
        return 0

    lax.fori_loop(0, nkt, to_bias, 0)

    m_ref[...] = jnp.full(m_ref.shape, MASK_NEG, F32)
    ot_ref[...] = jnp.zeros(ot_ref.shape, F32)

    def logits(h, kt, bias):
        rows = pl.ds(pl.multiple_of(kt * TK, TK), TK)
        return jnp.dot(k_ref[0, h // 2, rows, :], qpad_ref[h], preferred_element_type=F32).astype(BF16) + bias

    def half_step(kt, cur_ref, nxt_ref):
        kn = jnp.minimum(kt + 1, nkt - 1)
        bias_n = bias_ref[kn]
        for h in range(A_HEADS):
            s = cur_ref[h]
            nxt_ref[h] = logits(h, kn, bias_n)
            m = m_ref[h]
            m_tile = s.reshape(TK // 16, 16, TQ).max(axis=0).astype(F32).max(axis=0, keepdims=True)
            m_new = jnp.maximum(m, m_tile)
            p = jnp.exp2(s - m_new.astype(BF16))
            corr = jnp.exp2(m - m_new)
            m_ref[h] = m_new
            ot_ref[h] = ot_ref[h] * corr + jnp.dot(vt_ref[0, h, kt], p, preferred_element_type=F32)

    bias0 = bias_ref[0]
    for h in range(A_HEADS):
        sa_ref[h] = logits(h, 0, bias0)

    def kv_pair(i, _):
        half_step(2 * i, sa_ref, sb_ref)

        @pl.when(2 * i + 1 < nkt)
        def _():
            half_step(2 * i + 1, sb_ref, sa_ref)

        return 0

    lax.fori_loop(0, (nkt + 1) // 2, kv_pair, 0)

    for p in range(A_HEADS // 2):
        both = jnp.concatenate([ot_ref[h, 0:HEAD_DIM, :] / ot_ref[h, HEAD_DIM:HEAD_DIM + 1, :]
                                for h in (2 * p, 2 * p + 1)], axis=0)
        o_ref[0, :, p * LANES:(p + 1) * LANES] = both.T.astype(BF16)


def _dsa_attention(qt, qi4t, wt, k6, vt5, ki4):
    b, _, s = qt.shape
    nq = s // TQ
    return pl.pallas_call(
        _dsa_kernel,
        grid=(b, nq),
        in_specs=[pl.BlockSpec((1, A_WIDTH, TQ), lambda i, j: (i, 0, j)),
                  pl.BlockSpec((1, 4 * IDX_HEADS * IDX_DIM, TQ), lambda i, j: (i, 0, j)),
                  pl.BlockSpec((1, IDX_HEADS, TQ), lambda i, j: (i, 0, j)),
                  pl.BlockSpec((1, A_WIDTH // LANES, s, LANES), lambda i, j: (i, 0, 0, 0)),
                  pl.BlockSpec((1, A_HEADS, s // TK, V_ROWS, TK), lambda i, j: (i, 0, 0, 0, 0)),
                  pl.BlockSpec((1, s, 4 * IDX_DIM), lambda i, j: (i, 0, 0))],
        out_specs=pl.BlockSpec((1, TQ, A_WIDTH), lambda i, j: (i, j, 0)),
        out_shape=jax.ShapeDtypeStruct((b, s, A_WIDTH), BF16),
        scratch_shapes=[pltpu.VMEM((s // TK + 1, TK, TQ), F32),
                        pltpu.VMEM((A_HEADS, 2 * HEAD_DIM, TQ), BF16),
                        pltpu.VMEM((A_HEADS, V_ROWS, TQ), F32),
                        pltpu.VMEM((A_HEADS, 1, TQ), F32),
                        pltpu.VMEM((8, TQ), F32),
                        pltpu.VMEM((s // TK, TK, TQ), BF16),
                        pltpu.VMEM((A_HEADS, TK, TQ), BF16),
                        pltpu.VMEM((A_HEADS, TK, TQ), BF16)],
        compiler_params=_params(("parallel", "arbitrary")),
        name="dsa_attention",
    )(qt, qi4t, wt, k6, vt5, ki4)


def _window_kernel(q_ref, kp_ref, kc_ref, vp_ref, vc_ref, o_ref, lse_ref):
    i = pl.program_id(2)
    hb = B_WINDOW_STEPS
    r = lax.broadcasted_iota(jnp.int32, (hb, 2 * hb), 0)
    c = lax.broadcasted_iota(jnp.int32, (hb, 2 * hb), 1)
    dist = r + hb - c
    band = jnp.where(dist >= 0, jnp.where(dist <= B_WINDOW_STEPS, 0.0, NEG), NEG)
    first_col = jnp.where(i > 0, 0, hb)
    biases = (jnp.where(c >= first_col, band, NEG), band)
    lane = lax.broadcasted_iota(jnp.int32, (hb, LANES), 1)
    left = lane < HEAD_DIM
    for p in range(B_HEADS_PER_GROUP // 2):
        cols = slice(p * LANES, (p + 1) * LANES)
        q2, kc, vc = q_ref[0, 0, :, cols], kc_ref[0, 0, :, cols], vc_ref[0, 0, :, cols]
        windows = ((jnp.concatenate([kp_ref[0, 0, :, cols], kc[:hb]], axis=0),
                    jnp.concatenate([vp_ref[0, 0, :, cols], vc[:hb]], axis=0)), (kc, vc))
        for half, ((kw, vw), bias) in enumerate(zip(windows, biases)):
            rows = slice(half * hb, (half + 1) * hb)
            outs, lses = [], []
            for side in (left, ~left):
                qh = jnp.where(side, q2[rows], jnp.zeros((hb, LANES), BF16))
                s = lax.dot_general(qh, kw, (((1,), (1,)), ((), ())), preferred_element_type=F32) + bias
                m = s.max(axis=-1, keepdims=True)
                e = jnp.exp(s - m)
                l = e.sum(axis=-1, keepdims=True)
                outs.append(jnp.dot(e.astype(BF16), vw, preferred_element_type=F32) / l)
                lses.append(jnp.broadcast_to(m + jnp.log(l), (hb, LANES)))
            o_ref[0, 0, rows, cols] = jnp.where(left, outs[0], outs[1])
            lse_ref[0, 0, rows, cols] = jnp.where(left, lses[0], lses[1])


def _window_attention(q, k, v):
    b, d, n, _ = q.shape
    assert TB == 2 * B_WINDOW_STEPS
    cur = pl.BlockSpec((1, 1, TB, B_OUT_WIDTH), lambda bi, ri, ti: (bi, ri, ti, 0))
    prev = pl.BlockSpec((1, 1, TB // 2, B_OUT_WIDTH), lambda bi, ri, ti: (bi, ri, jnp.maximum(2 * ti - 1, 0), 0))
    return pl.pallas_call(
        _window_kernel,
        grid=(b, d, n // TB),
        in_specs=[cur, prev, cur, prev, cur],
        out_specs=[cur, cur],
        out_shape=[jax.ShapeDtypeStruct((b, d, n, B_OUT_WIDTH), F32)] * 2,
        compiler_params=_params(("parallel", "parallel", "arbitrary")),
        name=f"window_attention_d{d}",
    )(q, k, k, v, v)


def _layer_norm(y, g, b):
    mu = y.mean(axis=-1, keepdims=True)
    yc = y - mu
    var = (yc * yc).mean(axis=-1, keepdims=True)
    return yc * lax.rsqrt(var + LN_EPS) * g + b


def _merge_kernel(alpha, oa_ref, o0_ref, o1_ref, o2_ref, l0_ref, l1_ref, l2_ref, x_ref,
                  wg_ref, bg_ref, wa_ref, wb_ref, wo_ref, g_ref, b_ref, y_ref, tok_ref):
    def token_order(ref):
        d = ref.shape[1]
        if d == 1:
            return ref[0, 0]
        for r in range(d):
            for p in range(B_OUT_WIDTH // LANES):
                tok_ref[p, pl.ds(r, TM // d, stride=d), :] = ref[0, r, :, p * LANES:(p + 1) * LANES]
        return jnp.concatenate([tok_ref[p] for p in range(B_OUT_WIDTH // LANES)], axis=1)

    l0, l1, l2 = token_order(l0_ref), token_order(l1_ref), token_order(l2_ref)
    lm = jnp.maximum(jnp.maximum(l0, l1), l2)
    e0, e1, e2 = jnp.exp(l0 - lm), jnp.exp(l1 - lm), jnp.exp(l2 - lm)
    ob = (e0 * token_order(o0_ref) + e1 * token_order(o1_ref) + e2 * token_order(o2_ref)) / (e0 + e1 + e2)
    pa = jnp.dot(oa_ref[0], wa_ref[...], preferred_element_type=F32)
    pb = jnp.dot(ob.astype(BF16), wb_ref[...], preferred_element_type=F32)
    x = x_ref[0]
    xb = x.astype(BF16)

    def gate(lo, hi):
        z = jnp.dot(xb, wg_ref[:, lo:hi], preferred_element_type=F32) + bg_ref[:, lo:hi]
        return 1.0 / (1.0 + jnp.exp(-z))

    merged = gate(0, D_MODEL) * pa + gate(D_MODEL, 2 * D_MODEL) * pb
    mixed = jnp.dot(merged.astype(BF16), wo_ref[...], preferred_element_type=F32)
    y_ref[0] = _layer_norm(alpha * x + mixed, g_ref[...], b_ref[...])


def _merge(alpha, oa, obs, lses, x, w_gate, b_gate, wa, wb, wo, g, bb):
    b, s, _ = x.shape
    tok = lambda width: pl.BlockSpec((1, TM, width), lambda i, j: (i, j, 0))
    full = lambda a: pl.BlockSpec(a.shape, lambda i, j: (0, 0))
    streams = [pl.BlockSpec((1, a.shape[1], TM // a.shape[1], B_OUT_WIDTH), lambda i, j: (i, 0, j, 0))
               for a in list(obs) + list(lses)]
    return pl.pallas_call(
        functools.partial(_merge_kernel, alpha),
        grid=(b, s // TM),
        in_specs=[tok(A_WIDTH)] + streams + [tok(D_MODEL), full(w_gate),
                  full(b_gate), full(wa), full(wb), full(wo), full(g), full(bb)],
        out_specs=tok(D_MODEL),
        out_shape=jax.ShapeDtypeStruct((b, s, D_MODEL), F32),
        scratch_shapes=[pltpu.VMEM((B_OUT_WIDTH // LANES, TM, LANES), F32)],
        compiler_params=_params(("parallel", "parallel")),
        name="merge_outproj_norm",
    )(oa, *obs, *lses, x, w_gate, b_gate, wa, wb, wo, g, bb)


def _ffn_kernel(alpha, x_ref, wg_ref, wu_ref, wd_ref, g_ref, b_ref, y_ref):
    x = x_ref[...]
    xb = x.astype(BF16)
    acc = jnp.zeros((TM, D_MODEL), F32)
    for c in range(FFN_HIDDEN // FFN_CHUNK):
        cols = slice(c * FFN_CHUNK, (c + 1) * FFN_CHUNK)
        gate = jnp.dot(xb, wg_ref[:, cols], preferred_element_type=F32)
        up = jnp.dot(xb, wu_ref[:, cols], preferred_element_type=F32)
        h = gate / (1.0 + jnp.exp(-gate)) * up
        acc = acc + jnp.dot(h.astype(BF16), wd_ref[cols, :], preferred_element_type=F32)
    y_ref[...] = _layer_norm(alpha * x + acc, g_ref[...], b_ref[...])


def _ffn(alpha, x2, wg, wu, wd, g, bb):
    m = x2.shape[0]
    tok = pl.BlockSpec((TM, D_MODEL), lambda i: (i, 0))
    full = lambda a: pl.BlockSpec(a.shape, lambda i: (0, 0))
    return pl.pallas_call(
        functools.partial(_ffn_kernel, alpha),
        grid=(m // TM,),
        in_specs=[tok, full(wg), full(wu), full(wd), full(g), full(bb)],
        out_specs=tok,
        out_shape=jax.ShapeDtypeStruct((m, D_MODEL), F32),
        compiler_params=_params(("parallel",)),
        name="swiglu_norm",
    )(x2, wg, wu, wd, g, bb)


def _split_w_in(w):
    a, bw, hi = A_WIDTH, B_WIDTH, IDX_HEADS * IDX_DIM
    o = 0
    qa, o = w[:, o:o + a], o + a
    ka, o = w[:, o:o + a], o + a
    va, o = w[:, o:o + a], o + a
    qb, o = w[:, o:o + bw], o + bw
    kb, o = w[:, o:o + bw], o + bw
    vb, o = w[:, o:o + bw], o + bw
    qi, o = w[:, o:o + hi], o + hi
    ki, o = w[:, o:o + IDX_DIM], o + IDX_DIM
    wi, o = w[:, o:o + IDX_HEADS], o + IDX_HEADS
    w_gate = w[:, o:].astype(BF16)
    w_nat = jnp.concatenate([ka, qb, kb, vb, ki, ki], axis=1).astype(BF16)
    pad = jnp.zeros((w.shape[0], _T_ROWS - _T_WI - IDX_HEADS), w.dtype)
    w_t = jnp.concatenate([qa, va, qi, wi, pad], axis=1).T.astype(BF16)
    return w_nat, w_t, w_gate


def kernel(x, positions, w_in, b_gate, w_branch_a, w_branch_b, w_out, ln1_g, ln1_b,
           w_ffn_gate, w_ffn_up, w_ffn_down, ln2_g, ln2_b):
    b, s, d = x.shape
    depth = w_in.shape[0]
    assert d == D_MODEL and s % (max(B_DILATIONS) * TB) == 0 and s % TM == 0
    alpha = (2 * depth) ** 0.25
    cos, sin, cos_t, sin_t = _rope_tables(positions)
    row = lambda v: v.reshape(1, -1)
    for layer in range(depth):
        w_nat, w_t, w_gate = _split_w_in(w_in[layer])
        ka6, ki4, *streams = _proj_nat(x, cos, sin, w_nat)
        qt, vt5, qi4t, wt = _proj_t(x, cos_t, sin_t, w_t)
        oa = _dsa_attention(qt, qi4t, wt, ka6, vt5, ki4)
        groups = [_window_attention(*streams[3 * g:3 * g + 3]) for g in range(len(B_DILATIONS))]
        x1 = _merge(alpha, oa, [o for o, _ in groups], [l for _, l in groups], x, w_gate, row(b_gate[layer]),
                    w_branch_a[layer].astype(BF16), w_branch_b[layer].astype(BF16),
                    w_out[layer].astype(BF16), row(ln1_g[layer]), row(ln1_b[layer]))
        x2 = _ffn(alpha, x1.reshape(b * s, d), w_ffn_gate[layer].astype(BF16), w_ffn_up[layer].astype(BF16),
                  w_ffn_down[layer].astype(BF16), row(ln2_g[layer]), row(ln2_b[layer]))
        x = x2.reshape(b, s, d)
    return x
```

```python
import functools

import jax
import jax.numpy as jnp
from jax import lax
from jax.experimental import pallas as pl
from jax.experimental.pallas import tpu as pltpu

F32 = jnp.float32
BF16 = jnp.bfloat16

D_MODEL = 1024
HEAD_DIM = 64
ROT_HALF = 8
ROPE_THETA = 500000.0
ATTN_SCALE = HEAD_DIM ** -0.5
LOG2E = 1.4426950408889634
V_ROWS = HEAD_DIM + 16
A_HEADS = 12
A_WIDTH = A_HEADS * HEAD_DIM
IDX_HEADS = 8
IDX_DIM = 64
IDX_SCALE = (IDX_HEADS ** -0.5) * (IDX_DIM ** -0.5)
TOPK_MAX = 256
B_DILATIONS = (1, 4, 16)
B_WINDOW_STEPS = 128
B_HEADS_PER_GROUP = 4
B_WIDTH = 3 * B_HEADS_PER_GROUP * HEAD_DIM
B_OUT_WIDTH = B_HEADS_PER_GROUP * HEAD_DIM
FFN_HIDDEN = 2816
LN_EPS = 1e-5
NEG = -1e30
MASK_NEG = -(2.0 ** 100)

LANES = 128
VMEM_LIMIT = 56 * 1024 * 1024
TM = 512
TQ = 256
TK = 256
TB = 256
FFN_CHUNK = 256
BISECT_STEPS_PER_CHECK = 2
MERGE_ROWS = 256
MERGE_COLS = 256
ROPE_TILE = 2048

_N_KA, _N_QB, _N_KB, _N_VB = 0, 768, 1536, 2304
_N_KI, _N_NAT = 3072, 3200
_T_QA, _T_VA, _T_QI, _T_WI, _T_ROWS = 0, 768, 1536, 2048, 2064


def _params(sem):
    return pltpu.CompilerParams(dimension_semantics=sem, vmem_limit_bytes=VMEM_LIMIT)


def _rope_kernel(pc_ref, pr_ref, fr_ref, fc_ref, c_ref, s_ref, ct_ref, st_ref):
    ang = pc_ref[...].astype(F32) * fr_ref[...]
    d = lax.broadcasted_iota(jnp.int32, ang.shape, 1) & (HEAD_DIM - 1)
    cos, sin = jnp.cos(ang), jnp.sin(ang)
    c_ref[...] = jnp.where(d < 2 * ROT_HALF, cos, 1.0)
    s_ref[...] = jnp.where(d < ROT_HALF, -sin, jnp.where(d < 2 * ROT_HALF, sin, 0.0))
    ang_t = fc_ref[...] * pr_ref[...].astype(F32)
    ct_ref[...] = jnp.cos(ang_t)
    st_ref[...] = jnp.sin(ang_t)


def _rope_tables(positions):
    m = positions.size
    inv_freq = ROPE_THETA ** (-jnp.arange(0, 2 * ROT_HALF, 2, dtype=F32) / (2 * ROT_HALF))
    f_row = jnp.tile(inv_freq, LANES // ROT_HALF).reshape(1, LANES)
    f_col = inv_freq.reshape(ROT_HALF, 1)
    t = min(ROPE_TILE, m)
    return pl.pallas_call(
        _rope_kernel,
        grid=(m // t,),
        in_specs=[pl.BlockSpec((t, 1), lambda i: (i, 0)), pl.BlockSpec((1, t), lambda i: (0, i)),
                  pl.BlockSpec((1, LANES), lambda i: (0, 0)), pl.BlockSpec((ROT_HALF, 1), lambda i: (0, 0))],
        out_specs=[pl.BlockSpec((t, LANES), lambda i: (i, 0)), pl.BlockSpec((t, LANES), lambda i: (i, 0)),
                   pl.BlockSpec((ROT_HALF, t), lambda i: (0, i)), pl.BlockSpec((ROT_HALF, t), lambda i: (0, i))],
        out_shape=[jax.ShapeDtypeStruct((m, LANES), F32), jax.ShapeDtypeStruct((m, LANES), F32),
                   jax.ShapeDtypeStruct((ROT_HALF, m), F32), jax.ShapeDtypeStruct((ROT_HALF, m), F32)],
        compiler_params=_params(("parallel",)),
        name="rope_tables",
    )(positions.reshape(m, 1), positions.reshape(1, m), f_row, f_col)


def _proj_nat_kernel(x_ref, c_ref, s_ref, w_ref, ka_ref, ki_ref, *rest):
    streams, y_ref = rest[:-1], rest[-1]
    xb = x_ref[0].astype(BF16)
    cos, sin = c_ref[...], s_ref[...]
    lane = lax.broadcasted_iota(jnp.int32, cos.shape, 1)
    first = (lane & (HEAD_DIM - 1)) < ROT_HALF

    def rope(y):
        partner = jnp.where(first, pltpu.roll(y, LANES - ROT_HALF, 1), pltpu.roll(y, ROT_HALF, 1))
        return y * cos + partner * sin

    def proj(lo, hi):
        return jnp.dot(xb, w_ref[:, lo:hi], preferred_element_type=F32)

    blocks = A_WIDTH // LANES
    y = proj(_N_KA, _N_KA + A_WIDTH)
    for p in range(blocks):
        ka_ref[0, p] = rope(y[:, p * LANES:(p + 1) * LANES]).astype(BF16)

    def scatter_streams(which):
        per_group = B_OUT_WIDTH // LANES
        for g, d in enumerate(B_DILATIONS):
            out = streams[3 * g + which]
            for r in range(d):
                for p in range(per_group):
                    rows = y_ref[g * per_group + p, pl.ds(r, TM // d, stride=d), :]
                    out[0, r, :, p * LANES:(p + 1) * LANES] = rows.astype(BF16)

    y = proj(_N_QB, _N_QB + B_WIDTH)
    for p in range(blocks):
        y_ref[p] = rope(y[:, p * LANES:(p + 1) * LANES]) * (ATTN_SCALE * LOG2E)
    scatter_streams(0)
    y = proj(_N_KB, _N_KB + B_WIDTH)
    for p in range(blocks):
        y_ref[p] = rope(y[:, p * LANES:(p + 1) * LANES])
    scatter_streams(1)
    y = proj(_N_VB, _N_VB + B_WIDTH)
    for p in range(blocks):
        y_ref[p] = y[:, p * LANES:(p + 1) * LANES]
    scatter_streams(2)
    r = rope(proj(_N_KI, _N_KI + LANES))
    hi = r.astype(BF16).astype(F32)
    hl = jnp.where(lane < IDX_DIM, hi, r - hi).astype(BF16)
    ki_ref[0, :, 0:LANES] = hl
    ki_ref[0, :, LANES:2 * LANES] = hl


def _proj_nat(x, cos, sin, w_nat):
    b, s, d = x.shape
    nt = s // TM
    tok = lambda width: pl.BlockSpec((1, TM, width), lambda i, j: (i, j, 0))
    stream_specs, stream_shapes = [], []
    for dil in B_DILATIONS:
        for _ in range(3):
            stream_specs.append(pl.BlockSpec((1, dil, TM // dil, B_OUT_WIDTH), lambda i, j: (i, 0, j, 0)))
            stream_shapes.append(jax.ShapeDtypeStruct((b, dil, s // dil, B_OUT_WIDTH), BF16))
    return pl.pallas_call(
        _proj_nat_kernel,
        grid=(b, nt),
        in_specs=[tok(d),
                  pl.BlockSpec((TM, LANES), lambda i, j: (i * nt + j, 0)),
                  pl.BlockSpec((TM, LANES), lambda i, j: (i * nt + j, 0)),
                  pl.BlockSpec((d, _N_NAT), lambda i, j: (0, 0))],
        out_specs=[pl.BlockSpec((1, A_WIDTH // LANES, TM, LANES), lambda i, j: (i, 0, j, 0)),
                   tok(2 * LANES)] + stream_specs,
        out_shape=[jax.ShapeDtypeStruct((b, A_WIDTH // LANES, s, LANES), BF16),
                   jax.ShapeDtypeStruct((b, s, 2 * LANES), BF16)] + stream_shapes,
        scratch_shapes=[pltpu.VMEM((B_WIDTH // LANES, TM, LANES), F32)],
        compiler_params=_params(("parallel", "parallel")),
        name="proj_token_major",
    )(x, cos, sin, w_nat)


def _proj_t_kernel(x_ref, ct_ref, st_ref, w_ref, qt_ref, vt_ref, qi_ref, wt_ref):
    xb = x_ref[0].astype(BF16)
    cos, sin = ct_ref[...], st_ref[...]

    def proj(lo, hi):
        return lax.dot_general(w_ref[lo:hi, :], xb, (((1,), (1,)), ((), ())), preferred_element_type=F32)

    def rope_head(y):
        x1, x2 = y[0:ROT_HALF], y[ROT_HALF:2 * ROT_HALF]
        return jnp.concatenate([x1 * cos - x2 * sin, x2 * cos + x1 * sin, y[2 * ROT_HALF:]], axis=0)

    y = proj(_T_QA, _T_QA + A_WIDTH)
    for h in range(A_HEADS):
        r = rope_head(y[h * HEAD_DIM:(h + 1) * HEAD_DIM])
        qt_ref[0, h * HEAD_DIM:(h + 1) * HEAD_DIM, :] = (r * (ATTN_SCALE * LOG2E)).astype(BF16)
    y = proj(_T_VA, _T_VA + A_WIDTH)
    for h in range(A_HEADS):
        for c in range(TM // TK):
            vt_ref[0, h, c, 0:HEAD_DIM, :] = y[h * HEAD_DIM:(h + 1) * HEAD_DIM, c * TK:(c + 1) * TK].astype(BF16)
            vt_ref[0, h, c, HEAD_DIM:V_ROWS, :] = jnp.ones((V_ROWS - HEAD_DIM, TK), BF16)
    y = proj(_T_QI, _T_QI + IDX_HEADS * IDX_DIM)
    for h in range(IDX_HEADS):
        r = rope_head(y[h * IDX_DIM:(h + 1) * IDX_DIM])
        hi = r.astype(BF16)
        lo = (r - hi.astype(F32)).astype(BF16)
        base = 4 * h * IDX_DIM
        qi_ref[0, base:base + IDX_DIM, :] = hi
        qi_ref[0, base + IDX_DIM:base + 2 * IDX_DIM, :] = hi
        qi_ref[0, base + 2 * IDX_DIM:base + 3 * IDX_DIM, :] = lo
        qi_ref[0, base + 3 * IDX_DIM:base + 4 * IDX_DIM, :] = lo
    wt_ref[0] = proj(_T_WI, _T_ROWS)[0:IDX_HEADS] * IDX_SCALE


def _proj_t(x, cos_t, sin_t, w_t):
    b, s, d = x.shape
    nt = s // TM
    return pl.pallas_call(
        _proj_t_kernel,
        grid=(b, nt),
        in_specs=[pl.BlockSpec((1, TM, d), lambda i, j: (i, j, 0)),
                  pl.BlockSpec((ROT_HALF, TM), lambda i, j: (0, i * nt + j)),
                  pl.BlockSpec((ROT_HALF, TM), lambda i, j: (0, i * nt + j)),
                  pl.BlockSpec((_T_ROWS, d), lambda i, j: (0, 0))],
        out_specs=[pl.BlockSpec((1, A_WIDTH, TM), lambda i, j: (i, 0, j)),
                   pl.BlockSpec((1, A_HEADS, TM // TK, V_ROWS, TK), lambda i, j: (i, 0, j, 0, 0)),
                   pl.BlockSpec((1, 4 * IDX_HEADS * IDX_DIM, TM), lambda i, j: (i, 0, j)),
                   pl.BlockSpec((1, IDX_HEADS, TM), lambda i, j: (i, 0, j))],
        out_shape=[jax.ShapeDtypeStruct((b, A_WIDTH, s), BF16),
                   jax.ShapeDtypeStruct((b, A_HEADS, s // TK, V_ROWS, TK), BF16),
                   jax.ShapeDtypeStruct((b, 4 * IDX_HEADS * IDX_DIM, s), BF16),
                   jax.ShapeDtypeStruct((b, IDX_HEADS, s), F32)],
        compiler_params=_params(("parallel", "parallel")),
        name="proj_feature_major",
    )(x, cos_t, sin_t, w_t)


def _dsa_kernel(qt_ref, qi_ref, wt_ref, k_ref, vt_ref, ki_ref, o_ref,
                sc_ref, qpad_ref, ot_ref, m_ref, st_ref, bias_ref, sa_ref, sb_ref):
    j = pl.program_id(1)
    nkt = j + 1

    for h in range(A_HEADS):
        off = (h % 2) * HEAD_DIM
        qpad_ref[h] = jnp.zeros((2 * HEAD_DIM, TQ), BF16)
        qpad_ref[h, off:off + HEAD_DIM, :] = qt_ref[0, h * HEAD_DIM:(h + 1) * HEAD_DIM, :]

    qpos = j * TQ + lax.broadcasted_iota(jnp.int32, (1, TQ), 1)
    row_iota = lax.broadcasted_iota(jnp.int32, (TK, TQ), 0)
    w = wt_ref[0]

    def fold(t):
        return t.reshape(TK // 8, 8, TQ)

    def score_tile(kt, weight, carry):
        mn, mx, pos, nn = carry
        ki = ki_ref[0, pl.ds(pl.multiple_of(kt * TK, TK), TK), :]
        acc = jnp.zeros((TK, TQ), F32)
        for h in range(IDX_HEADS):
            s = jnp.dot(ki, qi_ref[0, 4 * h * IDX_DIM:4 * (h + 1) * IDX_DIM, :], preferred_element_type=F32)
            acc = acc + w[h:h + 1, :] * jnp.maximum(s, 0.0)
        causal = (kt * TK + row_iota) <= qpos
        val = jnp.where(causal, acc, -jnp.inf)
        sc_ref[kt] = val
        return (jnp.minimum(mn, fold(jnp.where(causal, acc, jnp.inf)).min(axis=0)),
                jnp.maximum(mx, fold(val).max(axis=0)),
                pos + weight * fold(jnp.where(val > 0.0, 1.0, 0.0)).sum(axis=0),
                nn + weight * fold(jnp.where(val >= 0.0, 1.0, 0.0)).sum(axis=0))

    def score_pair(i, carry):
        carry = score_tile(2 * i, 1.0, carry)
        second = jnp.minimum(2 * i + 1, nkt - 1)
        return score_tile(second, jnp.where(2 * i + 1 < nkt, 1.0, 0.0), carry)

    zeros8 = jnp.zeros((8, TQ), F32)
    mn8, mx8, pos8, nn8 = lax.fori_loop(0, (nkt + 1) // 2, score_pair,
                                        (jnp.full((8, TQ), jnp.inf, F32), jnp.full((8, TQ), -jnp.inf, F32),
                                         zeros8, zeros8))
    mn = mn8.min(axis=0, keepdims=True)
    mx = mx8.max(axis=0, keepdims=True)
    c_pos = pos8.sum(axis=0, keepdims=True)
    c_nn = nn8.sum(axis=0, keepdims=True)

    sc_ref[nkt] = jnp.full((TK, TQ), -jnp.inf, F32)

    def count(preds):
        def part(p, kt):
            return p(sc_ref[kt]).reshape(TK // 8, 8, TQ).sum(axis=0)

        def body(i, cnts):
            return tuple(c + part(p, 2 * i) + part(p, 2 * i + 1) for c, p in zip(cnts, preds))

        cnts = lax.fori_loop(0, (nkt + 1) // 2, body, tuple(jnp.zeros((8, TQ), F32) for _ in preds))
        return [c.sum(axis=0, keepdims=True) for c in cnts]

    def ge(cand):
        return lambda t: jnp.where(t >= cand, 1.0, 0.0)

    n_causal = (qpos + 1).astype(F32)
    k_q = jnp.minimum(n_causal, float(TOPK_MAX))
    c_mx, = count([ge(mx)])
    select = n_causal > k_q
    at_max = select & (c_mx >= k_q)
    search = select & (c_mx < k_q)
    at_zero = search & (c_pos < k_q) & (c_nn >= k_q)
    above = search & (c_pos >= k_q)
    below = search & (c_nn < k_q)
    lo0 = jnp.where(at_max, mx, jnp.where(at_zero | above, 0.0, mn))
    c_lo0 = jnp.where(at_max, c_mx, jnp.where(at_zero | above, c_nn, n_causal))
    st_ref[0:1, :] = lo0
    st_ref[1:2, :] = jnp.where(below, 0.0, mx)
    st_ref[2:3, :] = c_lo0
    st_ref[3:4, :] = jnp.where(at_max, 0.0, jnp.where(at_zero, c_pos, jnp.where(below, c_nn, c_mx)))
    act0 = jnp.where((above | below) & (c_lo0 > k_q), 1.0, 0.0)
    st_ref[4:5, :] = act0

    def bisect_once():
        lo, hi, c_lo, c_hi = st_ref[0:1, :], st_ref[1:2, :], st_ref[2:3, :], st_ref[3:4, :]
        act = st_ref[4:5, :] > 0.0
        mid = 0.5 * lo + 0.5 * hi
        live = act & (mid > lo) & (mid < hi)
        c, = count([ge(mid)])
        up = live & (c >= k_q)
        dn = live & (c < k_q)
        c_lo = jnp.where(up, c, c_lo)
        st_ref[0:1, :] = jnp.where(up, mid, lo)
        st_ref[1:2, :] = jnp.where(dn, mid, hi)
        st_ref[2:3, :] = c_lo
        st_ref[3:4, :] = jnp.where(dn, c, c_hi)
        act_new = jnp.where(live & (c_lo > k_q), 1.0, 0.0)
        st_ref[4:5, :] = act_new
        return act_new

    def bisect(go):
        for _ in range(BISECT_STEPS_PER_CHECK - 1):
            bisect_once()
        return (jnp.max(bisect_once()) > 0.0).astype(jnp.int32)

    lax.while_loop(lambda go: go > 0, bisect, (jnp.max(act0) > 0.0).astype(jnp.int32))

    lo = st_ref[0:1, :]
    tie = st_ref[2:3, :] > k_q

    @pl.when(jnp.max(jnp.where(tie, 1.0, 0.0)) > 0.0)
    def _():
        need = jnp.where(tie, k_q - st_ref[3:4, :], jnp.inf)
        tri = jnp.where(lax.broadcasted_iota(jnp.int32, (TK, TK), 0) >= lax.broadcasted_iota(jnp.int32, (TK, TK), 1),
                        1.0, 0.0).astype(BF16)

        def drop(kt, seen):
            t = sc_ref[kt]
            eq = jnp.where(t == lo, 1.0, 0.0)
            rank = jnp.dot(tri, eq.astype(BF16), preferred_element_type=F32) + seen
            sc_ref[kt] = jnp.where(t == lo, jnp.where(rank > need, -jnp.inf, t), t)
            return seen + fold(eq).sum(axis=0).sum(axis=0, keepdims=True)

        def drop_pair(i, seen):
            return drop(2 * i + 1, drop(2 * i, seen))

        lax.fori_loop(0, (nkt + 1) // 2, drop_pair, jnp.zeros((1, TQ), F32))

    def to_bias(kt, _):
        bias_ref[kt] = jnp.where(sc_ref[kt] >= lo, 0.0, MASK_NEG).astype(BF16)
        return 0

    lax.fori_loop(0, nkt, to_bias, 0)

    m_ref[...] = jnp.full(m_ref.shape, MASK_NEG, F32)
    ot_ref[...] = jnp.zeros(ot_ref.shape, F32)

    def logits(h, kt, bias):
        rows = pl.ds(pl.multiple_of(kt * TK, TK), TK)
        return jnp.dot(k_ref[0, h // 2, rows, :], qpad_ref[h], preferred_element_type=F32).astype(BF16) + bias

    def half_step(kt, cur_ref, nxt_ref):
        kn = jnp.minimum(kt + 1, nkt - 1)
        bias_n = bias_ref[kn]
        for h in range(A_HEADS):
            s = cur_ref[h]
            nxt_ref[h] = logits(h, kn, bias_n)
            m = m_ref[h]
            m_tile = s.reshape(TK // 16, 16, TQ).max(axis=0).astype(F32).max(axis=0, keepdims=True)
            m_new = jnp.maximum(m, m_tile)
            p = jnp.exp2(s - m_new.astype(BF16))
            corr = jnp.exp2(m - m_new)
            m_ref[h] = m_new
            ot_ref[h] = ot_ref[h] * corr + jnp.dot(vt_ref[0, h, kt], p, preferred_element_type=F32)

    bias0 = bias_ref[0]
    for h in range(A_HEADS):
        sa_ref[h] = logits(h, 0, bias0)

    def kv_pair(i, _):
        half_step(2 * i, sa_ref, sb_ref)

        @pl.when(2 * i + 1 < nkt)
        def _():
            half_step(2 * i + 1, sb_ref, sa_ref)

        return 0

    lax.fori_loop(0, (nkt + 1) // 2, kv_pair, 0)

    for p in range(A_HEADS // 2):
        both = jnp.concatenate([ot_ref[h, 0:HEAD_DIM, :] / ot_ref[h, HEAD_DIM:HEAD_DIM + 1, :]
                                for h in (2 * p, 2 * p + 1)], axis=0)
        o_ref[0, :, p * LANES:(p + 1) * LANES] = both.T.astype(BF16)


def _dsa_attention(qt, qi4t, wt, k6, vt5, ki4):
    b, _, s = qt.shape
    nq = s // TQ
    return pl.pallas_call(
        _dsa_kernel,
        grid=(b, nq),
        in_specs=[pl.BlockSpec((1, A_WIDTH, TQ), lambda i, j: (i, 0, j)),
                  pl.BlockSpec((1, 4 * IDX_HEADS * IDX_DIM, TQ), lambda i, j: (i, 0, j)),
                  pl.BlockSpec((1, IDX_HEADS, TQ), lambda i, j: (i, 0, j)),
                  pl.BlockSpec((1, A_WIDTH // LANES, s, LANES), lambda i, j: (i, 0, 0, 0)),
                  pl.BlockSpec((1, A_HEADS, s // TK, V_ROWS, TK), lambda i, j: (i, 0, 0, 0, 0)),
                  pl.BlockSpec((1, s, 4 * IDX_DIM), lambda i, j: (i, 0, 0))],
        out_specs=pl.BlockSpec((1, TQ, A_WIDTH), lambda i, j: (i, j, 0)),
        out_shape=jax.ShapeDtypeStruct((b, s, A_WIDTH), BF16),
        scratch_shapes=[pltpu.VMEM((s // TK + 1, TK, TQ), F32),
                        pltpu.VMEM((A_HEADS, 2 * HEAD_DIM, TQ), BF16),
                        pltpu.VMEM((A_HEADS, V_ROWS, TQ), F32),
                        pltpu.VMEM((A_HEADS, 1, TQ), F32),
                        pltpu.VMEM((8, TQ), F32),
                        pltpu.VMEM((s // TK, TK, TQ), BF16),
                        pltpu.VMEM((A_HEADS, TK, TQ), BF16),
                        pltpu.VMEM((A_HEADS, TK, TQ), BF16)],
        compiler_params=_params(("parallel", "arbitrary")),
        name="dsa_attention",
    )(qt, qi4t, wt, k6, vt5, ki4)


def _window_kernel(q_ref, kp_ref, kc_ref, vp_ref, vc_ref, o_ref, lse_ref):
    i = pl.program_id(2)
    hb = B_WINDOW_STEPS
    r = lax.broadcasted_iota(jnp.int32, (hb, 2 * hb), 0)
    c = lax.broadcasted_iota(jnp.int32, (hb, 2 * hb), 1)
    dist = r + hb - c
    band = jnp.where(dist >= 0, jnp.where(dist <= B_WINDOW_STEPS, 0.0, NEG), NEG)
    first_col = jnp.where(i > 0, 0, hb)
    biases = (jnp.where(c >= first_col, band, NEG), band)
    lane = lax.broadcasted_iota(jnp.int32, (hb, LANES), 1)
    left = lane < HEAD_DIM
    for p in range(B_HEADS_PER_GROUP // 2):
        cols = slice(p * LANES, (p + 1) * LANES)
        q2, kc, vc = q_ref[0, 0, :, cols], kc_ref[0, 0, :, cols], vc_ref[0, 0, :, cols]
        windows = ((jnp.concatenate([kp_ref[0, 0, :, cols], kc[:hb]], axis=0),
                    jnp.concatenate([vp_ref[0, 0, :, cols], vc[:hb]], axis=0)), (kc, vc))
        ones = jnp.ones((2 * hb, LANES), BF16)
        for half, ((kw, vw), bias) in enumerate(zip(windows, biases)):
            rows = slice(half * hb, (half + 1) * hb)
            v1 = jnp.concatenate([vw, ones], axis=1)
            outs, lses = [], []
            for side in (left, ~left):
                qh = jnp.where(side, q2[rows], jnp.zeros((hb, LANES), BF16))
                s = lax.dot_general(qh, kw, (((1,), (1,)), ((), ())), preferred_element_type=F32) + bias
                m = s.max(axis=-1, keepdims=True)
                e = jnp.exp2((s - m).astype(BF16))
                ol = jnp.dot(e, v1, preferred_element_type=F32)
                l = ol[:, LANES:]
                outs.append(ol[:, :LANES] / l)
                lses.append(m + jnp.log2(l))
            o_ref[0, 0, rows, cols] = jnp.where(left, outs[0], outs[1])
            lse_ref[0, 0, rows, cols] = jnp.where(left, lses[0], lses[1])


def _window_attention(q, k, v):
    b, d, n, _ = q.shape
    assert TB == 2 * B_WINDOW_STEPS
    cur = pl.BlockSpec((1, 1, TB, B_OUT_WIDTH), lambda bi, ri, ti: (bi, ri, ti, 0))
    prev = pl.BlockSpec((1, 1, TB // 2, B_OUT_WIDTH), lambda bi, ri, ti: (bi, ri, jnp.maximum(2 * ti - 1, 0), 0))
    return pl.pallas_call(
        _window_kernel,
        grid=(b, d, n // TB),
        in_specs=[cur, prev, cur, prev, cur],
        out_specs=[cur, cur],
        out_shape=[jax.ShapeDtypeStruct((b, d, n, B_OUT_WIDTH), F32)] * 2,
        compiler_params=_params(("parallel", "parallel", "arbitrary")),
        name=f"window_attention_d{d}",
    )(q, k, k, v, v)


def _layer_norm(y, g, b):
    mu = y.mean(axis=-1, keepdims=True)
    yc = y - mu
    var = (yc * yc).mean(axis=-1, keepdims=True)
    return yc * lax.rsqrt(var + LN_EPS) * g + b


def _merge_kernel(alpha, oa_ref, o0_ref, o1_ref, o2_ref, l0_ref, l1_ref, l2_ref, x_ref,
                  wg_ref, bg_ref, wa_ref, wb_ref, wo_ref, g_ref, b_ref, y_ref,
                  tok_ref, ob_ref, xb_ref, mg_ref):
    halves = B_OUT_WIDTH // LANES

    for n, ref in enumerate((o1_ref, o2_ref, l1_ref, l2_ref)):
        d = ref.shape[1]
        for r in range(d):
            for p in range(halves):
                tok_ref[n, p, pl.ds(r, TM // d, stride=d), :] = ref[0, r, :, p * LANES:(p + 1) * LANES]
    for c in range(TM // MERGE_ROWS):
        rows = slice(c * MERGE_ROWS, (c + 1) * MERGE_ROWS)
        for p in range(halves):
            cols = slice(p * LANES, (p + 1) * LANES)
            l0, l1, l2 = l0_ref[0, 0, rows, cols], tok_ref[2, p, rows, :], tok_ref[3, p, rows, :]
            lm = jnp.maximum(jnp.maximum(l0, l1), l2)
            e0, e1, e2 = jnp.exp2(l0 - lm), jnp.exp2(l1 - lm), jnp.exp2(l2 - lm)
            ob = (e0 * o0_ref[0, 0, rows, cols] + e1 * tok_ref[0, p, rows, :] + e2 * tok_ref[1, p, rows, :])
            ob_ref[rows, cols] = (ob / (e0 + e1 + e2)).astype(BF16)

    xb_ref[...] = x_ref[0].astype(BF16)
    for c in range(D_MODEL // MERGE_COLS):
        cols = slice(c * MERGE_COLS, (c + 1) * MERGE_COLS)
        gcols = slice(D_MODEL + c * MERGE_COLS, D_MODEL + (c + 1) * MERGE_COLS)

        def gate(sel):
            z = jnp.dot(xb_ref[...], wg_ref[:, sel], preferred_element_type=F32) + bg_ref[:, sel]
            return 1.0 / (1.0 + jnp.exp(-z))

        pa = jnp.dot(oa_ref[0], wa_ref[:, cols], preferred_element_type=F32)
        pb = jnp.dot(ob_ref[...], wb_ref[:, cols], preferred_element_type=F32)
        mg_ref[:, cols] = (gate(cols) * pa + gate(gcols) * pb).astype(BF16)

    for c in range(TM // MERGE_ROWS):
        rows = slice(c * MERGE_ROWS, (c + 1) * MERGE_ROWS)
        mixed = jnp.dot(mg_ref[rows, :], wo_ref[...], preferred_element_type=F32)
        y_ref[0, rows, :] = _layer_norm(alpha * x_ref[0, rows, :] + mixed, g_ref[...], b_ref[...])


def _merge(alpha, oa, obs, lses, x, w_gate, b_gate, wa, wb, wo, g, bb):
    b, s, _ = x.shape
    tok = lambda width: pl.BlockSpec((1, TM, width), lambda i, j: (i, j, 0))
    full = lambda a: pl.BlockSpec(a.shape, lambda i, j: (0, 0))
    streams = [pl.BlockSpec((1, a.shape[1], TM // a.shape[1], B_OUT_WIDTH), lambda i, j: (i, 0, j, 0))
               for a in list(obs) + list(lses)]
    return pl.pallas_call(
        functools.partial(_merge_kernel, alpha),
        grid=(b, s // TM),
        in_specs=[tok(A_WIDTH)] + streams + [tok(D_MODEL), full(w_gate),
                  full(b_gate), full(wa), full(wb), full(wo), full(g), full(bb)],
        out_specs=tok(D_MODEL),
        out_shape=jax.ShapeDtypeStruct((b, s, D_MODEL), F32),
        scratch_shapes=[pltpu.VMEM((4, B_OUT_WIDTH // LANES, TM, LANES), F32),
                        pltpu.VMEM((TM, B_OUT_WIDTH), BF16),
                        pltpu.VMEM((TM, D_MODEL), BF16),
                        pltpu.VMEM((TM, D_MODEL), BF16)],
        compiler_params=_params(("parallel", "parallel")),
        name="merge_outproj_norm",
    )(oa, *obs, *lses, x, w_gate, b_gate, wa, wb, wo, g, bb)


def _ffn_kernel(alpha, x_ref, wg_ref, wu_ref, wd_ref, g_ref, b_ref, y_ref):
    x = x_ref[...]
    xb = x.astype(BF16)
    acc = jnp.zeros((TM, D_MODEL), F32)
    for c in range(FFN_HIDDEN // FFN_CHUNK):
        cols = slice(c * FFN_CHUNK, (c + 1) * FFN_CHUNK)
        gate = jnp.dot(xb, wg_ref[:, cols], preferred_element_type=F32)
        up = jnp.dot(xb, wu_ref[:, cols], preferred_element_type=F32)
        h = gate / (1.0 + jnp.exp(-gate)) * up
        acc = acc + jnp.dot(h.astype(BF16), wd_ref[cols, :], preferred_element_type=F32)
    y_ref[...] = _layer_norm(alpha * x + acc, g_ref[...], b_ref[...])


def _ffn(alpha, x2, wg, wu, wd, g, bb):
    m = x2.shape[0]
    tok = pl.BlockSpec((TM, D_MODEL), lambda i: (i, 0))
    full = lambda a: pl.BlockSpec(a.shape, lambda i: (0, 0))
    return pl.pallas_call(
        functools.partial(_ffn_kernel, alpha),
        grid=(m // TM,),
        in_specs=[tok, full(wg), full(wu), full(wd), full(g), full(bb)],
        out_specs=tok,
        out_shape=jax.ShapeDtypeStruct((m, D_MODEL), F32),
        compiler_params=_params(("parallel",)),
        name="swiglu_norm",
    )(x2, wg, wu, wd, g, bb)


def _split_w_in(w):
    a, bw, hi = A_WIDTH, B_WIDTH, IDX_HEADS * IDX_DIM
    o = 0
    qa, o = w[:, o:o + a], o + a
    ka, o = w[:, o:o + a], o + a
    va, o = w[:, o:o + a], o + a
    qb, o = w[:, o:o + bw], o + bw
    kb, o = w[:, o:o + bw], o + bw
    vb, o = w[:, o:o + bw], o + bw
    qi, o = w[:, o:o + hi], o + hi
    ki, o = w[:, o:o + IDX_DIM], o + IDX_DIM
    wi, o = w[:, o:o + IDX_HEADS], o + IDX_HEADS
    w_gate = w[:, o:].astype(BF16)
    w_nat = jnp.concatenate([ka, qb, kb, vb, ki, ki], axis=1).astype(BF16)
    pad = jnp.zeros((w.shape[0], _T_ROWS - _T_WI - IDX_HEADS), w.dtype)
    w_t = jnp.concatenate([qa, va, qi, wi, pad], axis=1).T.astype(BF16)
    return w_nat, w_t, w_gate


def kernel(x, positions, w_in, b_gate, w_branch_a, w_branch_b, w_out, ln1_g, ln1_b,
           w_ffn_gate, w_ffn_up, w_ffn_down, ln2_g, ln2_b):
    b, s, d = x.shape
    depth = w_in.shape[0]
    assert d == D_MODEL and s % (max(B_DILATIONS) * TB) == 0 and s % TM == 0
    alpha = (2 * depth) ** 0.25
    cos, sin, cos_t, sin_t = _rope_tables(positions)
    row = lambda v: v.reshape(1, -1)
    for layer in range(depth):
        w_nat, w_t, w_gate = _split_w_in(w_in[layer])
        ka6, ki4, *streams = _proj_nat(x, cos, sin, w_nat)
        qt, vt5, qi4t, wt = _proj_t(x, cos_t, sin_t, w_t)
        oa = _dsa_attention(qt, qi4t, wt, ka6, vt5, ki4)
        groups = [_window_attention(*streams[3 * g:3 * g + 3]) for g in range(len(B_DILATIONS))]
        x1 = _merge(alpha, oa, [o for o, _ in groups], [l for _, l in groups], x, w_gate, row(b_gate[layer]),
                    w_branch_a[layer].astype(BF16), w_branch_b[layer].astype(BF16),
                    w_out[layer].astype(BF16), row(ln1_g[layer]), row(ln1_b[layer]))
        x2 = _ffn(alpha, x1.reshape(b * s, d), w_ffn_gate[layer].astype(BF16), w_ffn_up[layer].astype(BF16),
                  w_ffn_down[layer].astype(BF16), row(ln2_g[layer]), row(ln2_b[layer]))
        x = x2.reshape(b, s, d)
    return x
```

```python
import functools

import jax
import jax.numpy as jnp
from jax import lax
from jax.experimental import pallas as pl
from jax.experimental.pallas import tpu as pltpu

F32 = jnp.float32
BF16 = jnp.bfloat16

D_MODEL = 1024
HEAD_DIM = 64
ROT_HALF = 8
ROPE_THETA = 500000.0
ATTN_SCALE = HEAD_DIM ** -0.5
LOG2E = 1.4426950408889634
V_ROWS = HEAD_DIM + 16
A_HEADS = 12
A_WIDTH = A_HEADS * HEAD_DIM
IDX_HEADS = 8
IDX_DIM = 64
IDX_SCALE = (IDX_HEADS ** -0.5) * (IDX_DIM ** -0.5)
TOPK_MAX = 256
B_DILATIONS = (1, 4, 16)
B_WINDOW_STEPS = 128
B_HEADS_PER_GROUP = 4
B_WIDTH = 3 * B_HEADS_PER_GROUP * HEAD_DIM
B_OUT_WIDTH = B_HEADS_PER_GROUP * HEAD_DIM
FFN_HIDDEN = 2816
LN_EPS = 1e-5
NEG = -1e30
MASK_NEG = -(2.0 ** 100)

LANES = 128
VMEM_LIMIT = 56 * 1024 * 1024
TM = 512
TQ = 256
TK = 256
TB = 256
FFN_CHUNK = 256
BISECT_STEPS_PER_CHECK = 2
WINDOW_TILES_PER_STEP = 8
MERGE_ROWS = 256
MERGE_COLS = 256
ROPE_TILE = 2048

_N_KA, _N_QB, _N_KB, _N_VB = 0, 768, 1536, 2304
_N_KI, _N_NAT = 3072, 3200
_T_QA, _T_VA, _T_QI, _T_WI, _T_ROWS = 0, 768, 1536, 2048, 2064


def _params(sem):
    return pltpu.CompilerParams(dimension_semantics=sem, vmem_limit_bytes=VMEM_LIMIT)


def _rope_kernel(pr_ref, fc_ref, c_ref, s_ref, ct_ref, st_ref):
    ang_t = fc_ref[...] * pr_ref[...].astype(F32)
    cos_t, sin_t = jnp.cos(ang_t), jnp.sin(ang_t)
    ct_ref[...] = cos_t
    st_ref[...] = sin_t
    reps = LANES // ROT_HALF
    cos = jnp.concatenate([cos_t] * reps, axis=0).T
    sin = jnp.concatenate([sin_t] * reps, axis=0).T
    d = lax.broadcasted_iota(jnp.int32, cos.shape, 1) & (HEAD_DIM - 1)
    c_ref[...] = jnp.where(d < 2 * ROT_HALF, cos, 1.0)
    s_ref[...] = jnp.where(d < ROT_HALF, -sin, jnp.where(d < 2 * ROT_HALF, sin, 0.0))


def _rope_tables(positions):
    m = positions.size
    inv_freq = ROPE_THETA ** (-jnp.arange(0, 2 * ROT_HALF, 2, dtype=F32) / (2 * ROT_HALF))
    t = min(ROPE_TILE, m)
    return pl.pallas_call(
        _rope_kernel,
        grid=(m // t,),
        in_specs=[pl.BlockSpec((1, t), lambda i: (0, i)), pl.BlockSpec((ROT_HALF, 1), lambda i: (0, 0))],
        out_specs=[pl.BlockSpec((t, LANES), lambda i: (i, 0)), pl.BlockSpec((t, LANES), lambda i: (i, 0)),
                   pl.BlockSpec((ROT_HALF, t), lambda i: (0, i)), pl.BlockSpec((ROT_HALF, t), lambda i: (0, i))],
        out_shape=[jax.ShapeDtypeStruct((m, LANES), F32), jax.ShapeDtypeStruct((m, LANES), F32),
                   jax.ShapeDtypeStruct((ROT_HALF, m), F32), jax.ShapeDtypeStruct((ROT_HALF, m), F32)],
        compiler_params=_params(("parallel",)),
        name="rope_tables",
    )(positions.reshape(1, m), inv_freq.reshape(ROT_HALF, 1))


def _proj_nat_kernel(x_ref, c_ref, s_ref, w_ref, ka_ref, ki_ref, *rest):
    streams, y_ref = rest[:-1], rest[-1]
    xb = x_ref[0].astype(BF16)
    cos, sin = c_ref[...], s_ref[...]
    lane = lax.broadcasted_iota(jnp.int32, cos.shape, 1)
    first = (lane & (HEAD_DIM - 1)) < ROT_HALF

    def rope(y):
        partner = jnp.where(first, pltpu.roll(y, LANES - ROT_HALF, 1), pltpu.roll(y, ROT_HALF, 1))
        return y * cos + partner * sin

    def proj(lo, hi):
        return jnp.dot(xb, w_ref[:, lo:hi], preferred_element_type=F32)

    blocks = A_WIDTH // LANES
    y = proj(_N_KA, _N_KA + A_WIDTH)
    for p in range(blocks):
        ka_ref[0, p] = rope(y[:, p * LANES:(p + 1) * LANES]).astype(BF16)

    def scatter_streams(which):
        per_group = B_OUT_WIDTH // LANES
        for g, d in enumerate(B_DILATIONS):
            out = streams[3 * g + which]
            for r in range(d):
                for p in range(per_group):
                    rows = y_ref[g * per_group + p, pl.ds(r, TM // d, stride=d), :]
                    out[0, r, :, p * LANES:(p + 1) * LANES] = rows.astype(BF16)

    y = proj(_N_QB, _N_QB + B_WIDTH)
    for p in range(blocks):
        y_ref[p] = rope(y[:, p * LANES:(p + 1) * LANES]) * (ATTN_SCALE * LOG2E)
    scatter_streams(0)
    y = proj(_N_KB, _N_KB + B_WIDTH)
    for p in range(blocks):
        y_ref[p] = rope(y[:, p * LANES:(p + 1) * LANES])
    scatter_streams(1)
    y = proj(_N_VB, _N_VB + B_WIDTH)
    for p in range(blocks):
        y_ref[p] = y[:, p * LANES:(p + 1) * LANES]
    scatter_streams(2)
    r = rope(proj(_N_KI, _N_KI + LANES))
    hi = r.astype(BF16).astype(F32)
    hl = jnp.where(lane < IDX_DIM, hi, r - hi).astype(BF16)
    ki_ref[0, :, 0:LANES] = hl
    ki_ref[0, :, LANES:2 * LANES] = hl


def _proj_nat(x, cos, sin, w_nat):
    b, s, d = x.shape
    nt = s // TM
    tok = lambda width: pl.BlockSpec((1, TM, width), lambda i, j: (i, j, 0))
    stream_specs, stream_shapes = [], []
    for dil in B_DILATIONS:
        for _ in range(3):
            stream_specs.append(pl.BlockSpec((1, dil, TM // dil, B_OUT_WIDTH), lambda i, j: (i, 0, j, 0)))
            stream_shapes.append(jax.ShapeDtypeStruct((b, dil, s // dil, B_OUT_WIDTH), BF16))
    return pl.pallas_call(
        _proj_nat_kernel,
        grid=(b, nt),
        in_specs=[tok(d),
                  pl.BlockSpec((TM, LANES), lambda i, j: (i * nt + j, 0)),
                  pl.BlockSpec((TM, LANES), lambda i, j: (i * nt + j, 0)),
                  pl.BlockSpec((d, _N_NAT), lambda i, j: (0, 0))],
        out_specs=[pl.BlockSpec((1, A_WIDTH // LANES, TM, LANES), lambda i, j: (i, 0, j, 0)),
                   tok(2 * LANES)] + stream_specs,
        out_shape=[jax.ShapeDtypeStruct((b, A_WIDTH // LANES, s, LANES), BF16),
                   jax.ShapeDtypeStruct((b, s, 2 * LANES), BF16)] + stream_shapes,
        scratch_shapes=[pltpu.VMEM((B_WIDTH // LANES, TM, LANES), F32)],
        compiler_params=_params(("parallel", "parallel")),
        name="proj_token_major",
    )(x, cos, sin, w_nat)


def _proj_t_kernel(x_ref, ct_ref, st_ref, w_ref, qt_ref, vt_ref, qi_ref, wt_ref):
    xb = x_ref[0].astype(BF16)
    cos, sin = ct_ref[...], st_ref[...]

    def proj(lo, hi):
        return lax.dot_general(w_ref[lo:hi, :], xb, (((1,), (1,)), ((), ())), preferred_element_type=F32)

    def rope_head(y):
        x1, x2 = y[0:ROT_HALF], y[ROT_HALF:2 * ROT_HALF]
        return jnp.concatenate([x1 * cos - x2 * sin, x2 * cos + x1 * sin, y[2 * ROT_HALF:]], axis=0)

    y = proj(_T_QA, _T_QA + A_WIDTH)
    for h in range(A_HEADS):
        r = rope_head(y[h * HEAD_DIM:(h + 1) * HEAD_DIM])
        qt_ref[0, h * HEAD_DIM:(h + 1) * HEAD_DIM, :] = (r * (ATTN_SCALE * LOG2E)).astype(BF16)
    y = proj(_T_VA, _T_VA + A_WIDTH)
    for h in range(A_HEADS):
        for c in range(TM // TK):
            vt_ref[0, h, c, 0:HEAD_DIM, :] = y[h * HEAD_DIM:(h + 1) * HEAD_DIM, c * TK:(c + 1) * TK].astype(BF16)
            vt_ref[0, h, c, HEAD_DIM:V_ROWS, :] = jnp.ones((V_ROWS - HEAD_DIM, TK), BF16)
    y = proj(_T_QI, _T_QI + IDX_HEADS * IDX_DIM)
    for h in range(IDX_HEADS):
        r = rope_head(y[h * IDX_DIM:(h + 1) * IDX_DIM])
        hi = r.astype(BF16)
        lo = (r - hi.astype(F32)).astype(BF16)
        base = 4 * h * IDX_DIM
        qi_ref[0, base:base + IDX_DIM, :] = hi
        qi_ref[0, base + IDX_DIM:base + 2 * IDX_DIM, :] = hi
        qi_ref[0, base + 2 * IDX_DIM:base + 3 * IDX_DIM, :] = lo
        qi_ref[0, base + 3 * IDX_DIM:base + 4 * IDX_DIM, :] = lo
    wt_ref[0] = proj(_T_WI, _T_ROWS)[0:IDX_HEADS] * IDX_SCALE


def _proj_t(x, cos_t, sin_t, w_t):
    b, s, d = x.shape
    nt = s // TM
    return pl.pallas_call(
        _proj_t_kernel,
        grid=(b, nt),
        in_specs=[pl.BlockSpec((1, TM, d), lambda i, j: (i, j, 0)),
                  pl.BlockSpec((ROT_HALF, TM), lambda i, j: (0, i * nt + j)),
                  pl.BlockSpec((ROT_HALF, TM), lambda i, j: (0, i * nt + j)),
                  pl.BlockSpec((_T_ROWS, d), lambda i, j: (0, 0))],
        out_specs=[pl.BlockSpec((1, A_WIDTH, TM), lambda i, j: (i, 0, j)),
                   pl.BlockSpec((1, A_HEADS, TM // TK, V_ROWS, TK), lambda i, j: (i, 0, j, 0, 0)),
                   pl.BlockSpec((1, 4 * IDX_HEADS * IDX_DIM, TM), lambda i, j: (i, 0, j)),
                   pl.BlockSpec((1, IDX_HEADS, TM), lambda i, j: (i, 0, j))],
        out_shape=[jax.ShapeDtypeStruct((b, A_WIDTH, s), BF16),
                   jax.ShapeDtypeStruct((b, A_HEADS, s // TK, V_ROWS, TK), BF16),
                   jax.ShapeDtypeStruct((b, 4 * IDX_HEADS * IDX_DIM, s), BF16),
                   jax.ShapeDtypeStruct((b, IDX_HEADS, s), F32)],
        compiler_params=_params(("parallel", "parallel")),
        name="proj_feature_major",
    )(x, cos_t, sin_t, w_t)


def _dsa_kernel(qt_ref, qi_ref, wt_ref, k_ref, vt_ref, ki_ref, o_ref,
                sc_ref, qpad_ref, ot_ref, m_ref, st_ref, bias_ref, sa_ref, sb_ref):
    j = pl.program_id(1)
    nkt = j + 1

    for h in range(A_HEADS):
        off = (h % 2) * HEAD_DIM
        qpad_ref[h] = jnp.zeros((2 * HEAD_DIM, TQ), BF16)
        qpad_ref[h, off:off + HEAD_DIM, :] = qt_ref[0, h * HEAD_DIM:(h + 1) * HEAD_DIM, :]

    qpos = j * TQ + lax.broadcasted_iota(jnp.int32, (1, TQ), 1)
    row_iota = lax.broadcasted_iota(jnp.int32, (TK, TQ), 0)
    w = wt_ref[0]

    def fold(t):
        return t.reshape(TK // 8, 8, TQ)

    def score_tile(kt, weight, carry):
        mn, mx, pos, nn = carry
        ki = ki_ref[0, pl.ds(pl.multiple_of(kt * TK, TK), TK), :]
        acc = jnp.zeros((TK, TQ), F32)
        for h in range(IDX_HEADS):
            s = jnp.dot(ki, qi_ref[0, 4 * h * IDX_DIM:4 * (h + 1) * IDX_DIM, :], preferred_element_type=F32)
            acc = acc + w[h:h + 1, :] * jnp.maximum(s, 0.0)
        causal = (kt * TK + row_iota) <= qpos
        val = jnp.where(causal, acc, -jnp.inf)
        sc_ref[kt] = val
        return (jnp.minimum(mn, fold(jnp.where(causal, acc, jnp.inf)).min(axis=0)),
                jnp.maximum(mx, fold(val).max(axis=0)),
                pos + weight * fold(jnp.where(val > 0.0, 1.0, 0.0)).sum(axis=0),
                nn + weight * fold(jnp.where(val >= 0.0, 1.0, 0.0)).sum(axis=0))

    def score_pair(i, carry):
        carry = score_tile(2 * i, 1.0, carry)
        second = jnp.minimum(2 * i + 1, nkt - 1)
        return score_tile(second, jnp.where(2 * i + 1 < nkt, 1.0, 0.0), carry)

    zeros8 = jnp.zeros((8, TQ), F32)
    mn8, mx8, pos8, nn8 = lax.fori_loop(0, (nkt + 1) // 2, score_pair,
                                        (jnp.full((8, TQ), jnp.inf, F32), jnp.full((8, TQ), -jnp.inf, F32),
                                         zeros8, zeros8))
    mn = mn8.min(axis=0, keepdims=True)
    mx = mx8.max(axis=0, keepdims=True)
    c_pos = pos8.sum(axis=0, keepdims=True)
    c_nn = nn8.sum(axis=0, keepdims=True)

    sc_ref[nkt] = jnp.full((TK, TQ), -jnp.inf, F32)

    def count(preds):
        def part(p, kt):
            return p(sc_ref[kt]).reshape(TK // 8, 8, TQ).sum(axis=0)

        def body(i, cnts):
            return tuple(c + part(p, 2 * i) + part(p, 2 * i + 1) for c, p in zip(cnts, preds))

        cnts = lax.fori_loop(0, (nkt + 1) // 2, body, tuple(jnp.zeros((8, TQ), F32) for _ in preds))
        return [c.sum(axis=0, keepdims=True) for c in cnts]

    def ge(cand):
        return lambda t: jnp.where(t >= cand, 1.0, 0.0)

    n_causal = (qpos + 1).astype(F32)
    k_q = jnp.minimum(n_causal, float(TOPK_MAX))
    c_mx, = count([ge(mx)])
    select = n_causal > k_q
    at_max = select & (c_mx >= k_q)
    search = select & (c_mx < k_q)
    at_zero = search & (c_pos < k_q) & (c_nn >= k_q)
    above = search & (c_pos >= k_q)
    below = search & (c_nn < k_q)
    lo0 = jnp.where(at_max, mx, jnp.where(at_zero | above, 0.0, mn))
    c_lo0 = jnp.where(at_max, c_mx, jnp.where(at_zero | above, c_nn, n_causal))
    st_ref[0:1, :] = lo0
    st_ref[1:2, :] = jnp.where(below, 0.0, mx)
    st_ref[2:3, :] = c_lo0
    st_ref[3:4, :] = jnp.where(at_max, 0.0, jnp.where(at_zero, c_pos, jnp.where(below, c_nn, c_mx)))
    act0 = jnp.where((above | below) & (c_lo0 > k_q), 1.0, 0.0)
    st_ref[4:5, :] = act0

    def bisect_once():
        lo, hi, c_lo, c_hi = st_ref[0:1, :], st_ref[1:2, :], st_ref[2:3, :], st_ref[3:4, :]
        act = st_ref[4:5, :] > 0.0
        mid = 0.5 * lo + 0.5 * hi
        live = act & (mid > lo) & (mid < hi)
        c, = count([ge(mid)])
        up = live & (c >= k_q)
        dn = live & (c < k_q)
        c_lo = jnp.where(up, c, c_lo)
        st_ref[0:1, :] = jnp.where(up, mid, lo)
        st_ref[1:2, :] = jnp.where(dn, mid, hi)
        st_ref[2:3, :] = c_lo
        st_ref[3:4, :] = jnp.where(dn, c, c_hi)
        act_new = jnp.where(live & (c_lo > k_q), 1.0, 0.0)
        st_ref[4:5, :] = act_new
        return act_new

    def bisect(go):
        for _ in range(BISECT_STEPS_PER_CHECK - 1):
            bisect_once()
        return (jnp.max(bisect_once()) > 0.0).astype(jnp.int32)

    lax.while_loop(lambda go: go > 0, bisect, (jnp.max(act0) > 0.0).astype(jnp.int32))

    lo = st_ref[0:1, :]
    tie = st_ref[2:3, :] > k_q

    @pl.when(jnp.max(jnp.where(tie, 1.0, 0.0)) > 0.0)
    def _():
        need = jnp.where(tie, k_q - st_ref[3:4, :], jnp.inf)
        tri = jnp.where(lax.broadcasted_iota(jnp.int32, (TK, TK), 0) >= lax.broadcasted_iota(jnp.int32, (TK, TK), 1),
                        1.0, 0.0).astype(BF16)

        def drop(kt, seen):
            t = sc_ref[kt]
            eq = jnp.where(t == lo, 1.0, 0.0)
            rank = jnp.dot(tri, eq.astype(BF16), preferred_element_type=F32) + seen
            sc_ref[kt] = jnp.where(t == lo, jnp.where(rank > need, -jnp.inf, t), t)
            return seen + fold(eq).sum(axis=0).sum(axis=0, keepdims=True)

        def drop_pair(i, seen):
            return drop(2 * i + 1, drop(2 * i, seen))

        lax.fori_loop(0, (nkt + 1) // 2, drop_pair, jnp.zeros((1, TQ), F32))

    def to_bias(kt, _):
        bias_ref[kt] = jnp.where(sc_ref[kt] >= lo, 0.0, MASK_NEG).astype(BF16)
        return 0

    lax.fori_loop(0, nkt, to_bias, 0)

    m_ref[...] = jnp.full(m_ref.shape, MASK_NEG, F32)
    ot_ref[...] = jnp.zeros(ot_ref.shape, F32)

    def logits(h, kt, bias):
        rows = pl.ds(pl.multiple_of(kt * TK, TK), TK)
        return jnp.dot(k_ref[0, h // 2, rows, :], qpad_ref[h], preferred_element_type=F32).astype(BF16) + bias

    def half_step(kt, cur_ref, nxt_ref):
        kn = jnp.minimum(kt + 1, nkt - 1)
        bias_n = bias_ref[kn]
        for h in range(A_HEADS):
            s = cur_ref[h]
            nxt_ref[h] = logits(h, kn, bias_n)
            m = m_ref[h]
            m_tile = s.reshape(TK // 16, 16, TQ).max(axis=0).astype(F32).max(axis=0, keepdims=True)
            m_new = jnp.maximum(m, m_tile)
            p = jnp.exp2(s - m_new.astype(BF16))
            corr = jnp.exp2(m - m_new)
            m_ref[h] = m_new
            ot_ref[h] = ot_ref[h] * corr + jnp.dot(vt_ref[0, h, kt], p, preferred_element_type=F32)

    bias0 = bias_ref[0]
    for h in range(A_HEADS):
        sa_ref[h] = logits(h, 0, bias0)

    def kv_pair(i, _):
        half_step(2 * i, sa_ref, sb_ref)

        @pl.when(2 * i + 1 < nkt)
        def _():
            half_step(2 * i + 1, sb_ref, sa_ref)

        return 0

    lax.fori_loop(0, (nkt + 1) // 2, kv_pair, 0)

    for p in range(A_HEADS // 2):
        both = jnp.concatenate([ot_ref[h, 0:HEAD_DIM, :] / ot_ref[h, HEAD_DIM:HEAD_DIM + 1, :]
                                for h in (2 * p, 2 * p + 1)], axis=0)
        o_ref[0, :, p * LANES:(p + 1) * LANES] = both.T.astype(BF16)


def _dsa_attention(qt, qi4t, wt, k6, vt5, ki4):
    b, _, s = qt.shape
    nq = s // TQ
    return pl.pallas_call(
        _dsa_kernel,
        grid=(b, nq),
        in_specs=[pl.BlockSpec((1, A_WIDTH, TQ), lambda i, j: (i, 0, j)),
                  pl.BlockSpec((1, 4 * IDX_HEADS * IDX_DIM, TQ), lambda i, j: (i, 0, j)),
                  pl.BlockSpec((1, IDX_HEADS, TQ), lambda i, j: (i, 0, j)),
                  pl.BlockSpec((1, A_WIDTH // LANES, s, LANES), lambda i, j: (i, 0, 0, 0)),
                  pl.BlockSpec((1, A_HEADS, s // TK, V_ROWS, TK), lambda i, j: (i, 0, 0, 0, 0)),
                  pl.BlockSpec((1, s, 4 * IDX_DIM), lambda i, j: (i, 0, 0))],
        out_specs=pl.BlockSpec((1, TQ, A_WIDTH), lambda i, j: (i, j, 0)),
        out_shape=jax.ShapeDtypeStruct((b, s, A_WIDTH), BF16),
        scratch_shapes=[pltpu.VMEM((s // TK + 1, TK, TQ), F32),
                        pltpu.VMEM((A_HEADS, 2 * HEAD_DIM, TQ), BF16),
                        pltpu.VMEM((A_HEADS, V_ROWS, TQ), F32),
                        pltpu.VMEM((A_HEADS, 1, TQ), F32),
                        pltpu.VMEM((8, TQ), F32),
                        pltpu.VMEM((s // TK, TK, TQ), BF16),
                        pltpu.VMEM((A_HEADS, TK, TQ), BF16),
                        pltpu.VMEM((A_HEADS, TK, TQ), BF16)],
        compiler_params=_params(("parallel", "arbitrary")),
        name="dsa_attention",
    )(qt, qi4t, wt, k6, vt5, ki4)


def _window_kernel(q_ref, kp_ref, kc_ref, vp_ref, vc_ref, o_ref, lse_ref):
    hb = B_WINDOW_STEPS
    n_streams, n_rows = q_ref.shape[1], q_ref.shape[2]
    r = lax.broadcasted_iota(jnp.int32, (hb, 2 * hb), 0)
    c = lax.broadcasted_iota(jnp.int32, (hb, 2 * hb), 1)
    dist = r + hb - c
    band = jnp.where(dist >= 0, jnp.where(dist <= B_WINDOW_STEPS, 0.0, NEG), NEG)
    first_col = jnp.where(pl.program_id(2) > 0, 0, hb)
    band_first = jnp.where(c >= first_col, band, NEG)
    lane = lax.broadcasted_iota(jnp.int32, (hb, LANES), 1)
    left = lane < HEAD_DIM
    ones = jnp.ones((2 * hb, LANES), BF16)
    for sb in range(n_streams):
        for t in range(n_rows // TB):
            for p in range(B_HEADS_PER_GROUP // 2):
                cols = slice(p * LANES, (p + 1) * LANES)
                tile = slice(t * TB, (t + 1) * TB)
                q2, kc, vc = q_ref[0, sb, tile, cols], kc_ref[0, sb, tile, cols], vc_ref[0, sb, tile, cols]
                if t == 0:
                    k_before, v_before = kp_ref[0, sb, :, cols], vp_ref[0, sb, :, cols]
                else:
                    before = slice(t * TB - hb, t * TB)
                    k_before, v_before = kc_ref[0, sb, before, cols], vc_ref[0, sb, before, cols]
                windows = ((jnp.concatenate([k_before, kc[:hb]], axis=0),
                            jnp.concatenate([v_before, vc[:hb]], axis=0)), (kc, vc))
                for half, ((kw, vw), bias) in enumerate(zip(windows, (band_first if t == 0 else band, band))):
                    rows = slice(t * TB + half * hb, t * TB + (half + 1) * hb)
                    v1 = jnp.concatenate([vw, ones], axis=1)
                    outs, lses = [], []
                    for side in (left, ~left):
                        qh = jnp.where(side, q2[half * hb:(half + 1) * hb], jnp.zeros((hb, LANES), BF16))
                        s = lax.dot_general(qh, kw, (((1,), (1,)), ((), ())), preferred_element_type=F32) + bias
                        m = s.max(axis=-1, keepdims=True)
                        e = jnp.exp2((s - m).astype(BF16))
                        ol = jnp.dot(e, v1, preferred_element_type=F32)
                        l = ol[:, LANES:]
                        outs.append(ol[:, :LANES] / l)
                        lses.append(m + jnp.log2(l))
                    o_ref[0, sb, rows, cols] = jnp.where(left, outs[0], outs[1])
                    lse_ref[0, sb, rows, cols] = jnp.where(left, lses[0], lses[1])


def _window_attention(q, k, v):
    b, d, n, _ = q.shape
    assert TB == 2 * B_WINDOW_STEPS
    rows = min(n, WINDOW_TILES_PER_STEP * TB)
    streams = min(d, WINDOW_TILES_PER_STEP * TB // rows)
    halves = rows // (TB // 2)
    cur = pl.BlockSpec((1, streams, rows, B_OUT_WIDTH), lambda bi, ri, ti: (bi, ri, ti, 0))
    prev = pl.BlockSpec((1, streams, TB // 2, B_OUT_WIDTH),
                        lambda bi, ri, ti: (bi, ri, jnp.maximum(halves * ti - 1, 0), 0))
    return pl.pallas_call(
        _window_kernel,
        grid=(b, d // streams, n // rows),
        in_specs=[cur, prev, cur, prev, cur],
        out_specs=[cur, cur],
        out_shape=[jax.ShapeDtypeStruct((b, d, n, B_OUT_WIDTH), F32)] * 2,
        compiler_params=_params(("parallel", "parallel", "arbitrary")),
        name=f"window_attention_d{d}",
    )(q, k, k, v, v)


def _layer_norm(y, g, b):
    mu = y.mean(axis=-1, keepdims=True)
    yc = y - mu
    var = (yc * yc).mean(axis=-1, keepdims=True)
    return yc * lax.rsqrt(var + LN_EPS) * g + b


def _merge_kernel(alpha, oa_ref, o0_ref, o1_ref, o2_ref, l0_ref, l1_ref, l2_ref, x_ref,
                  wg_ref, bg_ref, wa_ref, wb_ref, wo_ref, g_ref, b_ref, y_ref,
                  tok_ref, ob_ref, xb_ref, mg_ref):
    halves = B_OUT_WIDTH // LANES

    for n, ref in enumerate((o1_ref, o2_ref, l1_ref, l2_ref)):
        d = ref.shape[1]
        for r in range(d):
            for p in range(halves):
                tok_ref[n, p, pl.ds(r, TM // d, stride=d), :] = ref[0, r, :, p * LANES:(p + 1) * LANES]
    for c in range(TM // MERGE_ROWS):
        rows = slice(c * MERGE_ROWS, (c + 1) * MERGE_ROWS)
        for p in range(halves):
            cols = slice(p * LANES, (p + 1) * LANES)
            l0, l1, l2 = l0_ref[0, 0, rows, cols], tok_ref[2, p, rows, :], tok_ref[3, p, rows, :]
            lm = jnp.maximum(jnp.maximum(l0, l1), l2)
            e0, e1, e2 = jnp.exp2(l0 - lm), jnp.exp2(l1 - lm), jnp.exp2(l2 - lm)
            ob = (e0 * o0_ref[0, 0, rows, cols] + e1 * tok_ref[0, p, rows, :] + e2 * tok_ref[1, p, rows, :])
            ob_ref[rows, cols] = (ob / (e0 + e1 + e2)).astype(BF16)

    xb_ref[...] = x_ref[0].astype(BF16)
    for c in range(D_MODEL // MERGE_COLS):
        cols = slice(c * MERGE_COLS, (c + 1) * MERGE_COLS)
        gcols = slice(D_MODEL + c * MERGE_COLS, D_MODEL + (c + 1) * MERGE_COLS)

        def gate(sel):
            z = jnp.dot(xb_ref[...], wg_ref[:, sel], preferred_element_type=F32) + bg_ref[:, sel]
            return 1.0 / (1.0 + jnp.exp(-z))

        pa = jnp.dot(oa_ref[0], wa_ref[:, cols], preferred_element_type=F32)
        pb = jnp.dot(ob_ref[...], wb_ref[:, cols], preferred_element_type=F32)
        mg_ref[:, cols] = (gate(cols) * pa + gate(gcols) * pb).astype(BF16)

    for c in range(TM // MERGE_ROWS):
        rows = slice(c * MERGE_ROWS, (c + 1) * MERGE_ROWS)
        mixed = jnp.dot(mg_ref[rows, :], wo_ref[...], preferred_element_type=F32)
        y_ref[0, rows, :] = _layer_norm(alpha * x_ref[0, rows, :] + mixed, g_ref[...], b_ref[...])


def _merge(alpha, oa, obs, lses, x, w_gate, b_gate, wa, wb, wo, g, bb):
    b, s, _ = x.shape
    tok = lambda width: pl.BlockSpec((1, TM, width), lambda i, j: (i, j, 0))
    full = lambda a: pl.BlockSpec(a.shape, lambda i, j: (0, 0))
    streams = [pl.BlockSpec((1, a.shape[1], TM // a.shape[1], B_OUT_WIDTH), lambda i, j: (i, 0, j, 0))
               for a in list(obs) + list(lses)]
    return pl.pallas_call(
        functools.partial(_merge_kernel, alpha),
        grid=(b, s // TM),
        in_specs=[tok(A_WIDTH)] + streams + [tok(D_MODEL), full(w_gate),
                  full(b_gate), full(wa), full(wb), full(wo), full(g), full(bb)],
        out_specs=tok(D_MODEL),
        out_shape=jax.ShapeDtypeStruct((b, s, D_MODEL), F32),
        scratch_shapes=[pltpu.VMEM((4, B_OUT_WIDTH // LANES, TM, LANES), F32),
                        pltpu.VMEM((TM, B_OUT_WIDTH), BF16),
                        pltpu.VMEM((TM, D_MODEL), BF16),
                        pltpu.VMEM((TM, D_MODEL), BF16)],
        compiler_params=_params(("parallel", "parallel")),
        name="merge_outproj_norm",
    )(oa, *obs, *lses, x, w_gate, b_gate, wa, wb, wo, g, bb)


def _ffn_kernel(alpha, x_ref, wg_ref, wu_ref, wd_ref, g_ref, b_ref, y_ref):
    x = x_ref[...]
    xb = x.astype(BF16)
    acc = jnp.zeros((TM, D_MODEL), F32)
    for c in range(FFN_HIDDEN // FFN_CHUNK):
        cols = slice(c * FFN_CHUNK, (c + 1) * FFN_CHUNK)
        gate = jnp.dot(xb, wg_ref[:, cols], preferred_element_type=F32)
        up = jnp.dot(xb, wu_ref[:, cols], preferred_element_type=F32)
        h = gate / (1.0 + jnp.exp(-gate)) * up
        acc = acc + jnp.dot(h.astype(BF16), wd_ref[cols, :], preferred_element_type=F32)
    y_ref[...] = _layer_norm(alpha * x + acc, g_ref[...], b_ref[...])


def _ffn(alpha, x2, wg, wu, wd, g, bb):
    m = x2.shape[0]
    tok = pl.BlockSpec((TM, D_MODEL), lambda i: (i, 0))
    full = lambda a: pl.BlockSpec(a.shape, lambda i: (0, 0))
    return pl.pallas_call(
        functools.partial(_ffn_kernel, alpha),
        grid=(m // TM,),
        in_specs=[tok, full(wg), full(wu), full(wd), full(g), full(bb)],
        out_specs=tok,
        out_shape=jax.ShapeDtypeStruct((m, D_MODEL), F32),
        compiler_params=_params(("parallel",)),
        name="swiglu_norm",
    )(x2, wg, wu, wd, g, bb)


def _split_w_in(w):
    a, bw, hi = A_WIDTH, B_WIDTH, IDX_HEADS * IDX_DIM
    o = 0
    qa, o = w[:, o:o + a], o + a
    ka, o = w[:, o:o + a], o + a
    va, o = w[:, o:o + a], o + a
    qb, o = w[:, o:o + bw], o + bw
    kb, o = w[:, o:o + bw], o + bw
    vb, o = w[:, o:o + bw], o + bw
    qi, o = w[:, o:o + hi], o + hi
    ki, o = w[:, o:o + IDX_DIM], o + IDX_DIM
    wi, o = w[:, o:o + IDX_HEADS], o + IDX_HEADS
    w_gate = w[:, o:].astype(BF16)
    w_nat = jnp.concatenate([ka, qb, kb, vb, ki, ki], axis=1).astype(BF16)
    pad = jnp.zeros((w.shape[0], _T_ROWS - _T_WI - IDX_HEADS), w.dtype)
    w_t = jnp.concatenate([qa, va, qi, wi, pad], axis=1).T.astype(BF16)
    return w_nat, w_t, w_gate


def kernel(x, positions, w_in, b_gate, w_branch_a, w_branch_b, w_out, ln1_g, ln1_b,
           w_ffn_gate, w_ffn_up, w_ffn_down, ln2_g, ln2_b):
    b, s, d = x.shape
    depth = w_in.shape[0]
    assert d == D_MODEL and s % (max(B_DILATIONS) * TB) == 0 and s % TM == 0
    alpha = (2 * depth) ** 0.25
    cos, sin, cos_t, sin_t = _rope_tables(positions)
    row = lambda v: v.reshape(1, -1)
    for layer in range(depth):
        w_nat, w_t, w_gate = _split_w_in(w_in[layer])
        ka6, ki4, *streams = _proj_nat(x, cos, sin, w_nat)
        qt, vt5, qi4t, wt = _proj_t(x, cos_t, sin_t, w_t)
        oa = _dsa_attention(qt, qi4t, wt, ka6, vt5, ki4)
        groups = [_window_attention(*streams[3 * g:3 * g + 3]) for g in range(len(B_DILATIONS))]
        x1 = _merge(alpha, oa, [o for o, _ in groups], [l for _, l in groups], x, w_gate, row(b_gate[layer]),
                    w_branch_a[layer].astype(BF16), w_branch_b[layer].astype(BF16),
                    w_out[layer].astype(BF16), row(ln1_g[layer]), row(ln1_b[layer]))
        x2 = _ffn(alpha, x1.reshape(b * s, d), w_ffn_gate[layer].astype(BF16), w_ffn_up[layer].astype(BF16),
                  w_ffn_down[layer].astype(BF16), row(ln2_g[layer]), row(ln2_b[layer]))
        x = x2.reshape(b, s, d)
    return x
```

```python
import functools

import jax
import jax.numpy as jnp
from jax import lax
from jax.experimental import pallas as pl
from jax.experimental.pallas import tpu as pltpu

F32 = jnp.float32
BF16 = jnp.bfloat16

D_MODEL = 1024
HEAD_DIM = 64
ROT_HALF = 8
ROPE_THETA = 500000.0
ATTN_SCALE = HEAD_DIM ** -0.5
LOG2E = 1.4426950408889634
V_ROWS = HEAD_DIM + 16
A_HEADS = 12
A_WIDTH = A_HEADS * HEAD_DIM
IDX_HEADS = 8
IDX_DIM = 64
IDX_SCALE = (IDX_HEADS ** -0.5) * (IDX_DIM ** -0.5)
TOPK_MAX = 256
B_DILATIONS = (1, 4, 16)
B_WINDOW_STEPS = 128
B_HEADS_PER_GROUP = 4
B_WIDTH = 3 * B_HEADS_PER_GROUP * HEAD_DIM
B_OUT_WIDTH = B_HEADS_PER_GROUP * HEAD_DIM
FFN_HIDDEN = 2816
LN_EPS = 1e-5
NEG = -1e30
MASK_NEG = -(2.0 ** 100)

LANES = 128
VMEM_LIMIT = 56 * 1024 * 1024
TM = 512
TQ = 256
TK = 256
TB = 256
FFN_CHUNK = 256
BISECT_STEPS_PER_CHECK = 2
WINDOW_TILES_PER_STEP = 8
MERGE_ROWS = 256
MERGE_COLS = 256
ROPE_TILE = 2048

_N_KA, _N_QB, _N_KB, _N_VB = 0, 768, 1536, 2304
_N_KI, _N_NAT = 3072, 3200
_T_QA, _T_VA, _T_QI, _T_WI, _T_ROWS = 0, 768, 1536, 2048, 2064


def _params(sem):
    return pltpu.CompilerParams(dimension_semantics=sem, vmem_limit_bytes=VMEM_LIMIT)


def _rope_kernel(pr_ref, fc_ref, c_ref, s_ref, ct_ref, st_ref):
    ang_t = fc_ref[...] * pr_ref[...].astype(F32)
    cos_t, sin_t = jnp.cos(ang_t), jnp.sin(ang_t)
    ct_ref[...] = cos_t
    st_ref[...] = sin_t
    reps = LANES // ROT_HALF
    cos = jnp.concatenate([cos_t] * reps, axis=0).T
    sin = jnp.concatenate([sin_t] * reps, axis=0).T
    d = lax.broadcasted_iota(jnp.int32, cos.shape, 1) & (HEAD_DIM - 1)
    c_ref[...] = jnp.where(d < 2 * ROT_HALF, cos, 1.0)
    s_ref[...] = jnp.where(d < ROT_HALF, -sin, jnp.where(d < 2 * ROT_HALF, sin, 0.0))


def _rope_tables(positions):
    m = positions.size
    inv_freq = ROPE_THETA ** (-jnp.arange(0, 2 * ROT_HALF, 2, dtype=F32) / (2 * ROT_HALF))
    t = min(ROPE_TILE, m)
    return pl.pallas_call(
        _rope_kernel,
        grid=(m // t,),
        in_specs=[pl.BlockSpec((1, t), lambda i: (0, i)), pl.BlockSpec((ROT_HALF, 1), lambda i: (0, 0))],
        out_specs=[pl.BlockSpec((t, LANES), lambda i: (i, 0)), pl.BlockSpec((t, LANES), lambda i: (i, 0)),
                   pl.BlockSpec((ROT_HALF, t), lambda i: (0, i)), pl.BlockSpec((ROT_HALF, t), lambda i: (0, i))],
        out_shape=[jax.ShapeDtypeStruct((m, LANES), F32), jax.ShapeDtypeStruct((m, LANES), F32),
                   jax.ShapeDtypeStruct((ROT_HALF, m), F32), jax.ShapeDtypeStruct((ROT_HALF, m), F32)],
        compiler_params=_params(("parallel",)),
        name="rope_tables",
    )(positions.reshape(1, m), inv_freq.reshape(ROT_HALF, 1))


def _proj_nat_kernel(x_ref, c_ref, s_ref, w_ref, ka_ref, ki_ref, *rest):
    streams, y_ref = rest[:-1], rest[-1]
    xb = x_ref[0].astype(BF16)
    cos, sin = c_ref[...], s_ref[...]
    lane = lax.broadcasted_iota(jnp.int32, cos.shape, 1)
    first = (lane & (HEAD_DIM - 1)) < ROT_HALF

    def rope(y):
        partner = jnp.where(first, pltpu.roll(y, LANES - ROT_HALF, 1), pltpu.roll(y, ROT_HALF, 1))
        return y * cos + partner * sin

    def proj(lo, hi):
        return jnp.dot(xb, w_ref[:, lo:hi], preferred_element_type=F32)

    blocks = A_WIDTH // LANES
    y = proj(_N_KA, _N_KA + A_WIDTH)
    for p in range(blocks):
        ka_ref[0, p] = rope(y[:, p * LANES:(p + 1) * LANES]).astype(BF16)

    def scatter_streams(which):
        per_group = B_OUT_WIDTH // LANES
        for g, d in enumerate(B_DILATIONS):
            out = streams[3 * g + which]
            for r in range(d):
                for p in range(per_group):
                    rows = y_ref[g * per_group + p, pl.ds(r, TM // d, stride=d), :]
                    out[0, r, :, p * LANES:(p + 1) * LANES] = rows.astype(BF16)

    y = proj(_N_QB, _N_QB + B_WIDTH)
    for p in range(blocks):
        y_ref[p] = rope(y[:, p * LANES:(p + 1) * LANES]) * (ATTN_SCALE * LOG2E)
    scatter_streams(0)
    y = proj(_N_KB, _N_KB + B_WIDTH)
    for p in range(blocks):
        y_ref[p] = rope(y[:, p * LANES:(p + 1) * LANES])
    scatter_streams(1)
    y = proj(_N_VB, _N_VB + B_WIDTH)
    for p in range(blocks):
        y_ref[p] = y[:, p * LANES:(p + 1) * LANES]
    scatter_streams(2)
    r = rope(proj(_N_KI, _N_KI + LANES))
    hi = r.astype(BF16).astype(F32)
    hl = jnp.where(lane < IDX_DIM, hi, r - hi).astype(BF16)
    ki_ref[0, :, 0:LANES] = hl
    ki_ref[0, :, LANES:2 * LANES] = hl


def _proj_nat(x, cos, sin, w_nat):
    b, s, d = x.shape
    nt = s // TM
    tok = lambda width: pl.BlockSpec((1, TM, width), lambda i, j: (i, j, 0))
    stream_specs, stream_shapes = [], []
    for dil in B_DILATIONS:
        for _ in range(3):
            stream_specs.append(pl.BlockSpec((1, dil, TM // dil, B_OUT_WIDTH), lambda i, j: (i, 0, j, 0)))
            stream_shapes.append(jax.ShapeDtypeStruct((b, dil, s // dil, B_OUT_WIDTH), BF16))
    return pl.pallas_call(
        _proj_nat_kernel,
        grid=(b, nt),
        in_specs=[tok(d),
                  pl.BlockSpec((TM, LANES), lambda i, j: (i * nt + j, 0)),
                  pl.BlockSpec((TM, LANES), lambda i, j: (i * nt + j, 0)),
                  pl.BlockSpec((d, _N_NAT), lambda i, j: (0, 0))],
        out_specs=[pl.BlockSpec((1, A_WIDTH // LANES, TM, LANES), lambda i, j: (i, 0, j, 0)),
                   tok(2 * LANES)] + stream_specs,
        out_shape=[jax.ShapeDtypeStruct((b, A_WIDTH // LANES, s, LANES), BF16),
                   jax.ShapeDtypeStruct((b, s, 2 * LANES), BF16)] + stream_shapes,
        scratch_shapes=[pltpu.VMEM((B_WIDTH // LANES, TM, LANES), F32)],
        compiler_params=_params(("parallel", "parallel")),
        name="proj_token_major",
    )(x, cos, sin, w_nat)


def _proj_t_kernel(x_ref, ct_ref, st_ref, w_ref, qt_ref, vt_ref, qi_ref, wt_ref):
    xb = x_ref[0].astype(BF16)
    cos, sin = ct_ref[...], st_ref[...]

    def proj(lo, hi):
        return lax.dot_general(w_ref[lo:hi, :], xb, (((1,), (1,)), ((), ())), preferred_element_type=F32)

    def rope_head(y):
        x1, x2 = y[0:ROT_HALF], y[ROT_HALF:2 * ROT_HALF]
        return jnp.concatenate([x1 * cos - x2 * sin, x2 * cos + x1 * sin, y[2 * ROT_HALF:]], axis=0)

    y = proj(_T_QA, _T_QA + A_WIDTH)
    for h in range(A_HEADS):
        r = rope_head(y[h * HEAD_DIM:(h + 1) * HEAD_DIM])
        qt_ref[0, h * HEAD_DIM:(h + 1) * HEAD_DIM, :] = (r * (ATTN_SCALE * LOG2E)).astype(BF16)
    y = proj(_T_VA, _T_VA + A_WIDTH)
    for h in range(A_HEADS):
        for c in range(TM // TK):
            vt_ref[0, h, c, 0:HEAD_DIM, :] = y[h * HEAD_DIM:(h + 1) * HEAD_DIM, c * TK:(c + 1) * TK].astype(BF16)
            vt_ref[0, h, c, HEAD_DIM:V_ROWS, :] = jnp.ones((V_ROWS - HEAD_DIM, TK), BF16)
    y = proj(_T_QI, _T_QI + IDX_HEADS * IDX_DIM)
    for h in range(IDX_HEADS):
        r = rope_head(y[h * IDX_DIM:(h + 1) * IDX_DIM])
        hi = r.astype(BF16)
        lo = (r - hi.astype(F32)).astype(BF16)
        base = 4 * h * IDX_DIM
        qi_ref[0, base:base + IDX_DIM, :] = hi
        qi_ref[0, base + IDX_DIM:base + 2 * IDX_DIM, :] = hi
        qi_ref[0, base + 2 * IDX_DIM:base + 3 * IDX_DIM, :] = lo
        qi_ref[0, base + 3 * IDX_DIM:base + 4 * IDX_DIM, :] = lo
    wt_ref[0] = proj(_T_WI, _T_ROWS)[0:IDX_HEADS] * IDX_SCALE


def _proj_t(x, cos_t, sin_t, w_t):
    b, s, d = x.shape
    nt = s // TM
    return pl.pallas_call(
        _proj_t_kernel,
        grid=(b, nt),
        in_specs=[pl.BlockSpec((1, TM, d), lambda i, j: (i, j, 0)),
                  pl.BlockSpec((ROT_HALF, TM), lambda i, j: (0, i * nt + j)),
                  pl.BlockSpec((ROT_HALF, TM), lambda i, j: (0, i * nt + j)),
                  pl.BlockSpec((_T_ROWS, d), lambda i, j: (0, 0))],
        out_specs=[pl.BlockSpec((1, A_WIDTH, TM), lambda i, j: (i, 0, j)),
                   pl.BlockSpec((1, A_HEADS, TM // TK, V_ROWS, TK), lambda i, j: (i, 0, j, 0, 0)),
                   pl.BlockSpec((1, 4 * IDX_HEADS * IDX_DIM, TM), lambda i, j: (i, 0, j)),
                   pl.BlockSpec((1, IDX_HEADS, TM), lambda i, j: (i, 0, j))],
        out_shape=[jax.ShapeDtypeStruct((b, A_WIDTH, s), BF16),
                   jax.ShapeDtypeStruct((b, A_HEADS, s // TK, V_ROWS, TK), BF16),
                   jax.ShapeDtypeStruct((b, 4 * IDX_HEADS * IDX_DIM, s), BF16),
                   jax.ShapeDtypeStruct((b, IDX_HEADS, s), F32)],
        compiler_params=_params(("parallel", "parallel")),
        name="proj_feature_major",
    )(x, cos_t, sin_t, w_t)


def _dsa_kernel(qt_ref, qi_ref, wt_ref, k_ref, vt_ref, ki_ref, o_ref,
                sc_ref, qpad_ref, ot_ref, m_ref, st_ref, stat_ref, sa_ref, sb_ref):
    j = pl.program_id(1)
    nkt = j + 1

    for h in range(A_HEADS):
        off = (h % 2) * HEAD_DIM
        qpad_ref[h] = jnp.zeros((2 * HEAD_DIM, TQ), BF16)
        qpad_ref[h, off:off + HEAD_DIM, :] = qt_ref[0, h * HEAD_DIM:(h + 1) * HEAD_DIM, :]

    qpos = j * TQ + lax.broadcasted_iota(jnp.int32, (1, TQ), 1)
    row_iota = lax.broadcasted_iota(jnp.int32, (TK, TQ), 0)
    w = wt_ref[0]

    def fold(t):
        return t.reshape(TK // 8, 8, TQ)

    def score_tile(kt, diagonal, carry):
        mn, mx, pos, nn = carry
        ki = ki_ref[0, pl.ds(pl.multiple_of(kt * TK, TK), TK), :]
        acc = jnp.zeros((TK, TQ), F32)
        for h in range(IDX_HEADS):
            s = jnp.dot(ki, qi_ref[0, 4 * h * IDX_DIM:4 * (h + 1) * IDX_DIM, :], preferred_element_type=F32)
            acc = acc + w[h:h + 1, :] * jnp.maximum(s, 0.0)
        if diagonal:
            causal = (kt * TK + row_iota) <= qpos
            val = jnp.where(causal, acc, -jnp.inf)
            low = jnp.where(causal, acc, jnp.inf)
        else:
            val = low = acc
        sc_ref[kt] = val
        return (jnp.minimum(mn, fold(low).min(axis=0)),
                jnp.maximum(mx, fold(val).max(axis=0)),
                pos + fold(jnp.where(val > 0.0, 1.0, 0.0)).sum(axis=0),
                nn + fold(jnp.where(val >= 0.0, 1.0, 0.0)).sum(axis=0))

    def load_stats():
        return tuple(stat_ref[n] for n in range(4))

    def store_stats(stats):
        for n, v in enumerate(stats):
            stat_ref[n] = v

    zeros8 = jnp.zeros((8, TQ), F32)
    store_stats(lax.fori_loop(
        0, j // 2, lambda i, c: score_tile(2 * i + 1, False, score_tile(2 * i, False, c)),
        (jnp.full((8, TQ), jnp.inf, F32), jnp.full((8, TQ), -jnp.inf, F32), zeros8, zeros8)))

    @pl.when(j % 2 == 1)
    def _():
        store_stats(score_tile(j, True, score_tile(j - 1, False, load_stats())))

    @pl.when(j % 2 == 0)
    def _():
        store_stats(score_tile(j, True, load_stats()))

    mn8, mx8, pos8, nn8 = load_stats()
    mn = mn8.min(axis=0, keepdims=True)
    mx = mx8.max(axis=0, keepdims=True)
    c_pos = pos8.sum(axis=0, keepdims=True)
    c_nn = nn8.sum(axis=0, keepdims=True)

    sc_ref[nkt] = jnp.full((TK, TQ), -jnp.inf, F32)

    def count(preds):
        def part(p, kt):
            return p(sc_ref[kt]).reshape(TK // 8, 8, TQ).sum(axis=0)

        def body(i, cnts):
            return tuple(c + part(p, 2 * i) + part(p, 2 * i + 1) for c, p in zip(cnts, preds))

        cnts = lax.fori_loop(0, (nkt + 1) // 2, body, tuple(jnp.zeros((8, TQ), F32) for _ in preds))
        return [c.sum(axis=0, keepdims=True) for c in cnts]

    def ge(cand):
        return lambda t: jnp.where(t >= cand, 1.0, 0.0)

    n_causal = (qpos + 1).astype(F32)
    k_q = jnp.minimum(n_causal, float(TOPK_MAX))
    c_mx, = count([ge(mx)])
    select = n_causal > k_q
    at_max = select & (c_mx >= k_q)
    search = select & (c_mx < k_q)
    at_zero = search & (c_pos < k_q) & (c_nn >= k_q)
    above = search & (c_pos >= k_q)
    below = search & (c_nn < k_q)
    lo0 = jnp.where(at_max, mx, jnp.where(at_zero | above, 0.0, mn))
    c_lo0 = jnp.where(at_max, c_mx, jnp.where(at_zero | above, c_nn, n_causal))
    st_ref[0:1, :] = lo0
    st_ref[1:2, :] = jnp.where(below, 0.0, mx)
    st_ref[2:3, :] = c_lo0
    st_ref[3:4, :] = jnp.where(at_max, 0.0, jnp.where(at_zero, c_pos, jnp.where(below, c_nn, c_mx)))
    act0 = jnp.where((above | below) & (c_lo0 > k_q), 1.0, 0.0)
    st_ref[4:5, :] = act0

    def bisect_once():
        lo, hi, c_lo, c_hi = st_ref[0:1, :], st_ref[1:2, :], st_ref[2:3, :], st_ref[3:4, :]
        act = st_ref[4:5, :] > 0.0
        mid = 0.5 * lo + 0.5 * hi
        live = act & (mid > lo) & (mid < hi)
        c, = count([ge(mid)])
        up = live & (c >= k_q)
        dn = live & (c < k_q)
        c_lo = jnp.where(up, c, c_lo)
        st_ref[0:1, :] = jnp.where(up, mid, lo)
        st_ref[1:2, :] = jnp.where(dn, mid, hi)
        st_ref[2:3, :] = c_lo
        st_ref[3:4, :] = jnp.where(dn, c, c_hi)
        act_new = jnp.where(live & (c_lo > k_q), 1.0, 0.0)
        st_ref[4:5, :] = act_new
        return act_new

    def bisect(go):
        for _ in range(BISECT_STEPS_PER_CHECK - 1):
            bisect_once()
        return (jnp.max(bisect_once()) > 0.0).astype(jnp.int32)

    lax.while_loop(lambda go: go > 0, bisect, (jnp.max(act0) > 0.0).astype(jnp.int32))

    lo = st_ref[0:1, :]
    tie = st_ref[2:3, :] > k_q

    @pl.when(jnp.max(jnp.where(tie, 1.0, 0.0)) > 0.0)
    def _():
        need = jnp.where(tie, k_q - st_ref[3:4, :], jnp.inf)
        tri = jnp.where(lax.broadcasted_iota(jnp.int32, (TK, TK), 0) >= lax.broadcasted_iota(jnp.int32, (TK, TK), 1),
                        1.0, 0.0).astype(BF16)

        def drop(kt, seen):
            t = sc_ref[kt]
            eq = jnp.where(t == lo, 1.0, 0.0)
            rank = jnp.dot(tri, eq.astype(BF16), preferred_element_type=F32) + seen
            sc_ref[kt] = jnp.where(t == lo, jnp.where(rank > need, -jnp.inf, t), t)
            return seen + fold(eq).sum(axis=0).sum(axis=0, keepdims=True)

        def drop_pair(i, seen):
            return drop(2 * i + 1, drop(2 * i, seen))

        lax.fori_loop(0, (nkt + 1) // 2, drop_pair, jnp.zeros((1, TQ), F32))

    m_ref[...] = jnp.full(m_ref.shape, MASK_NEG, F32)
    ot_ref[...] = jnp.zeros(ot_ref.shape, F32)

    def mask_bias(kt):
        return jnp.where(sc_ref[kt] >= lo, 0.0, MASK_NEG).astype(BF16)

    def logits(h, kt, bias):
        rows = pl.ds(pl.multiple_of(kt * TK, TK), TK)
        return jnp.dot(k_ref[0, h // 2, rows, :], qpad_ref[h], preferred_element_type=F32).astype(BF16) + bias

    def half_step(kt, cur_ref, nxt_ref):
        if nxt_ref is not None:
            bias_n = mask_bias(kt + 1)
        for h in range(A_HEADS):
            s = cur_ref[h]
            if nxt_ref is not None:
                nxt_ref[h] = logits(h, kt + 1, bias_n)
            m = m_ref[h]
            m_tile = s.reshape(TK // 16, 16, TQ).max(axis=0).astype(F32).max(axis=0, keepdims=True)
            m_new = jnp.maximum(m, m_tile)
            p = jnp.exp2(s - m_new.astype(BF16))
            corr = jnp.exp2(m - m_new)
            m_ref[h] = m_new
            ot_ref[h] = ot_ref[h] * corr + jnp.dot(vt_ref[0, h, kt], p, preferred_element_type=F32)

    bias0 = mask_bias(0)
    for h in range(A_HEADS):
        sa_ref[h] = logits(h, 0, bias0)

    def kv_pair(i, _):
        half_step(2 * i, sa_ref, sb_ref)
        half_step(2 * i + 1, sb_ref, sa_ref)
        return 0

    lax.fori_loop(0, (nkt - 1) // 2, kv_pair, 0)

    @pl.when(nkt % 2 == 1)
    def _():
        half_step(nkt - 1, sa_ref, None)

    @pl.when(nkt % 2 == 0)
    def _():
        half_step(nkt - 2, sa_ref, sb_ref)
        half_step(nkt - 1, sb_ref, None)

    for p in range(A_HEADS // 2):
        both = jnp.concatenate([ot_ref[h, 0:HEAD_DIM, :] / ot_ref[h, HEAD_DIM:HEAD_DIM + 1, :]
                                for h in (2 * p, 2 * p + 1)], axis=0)
        o_ref[0, :, p * LANES:(p + 1) * LANES] = both.T.astype(BF16)


def _dsa_attention(qt, qi4t, wt, k6, vt5, ki4):
    b, _, s = qt.shape
    nq = s // TQ
    return pl.pallas_call(
        _dsa_kernel,
        grid=(b, nq),
        in_specs=[pl.BlockSpec((1, A_WIDTH, TQ), lambda i, j: (i, 0, j)),
                  pl.BlockSpec((1, 4 * IDX_HEADS * IDX_DIM, TQ), lambda i, j: (i, 0, j)),
                  pl.BlockSpec((1, IDX_HEADS, TQ), lambda i, j: (i, 0, j)),
                  pl.BlockSpec((1, A_WIDTH // LANES, s, LANES), lambda i, j: (i, 0, 0, 0)),
                  pl.BlockSpec((1, A_HEADS, s // TK, V_ROWS, TK), lambda i, j: (i, 0, 0, 0, 0)),
                  pl.BlockSpec((1, s, 4 * IDX_DIM), lambda i, j: (i, 0, 0))],
        out_specs=pl.BlockSpec((1, TQ, A_WIDTH), lambda i, j: (i, j, 0)),
        out_shape=jax.ShapeDtypeStruct((b, s, A_WIDTH), BF16),
        scratch_shapes=[pltpu.VMEM((s // TK + 1, TK, TQ), F32),
                        pltpu.VMEM((A_HEADS, 2 * HEAD_DIM, TQ), BF16),
                        pltpu.VMEM((A_HEADS, V_ROWS, TQ), F32),
                        pltpu.VMEM((A_HEADS, 1, TQ), F32),
                        pltpu.VMEM((8, TQ), F32),
                        pltpu.VMEM((4, 8, TQ), F32),
                        pltpu.VMEM((A_HEADS, TK, TQ), BF16),
                        pltpu.VMEM((A_HEADS, TK, TQ), BF16)],
        compiler_params=_params(("parallel", "arbitrary")),
        name="dsa_attention",
    )(qt, qi4t, wt, k6, vt5, ki4)


def _window_kernel(q_ref, kp_ref, kc_ref, vp_ref, vc_ref, o_ref, lse_ref):
    hb = B_WINDOW_STEPS
    n_streams, n_rows = q_ref.shape[1], q_ref.shape[2]
    r = lax.broadcasted_iota(jnp.int32, (hb, 2 * hb), 0)
    c = lax.broadcasted_iota(jnp.int32, (hb, 2 * hb), 1)
    dist = r + hb - c
    band = jnp.where(dist >= 0, jnp.where(dist <= B_WINDOW_STEPS, 0.0, NEG), NEG)
    first_col = jnp.where(pl.program_id(2) > 0, 0, hb)
    band_first = jnp.where(c >= first_col, band, NEG)
    lane = lax.broadcasted_iota(jnp.int32, (hb, LANES), 1)
    left = lane < HEAD_DIM
    ones = jnp.ones((2 * hb, LANES), BF16)
    for sb in range(n_streams):
        for t in range(n_rows // TB):
            for p in range(B_HEADS_PER_GROUP // 2):
                cols = slice(p * LANES, (p + 1) * LANES)
                tile = slice(t * TB, (t + 1) * TB)
                q2, kc, vc = q_ref[0, sb, tile, cols], kc_ref[0, sb, tile, cols], vc_ref[0, sb, tile, cols]
                if t == 0:
                    k_before, v_before = kp_ref[0, sb, :, cols], vp_ref[0, sb, :, cols]
                else:
                    before = slice(t * TB - hb, t * TB)
                    k_before, v_before = kc_ref[0, sb, before, cols], vc_ref[0, sb, before, cols]
                windows = ((jnp.concatenate([k_before, kc[:hb]], axis=0),
                            jnp.concatenate([v_before, vc[:hb]], axis=0)), (kc, vc))
                for half, ((kw, vw), bias) in enumerate(zip(windows, (band_first if t == 0 else band, band))):
                    rows = slice(t * TB + half * hb, t * TB + (half + 1) * hb)
                    v1 = jnp.concatenate([vw, ones], axis=1)
                    outs, lses = [], []
                    for side in (left, ~left):
                        qh = jnp.where(side, q2[half * hb:(half + 1) * hb], jnp.zeros((hb, LANES), BF16))
                        s = lax.dot_general(qh, kw, (((1,), (1,)), ((), ())), preferred_element_type=F32) + bias
                        m = s.max(axis=-1, keepdims=True)
                        e = jnp.exp2((s - m).astype(BF16))
                        ol = jnp.dot(e, v1, preferred_element_type=F32)
                        l = ol[:, LANES:]
                        outs.append(ol[:, :LANES] / l)
                        lses.append(m + jnp.log2(l))
                    o_ref[0, sb, rows, cols] = jnp.where(left, outs[0], outs[1])
                    lse_ref[0, sb, rows, cols] = jnp.where(left, lses[0], lses[1])


def _window_attention(q, k, v):
    b, d, n, _ = q.shape
    assert TB == 2 * B_WINDOW_STEPS
    rows = min(n, WINDOW_TILES_PER_STEP * TB)
    streams = min(d, WINDOW_TILES_PER_STEP * TB // rows)
    halves = rows // (TB // 2)
    cur = pl.BlockSpec((1, streams, rows, B_OUT_WIDTH), lambda bi, ri, ti: (bi, ri, ti, 0))
    prev = pl.BlockSpec((1, streams, TB // 2, B_OUT_WIDTH),
                        lambda bi, ri, ti: (bi, ri, jnp.maximum(halves * ti - 1, 0), 0))
    return pl.pallas_call(
        _window_kernel,
        grid=(b, d // streams, n // rows),
        in_specs=[cur, prev, cur, prev, cur],
        out_specs=[cur, cur],
        out_shape=[jax.ShapeDtypeStruct((b, d, n, B_OUT_WIDTH), F32)] * 2,
        compiler_params=_params(("parallel", "parallel", "arbitrary")),
        name=f"window_attention_d{d}",
    )(q, k, k, v, v)


def _layer_norm(y, g, b):
    mu = y.mean(axis=-1, keepdims=True)
    yc = y - mu
    var = (yc * yc).mean(axis=-1, keepdims=True)
    return yc * lax.rsqrt(var + LN_EPS) * g + b


def _merge_kernel(alpha, oa_ref, o0_ref, o1_ref, o2_ref, l0_ref, l1_ref, l2_ref, x_ref,
                  wg_ref, bg_ref, wa_ref, wb_ref, wo_ref, g_ref, b_ref, y_ref,
                  tok_ref, ob_ref, xb_ref, mg_ref):
    halves = B_OUT_WIDTH // LANES

    for n, ref in enumerate((o1_ref, o2_ref, l1_ref, l2_ref)):
        d = ref.shape[1]
        for r in range(d):
            for p in range(halves):
                tok_ref[n, p, pl.ds(r, TM // d, stride=d), :] = ref[0, r, :, p * LANES:(p + 1) * LANES]
    for c in range(TM // MERGE_ROWS):
        rows = slice(c * MERGE_ROWS, (c + 1) * MERGE_ROWS)
        for p in range(halves):
            cols = slice(p * LANES, (p + 1) * LANES)
            l0, l1, l2 = l0_ref[0, 0, rows, cols], tok_ref[2, p, rows, :], tok_ref[3, p, rows, :]
            lm = jnp.maximum(jnp.maximum(l0, l1), l2)
            e0, e1, e2 = jnp.exp2(l0 - lm), jnp.exp2(l1 - lm), jnp.exp2(l2 - lm)
            ob = (e0 * o0_ref[0, 0, rows, cols] + e1 * tok_ref[0, p, rows, :] + e2 * tok_ref[1, p, rows, :])
            ob_ref[rows, cols] = (ob / (e0 + e1 + e2)).astype(BF16)

    xb_ref[...] = x_ref[0].astype(BF16)
    for c in range(D_MODEL // MERGE_COLS):
        cols = slice(c * MERGE_COLS, (c + 1) * MERGE_COLS)
        gcols = slice(D_MODEL + c * MERGE_COLS, D_MODEL + (c + 1) * MERGE_COLS)

        def gate(sel):
            z = jnp.dot(xb_ref[...], wg_ref[:, sel], preferred_element_type=F32) + bg_ref[:, sel]
            return 1.0 / (1.0 + jnp.exp(-z))

        pa = jnp.dot(oa_ref[0], wa_ref[:, cols], preferred_element_type=F32)
        pb = jnp.dot(ob_ref[...], wb_ref[:, cols], preferred_element_type=F32)
        mg_ref[:, cols] = (gate(cols) * pa + gate(gcols) * pb).astype(BF16)

    for c in range(TM // MERGE_ROWS):
        rows = slice(c * MERGE_ROWS, (c + 1) * MERGE_ROWS)
        mixed = jnp.dot(mg_ref[rows, :], wo_ref[...], preferred_element_type=F32)
        y_ref[0, rows, :] = _layer_norm(alpha * x_ref[0, rows, :] + mixed, g_ref[...], b_ref[...])


def _merge(alpha, oa, obs, lses, x, w_gate, b_gate, wa, wb, wo, g, bb):
    b, s, _ = x.shape
    tok = lambda width: pl.BlockSpec((1, TM, width), lambda i, j: (i, j, 0))
    full = lambda a: pl.BlockSpec(a.shape, lambda i, j: (0, 0))
    streams = [pl.BlockSpec((1, a.shape[1], TM // a.shape[1], B_OUT_WIDTH), lambda i, j: (i, 0, j, 0))
               for a in list(obs) + list(lses)]
    return pl.pallas_call(
        functools.partial(_merge_kernel, alpha),
        grid=(b, s // TM),
        in_specs=[tok(A_WIDTH)] + streams + [tok(D_MODEL), full(w_gate),
                  full(b_gate), full(wa), full(wb), full(wo), full(g), full(bb)],
        out_specs=tok(D_MODEL),
        out_shape=jax.ShapeDtypeStruct((b, s, D_MODEL), F32),
        scratch_shapes=[pltpu.VMEM((4, B_OUT_WIDTH // LANES, TM, LANES), F32),
                        pltpu.VMEM((TM, B_OUT_WIDTH), BF16),
                        pltpu.VMEM((TM, D_MODEL), BF16),
                        pltpu.VMEM((TM, D_MODEL), BF16)],
        compiler_params=_params(("parallel", "parallel")),
        name="merge_outproj_norm",
    )(oa, *obs, *lses, x, w_gate, b_gate, wa, wb, wo, g, bb)


def _ffn_kernel(alpha, x_ref, wg_ref, wu_ref, wd_ref, g_ref, b_ref, y_ref):
    x = x_ref[...]
    xb = x.astype(BF16)
    acc = jnp.zeros((TM, D_MODEL), F32)
    for c in range(FFN_HIDDEN // FFN_CHUNK):
        cols = slice(c * FFN_CHUNK, (c + 1) * FFN_CHUNK)
        gate = jnp.dot(xb, wg_ref[:, cols], preferred_element_type=F32)
        up = jnp.dot(xb, wu_ref[:, cols], preferred_element_type=F32)
        h = gate / (1.0 + jnp.exp(-gate)) * up
        acc = acc + jnp.dot(h.astype(BF16), wd_ref[cols, :], preferred_element_type=F32)
    y_ref[...] = _layer_norm(alpha * x + acc, g_ref[...], b_ref[...])


def _ffn(alpha, x2, wg, wu, wd, g, bb):
    m = x2.shape[0]
    tok = pl.BlockSpec((TM, D_MODEL), lambda i: (i, 0))
    full = lambda a: pl.BlockSpec(a.shape, lambda i: (0, 0))
    return pl.pallas_call(
        functools.partial(_ffn_kernel, alpha),
        grid=(m // TM,),
        in_specs=[tok, full(wg), full(wu), full(wd), full(g), full(bb)],
        out_specs=tok,
        out_shape=jax.ShapeDtypeStruct((m, D_MODEL), F32),
        compiler_params=_params(("parallel",)),
        name="swiglu_norm",
    )(x2, wg, wu, wd, g, bb)


def _split_w_in(w):
    a, bw, hi = A_WIDTH, B_WIDTH, IDX_HEADS * IDX_DIM
    o = 0
    qa, o = w[:, o:o + a], o + a
    ka, o = w[:, o:o + a], o + a
    va, o = w[:, o:o + a], o + a
    qb, o = w[:, o:o + bw], o + bw
    kb, o = w[:, o:o + bw], o + bw
    vb, o = w[:, o:o + bw], o + bw
    qi, o = w[:, o:o + hi], o + hi
    ki, o = w[:, o:o + IDX_DIM], o + IDX_DIM
    wi, o = w[:, o:o + IDX_HEADS], o + IDX_HEADS
    w_gate = w[:, o:].astype(BF16)
    w_nat = jnp.concatenate([ka, qb, kb, vb, ki, ki], axis=1).astype(BF16)
    pad = jnp.zeros((w.shape[0], _T_ROWS - _T_WI - IDX_HEADS), w.dtype)
    w_t = jnp.concatenate([qa, va, qi, wi, pad], axis=1).T.astype(BF16)
    return w_nat, w_t, w_gate


def kernel(x, positions, w_in, b_gate, w_branch_a, w_branch_b, w_out, ln1_g, ln1_b,
           w_ffn_gate, w_ffn_up, w_ffn_down, ln2_g, ln2_b):
    b, s, d = x.shape
    depth = w_in.shape[0]
    assert d == D_MODEL and s % (max(B_DILATIONS) * TB) == 0 and s % TM == 0
    alpha = (2 * depth) ** 0.25
    cos, sin, cos_t, sin_t = _rope_tables(positions)
    row = lambda v: v.reshape(1, -1)
    for layer in range(depth):
        w_nat, w_t, w_gate = _split_w_in(w_in[layer])
        ka6, ki4, *streams = _proj_nat(x, cos, sin, w_nat)
        qt, vt5, qi4t, wt = _proj_t(x, cos_t, sin_t, w_t)
        oa = _dsa_attention(qt, qi4t, wt, ka6, vt5, ki4)
        groups = [_window_attention(*streams[3 * g:3 * g + 3]) for g in range(len(B_DILATIONS))]
        x1 = _merge(alpha, oa, [o for o, _ in groups], [l for _, l in groups], x, w_gate, row(b_gate[layer]),
                    w_branch_a[layer].astype(BF16), w_branch_b[layer].astype(BF16),
                    w_out[layer].astype(BF16), row(ln1_g[layer]), row(ln1_b[layer]))
        x2 = _ffn(alpha, x1.reshape(b * s, d), w_ffn_gate[layer].astype(BF16), w_ffn_up[layer].astype(BF16),
                  w_ffn_down[layer].astype(BF16), row(ln2_g[layer]), row(ln2_b[layer]))
        x = x2.reshape(b, s, d)
    return x
```

```python
import functools

import jax
import jax.numpy as jnp
from jax import lax
from jax.experimental import pallas as pl
from jax.experimental.pallas import tpu as pltpu

F32 = jnp.float32
BF16 = jnp.bfloat16

D_MODEL = 1024
HEAD_DIM = 64
ROT_HALF = 8
ROPE_THETA = 500000.0
ATTN_SCALE = HEAD_DIM ** -0.5
LOG2E = 1.4426950408889634
V_ROWS = HEAD_DIM + 16
A_HEADS = 12
A_WIDTH = A_HEADS * HEAD_DIM
IDX_HEADS = 8
IDX_DIM = 64
IDX_SCALE = (IDX_HEADS ** -0.5) * (IDX_DIM ** -0.5)
TOPK_MAX = 256
B_DILATIONS = (1, 4, 16)
B_WINDOW_STEPS = 128
B_HEADS_PER_GROUP = 4
B_WIDTH = 3 * B_HEADS_PER_GROUP * HEAD_DIM
B_OUT_WIDTH = B_HEADS_PER_GROUP * HEAD_DIM
FFN_HIDDEN = 2816
LN_EPS = 1e-5
NEG = -1e30
MASK_NEG = -(2.0 ** 100)

LANES = 128
VMEM_LIMIT = 56 * 1024 * 1024
TM = 512
TQ = 256
TK = 256
TB = 256
FFN_CHUNK = 256
BISECT_STEPS_PER_CHECK = 2
BISECT_BLIND_STEPS = 12
WINDOW_TILES_PER_STEP = 8
MERGE_ROWS = 256
MERGE_COLS = 256
ROPE_TILE = 2048

_N_KA, _N_QB, _N_KB, _N_VB = 0, 768, 1536, 2304
_N_KI, _N_NAT = 3072, 3200
_T_QA, _T_VA, _T_QI, _T_WI, _T_ROWS = 0, 768, 1536, 2048, 2064


def _params(sem):
    return pltpu.CompilerParams(dimension_semantics=sem, vmem_limit_bytes=VMEM_LIMIT)


def _rope_kernel(pr_ref, fc_ref, c_ref, s_ref, ct_ref, st_ref):
    ang_t = fc_ref[...] * pr_ref[...].astype(F32)
    cos_t, sin_t = jnp.cos(ang_t), jnp.sin(ang_t)
    ct_ref[...] = cos_t
    st_ref[...] = sin_t
    reps = LANES // ROT_HALF
    cos = jnp.concatenate([cos_t] * reps, axis=0).T
    sin = jnp.concatenate([sin_t] * reps, axis=0).T
    d = lax.broadcasted_iota(jnp.int32, cos.shape, 1) & (HEAD_DIM - 1)
    c_ref[...] = jnp.where(d < 2 * ROT_HALF, cos, 1.0)
    s_ref[...] = jnp.where(d < ROT_HALF, -sin, jnp.where(d < 2 * ROT_HALF, sin, 0.0))


def _rope_tables(positions):
    m = positions.size
    inv_freq = ROPE_THETA ** (-jnp.arange(0, 2 * ROT_HALF, 2, dtype=F32) / (2 * ROT_HALF))
    t = min(ROPE_TILE, m)
    return pl.pallas_call(
        _rope_kernel,
        grid=(m // t,),
        in_specs=[pl.BlockSpec((1, t), lambda i: (0, i)), pl.BlockSpec((ROT_HALF, 1), lambda i: (0, 0))],
        out_specs=[pl.BlockSpec((t, LANES), lambda i: (i, 0)), pl.BlockSpec((t, LANES), lambda i: (i, 0)),
                   pl.BlockSpec((ROT_HALF, t), lambda i: (0, i)), pl.BlockSpec((ROT_HALF, t), lambda i: (0, i))],
        out_shape=[jax.ShapeDtypeStruct((m, LANES), F32), jax.ShapeDtypeStruct((m, LANES), F32),
                   jax.ShapeDtypeStruct((ROT_HALF, m), F32), jax.ShapeDtypeStruct((ROT_HALF, m), F32)],
        compiler_params=_params(("parallel",)),
        name="rope_tables",
    )(positions.reshape(1, m), inv_freq.reshape(ROT_HALF, 1))


def _proj_nat_kernel(x_ref, c_ref, s_ref, w_ref, ka_ref, ki_ref, *rest):
    streams, y_ref = rest[:-1], rest[-1]
    xb = x_ref[0].astype(BF16)
    cos, sin = c_ref[...], s_ref[...]
    lane = lax.broadcasted_iota(jnp.int32, cos.shape, 1)
    first = (lane & (HEAD_DIM - 1)) < ROT_HALF

    def rope(y):
        partner = jnp.where(first, pltpu.roll(y, LANES - ROT_HALF, 1), pltpu.roll(y, ROT_HALF, 1))
        return y * cos + partner * sin

    def proj(lo, hi):
        return jnp.dot(xb, w_ref[:, lo:hi], preferred_element_type=F32)

    blocks = A_WIDTH // LANES
    y = proj(_N_KA, _N_KA + A_WIDTH)
    for p in range(blocks):
        ka_ref[0, p] = rope(y[:, p * LANES:(p + 1) * LANES]).astype(BF16)

    def scatter_streams(which):
        per_group = B_OUT_WIDTH // LANES
        for g, d in enumerate(B_DILATIONS):
            out = streams[3 * g + which]
            for r in range(d):
                for p in range(per_group):
                    rows = y_ref[g * per_group + p, pl.ds(r, TM // d, stride=d), :]
                    out[0, r, :, p * LANES:(p + 1) * LANES] = rows.astype(BF16)

    y = proj(_N_QB, _N_QB + B_WIDTH)
    for p in range(blocks):
        y_ref[p] = rope(y[:, p * LANES:(p + 1) * LANES]) * (ATTN_SCALE * LOG2E)
    scatter_streams(0)
    y = proj(_N_KB, _N_KB + B_WIDTH)
    for p in range(blocks):
        y_ref[p] = rope(y[:, p * LANES:(p + 1) * LANES])
    scatter_streams(1)
    y = proj(_N_VB, _N_VB + B_WIDTH)
    for p in range(blocks):
        y_ref[p] = y[:, p * LANES:(p + 1) * LANES]
    scatter_streams(2)
    r = rope(proj(_N_KI, _N_KI + LANES))
    hi = r.astype(BF16).astype(F32)
    hl = jnp.where(lane < IDX_DIM, hi, r - hi).astype(BF16)
    ki_ref[0, :, 0:LANES] = hl
    ki_ref[0, :, LANES:2 * LANES] = hl


def _proj_nat(x, cos, sin, w_nat):
    b, s, d = x.shape
    nt = s // TM
    tok = lambda width: pl.BlockSpec((1, TM, width), lambda i, j: (i, j, 0))
    stream_specs, stream_shapes = [], []
    for dil in B_DILATIONS:
        for _ in range(3):
            stream_specs.append(pl.BlockSpec((1, dil, TM // dil, B_OUT_WIDTH), lambda i, j: (i, 0, j, 0)))
            stream_shapes.append(jax.ShapeDtypeStruct((b, dil, s // dil, B_OUT_WIDTH), BF16))
    return pl.pallas_call(
        _proj_nat_kernel,
        grid=(b, nt),
        in_specs=[tok(d),
                  pl.BlockSpec((TM, LANES), lambda i, j: (i * nt + j, 0)),
                  pl.BlockSpec((TM, LANES), lambda i, j: (i * nt + j, 0)),
                  pl.BlockSpec((d, _N_NAT), lambda i, j: (0, 0))],
        out_specs=[pl.BlockSpec((1, A_WIDTH // LANES, TM, LANES), lambda i, j: (i, 0, j, 0)),
                   tok(2 * LANES)] + stream_specs,
        out_shape=[jax.ShapeDtypeStruct((b, A_WIDTH // LANES, s, LANES), BF16),
                   jax.ShapeDtypeStruct((b, s, 2 * LANES), BF16)] + stream_shapes,
        scratch_shapes=[pltpu.VMEM((B_WIDTH // LANES, TM, LANES), F32)],
        compiler_params=_params(("parallel", "parallel")),
        name="proj_token_major",
    )(x, cos, sin, w_nat)


def _proj_t_kernel(x_ref, ct_ref, st_ref, w_ref, qt_ref, vt_ref, qi_ref, wt_ref):
    xb = x_ref[0].astype(BF16)
    cos, sin = ct_ref[...], st_ref[...]

    def proj(lo, hi):
        return lax.dot_general(w_ref[lo:hi, :], xb, (((1,), (1,)), ((), ())), preferred_element_type=F32)

    def rope_head(y):
        x1, x2 = y[0:ROT_HALF], y[ROT_HALF:2 * ROT_HALF]
        return jnp.concatenate([x1 * cos - x2 * sin, x2 * cos + x1 * sin, y[2 * ROT_HALF:]], axis=0)

    y = proj(_T_QA, _T_QA + A_WIDTH)
    for h in range(A_HEADS):
        r = rope_head(y[h * HEAD_DIM:(h + 1) * HEAD_DIM])
        qt_ref[0, h * HEAD_DIM:(h + 1) * HEAD_DIM, :] = (r * (ATTN_SCALE * LOG2E)).astype(BF16)
    y = proj(_T_VA, _T_VA + A_WIDTH)
    for h in range(A_HEADS):
        for c in range(TM // TK):
            vt_ref[0, h, c, 0:HEAD_DIM, :] = y[h * HEAD_DIM:(h + 1) * HEAD_DIM, c * TK:(c + 1) * TK].astype(BF16)
            vt_ref[0, h, c, HEAD_DIM:V_ROWS, :] = jnp.ones((V_ROWS - HEAD_DIM, TK), BF16)
    y = proj(_T_QI, _T_QI + IDX_HEADS * IDX_DIM)
    for h in range(IDX_HEADS):
        r = rope_head(y[h * IDX_DIM:(h + 1) * IDX_DIM])
        hi = r.astype(BF16)
        lo = (r - hi.astype(F32)).astype(BF16)
        base = 4 * h * IDX_DIM
        qi_ref[0, base:base + IDX_DIM, :] = hi
        qi_ref[0, base + IDX_DIM:base + 2 * IDX_DIM, :] = hi
        qi_ref[0, base + 2 * IDX_DIM:base + 3 * IDX_DIM, :] = lo
        qi_ref[0, base + 3 * IDX_DIM:base + 4 * IDX_DIM, :] = lo
    wt_ref[0] = proj(_T_WI, _T_ROWS)[0:IDX_HEADS] * IDX_SCALE


def _proj_t(x, cos_t, sin_t, w_t):
    b, s, d = x.shape
    nt = s // TM
    return pl.pallas_call(
        _proj_t_kernel,
        grid=(b, nt),
        in_specs=[pl.BlockSpec((1, TM, d), lambda i, j: (i, j, 0)),
                  pl.BlockSpec((ROT_HALF, TM), lambda i, j: (0, i * nt + j)),
                  pl.BlockSpec((ROT_HALF, TM), lambda i, j: (0, i * nt + j)),
                  pl.BlockSpec((_T_ROWS, d), lambda i, j: (0, 0))],
        out_specs=[pl.BlockSpec((1, A_WIDTH, TM), lambda i, j: (i, 0, j)),
                   pl.BlockSpec((1, A_HEADS, TM // TK, V_ROWS, TK), lambda i, j: (i, 0, j, 0, 0)),
                   pl.BlockSpec((1, 4 * IDX_HEADS * IDX_DIM, TM), lambda i, j: (i, 0, j)),
                   pl.BlockSpec((1, IDX_HEADS, TM), lambda i, j: (i, 0, j))],
        out_shape=[jax.ShapeDtypeStruct((b, A_WIDTH, s), BF16),
                   jax.ShapeDtypeStruct((b, A_HEADS, s // TK, V_ROWS, TK), BF16),
                   jax.ShapeDtypeStruct((b, 4 * IDX_HEADS * IDX_DIM, s), BF16),
                   jax.ShapeDtypeStruct((b, IDX_HEADS, s), F32)],
        compiler_params=_params(("parallel", "parallel")),
        name="proj_feature_major",
    )(x, cos_t, sin_t, w_t)


def _dsa_kernel(qt_ref, qi_ref, wt_ref, k_ref, vt_ref, ki_ref, o_ref,
                sc_ref, qpad_ref, ot_ref, m_ref, st_ref, stat_ref, sa_ref, sb_ref):
    j = pl.program_id(1)
    nkt = j + 1

    for h in range(A_HEADS):
        off = (h % 2) * HEAD_DIM
        qpad_ref[h] = jnp.zeros((2 * HEAD_DIM, TQ), BF16)
        qpad_ref[h, off:off + HEAD_DIM, :] = qt_ref[0, h * HEAD_DIM:(h + 1) * HEAD_DIM, :]

    qpos = j * TQ + lax.broadcasted_iota(jnp.int32, (1, TQ), 1)
    row_iota = lax.broadcasted_iota(jnp.int32, (TK, TQ), 0)
    w = wt_ref[0]

    def fold(t):
        return t.reshape(TK // 8, 8, TQ)

    def score_tile(kt, diagonal, carry):
        mn, mx, pos, nn = carry
        ki = ki_ref[0, pl.ds(pl.multiple_of(kt * TK, TK), TK), :]
        acc = jnp.zeros((TK, TQ), F32)
        for h in range(IDX_HEADS):
            s = jnp.dot(ki, qi_ref[0, 4 * h * IDX_DIM:4 * (h + 1) * IDX_DIM, :], preferred_element_type=F32)
            acc = acc + w[h:h + 1, :] * jnp.maximum(s, 0.0)
        if diagonal:
            causal = (kt * TK + row_iota) <= qpos
            val = jnp.where(causal, acc, -jnp.inf)
            low = jnp.where(causal, acc, jnp.inf)
        else:
            val = low = acc
        sc_ref[kt] = val
        return (jnp.minimum(mn, fold(low).min(axis=0)),
                jnp.maximum(mx, fold(val).max(axis=0)),
                pos + fold(jnp.where(val > 0.0, 1.0, 0.0)).sum(axis=0),
                nn + fold(jnp.where(val >= 0.0, 1.0, 0.0)).sum(axis=0))

    def load_stats():
        return tuple(stat_ref[n] for n in range(4))

    def store_stats(stats):
        for n, v in enumerate(stats):
            stat_ref[n] = v

    zeros8 = jnp.zeros((8, TQ), F32)
    store_stats(lax.fori_loop(
        0, j // 2, lambda i, c: score_tile(2 * i + 1, False, score_tile(2 * i, False, c)),
        (jnp.full((8, TQ), jnp.inf, F32), jnp.full((8, TQ), -jnp.inf, F32), zeros8, zeros8)))

    @pl.when(j % 2 == 1)
    def _():
        store_stats(score_tile(j, True, score_tile(j - 1, False, load_stats())))

    @pl.when(j % 2 == 0)
    def _():
        store_stats(score_tile(j, True, load_stats()))

    mn8, mx8, pos8, nn8 = load_stats()
    mn = mn8.min(axis=0, keepdims=True)
    mx = mx8.max(axis=0, keepdims=True)
    c_pos = pos8.sum(axis=0, keepdims=True)
    c_nn = nn8.sum(axis=0, keepdims=True)

    sc_ref[nkt] = jnp.full((TK, TQ), -jnp.inf, F32)

    def count(preds):
        def part(p, kt):
            return p(sc_ref[kt]).reshape(TK // 8, 8, TQ).sum(axis=0)

        def body(i, cnts):
            return tuple(c + part(p, 2 * i) + part(p, 2 * i + 1) for c, p in zip(cnts, preds))

        cnts = lax.fori_loop(0, (nkt + 1) // 2, body, tuple(jnp.zeros((8, TQ), F32) for _ in preds))
        return [c.sum(axis=0, keepdims=True) for c in cnts]

    def ge(cand):
        return lambda t: jnp.where(t >= cand, 1.0, 0.0)

    n_causal = (qpos + 1).astype(F32)
    k_q = jnp.minimum(n_causal, float(TOPK_MAX))
    c_mx, = count([ge(mx)])
    select = n_causal > k_q
    at_max = select & (c_mx >= k_q)
    search = select & (c_mx < k_q)
    at_zero = search & (c_pos < k_q) & (c_nn >= k_q)
    above = search & (c_pos >= k_q)
    below = search & (c_nn < k_q)
    lo0 = jnp.where(at_max, mx, jnp.where(at_zero | above, 0.0, mn))
    c_lo0 = jnp.where(at_max, c_mx, jnp.where(at_zero | above, c_nn, n_causal))
    st_ref[0:1, :] = lo0
    st_ref[1:2, :] = jnp.where(below, 0.0, mx)
    st_ref[2:3, :] = c_lo0
    st_ref[3:4, :] = jnp.where(at_max, 0.0, jnp.where(at_zero, c_pos, jnp.where(below, c_nn, c_mx)))
    act0 = jnp.where((above | below) & (c_lo0 > k_q), 1.0, 0.0)
    st_ref[4:5, :] = act0

    def bisect_once():
        lo, hi, c_lo, c_hi = st_ref[0:1, :], st_ref[1:2, :], st_ref[2:3, :], st_ref[3:4, :]
        act = st_ref[4:5, :] > 0.0
        mid = 0.5 * lo + 0.5 * hi
        live = act & (mid > lo) & (mid < hi)
        c, = count([ge(mid)])
        up = live & (c >= k_q)
        dn = live & (c < k_q)
        c_lo = jnp.where(up, c, c_lo)
        st_ref[0:1, :] = jnp.where(up, mid, lo)
        st_ref[1:2, :] = jnp.where(dn, mid, hi)
        st_ref[2:3, :] = c_lo
        st_ref[3:4, :] = jnp.where(dn, c, c_hi)
        act_new = jnp.where(live & (c_lo > k_q), 1.0, 0.0)
        st_ref[4:5, :] = act_new
        return act_new

    def bisect(go):
        for _ in range(BISECT_STEPS_PER_CHECK - 1):
            bisect_once()
        return (jnp.max(bisect_once()) > 0.0).astype(jnp.int32)

    @pl.when(jnp.max(act0) > 0.0)
    def _():
        def blind(_, carry):
            bisect_once()
            bisect_once()
            return carry

        lax.fori_loop(0, BISECT_BLIND_STEPS // 2, blind, 0)

    lax.while_loop(lambda go: go > 0, bisect, (jnp.max(st_ref[4:5, :]) > 0.0).astype(jnp.int32))

    lo = st_ref[0:1, :]
    tie = st_ref[2:3, :] > k_q

    @pl.when(jnp.max(jnp.where(tie, 1.0, 0.0)) > 0.0)
    def _():
        need = jnp.where(tie, k_q - st_ref[3:4, :], jnp.inf)
        tri = jnp.where(lax.broadcasted_iota(jnp.int32, (TK, TK), 0) >= lax.broadcasted_iota(jnp.int32, (TK, TK), 1),
                        1.0, 0.0).astype(BF16)

        def drop(kt, seen):
            t = sc_ref[kt]
            eq = jnp.where(t == lo, 1.0, 0.0)
            rank = jnp.dot(tri, eq.astype(BF16), preferred_element_type=F32) + seen
            sc_ref[kt] = jnp.where(t == lo, jnp.where(rank > need, -jnp.inf, t), t)
            return seen + fold(eq).sum(axis=0).sum(axis=0, keepdims=True)

        def drop_pair(i, seen):
            return drop(2 * i + 1, drop(2 * i, seen))

        lax.fori_loop(0, (nkt + 1) // 2, drop_pair, jnp.zeros((1, TQ), F32))

    m_ref[...] = jnp.full(m_ref.shape, MASK_NEG, F32)
    ot_ref[...] = jnp.zeros(ot_ref.shape, F32)

    def mask_bias(kt):
        return jnp.where(sc_ref[kt] >= lo, 0.0, MASK_NEG).astype(BF16)

    def logits(h, kt, bias):
        rows = pl.ds(pl.multiple_of(kt * TK, TK), TK)
        return jnp.dot(k_ref[0, h // 2, rows, :], qpad_ref[h], preferred_element_type=F32).astype(BF16) + bias

    def half_step(kt, cur_ref, nxt_ref):
        if nxt_ref is not None:
            bias_n = mask_bias(kt + 1)
        for h in range(A_HEADS):
            s = cur_ref[h]
            if nxt_ref is not None:
                nxt_ref[h] = logits(h, kt + 1, bias_n)
            m = m_ref[h]
            m_tile = s.reshape(TK // 16, 16, TQ).max(axis=0).astype(F32).max(axis=0, keepdims=True)
            m_new = jnp.maximum(m, m_tile)
            p = jnp.exp2(s - m_new.astype(BF16))
            corr = jnp.exp2(m - m_new)
            m_ref[h] = m_new
            ot_ref[h] = ot_ref[h] * corr + jnp.dot(vt_ref[0, h, kt], p, preferred_element_type=F32)

    bias0 = mask_bias(0)
    for h in range(A_HEADS):
        sa_ref[h] = logits(h, 0, bias0)

    def kv_pair(i, _):
        half_step(2 * i, sa_ref, sb_ref)
        half_step(2 * i + 1, sb_ref, sa_ref)
        return 0

    lax.fori_loop(0, (nkt - 1) // 2, kv_pair, 0)

    @pl.when(nkt % 2 == 1)
    def _():
        half_step(nkt - 1, sa_ref, None)

    @pl.when(nkt % 2 == 0)
    def _():
        half_step(nkt - 2, sa_ref, sb_ref)
        half_step(nkt - 1, sb_ref, None)

    for p in range(A_HEADS // 2):
        both = jnp.concatenate([ot_ref[h, 0:HEAD_DIM, :] / ot_ref[h, HEAD_DIM:HEAD_DIM + 1, :]
                                for h in (2 * p, 2 * p + 1)], axis=0)
        o_ref[0, :, p * LANES:(p + 1) * LANES] = both.T.astype(BF16)


def _dsa_attention(qt, qi4t, wt, k6, vt5, ki4):
    b, _, s = qt.shape
    nq = s // TQ
    return pl.pallas_call(
        _dsa_kernel,
        grid=(b, nq),
        in_specs=[pl.BlockSpec((1, A_WIDTH, TQ), lambda i, j: (i, 0, j)),
                  pl.BlockSpec((1, 4 * IDX_HEADS * IDX_DIM, TQ), lambda i, j: (i, 0, j)),
                  pl.BlockSpec((1, IDX_HEADS, TQ), lambda i, j: (i, 0, j)),
                  pl.BlockSpec((1, A_WIDTH // LANES, s, LANES), lambda i, j: (i, 0, 0, 0)),
                  pl.BlockSpec((1, A_HEADS, s // TK, V_ROWS, TK), lambda i, j: (i, 0, 0, 0, 0)),
                  pl.BlockSpec((1, s, 4 * IDX_DIM), lambda i, j: (i, 0, 0))],
        out_specs=pl.BlockSpec((1, TQ, A_WIDTH), lambda i, j: (i, j, 0)),
        out_shape=jax.ShapeDtypeStruct((b, s, A_WIDTH), BF16),
        scratch_shapes=[pltpu.VMEM((s // TK + 1, TK, TQ), F32),
                        pltpu.VMEM((A_HEADS, 2 * HEAD_DIM, TQ), BF16),
                        pltpu.VMEM((A_HEADS, V_ROWS, TQ), F32),
                        pltpu.VMEM((A_HEADS, 1, TQ), F32),
                        pltpu.VMEM((8, TQ), F32),
                        pltpu.VMEM((4, 8, TQ), F32),
                        pltpu.VMEM((A_HEADS, TK, TQ), BF16),
                        pltpu.VMEM((A_HEADS, TK, TQ), BF16)],
        compiler_params=_params(("parallel", "arbitrary")),
        name="dsa_attention",
    )(qt, qi4t, wt, k6, vt5, ki4)


def _window_kernel(q_ref, kp_ref, kc_ref, vp_ref, vc_ref, o_ref, lse_ref):
    hb = B_WINDOW_STEPS
    n_streams, n_rows = q_ref.shape[1], q_ref.shape[2]
    r = lax.broadcasted_iota(jnp.int32, (hb, 2 * hb), 0)
    c = lax.broadcasted_iota(jnp.int32, (hb, 2 * hb), 1)
    dist = r + hb - c
    band = jnp.where(dist >= 0, jnp.where(dist <= B_WINDOW_STEPS, 0.0, NEG), NEG)
    first_col = jnp.where(pl.program_id(2) > 0, 0, hb)
    band_first = jnp.where(c >= first_col, band, NEG)
    lane = lax.broadcasted_iota(jnp.int32, (hb, LANES), 1)
    left = lane < HEAD_DIM
    ones = jnp.ones((2 * hb, LANES), BF16)
    for sb in range(n_streams):
        for t in range(n_rows // TB):
            for p in range(B_HEADS_PER_GROUP // 2):
                cols = slice(p * LANES, (p + 1) * LANES)
                tile = slice(t * TB, (t + 1) * TB)
                q2, kc, vc = q_ref[0, sb, tile, cols], kc_ref[0, sb, tile, cols], vc_ref[0, sb, tile, cols]
                if t == 0:
                    k_before, v_before = kp_ref[0, sb, :, cols], vp_ref[0, sb, :, cols]
                else:
                    before = slice(t * TB - hb, t * TB)
                    k_before, v_before = kc_ref[0, sb, before, cols], vc_ref[0, sb, before, cols]
                windows = ((jnp.concatenate([k_before, kc[:hb]], axis=0),
                            jnp.concatenate([v_before, vc[:hb]], axis=0)), (kc, vc))
                for half, ((kw, vw), bias) in enumerate(zip(windows, (band_first if t == 0 else band, band))):
                    rows = slice(t * TB + half * hb, t * TB + (half + 1) * hb)
                    v1 = jnp.concatenate([vw, ones], axis=1)
                    outs, lses = [], []
                    for side in (left, ~left):
                        qh = jnp.where(side, q2[half * hb:(half + 1) * hb], jnp.zeros((hb, LANES), BF16))
                        s = lax.dot_general(qh, kw, (((1,), (1,)), ((), ())), preferred_element_type=F32) + bias
                        m = s.max(axis=-1, keepdims=True)
                        e = jnp.exp2((s - m).astype(BF16))
                        ol = jnp.dot(e, v1, preferred_element_type=F32)
                        l = ol[:, LANES:]
                        outs.append(ol[:, :LANES] / l)
                        lses.append(m + jnp.log2(l))
                    o_ref[0, sb, rows, cols] = jnp.where(left, outs[0], outs[1])
                    lse_ref[0, sb, rows, cols] = jnp.where(left, lses[0], lses[1])


def _window_attention(q, k, v):
    b, d, n, _ = q.shape
    assert TB == 2 * B_WINDOW_STEPS
    rows = min(n, WINDOW_TILES_PER_STEP * TB)
    streams = min(d, WINDOW_TILES_PER_STEP * TB // rows)
    halves = rows // (TB // 2)
    cur = pl.BlockSpec((1, streams, rows, B_OUT_WIDTH), lambda bi, ri, ti: (bi, ri, ti, 0))
    prev = pl.BlockSpec((1, streams, TB // 2, B_OUT_WIDTH),
                        lambda bi, ri, ti: (bi, ri, jnp.maximum(halves * ti - 1, 0), 0))
    return pl.pallas_call(
        _window_kernel,
        grid=(b, d // streams, n // rows),
        in_specs=[cur, prev, cur, prev, cur],
        out_specs=[cur, cur],
        out_shape=[jax.ShapeDtypeStruct((b, d, n, B_OUT_WIDTH), F32)] * 2,
        compiler_params=_params(("parallel", "parallel", "arbitrary")),
        name=f"window_attention_d{d}",
    )(q, k, k, v, v)


def _layer_norm(y, g, b):
    mu = y.mean(axis=-1, keepdims=True)
    yc = y - mu
    var = (yc * yc).mean(axis=-1, keepdims=True)
    return yc * lax.rsqrt(var + LN_EPS) * g + b


def _merge_kernel(alpha, oa_ref, o0_ref, o1_ref, o2_ref, l0_ref, l1_ref, l2_ref, x_ref,
                  wg_ref, bg_ref, wa_ref, wb_ref, wo_ref, g_ref, b_ref, y_ref,
                  tok_ref, ob_ref, xb_ref, mg_ref):
    halves = B_OUT_WIDTH // LANES

    for n, ref in enumerate((o1_ref, o2_ref, l1_ref, l2_ref)):
        d = ref.shape[1]
        for r in range(d):
            for p in range(halves):
                tok_ref[n, p, pl.ds(r, TM // d, stride=d), :] = ref[0, r, :, p * LANES:(p + 1) * LANES]
    for c in range(TM // MERGE_ROWS):
        rows = slice(c * MERGE_ROWS, (c + 1) * MERGE_ROWS)
        for p in range(halves):
            cols = slice(p * LANES, (p + 1) * LANES)
            l0, l1, l2 = l0_ref[0, 0, rows, cols], tok_ref[2, p, rows, :], tok_ref[3, p, rows, :]
            lm = jnp.maximum(jnp.maximum(l0, l1), l2)
            e0, e1, e2 = jnp.exp2(l0 - lm), jnp.exp2(l1 - lm), jnp.exp2(l2 - lm)
            ob = (e0 * o0_ref[0, 0, rows, cols] + e1 * tok_ref[0, p, rows, :] + e2 * tok_ref[1, p, rows, :])
            ob_ref[rows, cols] = (ob / (e0 + e1 + e2)).astype(BF16)

    xb_ref[...] = x_ref[0].astype(BF16)
    for c in range(D_MODEL // MERGE_COLS):
        cols = slice(c * MERGE_COLS, (c + 1) * MERGE_COLS)
        gcols = slice(D_MODEL + c * MERGE_COLS, D_MODEL + (c + 1) * MERGE_COLS)

        def gate(sel):
            z = jnp.dot(xb_ref[...], wg_ref[:, sel], preferred_element_type=F32) + bg_ref[:, sel]
            return 1.0 / (1.0 + jnp.exp(-z))

        pa = jnp.dot(oa_ref[0], wa_ref[:, cols], preferred_element_type=F32)
        pb = jnp.dot(ob_ref[...], wb_ref[:, cols], preferred_element_type=F32)
        mg_ref[:, cols] = (gate(cols) * pa + gate(gcols) * pb).astype(BF16)

    for c in range(TM // MERGE_ROWS):
        rows = slice(c * MERGE_ROWS, (c + 1) * MERGE_ROWS)
        mixed = jnp.dot(mg_ref[rows, :], wo_ref[...], preferred_element_type=F32)
        y_ref[0, rows, :] = _layer_norm(alpha * x_ref[0, rows, :] + mixed, g_ref[...], b_ref[...])


def _merge(alpha, oa, obs, lses, x, w_gate, b_gate, wa, wb, wo, g, bb):
    b, s, _ = x.shape
    tok = lambda width: pl.BlockSpec((1, TM, width), lambda i, j: (i, j, 0))
    full = lambda a: pl.BlockSpec(a.shape, lambda i, j: (0, 0))
    streams = [pl.BlockSpec((1, a.shape[1], TM // a.shape[1], B_OUT_WIDTH), lambda i, j: (i, 0, j, 0))
               for a in list(obs) + list(lses)]
    return pl.pallas_call(
        functools.partial(_merge_kernel, alpha),
        grid=(b, s // TM),
        in_specs=[tok(A_WIDTH)] + streams + [tok(D_MODEL), full(w_gate),
                  full(b_gate), full(wa), full(wb), full(wo), full(g), full(bb)],
        out_specs=tok(D_MODEL),
        out_shape=jax.ShapeDtypeStruct((b, s, D_MODEL), F32),
        scratch_shapes=[pltpu.VMEM((4, B_OUT_WIDTH // LANES, TM, LANES), F32),
                        pltpu.VMEM((TM, B_OUT_WIDTH), BF16),
                        pltpu.VMEM((TM, D_MODEL), BF16),
                        pltpu.VMEM((TM, D_MODEL), BF16)],
        compiler_params=_params(("parallel", "parallel")),
        name="merge_outproj_norm",
    )(oa, *obs, *lses, x, w_gate, b_gate, wa, wb, wo, g, bb)


def _ffn_kernel(alpha, x_ref, wg_ref, wu_ref, wd_ref, g_ref, b_ref, y_ref):
    x = x_ref[...]
    xb = x.astype(BF16)
    acc = jnp.zeros((TM, D_MODEL), F32)
    for c in range(FFN_HIDDEN // FFN_CHUNK):
        cols = slice(c * FFN_CHUNK, (c + 1) * FFN_CHUNK)
        gate = jnp.dot(xb, wg_ref[:, cols], preferred_element_type=F32)
        up = jnp.dot(xb, wu_ref[:, cols], preferred_element_type=F32)
        h = gate / (1.0 + jnp.exp(-gate)) * up
        acc = acc + jnp.dot(h.astype(BF16), wd_ref[cols, :], preferred_element_type=F32)
    y_ref[...] = _layer_norm(alpha * x + acc, g_ref[...], b_ref[...])


def _ffn(alpha, x2, wg, wu, wd, g, bb):
    m = x2.shape[0]
    tok = pl.BlockSpec((TM, D_MODEL), lambda i: (i, 0))
    full = lambda a: pl.BlockSpec(a.shape, lambda i: (0, 0))
    return pl.pallas_call(
        functools.partial(_ffn_kernel, alpha),
        grid=(m // TM,),
        in_specs=[tok, full(wg), full(wu), full(wd), full(g), full(bb)],
        out_specs=tok,
        out_shape=jax.ShapeDtypeStruct((m, D_MODEL), F32),
        compiler_params=_params(("parallel",)),
        name="swiglu_norm",
    )(x2, wg, wu, wd, g, bb)


def _split_w_in(w):
    a, bw, hi = A_WIDTH, B_WIDTH, IDX_HEADS * IDX_DIM
    o = 0
    qa, o = w[:, o:o + a], o + a
    ka, o = w[:, o:o + a], o + a
    va, o = w[:, o:o + a], o + a
    qb, o = w[:, o:o + bw], o + bw
    kb, o = w[:, o:o + bw], o + bw
    vb, o = w[:, o:o + bw], o + bw
    qi, o = w[:, o:o + hi], o + hi
    ki, o = w[:, o:o + IDX_DIM], o + IDX_DIM
    wi, o = w[:, o:o + IDX_HEADS], o + IDX_HEADS
    w_gate = w[:, o:].astype(BF16)
    w_nat = jnp.concatenate([ka, qb, kb, vb, ki, ki], axis=1).astype(BF16)
    pad = jnp.zeros((w.shape[0], _T_ROWS - _T_WI - IDX_HEADS), w.dtype)
    w_t = jnp.concatenate([qa, va, qi, wi, pad], axis=1).T.astype(BF16)
    return w_nat, w_t, w_gate


def kernel(x, positions, w_in, b_gate, w_branch_a, w_branch_b, w_out, ln1_g, ln1_b,
           w_ffn_gate, w_ffn_up, w_ffn_down, ln2_g, ln2_b):
    b, s, d = x.shape
    depth = w_in.shape[0]
    assert d == D_MODEL and s % (max(B_DILATIONS) * TB) == 0 and s % TM == 0
    alpha = (2 * depth) ** 0.25
    cos, sin, cos_t, sin_t = _rope_tables(positions)
    row = lambda v: v.reshape(1, -1)
    for layer in range(depth):
        w_nat, w_t, w_gate = _split_w_in(w_in[layer])
        ka6, ki4, *streams = _proj_nat(x, cos, sin, w_nat)
        qt, vt5, qi4t, wt = _proj_t(x, cos_t, sin_t, w_t)
        oa = _dsa_attention(qt, qi4t, wt, ka6, vt5, ki4)
        groups = [_window_attention(*streams[3 * g:3 * g + 3]) for g in range(len(B_DILATIONS))]
        x1 = _merge(alpha, oa, [o for o, _ in groups], [l for _, l in groups], x, w_gate, row(b_gate[layer]),
                    w_branch_a[layer].astype(BF16), w_branch_b[layer].astype(BF16),
                    w_out[layer].astype(BF16), row(ln1_g[layer]), row(ln1_b[layer]))
        x2 = _ffn(alpha, x1.reshape(b * s, d), w_ffn_gate[layer].astype(BF16), w_ffn_up[layer].astype(BF16),
                  w_ffn_down[layer].astype(BF16), row(ln2_g[layer]), row(ln2_b[layer]))
        x = x2.reshape(b, s, d)
    return x
```

```python
import functools

import jax
import jax.numpy as jnp
from jax import lax
from jax.experimental import pallas as pl
from jax.experimental.pallas import tpu as pltpu

F32 = jnp.float32
BF16 = jnp.bfloat16

D_MODEL = 1024
HEAD_DIM = 64
ROT_HALF = 8
ROPE_THETA = 500000.0
ATTN_SCALE = HEAD_DIM ** -0.5
LOG2E = 1.4426950408889634
V_ROWS = HEAD_DIM + 16
A_HEADS = 12
A_WIDTH = A_HEADS * HEAD_DIM
IDX_HEADS = 8
IDX_DIM = 64
IDX_SCALE = (IDX_HEADS ** -0.5) * (IDX_DIM ** -0.5)
TOPK_MAX = 256
B_DILATIONS = (1, 4, 16)
B_WINDOW_STEPS = 128
B_HEADS_PER_GROUP = 4
B_WIDTH = 3 * B_HEADS_PER_GROUP * HEAD_DIM
B_OUT_WIDTH = B_HEADS_PER_GROUP * HEAD_DIM
FFN_HIDDEN = 2816
LN_EPS = 1e-5
NEG = -1e30
MASK_NEG = -(2.0 ** 100)

LANES = 128
VMEM_LIMIT = 56 * 1024 * 1024
TM = 512
TM_FFN = 1024
TM_MERGE = 1024
TQ = 256
TK = 256
TB = 256
FFN_CHUNK = 256
BISECT_STEPS_PER_CHECK = 2
BISECT_BLIND_STEPS = 12
WINDOW_TILES_PER_STEP = 8
MERGE_ROWS = 256
MERGE_COLS = 256
ROPE_TILE = 2048

_N_KA, _N_QB, _N_KB, _N_VB = 0, 768, 1536, 2304
_N_KI, _N_NAT = 3072, 3200
_T_QA, _T_VA, _T_QI, _T_WI, _T_ROWS = 0, 768, 1536, 2048, 2064


def _params(sem):
    return pltpu.CompilerParams(dimension_semantics=sem, vmem_limit_bytes=VMEM_LIMIT)


def _rope_kernel(pr_ref, fc_ref, c_ref, s_ref, ct_ref, st_ref):
    ang_t = fc_ref[...] * pr_ref[...].astype(F32)
    cos_t, sin_t = jnp.cos(ang_t), jnp.sin(ang_t)
    ct_ref[...] = cos_t
    st_ref[...] = sin_t
    reps = LANES // ROT_HALF
    cos = jnp.concatenate([cos_t] * reps, axis=0).T
    sin = jnp.concatenate([sin_t] * reps, axis=0).T
    d = lax.broadcasted_iota(jnp.int32, cos.shape, 1) & (HEAD_DIM - 1)
    c_ref[...] = jnp.where(d < 2 * ROT_HALF, cos, 1.0)
    s_ref[...] = jnp.where(d < ROT_HALF, -sin, jnp.where(d < 2 * ROT_HALF, sin, 0.0))


def _rope_tables(positions):
    m = positions.size
    inv_freq = ROPE_THETA ** (-jnp.arange(0, 2 * ROT_HALF, 2, dtype=F32) / (2 * ROT_HALF))
    t = min(ROPE_TILE, m)
    return pl.pallas_call(
        _rope_kernel,
        grid=(m // t,),
        in_specs=[pl.BlockSpec((1, t), lambda i: (0, i)), pl.BlockSpec((ROT_HALF, 1), lambda i: (0, 0))],
        out_specs=[pl.BlockSpec((t, LANES), lambda i: (i, 0)), pl.BlockSpec((t, LANES), lambda i: (i, 0)),
                   pl.BlockSpec((ROT_HALF, t), lambda i: (0, i)), pl.BlockSpec((ROT_HALF, t), lambda i: (0, i))],
        out_shape=[jax.ShapeDtypeStruct((m, LANES), F32), jax.ShapeDtypeStruct((m, LANES), F32),
                   jax.ShapeDtypeStruct((ROT_HALF, m), F32), jax.ShapeDtypeStruct((ROT_HALF, m), F32)],
        compiler_params=_params(("parallel",)),
        name="rope_tables",
    )(positions.reshape(1, m), inv_freq.reshape(ROT_HALF, 1))


def _proj_nat_kernel(x_ref, c_ref, s_ref, w_ref, ka_ref, ki_ref, *rest):
    streams, y_ref = rest[:-1], rest[-1]
    xb = x_ref[0].astype(BF16)
    cos, sin = c_ref[...], s_ref[...]
    lane = lax.broadcasted_iota(jnp.int32, cos.shape, 1)
    first = (lane & (HEAD_DIM - 1)) < ROT_HALF

    def rope(y):
        partner = jnp.where(first, pltpu.roll(y, LANES - ROT_HALF, 1), pltpu.roll(y, ROT_HALF, 1))
        return y * cos + partner * sin

    def proj(lo, hi):
        return jnp.dot(xb, w_ref[:, lo:hi], preferred_element_type=F32)

    blocks = A_WIDTH // LANES
    y = proj(_N_KA, _N_KA + A_WIDTH)
    for p in range(blocks):
        ka_ref[0, p] = rope(y[:, p * LANES:(p + 1) * LANES]).astype(BF16)

    def scatter_streams(which):
        per_group = B_OUT_WIDTH // LANES
        for g, d in enumerate(B_DILATIONS):
            out = streams[3 * g + which]
            for r in range(d):
                for p in range(per_group):
                    rows = y_ref[g * per_group + p, pl.ds(r, TM // d, stride=d), :]
                    out[0, r, :, p * LANES:(p + 1) * LANES] = rows.astype(BF16)

    y = proj(_N_QB, _N_QB + B_WIDTH)
    for p in range(blocks):
        y_ref[p] = rope(y[:, p * LANES:(p + 1) * LANES]) * (ATTN_SCALE * LOG2E)
    scatter_streams(0)
    y = proj(_N_KB, _N_KB + B_WIDTH)
    for p in range(blocks):
        y_ref[p] = rope(y[:, p * LANES:(p + 1) * LANES])
    scatter_streams(1)
    y = proj(_N_VB, _N_VB + B_WIDTH)
    for p in range(blocks):
        y_ref[p] = y[:, p * LANES:(p + 1) * LANES]
    scatter_streams(2)
    r = rope(proj(_N_KI, _N_KI + LANES))
    hi = r.astype(BF16).astype(F32)
    hl = jnp.where(lane < IDX_DIM, hi, r - hi).astype(BF16)
    ki_ref[0, :, 0:LANES] = hl
    ki_ref[0, :, LANES:2 * LANES] = hl


def _proj_nat(x, cos, sin, w_nat):
    b, s, d = x.shape
    nt = s // TM
    tok = lambda width: pl.BlockSpec((1, TM, width), lambda i, j: (i, j, 0))
    stream_specs, stream_shapes = [], []
    for dil in B_DILATIONS:
        for _ in range(3):
            stream_specs.append(pl.BlockSpec((1, dil, TM // dil, B_OUT_WIDTH), lambda i, j: (i, 0, j, 0)))
            stream_shapes.append(jax.ShapeDtypeStruct((b, dil, s // dil, B_OUT_WIDTH), BF16))
    return pl.pallas_call(
        _proj_nat_kernel,
        grid=(b, nt),
        in_specs=[tok(d),
                  pl.BlockSpec((TM, LANES), lambda i, j: (i * nt + j, 0)),
                  pl.BlockSpec((TM, LANES), lambda i, j: (i * nt + j, 0)),
                  pl.BlockSpec((d, _N_NAT), lambda i, j: (0, 0))],
        out_specs=[pl.BlockSpec((1, A_WIDTH // LANES, TM, LANES), lambda i, j: (i, 0, j, 0)),
                   tok(2 * LANES)] + stream_specs,
        out_shape=[jax.ShapeDtypeStruct((b, A_WIDTH // LANES, s, LANES), BF16),
                   jax.ShapeDtypeStruct((b, s, 2 * LANES), BF16)] + stream_shapes,
        scratch_shapes=[pltpu.VMEM((B_WIDTH // LANES, TM, LANES), F32)],
        compiler_params=_params(("parallel", "parallel")),
        name="proj_token_major",
    )(x, cos, sin, w_nat)


def _proj_t_kernel(x_ref, ct_ref, st_ref, w_ref, qt_ref, vt_ref, qi_ref, wt_ref):
    xb = x_ref[0].astype(BF16)
    cos, sin = ct_ref[...], st_ref[...]

    def proj(lo, hi):
        return lax.dot_general(w_ref[lo:hi, :], xb, (((1,), (1,)), ((), ())), preferred_element_type=F32)

    def rope_head(y):
        x1, x2 = y[0:ROT_HALF], y[ROT_HALF:2 * ROT_HALF]
        return jnp.concatenate([x1 * cos - x2 * sin, x2 * cos + x1 * sin, y[2 * ROT_HALF:]], axis=0)

    y = proj(_T_QA, _T_QA + A_WIDTH)
    for h in range(A_HEADS):
        r = rope_head(y[h * HEAD_DIM:(h + 1) * HEAD_DIM])
        qt_ref[0, h * HEAD_DIM:(h + 1) * HEAD_DIM, :] = (r * (ATTN_SCALE * LOG2E)).astype(BF16)
    y = proj(_T_VA, _T_VA + A_WIDTH)
    for h in range(A_HEADS):
        for c in range(TM // TK):
            vt_ref[0, h, c, 0:HEAD_DIM, :] = y[h * HEAD_DIM:(h + 1) * HEAD_DIM, c * TK:(c + 1) * TK].astype(BF16)
            vt_ref[0, h, c, HEAD_DIM:V_ROWS, :] = jnp.ones((V_ROWS - HEAD_DIM, TK), BF16)
    y = proj(_T_QI, _T_QI + IDX_HEADS * IDX_DIM)
    for h in range(IDX_HEADS):
        r = rope_head(y[h * IDX_DIM:(h + 1) * IDX_DIM])
        hi = r.astype(BF16)
        lo = (r - hi.astype(F32)).astype(BF16)
        base = 4 * h * IDX_DIM
        qi_ref[0, base:base + IDX_DIM, :] = hi
        qi_ref[0, base + IDX_DIM:base + 2 * IDX_DIM, :] = hi
        qi_ref[0, base + 2 * IDX_DIM:base + 3 * IDX_DIM, :] = lo
        qi_ref[0, base + 3 * IDX_DIM:base + 4 * IDX_DIM, :] = lo
    wt_ref[0] = proj(_T_WI, _T_ROWS)[0:IDX_HEADS] * IDX_SCALE


def _proj_t(x, cos_t, sin_t, w_t):
    b, s, d = x.shape
    nt = s // TM
    return pl.pallas_call(
        _proj_t_kernel,
        grid=(b, nt),
        in_specs=[pl.BlockSpec((1, TM, d), lambda i, j: (i, j, 0)),
                  pl.BlockSpec((ROT_HALF, TM), lambda i, j: (0, i * nt + j)),
                  pl.BlockSpec((ROT_HALF, TM), lambda i, j: (0, i * nt + j)),
                  pl.BlockSpec((_T_ROWS, d), lambda i, j: (0, 0))],
        out_specs=[pl.BlockSpec((1, A_WIDTH, TM), lambda i, j: (i, 0, j)),
                   pl.BlockSpec((1, A_HEADS, TM // TK, V_ROWS, TK), lambda i, j: (i, 0, j, 0, 0)),
                   pl.BlockSpec((1, 4 * IDX_HEADS * IDX_DIM, TM), lambda i, j: (i, 0, j)),
                   pl.BlockSpec((1, IDX_HEADS, TM), lambda i, j: (i, 0, j))],
        out_shape=[jax.ShapeDtypeStruct((b, A_WIDTH, s), BF16),
                   jax.ShapeDtypeStruct((b, A_HEADS, s // TK, V_ROWS, TK), BF16),
                   jax.ShapeDtypeStruct((b, 4 * IDX_HEADS * IDX_DIM, s), BF16),
                   jax.ShapeDtypeStruct((b, IDX_HEADS, s), F32)],
        compiler_params=_params(("parallel", "parallel")),
        name="proj_feature_major",
    )(x, cos_t, sin_t, w_t)


def _dsa_kernel(qt_ref, qi_ref, wt_ref, k_ref, vt_ref, ki_ref, o_ref,
                sc_ref, qpad_ref, ot_ref, m_ref, st_ref, stat_ref, sa_ref, sb_ref):
    j = pl.program_id(1)
    nkt = j + 1

    for h in range(A_HEADS):
        off = (h % 2) * HEAD_DIM
        qpad_ref[h] = jnp.zeros((2 * HEAD_DIM, TQ), BF16)
        qpad_ref[h, off:off + HEAD_DIM, :] = qt_ref[0, h * HEAD_DIM:(h + 1) * HEAD_DIM, :]

    qpos = j * TQ + lax.broadcasted_iota(jnp.int32, (1, TQ), 1)
    row_iota = lax.broadcasted_iota(jnp.int32, (TK, TQ), 0)
    w = wt_ref[0]

    def fold(t):
        return t.reshape(TK // 8, 8, TQ)

    def score_tile(kt, diagonal, carry):
        mn, mx, pos, nn = carry
        ki = ki_ref[0, pl.ds(pl.multiple_of(kt * TK, TK), TK), :]
        acc = jnp.zeros((TK, TQ), F32)
        for h in range(IDX_HEADS):
            s = jnp.dot(ki, qi_ref[0, 4 * h * IDX_DIM:4 * (h + 1) * IDX_DIM, :], preferred_element_type=F32)
            acc = acc + w[h:h + 1, :] * jnp.maximum(s, 0.0)
        if diagonal:
            causal = (kt * TK + row_iota) <= qpos
            val = jnp.where(causal, acc, -jnp.inf)
            low = jnp.where(causal, acc, jnp.inf)
        else:
            val = low = acc
        sc_ref[kt] = val
        return (jnp.minimum(mn, fold(low).min(axis=0)),
                jnp.maximum(mx, fold(val).max(axis=0)),
                pos + fold(jnp.where(val > 0.0, 1.0, 0.0)).sum(axis=0),
                nn + fold(jnp.where(val >= 0.0, 1.0, 0.0)).sum(axis=0))

    def load_stats():
        return tuple(stat_ref[n] for n in range(4))

    def store_stats(stats):
        for n, v in enumerate(stats):
            stat_ref[n] = v

    zeros8 = jnp.zeros((8, TQ), F32)
    store_stats(lax.fori_loop(
        0, j // 2, lambda i, c: score_tile(2 * i + 1, False, score_tile(2 * i, False, c)),
        (jnp.full((8, TQ), jnp.inf, F32), jnp.full((8, TQ), -jnp.inf, F32), zeros8, zeros8)))

    @pl.when(j % 2 == 1)
    def _():
        store_stats(score_tile(j, True, score_tile(j - 1, False, load_stats())))

    @pl.when(j % 2 == 0)
    def _():
        store_stats(score_tile(j, True, load_stats()))

    mn8, mx8, pos8, nn8 = load_stats()
    mn = mn8.min(axis=0, keepdims=True)
    mx = mx8.max(axis=0, keepdims=True)
    c_pos = pos8.sum(axis=0, keepdims=True)
    c_nn = nn8.sum(axis=0, keepdims=True)

    sc_ref[nkt] = jnp.full((TK, TQ), -jnp.inf, F32)

    def count(preds):
        def part(p, kt):
            return p(sc_ref[kt]).reshape(TK // 8, 8, TQ).sum(axis=0)

        def body(i, cnts):
            return tuple(c + part(p, 2 * i) + part(p, 2 * i + 1) for c, p in zip(cnts, preds))

        cnts = lax.fori_loop(0, (nkt + 1) // 2, body, tuple(jnp.zeros((8, TQ), F32) for _ in preds))
        return [c.sum(axis=0, keepdims=True) for c in cnts]

    def ge(cand):
        return lambda t: jnp.where(t >= cand, 1.0, 0.0)

    n_causal = (qpos + 1).astype(F32)
    k_q = jnp.minimum(n_causal, float(TOPK_MAX))
    c_mx, = count([ge(mx)])
    select = n_causal > k_q
    at_max = select & (c_mx >= k_q)
    search = select & (c_mx < k_q)
    at_zero = search & (c_pos < k_q) & (c_nn >= k_q)
    above = search & (c_pos >= k_q)
    below = search & (c_nn < k_q)
    lo0 = jnp.where(at_max, mx, jnp.where(at_zero | above, 0.0, mn))
    c_lo0 = jnp.where(at_max, c_mx, jnp.where(at_zero | above, c_nn, n_causal))
    st_ref[0:1, :] = lo0
    st_ref[1:2, :] = jnp.where(below, 0.0, mx)
    st_ref[2:3, :] = c_lo0
    st_ref[3:4, :] = jnp.where(at_max, 0.0, jnp.where(at_zero, c_pos, jnp.where(below, c_nn, c_mx)))
    act0 = jnp.where((above | below) & (c_lo0 > k_q), 1.0, 0.0)
    st_ref[4:5, :] = act0

    def bisect_once():
        lo, hi, c_lo, c_hi = st_ref[0:1, :], st_ref[1:2, :], st_ref[2:3, :], st_ref[3:4, :]
        act = st_ref[4:5, :] > 0.0
        mid = 0.5 * lo + 0.5 * hi
        live = act & (mid > lo) & (mid < hi)
        c, = count([ge(mid)])
        up = live & (c >= k_q)
        dn = live & (c < k_q)
        c_lo = jnp.where(up, c, c_lo)
        st_ref[0:1, :] = jnp.where(up, mid, lo)
        st_ref[1:2, :] = jnp.where(dn, mid, hi)
        st_ref[2:3, :] = c_lo
        st_ref[3:4, :] = jnp.where(dn, c, c_hi)
        act_new = jnp.where(live & (c_lo > k_q), 1.0, 0.0)
        st_ref[4:5, :] = act_new
        return act_new

    def bisect(go):
        for _ in range(BISECT_STEPS_PER_CHECK - 1):
            bisect_once()
        return (jnp.max(bisect_once()) > 0.0).astype(jnp.int32)

    @pl.when(jnp.max(act0) > 0.0)
    def _():
        def blind(_, carry):
            bisect_once()
            bisect_once()
            return carry

        lax.fori_loop(0, BISECT_BLIND_STEPS // 2, blind, 0)

    lax.while_loop(lambda go: go > 0, bisect, (jnp.max(st_ref[4:5, :]) > 0.0).astype(jnp.int32))

    lo = st_ref[0:1, :]
    tie = st_ref[2:3, :] > k_q

    @pl.when(jnp.max(jnp.where(tie, 1.0, 0.0)) > 0.0)
    def _():
        need = jnp.where(tie, k_q - st_ref[3:4, :], jnp.inf)
        tri = jnp.where(lax.broadcasted_iota(jnp.int32, (TK, TK), 0) >= lax.broadcasted_iota(jnp.int32, (TK, TK), 1),
                        1.0, 0.0).astype(BF16)

        def drop(kt, seen):
            t = sc_ref[kt]
            eq = jnp.where(t == lo, 1.0, 0.0)
            rank = jnp.dot(tri, eq.astype(BF16), preferred_element_type=F32) + seen
            sc_ref[kt] = jnp.where(t == lo, jnp.where(rank > need, -jnp.inf, t), t)
            return seen + fold(eq).sum(axis=0).sum(axis=0, keepdims=True)

        def drop_pair(i, seen):
            return drop(2 * i + 1, drop(2 * i, seen))

        lax.fori_loop(0, (nkt + 1) // 2, drop_pair, jnp.zeros((1, TQ), F32))

    m_ref[...] = jnp.full(m_ref.shape, MASK_NEG, F32)
    ot_ref[...] = jnp.zeros(ot_ref.shape, F32)

    def mask_bias(kt):
        return jnp.where(sc_ref[kt] >= lo, 0.0, MASK_NEG).astype(BF16)

    def logits(h, kt, bias):
        rows = pl.ds(pl.multiple_of(kt * TK, TK), TK)
        return jnp.dot(k_ref[0, h // 2, rows, :], qpad_ref[h], preferred_element_type=F32).astype(BF16) + bias

    def half_step(kt, cur_ref, nxt_ref):
        if nxt_ref is not None:
            bias_n = mask_bias(kt + 1)
        for h in range(A_HEADS):
            s = cur_ref[h]
            if nxt_ref is not None:
                nxt_ref[h] = logits(h, kt + 1, bias_n)
            m = m_ref[h]
            m_tile = s.reshape(TK // 16, 16, TQ).max(axis=0).astype(F32).max(axis=0, keepdims=True)
            m_new = jnp.maximum(m, m_tile)
            p = jnp.exp2(s - m_new.astype(BF16))
            corr = jnp.exp2(m - m_new)
            m_ref[h] = m_new
            ot_ref[h] = ot_ref[h] * corr + jnp.dot(vt_ref[0, h, kt], p, preferred_element_type=F32)

    bias0 = mask_bias(0)
    for h in range(A_HEADS):
        sa_ref[h] = logits(h, 0, bias0)

    def kv_pair(i, _):
        half_step(2 * i, sa_ref, sb_ref)
        half_step(2 * i + 1, sb_ref, sa_ref)
        return 0

    lax.fori_loop(0, (nkt - 1) // 2, kv_pair, 0)

    @pl.when(nkt % 2 == 1)
    def _():
        half_step(nkt - 1, sa_ref, None)

    @pl.when(nkt % 2 == 0)
    def _():
        half_step(nkt - 2, sa_ref, sb_ref)
        half_step(nkt - 1, sb_ref, None)

    for p in range(A_HEADS // 2):
        both = jnp.concatenate([ot_ref[h, 0:HEAD_DIM, :] / ot_ref[h, HEAD_DIM:HEAD_DIM + 1, :]
                                for h in (2 * p, 2 * p + 1)], axis=0)
        o_ref[0, :, p * LANES:(p + 1) * LANES] = both.T.astype(BF16)


def _dsa_attention(qt, qi4t, wt, k6, vt5, ki4):
    b, _, s = qt.shape
    nq = s // TQ
    return pl.pallas_call(
        _dsa_kernel,
        grid=(b, nq),
        in_specs=[pl.BlockSpec((1, A_WIDTH, TQ), lambda i, j: (i, 0, j)),
                  pl.BlockSpec((1, 4 * IDX_HEADS * IDX_DIM, TQ), lambda i, j: (i, 0, j)),
                  pl.BlockSpec((1, IDX_HEADS, TQ), lambda i, j: (i, 0, j)),
                  pl.BlockSpec((1, A_WIDTH // LANES, s, LANES), lambda i, j: (i, 0, 0, 0)),
                  pl.BlockSpec((1, A_HEADS, s // TK, V_ROWS, TK), lambda i, j: (i, 0, 0, 0, 0)),
                  pl.BlockSpec((1, s, 4 * IDX_DIM), lambda i, j: (i, 0, 0))],
        out_specs=pl.BlockSpec((1, TQ, A_WIDTH), lambda i, j: (i, j, 0)),
        out_shape=jax.ShapeDtypeStruct((b, s, A_WIDTH), BF16),
        scratch_shapes=[pltpu.VMEM((s // TK + 1, TK, TQ), F32),
                        pltpu.VMEM((A_HEADS, 2 * HEAD_DIM, TQ), BF16),
                        pltpu.VMEM((A_HEADS, V_ROWS, TQ), F32),
                        pltpu.VMEM((A_HEADS, 1, TQ), F32),
                        pltpu.VMEM((8, TQ), F32),
                        pltpu.VMEM((4, 8, TQ), F32),
                        pltpu.VMEM((A_HEADS, TK, TQ), BF16),
                        pltpu.VMEM((A_HEADS, TK, TQ), BF16)],
        compiler_params=_params(("parallel", "arbitrary")),
        name="dsa_attention",
    )(qt, qi4t, wt, k6, vt5, ki4)


def _window_kernel(q_ref, kp_ref, kc_ref, vp_ref, vc_ref, o_ref, lse_ref):
    hb = B_WINDOW_STEPS
    n_streams, n_rows = q_ref.shape[1], q_ref.shape[2]
    r = lax.broadcasted_iota(jnp.int32, (hb, 2 * hb), 0)
    c = lax.broadcasted_iota(jnp.int32, (hb, 2 * hb), 1)
    dist = r + hb - c
    band = jnp.where(dist >= 0, jnp.where(dist <= B_WINDOW_STEPS, 0.0, NEG), NEG)
    first_col = jnp.where(pl.program_id(2) > 0, 0, hb)
    band_first = jnp.where(c >= first_col, band, NEG)
    lane = lax.broadcasted_iota(jnp.int32, (hb, LANES), 1)
    left = lane < HEAD_DIM
    ones = jnp.ones((2 * hb, LANES), BF16)
    for sb in range(n_streams):
        for t in range(n_rows // TB):
            for p in range(B_HEADS_PER_GROUP // 2):
                cols = slice(p * LANES, (p + 1) * LANES)
                tile = slice(t * TB, (t + 1) * TB)
                q2, kc, vc = q_ref[0, sb, tile, cols], kc_ref[0, sb, tile, cols], vc_ref[0, sb, tile, cols]
                if t == 0:
                    k_before, v_before = kp_ref[0, sb, :, cols], vp_ref[0, sb, :, cols]
                else:
                    before = slice(t * TB - hb, t * TB)
                    k_before, v_before = kc_ref[0, sb, before, cols], vc_ref[0, sb, before, cols]
                windows = ((jnp.concatenate([k_before, kc[:hb]], axis=0),
                            jnp.concatenate([v_before, vc[:hb]], axis=0)), (kc, vc))
                for half, ((kw, vw), bias) in enumerate(zip(windows, (band_first if t == 0 else band, band))):
                    rows = slice(t * TB + half * hb, t * TB + (half + 1) * hb)
                    v1 = jnp.concatenate([vw, ones], axis=1)
                    outs, lses = [], []
                    for side in (left, ~left):
                        qh = jnp.where(side, q2[half * hb:(half + 1) * hb], jnp.zeros((hb, LANES), BF16))
                        s = lax.dot_general(qh, kw, (((1,), (1,)), ((), ())), preferred_element_type=F32) + bias
                        m = s.max(axis=-1, keepdims=True)
                        e = jnp.exp2((s - m).astype(BF16))
                        ol = jnp.dot(e, v1, preferred_element_type=F32)
                        l = ol[:, LANES:]
                        outs.append(ol[:, :LANES] / l)
                        lses.append(m + jnp.log2(l))
                    o_ref[0, sb, rows, cols] = jnp.where(left, outs[0], outs[1])
                    lse_ref[0, sb, rows, cols] = jnp.where(left, lses[0], lses[1])


def _window_attention(q, k, v):
    b, d, n, _ = q.shape
    assert TB == 2 * B_WINDOW_STEPS
    rows = min(n, WINDOW_TILES_PER_STEP * TB)
    streams = min(d, WINDOW_TILES_PER_STEP * TB // rows)
    halves = rows // (TB // 2)
    cur = pl.BlockSpec((1, streams, rows, B_OUT_WIDTH), lambda bi, ri, ti: (bi, ri, ti, 0))
    prev = pl.BlockSpec((1, streams, TB // 2, B_OUT_WIDTH),
                        lambda bi, ri, ti: (bi, ri, jnp.maximum(halves * ti - 1, 0), 0))
    return pl.pallas_call(
        _window_kernel,
        grid=(b, d // streams, n // rows),
        in_specs=[cur, prev, cur, prev, cur],
        out_specs=[cur, cur],
        out_shape=[jax.ShapeDtypeStruct((b, d, n, B_OUT_WIDTH), F32)] * 2,
        compiler_params=_params(("parallel", "parallel", "arbitrary")),
        name=f"window_attention_d{d}",
    )(q, k, k, v, v)


def _layer_norm(y, g, b):
    mu = y.mean(axis=-1, keepdims=True)
    yc = y - mu
    var = (yc * yc).mean(axis=-1, keepdims=True)
    return yc * lax.rsqrt(var + LN_EPS) * g + b


def _merge_kernel(alpha, oa_ref, o0_ref, o1_ref, o2_ref, l0_ref, l1_ref, l2_ref, x_ref,
                  wg_ref, bg_ref, wa_ref, wb_ref, wo_ref, g_ref, b_ref, y_ref,
                  tok_ref, ob_ref, xb_ref, mg_ref):
    halves = B_OUT_WIDTH // LANES

    for n, ref in enumerate((o1_ref, o2_ref, l1_ref, l2_ref)):
        d = ref.shape[1]
        for r in range(d):
            for p in range(halves):
                tok_ref[n, p, pl.ds(r, TM_MERGE // d, stride=d), :] = ref[0, r, :, p * LANES:(p + 1) * LANES]
    for c in range(TM_MERGE // MERGE_ROWS):
        rows = slice(c * MERGE_ROWS, (c + 1) * MERGE_ROWS)
        for p in range(halves):
            cols = slice(p * LANES, (p + 1) * LANES)
            l0, l1, l2 = l0_ref[0, 0, rows, cols], tok_ref[2, p, rows, :], tok_ref[3, p, rows, :]
            lm = jnp.maximum(jnp.maximum(l0, l1), l2)
            e0, e1, e2 = jnp.exp2(l0 - lm), jnp.exp2(l1 - lm), jnp.exp2(l2 - lm)
            ob = (e0 * o0_ref[0, 0, rows, cols] + e1 * tok_ref[0, p, rows, :] + e2 * tok_ref[1, p, rows, :])
            ob_ref[rows, cols] = (ob / (e0 + e1 + e2)).astype(BF16)

    xb_ref[...] = x_ref[0].astype(BF16)
    for c in range(D_MODEL // MERGE_COLS):
        cols = slice(c * MERGE_COLS, (c + 1) * MERGE_COLS)
        gcols = slice(D_MODEL + c * MERGE_COLS, D_MODEL + (c + 1) * MERGE_COLS)

        def gate(sel):
            z = jnp.dot(xb_ref[...], wg_ref[:, sel], preferred_element_type=F32) + bg_ref[:, sel]
            return 1.0 / (1.0 + jnp.exp(-z))

        pa = jnp.dot(oa_ref[0], wa_ref[:, cols], preferred_element_type=F32)
        pb = jnp.dot(ob_ref[...], wb_ref[:, cols], preferred_element_type=F32)
        mg_ref[:, cols] = (gate(cols) * pa + gate(gcols) * pb).astype(BF16)

    for c in range(TM_MERGE // MERGE_ROWS):
        rows = slice(c * MERGE_ROWS, (c + 1) * MERGE_ROWS)
        mixed = jnp.dot(mg_ref[rows, :], wo_ref[...], preferred_element_type=F32)
        y_ref[0, rows, :] = _layer_norm(alpha * x_ref[0, rows, :] + mixed, g_ref[...], b_ref[...])


def _merge(alpha, oa, obs, lses, x, w_gate, b_gate, wa, wb, wo, g, bb):
    b, s, _ = x.shape
    tm = TM_MERGE
    tok = lambda width: pl.BlockSpec((1, tm, width), lambda i, j: (i, j, 0))
    full = lambda a: pl.BlockSpec(a.shape, lambda i, j: (0, 0), pipeline_mode=pl.Buffered(1))
    streams = [pl.BlockSpec((1, a.shape[1], tm // a.shape[1], B_OUT_WIDTH), lambda i, j: (i, 0, j, 0))
               for a in list(obs) + list(lses)]
    return pl.pallas_call(
        functools.partial(_merge_kernel, alpha),
        grid=(b, s // tm),
        in_specs=[tok(A_WIDTH)] + streams + [tok(D_MODEL), full(w_gate),
                  full(b_gate), full(wa), full(wb), full(wo), full(g), full(bb)],
        out_specs=tok(D_MODEL),
        out_shape=jax.ShapeDtypeStruct((b, s, D_MODEL), F32),
        scratch_shapes=[pltpu.VMEM((4, B_OUT_WIDTH // LANES, tm, LANES), F32),
                        pltpu.VMEM((tm, B_OUT_WIDTH), BF16),
                        pltpu.VMEM((tm, D_MODEL), BF16),
                        pltpu.VMEM((tm, D_MODEL), BF16)],
        compiler_params=_params(("parallel", "parallel")),
        name="merge_outproj_norm",
    )(oa, *obs, *lses, x, w_gate, b_gate, wa, wb, wo, g, bb)


def _ffn_kernel(alpha, x_ref, wg_ref, wu_ref, wd_ref, g_ref, b_ref, y_ref):
    for r in range(TM_FFN // TM):
        rows = slice(r * TM, (r + 1) * TM)
        x = x_ref[rows, :]
        xb = x.astype(BF16)
        acc = jnp.zeros((TM, D_MODEL), F32)
        for c in range(FFN_HIDDEN // FFN_CHUNK):
            cols = slice(c * FFN_CHUNK, (c + 1) * FFN_CHUNK)
            gate = jnp.dot(xb, wg_ref[:, cols], preferred_element_type=F32)
            up = jnp.dot(xb, wu_ref[:, cols], preferred_element_type=F32)
            h = gate / (1.0 + jnp.exp(-gate)) * up
            acc = acc + jnp.dot(h.astype(BF16), wd_ref[cols, :], preferred_element_type=F32)
        y_ref[rows, :] = _layer_norm(alpha * x + acc, g_ref[...], b_ref[...])


def _ffn(alpha, x2, wg, wu, wd, g, bb):
    m = x2.shape[0]
    tok = pl.BlockSpec((TM_FFN, D_MODEL), lambda i: (i, 0))
    full = lambda a: pl.BlockSpec(a.shape, lambda i: (0, 0), pipeline_mode=pl.Buffered(1))
    return pl.pallas_call(
        functools.partial(_ffn_kernel, alpha),
        grid=(m // TM_FFN,),
        in_specs=[tok, full(wg), full(wu), full(wd), full(g), full(bb)],
        out_specs=tok,
        out_shape=jax.ShapeDtypeStruct((m, D_MODEL), F32),
        compiler_params=_params(("parallel",)),
        name="swiglu_norm",
    )(x2, wg, wu, wd, g, bb)


def _split_w_in(w):
    a, bw, hi = A_WIDTH, B_WIDTH, IDX_HEADS * IDX_DIM
    o = 0
    qa, o = w[:, o:o + a], o + a
    ka, o = w[:, o:o + a], o + a
    va, o = w[:, o:o + a], o + a
    qb, o = w[:, o:o + bw], o + bw
    kb, o = w[:, o:o + bw], o + bw
    vb, o = w[:, o:o + bw], o + bw
    qi, o = w[:, o:o + hi], o + hi
    ki, o = w[:, o:o + IDX_DIM], o + IDX_DIM
    wi, o = w[:, o:o + IDX_HEADS], o + IDX_HEADS
    w_gate = w[:, o:].astype(BF16)
    w_nat = jnp.concatenate([ka, qb, kb, vb, ki, ki], axis=1).astype(BF16)
    pad = jnp.zeros((w.shape[0], _T_ROWS - _T_WI - IDX_HEADS), w.dtype)
    w_t = jnp.concatenate([qa, va, qi, wi, pad], axis=1).T.astype(BF16)
    return w_nat, w_t, w_gate


def kernel(x, positions, w_in, b_gate, w_branch_a, w_branch_b, w_out, ln1_g, ln1_b,
           w_ffn_gate, w_ffn_up, w_ffn_down, ln2_g, ln2_b):
    b, s, d = x.shape
    depth = w_in.shape[0]
    assert d == D_MODEL and s % (max(B_DILATIONS) * TB) == 0 and s % max(TM, TM_MERGE) == 0
    assert (b * s) % TM_FFN == 0 and B_DILATIONS[0] == 1
    alpha = (2 * depth) ** 0.25
    cos, sin, cos_t, sin_t = _rope_tables(positions)
    row = lambda v: v.reshape(1, -1)
    for layer in range(depth):
        w_nat, w_t, w_gate = _split_w_in(w_in[layer])
        ka6, ki4, *streams = _proj_nat(x, cos, sin, w_nat)
        qt, vt5, qi4t, wt = _proj_t(x, cos_t, sin_t, w_t)
        oa = _dsa_attention(qt, qi4t, wt, ka6, vt5, ki4)
        groups = [_window_attention(*streams[3 * g:3 * g + 3]) for g in range(len(B_DILATIONS))]
        x1 = _merge(alpha, oa, [o for o, _ in groups], [l for _, l in groups], x, w_gate, row(b_gate[layer]),
                    w_branch_a[layer].astype(BF16), w_branch_b[layer].astype(BF16),
                    w_out[layer].astype(BF16), row(ln1_g[layer]), row(ln1_b[layer]))
        x2 = _ffn(alpha, x1.reshape(b * s, d), w_ffn_gate[layer].astype(BF16), w_ffn_up[layer].astype(BF16),
                  w_ffn_down[layer].astype(BF16), row(ln2_g[layer]), row(ln2_b[layer]))
        x = x2.reshape(b, s, d)
    return x
```

```python
import functools

import jax
import jax.numpy as jnp
from jax import lax
from jax.experimental import pallas as pl
from jax.experimental.pallas import tpu as pltpu

F32 = jnp.float32
BF16 = jnp.bfloat16

D_MODEL = 1024
HEAD_DIM = 64
ROT_HALF = 8
ROPE_THETA = 500000.0
ATTN_SCALE = HEAD_DIM ** -0.5
LOG2E = 1.4426950408889634
V_ROWS = HEAD_DIM + 16
A_HEADS = 12
A_WIDTH = A_HEADS * HEAD_DIM
IDX_HEADS = 8
IDX_DIM = 64
IDX_SCALE = (IDX_HEADS ** -0.5) * (IDX_DIM ** -0.5)
TOPK_MAX = 256
B_DILATIONS = (1, 4, 16)
B_WINDOW_STEPS = 128
B_HEADS_PER_GROUP = 4
B_WIDTH = 3 * B_HEADS_PER_GROUP * HEAD_DIM
B_OUT_WIDTH = B_HEADS_PER_GROUP * HEAD_DIM
FFN_HIDDEN = 2816
LN_EPS = 1e-5
NEG = -1e30
MASK_NEG = -(2.0 ** 100)

LANES = 128
VMEM_LIMIT = 56 * 1024 * 1024
TM = 512
TM_FFN = 1024
TM_MERGE = 1024
TQ = 256
TK = 256
TB = 256
FFN_CHUNK = 256
KEY_TILES_PER_STEP = 4
BISECT_STEPS_PER_CHECK = 2
BISECT_BLIND_STEPS = 12
WINDOW_TILES_PER_STEP = 8
MERGE_ROWS = 256
MERGE_COLS = 256
ROPE_TILE = 2048

_N_KA, _N_QB, _N_KB, _N_VB = 0, 768, 1536, 2304
_N_KI, _N_NAT = 3072, 3200
_T_QA, _T_VA, _T_QI, _T_WI, _T_ROWS = 0, 768, 1536, 2048, 2064


def _params(sem):
    return pltpu.CompilerParams(dimension_semantics=sem, vmem_limit_bytes=VMEM_LIMIT)


def _rope_kernel(pr_ref, fc_ref, c_ref, s_ref, ct_ref, st_ref):
    ang_t = fc_ref[...] * pr_ref[...].astype(F32)
    cos_t, sin_t = jnp.cos(ang_t), jnp.sin(ang_t)
    ct_ref[...] = cos_t
    st_ref[...] = sin_t
    reps = LANES // ROT_HALF
    cos = jnp.concatenate([cos_t] * reps, axis=0).T
    sin = jnp.concatenate([sin_t] * reps, axis=0).T
    d = lax.broadcasted_iota(jnp.int32, cos.shape, 1) & (HEAD_DIM - 1)
    c_ref[...] = jnp.where(d < 2 * ROT_HALF, cos, 1.0)
    s_ref[...] = jnp.where(d < ROT_HALF, -sin, jnp.where(d < 2 * ROT_HALF, sin, 0.0))


def _rope_tables(positions):
    m = positions.size
    inv_freq = ROPE_THETA ** (-jnp.arange(0, 2 * ROT_HALF, 2, dtype=F32) / (2 * ROT_HALF))
    t = min(ROPE_TILE, m)
    return pl.pallas_call(
        _rope_kernel,
        grid=(m // t,),
        in_specs=[pl.BlockSpec((1, t), lambda i: (0, i)), pl.BlockSpec((ROT_HALF, 1), lambda i: (0, 0))],
        out_specs=[pl.BlockSpec((t, LANES), lambda i: (i, 0)), pl.BlockSpec((t, LANES), lambda i: (i, 0)),
                   pl.BlockSpec((ROT_HALF, t), lambda i: (0, i)), pl.BlockSpec((ROT_HALF, t), lambda i: (0, i))],
        out_shape=[jax.ShapeDtypeStruct((m, LANES), F32), jax.ShapeDtypeStruct((m, LANES), F32),
                   jax.ShapeDtypeStruct((ROT_HALF, m), F32), jax.ShapeDtypeStruct((ROT_HALF, m), F32)],
        compiler_params=_params(("parallel",)),
        name="rope_tables",
    )(positions.reshape(1, m), inv_freq.reshape(ROT_HALF, 1))


def _proj_nat_kernel(x_ref, c_ref, s_ref, w_ref, ka_ref, ki_ref, *rest):
    streams, y_ref = rest[:-1], rest[-1]
    xb = x_ref[0].astype(BF16)
    cos, sin = c_ref[...], s_ref[...]
    lane = lax.broadcasted_iota(jnp.int32, cos.shape, 1)
    first = (lane & (HEAD_DIM - 1)) < ROT_HALF

    def rope(y):
        partner = jnp.where(first, pltpu.roll(y, LANES - ROT_HALF, 1), pltpu.roll(y, ROT_HALF, 1))
        return y * cos + partner * sin

    def proj(lo, hi):
        return jnp.dot(xb, w_ref[:, lo:hi], preferred_element_type=F32)

    blocks = A_WIDTH // LANES
    y = proj(_N_KA, _N_KA + A_WIDTH)
    for p in range(blocks):
        ka_ref[0, p] = rope(y[:, p * LANES:(p + 1) * LANES]).astype(BF16)

    def scatter_streams(which):
        per_group = B_OUT_WIDTH // LANES
        for g, d in enumerate(B_DILATIONS):
            out = streams[3 * g + which]
            for r in range(d):
                for p in range(per_group):
                    rows = y_ref[g * per_group + p, pl.ds(r, TM // d, stride=d), :]
                    out[0, r, :, p * LANES:(p + 1) * LANES] = rows.astype(BF16)

    y = proj(_N_QB, _N_QB + B_WIDTH)
    for p in range(blocks):
        y_ref[p] = rope(y[:, p * LANES:(p + 1) * LANES]) * (ATTN_SCALE * LOG2E)
    scatter_streams(0)
    y = proj(_N_KB, _N_KB + B_WIDTH)
    for p in range(blocks):
        y_ref[p] = rope(y[:, p * LANES:(p + 1) * LANES])
    scatter_streams(1)
    y = proj(_N_VB, _N_VB + B_WIDTH)
    for p in range(blocks):
        y_ref[p] = y[:, p * LANES:(p + 1) * LANES]
    scatter_streams(2)
    r = rope(proj(_N_KI, _N_KI + LANES))
    hi = r.astype(BF16).astype(F32)
    hl = jnp.where(lane < IDX_DIM, hi, r - hi).astype(BF16)
    ki_ref[0, :, 0:LANES] = hl
    ki_ref[0, :, LANES:2 * LANES] = hl


def _proj_nat(x, cos, sin, w_nat):
    b, s, d = x.shape
    nt = s // TM
    tok = lambda width: pl.BlockSpec((1, TM, width), lambda i, j: (i, j, 0))
    stream_specs, stream_shapes = [], []
    for dil in B_DILATIONS:
        for _ in range(3):
            stream_specs.append(pl.BlockSpec((1, dil, TM // dil, B_OUT_WIDTH), lambda i, j: (i, 0, j, 0)))
            stream_shapes.append(jax.ShapeDtypeStruct((b, dil, s // dil, B_OUT_WIDTH), BF16))
    return pl.pallas_call(
        _proj_nat_kernel,
        grid=(b, nt),
        in_specs=[tok(d),
                  pl.BlockSpec((TM, LANES), lambda i, j: (i * nt + j, 0)),
                  pl.BlockSpec((TM, LANES), lambda i, j: (i * nt + j, 0)),
                  pl.BlockSpec((d, _N_NAT), lambda i, j: (0, 0))],
        out_specs=[pl.BlockSpec((1, A_WIDTH // LANES, TM, LANES), lambda i, j: (i, 0, j, 0)),
                   tok(2 * LANES)] + stream_specs,
        out_shape=[jax.ShapeDtypeStruct((b, A_WIDTH // LANES, s, LANES), BF16),
                   jax.ShapeDtypeStruct((b, s, 2 * LANES), BF16)] + stream_shapes,
        scratch_shapes=[pltpu.VMEM((B_WIDTH // LANES, TM, LANES), F32)],
        compiler_params=_params(("parallel", "parallel")),
        name="proj_token_major",
    )(x, cos, sin, w_nat)


def _proj_t_kernel(x_ref, ct_ref, st_ref, w_ref, qt_ref, vt_ref, qi_ref, wt_ref):
    xb = x_ref[0].astype(BF16)
    cos, sin = ct_ref[...], st_ref[...]

    def proj(lo, hi):
        return lax.dot_general(w_ref[lo:hi, :], xb, (((1,), (1,)), ((), ())), preferred_element_type=F32)

    def rope_head(y):
        x1, x2 = y[0:ROT_HALF], y[ROT_HALF:2 * ROT_HALF]
        return jnp.concatenate([x1 * cos - x2 * sin, x2 * cos + x1 * sin, y[2 * ROT_HALF:]], axis=0)

    y = proj(_T_QA, _T_QA + A_WIDTH)
    for h in range(A_HEADS):
        r = rope_head(y[h * HEAD_DIM:(h + 1) * HEAD_DIM])
        qt_ref[0, h * HEAD_DIM:(h + 1) * HEAD_DIM, :] = (r * (ATTN_SCALE * LOG2E)).astype(BF16)
    y = proj(_T_VA, _T_VA + A_WIDTH)
    for h in range(A_HEADS):
        for c in range(TM // TK):
            vt_ref[0, h, c, 0:HEAD_DIM, :] = y[h * HEAD_DIM:(h + 1) * HEAD_DIM, c * TK:(c + 1) * TK].astype(BF16)
            vt_ref[0, h, c, HEAD_DIM:V_ROWS, :] = jnp.ones((V_ROWS - HEAD_DIM, TK), BF16)
    y = proj(_T_QI, _T_QI + IDX_HEADS * IDX_DIM)
    for h in range(IDX_HEADS):
        r = rope_head(y[h * IDX_DIM:(h + 1) * IDX_DIM])
        hi = r.astype(BF16)
        lo = (r - hi.astype(F32)).astype(BF16)
        base = 4 * h * IDX_DIM
        qi_ref[0, base:base + IDX_DIM, :] = hi
        qi_ref[0, base + IDX_DIM:base + 2 * IDX_DIM, :] = hi
        qi_ref[0, base + 2 * IDX_DIM:base + 3 * IDX_DIM, :] = lo
        qi_ref[0, base + 3 * IDX_DIM:base + 4 * IDX_DIM, :] = lo
    wt_ref[0] = proj(_T_WI, _T_ROWS)[0:IDX_HEADS] * IDX_SCALE


def _proj_t(x, cos_t, sin_t, w_t):
    b, s, d = x.shape
    nt = s // TM
    return pl.pallas_call(
        _proj_t_kernel,
        grid=(b, nt),
        in_specs=[pl.BlockSpec((1, TM, d), lambda i, j: (i, j, 0)),
                  pl.BlockSpec((ROT_HALF, TM), lambda i, j: (0, i * nt + j)),
                  pl.BlockSpec((ROT_HALF, TM), lambda i, j: (0, i * nt + j)),
                  pl.BlockSpec((_T_ROWS, d), lambda i, j: (0, 0))],
        out_specs=[pl.BlockSpec((1, A_WIDTH, TM), lambda i, j: (i, 0, j)),
                   pl.BlockSpec((1, A_HEADS, TM // TK, V_ROWS, TK), lambda i, j: (i, 0, j, 0, 0)),
                   pl.BlockSpec((1, 4 * IDX_HEADS * IDX_DIM, TM), lambda i, j: (i, 0, j)),
                   pl.BlockSpec((1, IDX_HEADS, TM), lambda i, j: (i, 0, j))],
        out_shape=[jax.ShapeDtypeStruct((b, A_WIDTH, s), BF16),
                   jax.ShapeDtypeStruct((b, A_HEADS, s // TK, V_ROWS, TK), BF16),
                   jax.ShapeDtypeStruct((b, 4 * IDX_HEADS * IDX_DIM, s), BF16),
                   jax.ShapeDtypeStruct((b, IDX_HEADS, s), F32)],
        compiler_params=_params(("parallel", "parallel")),
        name="proj_feature_major",
    )(x, cos_t, sin_t, w_t)


def _dsa_kernel(qt_ref, qi_ref, wt_ref, k_ref, vt_ref, ki_ref, o_ref,
                sc_ref, qpad_ref, ot_ref, m_ref, st_ref, stat_ref, sa_ref, sb_ref):
    j = pl.program_id(1)
    nkt = j + 1

    for h in range(A_HEADS):
        off = (h % 2) * HEAD_DIM
        qpad_ref[h] = jnp.zeros((2 * HEAD_DIM, TQ), BF16)
        qpad_ref[h, off:off + HEAD_DIM, :] = qt_ref[0, h * HEAD_DIM:(h + 1) * HEAD_DIM, :]

    qpos = j * TQ + lax.broadcasted_iota(jnp.int32, (1, TQ), 1)
    row_iota = lax.broadcasted_iota(jnp.int32, (TK, TQ), 0)
    w = wt_ref[0]

    def fold(t):
        return t.reshape(TK // 8, 8, TQ)

    def score_tile(kt, diagonal, carry):
        mn, mx, pos, nn = carry
        ki = ki_ref[0, pl.ds(pl.multiple_of(kt * TK, TK), TK), :]
        acc = jnp.zeros((TK, TQ), F32)
        for h in range(IDX_HEADS):
            s = jnp.dot(ki, qi_ref[0, 4 * h * IDX_DIM:4 * (h + 1) * IDX_DIM, :], preferred_element_type=F32)
            acc = acc + w[h:h + 1, :] * jnp.maximum(s, 0.0)
        if diagonal:
            causal = (kt * TK + row_iota) <= qpos
            val = jnp.where(causal, acc, -jnp.inf)
            low = jnp.where(causal, acc, jnp.inf)
        else:
            val = low = acc
        sc_ref[kt] = val
        return (jnp.minimum(mn, fold(low).min(axis=0)),
                jnp.maximum(mx, fold(val).max(axis=0)),
                pos + fold(jnp.where(val > 0.0, 1.0, 0.0)).sum(axis=0),
                nn + fold(jnp.where(val >= 0.0, 1.0, 0.0)).sum(axis=0))

    def load_stats():
        return tuple(stat_ref[n] for n in range(4))

    def store_stats(stats):
        for n, v in enumerate(stats):
            stat_ref[n] = v

    def score_block(i, carry):
        for u in range(KEY_TILES_PER_STEP):
            carry = score_tile(KEY_TILES_PER_STEP * i + u, False, carry)
        return carry

    zeros8 = jnp.zeros((8, TQ), F32)
    store_stats(lax.fori_loop(
        0, j // KEY_TILES_PER_STEP, score_block,
        (jnp.full((8, TQ), jnp.inf, F32), jnp.full((8, TQ), -jnp.inf, F32), zeros8, zeros8)))

    for left in range(KEY_TILES_PER_STEP):
        @pl.when(j % KEY_TILES_PER_STEP == left)
        def _(left=left):
            stats = load_stats()
            for u in range(left):
                stats = score_tile(j - left + u, False, stats)
            store_stats(score_tile(j, True, stats))

    mn8, mx8, pos8, nn8 = load_stats()
    mn = mn8.min(axis=0, keepdims=True)
    mx = mx8.max(axis=0, keepdims=True)
    c_pos = pos8.sum(axis=0, keepdims=True)
    c_nn = nn8.sum(axis=0, keepdims=True)

    sc_ref[nkt] = jnp.full((TK, TQ), -jnp.inf, F32)

    def count(preds):
        def part(p, kt):
            return p(sc_ref[kt]).reshape(TK // 8, 8, TQ).sum(axis=0)

        def body(i, cnts):
            return tuple(c + part(p, 2 * i) + part(p, 2 * i + 1) for c, p in zip(cnts, preds))

        cnts = lax.fori_loop(0, (nkt + 1) // 2, body, tuple(jnp.zeros((8, TQ), F32) for _ in preds))
        return [c.sum(axis=0, keepdims=True) for c in cnts]

    def ge(cand):
        return lambda t: jnp.where(t >= cand, 1.0, 0.0)

    n_causal = (qpos + 1).astype(F32)
    k_q = jnp.minimum(n_causal, float(TOPK_MAX))
    c_mx, = count([ge(mx)])
    select = n_causal > k_q
    at_max = select & (c_mx >= k_q)
    search = select & (c_mx < k_q)
    at_zero = search & (c_pos < k_q) & (c_nn >= k_q)
    above = search & (c_pos >= k_q)
    below = search & (c_nn < k_q)
    lo0 = jnp.where(at_max, mx, jnp.where(at_zero | above, 0.0, mn))
    c_lo0 = jnp.where(at_max, c_mx, jnp.where(at_zero | above, c_nn, n_causal))
    st_ref[0:1, :] = lo0
    st_ref[1:2, :] = jnp.where(below, 0.0, mx)
    st_ref[2:3, :] = c_lo0
    st_ref[3:4, :] = jnp.where(at_max, 0.0, jnp.where(at_zero, c_pos, jnp.where(below, c_nn, c_mx)))
    act0 = jnp.where((above | below) & (c_lo0 > k_q), 1.0, 0.0)
    st_ref[4:5, :] = act0

    def bisect_once():
        lo, hi, c_lo, c_hi = st_ref[0:1, :], st_ref[1:2, :], st_ref[2:3, :], st_ref[3:4, :]
        act = st_ref[4:5, :] > 0.0
        mid = 0.5 * lo + 0.5 * hi
        live = act & (mid > lo) & (mid < hi)
        c, = count([ge(mid)])
        up = live & (c >= k_q)
        dn = live & (c < k_q)
        c_lo = jnp.where(up, c, c_lo)
        st_ref[0:1, :] = jnp.where(up, mid, lo)
        st_ref[1:2, :] = jnp.where(dn, mid, hi)
        st_ref[2:3, :] = c_lo
        st_ref[3:4, :] = jnp.where(dn, c, c_hi)
        act_new = jnp.where(live & (c_lo > k_q), 1.0, 0.0)
        st_ref[4:5, :] = act_new
        return act_new

    def bisect(go):
        for _ in range(BISECT_STEPS_PER_CHECK - 1):
            bisect_once()
        return (jnp.max(bisect_once()) > 0.0).astype(jnp.int32)

    @pl.when(jnp.max(act0) > 0.0)
    def _():
        def blind(_, carry):
            bisect_once()
            bisect_once()
            return carry

        lax.fori_loop(0, BISECT_BLIND_STEPS // 2, blind, 0)

    lax.while_loop(lambda go: go > 0, bisect, (jnp.max(st_ref[4:5, :]) > 0.0).astype(jnp.int32))

    lo = st_ref[0:1, :]
    tie = st_ref[2:3, :] > k_q

    @pl.when(jnp.max(jnp.where(tie, 1.0, 0.0)) > 0.0)
    def _():
        need = jnp.where(tie, k_q - st_ref[3:4, :], jnp.inf)
        tri = jnp.where(lax.broadcasted_iota(jnp.int32, (TK, TK), 0) >= lax.broadcasted_iota(jnp.int32, (TK, TK), 1),
                        1.0, 0.0).astype(BF16)

        def drop(kt, seen):
            t = sc_ref[kt]
            eq = jnp.where(t == lo, 1.0, 0.0)
            rank = jnp.dot(tri, eq.astype(BF16), preferred_element_type=F32) + seen
            sc_ref[kt] = jnp.where(t == lo, jnp.where(rank > need, -jnp.inf, t), t)
            return seen + fold(eq).sum(axis=0).sum(axis=0, keepdims=True)

        def drop_pair(i, seen):
            return drop(2 * i + 1, drop(2 * i, seen))

        lax.fori_loop(0, (nkt + 1) // 2, drop_pair, jnp.zeros((1, TQ), F32))

    m_ref[...] = jnp.full(m_ref.shape, MASK_NEG, F32)
    ot_ref[...] = jnp.zeros(ot_ref.shape, F32)

    def mask_bias(kt):
        return jnp.where(sc_ref[kt] >= lo, 0.0, MASK_NEG).astype(BF16)

    def logits(h, kt, bias):
        rows = pl.ds(pl.multiple_of(kt * TK, TK), TK)
        return jnp.dot(k_ref[0, h // 2, rows, :], qpad_ref[h], preferred_element_type=F32).astype(BF16) + bias

    def half_step(kt, cur_ref, nxt_ref):
        if nxt_ref is not None:
            bias_n = mask_bias(kt + 1)
        for h in range(A_HEADS):
            s = cur_ref[h]
            if nxt_ref is not None:
                nxt_ref[h] = logits(h, kt + 1, bias_n)
            m = m_ref[h]
            m_tile = s.reshape(TK // 16, 16, TQ).max(axis=0).astype(F32).max(axis=0, keepdims=True)
            m_new = jnp.maximum(m, m_tile)
            p = jnp.exp2(s - m_new.astype(BF16))
            corr = jnp.exp2(m - m_new)
            m_ref[h] = m_new
            ot_ref[h] = ot_ref[h] * corr + jnp.dot(vt_ref[0, h, kt], p, preferred_element_type=F32)

    bias0 = mask_bias(0)
    for h in range(A_HEADS):
        sa_ref[h] = logits(h, 0, bias0)

    def buffers(u):
        return (sa_ref, sb_ref) if u % 2 == 0 else (sb_ref, sa_ref)

    def kv_block(i, _):
        for u in range(KEY_TILES_PER_STEP):
            half_step(KEY_TILES_PER_STEP * i + u, *buffers(u))
        return 0

    n_blocks = (nkt - 1) // KEY_TILES_PER_STEP
    lax.fori_loop(0, n_blocks, kv_block, 0)

    for left in range(1, KEY_TILES_PER_STEP + 1):
        @pl.when(nkt - KEY_TILES_PER_STEP * n_blocks == left)
        def _(left=left):
            for u in range(left):
                cur_ref, nxt_ref = buffers(u)
                half_step(nkt - left + u, cur_ref, nxt_ref if u < left - 1 else None)

    for p in range(A_HEADS // 2):
        both = jnp.concatenate([ot_ref[h, 0:HEAD_DIM, :] / ot_ref[h, HEAD_DIM:HEAD_DIM + 1, :]
                                for h in (2 * p, 2 * p + 1)], axis=0)
        o_ref[0, :, p * LANES:(p + 1) * LANES] = both.T.astype(BF16)


def _dsa_attention(qt, qi4t, wt, k6, vt5, ki4):
    b, _, s = qt.shape
    nq = s // TQ
    return pl.pallas_call(
        _dsa_kernel,
        grid=(b, nq),
        in_specs=[pl.BlockSpec((1, A_WIDTH, TQ), lambda i, j: (i, 0, j)),
                  pl.BlockSpec((1, 4 * IDX_HEADS * IDX_DIM, TQ), lambda i, j: (i, 0, j)),
                  pl.BlockSpec((1, IDX_HEADS, TQ), lambda i, j: (i, 0, j)),
                  pl.BlockSpec((1, A_WIDTH // LANES, s, LANES), lambda i, j: (i, 0, 0, 0)),
                  pl.BlockSpec((1, A_HEADS, s // TK, V_ROWS, TK), lambda i, j: (i, 0, 0, 0, 0)),
                  pl.BlockSpec((1, s, 4 * IDX_DIM), lambda i, j: (i, 0, 0))],
        out_specs=pl.BlockSpec((1, TQ, A_WIDTH), lambda i, j: (i, j, 0)),
        out_shape=jax.ShapeDtypeStruct((b, s, A_WIDTH), BF16),
        scratch_shapes=[pltpu.VMEM((s // TK + 1, TK, TQ), F32),
                        pltpu.VMEM((A_HEADS, 2 * HEAD_DIM, TQ), BF16),
                        pltpu.VMEM((A_HEADS, V_ROWS, TQ), F32),
                        pltpu.VMEM((A_HEADS, 1, TQ), F32),
                        pltpu.VMEM((8, TQ), F32),
                        pltpu.VMEM((4, 8, TQ), F32),
                        pltpu.VMEM((A_HEADS, TK, TQ), BF16),
                        pltpu.VMEM((A_HEADS, TK, TQ), BF16)],
        compiler_params=_params(("parallel", "arbitrary")),
        name="dsa_attention",
    )(qt, qi4t, wt, k6, vt5, ki4)


def _window_kernel(q_ref, kp_ref, kc_ref, vp_ref, vc_ref, o_ref, lse_ref):
    hb = B_WINDOW_STEPS
    n_streams, n_rows = q_ref.shape[1], q_ref.shape[2]
    r = lax.broadcasted_iota(jnp.int32, (hb, 2 * hb), 0)
    c = lax.broadcasted_iota(jnp.int32, (hb, 2 * hb), 1)
    dist = r + hb - c
    band = jnp.where(dist >= 0, jnp.where(dist <= B_WINDOW_STEPS, 0.0, NEG), NEG)
    first_col = jnp.where(pl.program_id(2) > 0, 0, hb)
    band_first = jnp.where(c >= first_col, band, NEG)
    lane = lax.broadcasted_iota(jnp.int32, (hb, LANES), 1)
    left = lane < HEAD_DIM
    ones = jnp.ones((2 * hb, LANES), BF16)
    for sb in range(n_streams):
        for t in range(n_rows // TB):
            for p in range(B_HEADS_PER_GROUP // 2):
                cols = slice(p * LANES, (p + 1) * LANES)
                tile = slice(t * TB, (t + 1) * TB)
                q2, kc, vc = q_ref[0, sb, tile, cols], kc_ref[0, sb, tile, cols], vc_ref[0, sb, tile, cols]
                if t == 0:
                    k_before, v_before = kp_ref[0, sb, :, cols], vp_ref[0, sb, :, cols]
                else:
                    before = slice(t * TB - hb, t * TB)
                    k_before, v_before = kc_ref[0, sb, before, cols], vc_ref[0, sb, before, cols]
                windows = ((jnp.concatenate([k_before, kc[:hb]], axis=0),
                            jnp.concatenate([v_before, vc[:hb]], axis=0)), (kc, vc))
                for half, ((kw, vw), bias) in enumerate(zip(windows, (band_first if t == 0 else band, band))):
                    rows = slice(t * TB + half * hb, t * TB + (half + 1) * hb)
                    v1 = jnp.concatenate([vw, ones], axis=1)
                    outs, lses = [], []
                    for side in (left, ~left):
                        qh = jnp.where(side, q2[half * hb:(half + 1) * hb], jnp.zeros((hb, LANES), BF16))
                        s = lax.dot_general(qh, kw, (((1,), (1,)), ((), ())), preferred_element_type=F32) + bias
                        m = s.max(axis=-1, keepdims=True)
                        e = jnp.exp2((s - m).astype(BF16))
                        ol = jnp.dot(e, v1, preferred_element_type=F32)
                        l = ol[:, LANES:]
                        outs.append(ol[:, :LANES] / l)
                        lses.append(m + jnp.log2(l))
                    o_ref[0, sb, rows, cols] = jnp.where(left, outs[0], outs[1])
                    lse_ref[0, sb, rows, cols] = jnp.where(left, lses[0], lses[1])


def _window_attention(q, k, v):
    b, d, n, _ = q.shape
    assert TB == 2 * B_WINDOW_STEPS
    rows = min(n, WINDOW_TILES_PER_STEP * TB)
    streams = min(d, WINDOW_TILES_PER_STEP * TB // rows)
    halves = rows // (TB // 2)
    cur = pl.BlockSpec((1, streams, rows, B_OUT_WIDTH), lambda bi, ri, ti: (bi, ri, ti, 0))
    prev = pl.BlockSpec((1, streams, TB // 2, B_OUT_WIDTH),
                        lambda bi, ri, ti: (bi, ri, jnp.maximum(halves * ti - 1, 0), 0))
    return pl.pallas_call(
        _window_kernel,
        grid=(b, d // streams, n // rows),
        in_specs=[cur, prev, cur, prev, cur],
        out_specs=[cur, cur],
        out_shape=[jax.ShapeDtypeStruct((b, d, n, B_OUT_WIDTH), F32)] * 2,
        compiler_params=_params(("parallel", "parallel", "arbitrary")),
        name=f"window_attention_d{d}",
    )(q, k, k, v, v)


def _layer_norm(y, g, b):
    mu = y.mean(axis=-1, keepdims=True)
    yc = y - mu
    var = (yc * yc).mean(axis=-1, keepdims=True)
    return yc * lax.rsqrt(var + LN_EPS) * g + b


def _merge_kernel(alpha, oa_ref, o0_ref, o1_ref, o2_ref, l0_ref, l1_ref, l2_ref, x_ref,
                  wg_ref, bg_ref, wa_ref, wb_ref, wo_ref, g_ref, b_ref, y_ref,
                  tok_ref, ob_ref, xb_ref, mg_ref):
    halves = B_OUT_WIDTH // LANES

    for n, ref in enumerate((o1_ref, o2_ref, l1_ref, l2_ref)):
        d = ref.shape[1]
        for r in range(d):
            for p in range(halves):
                tok_ref[n, p, pl.ds(r, TM_MERGE // d, stride=d), :] = ref[0, r, :, p * LANES:(p + 1) * LANES]
    for c in range(TM_MERGE // MERGE_ROWS):
        rows = slice(c * MERGE_ROWS, (c + 1) * MERGE_ROWS)
        for p in range(halves):
            cols = slice(p * LANES, (p + 1) * LANES)
            l0, l1, l2 = l0_ref[0, 0, rows, cols], tok_ref[2, p, rows, :], tok_ref[3, p, rows, :]
            lm = jnp.maximum(jnp.maximum(l0, l1), l2)
            e0, e1, e2 = jnp.exp2(l0 - lm), jnp.exp2(l1 - lm), jnp.exp2(l2 - lm)
            ob = (e0 * o0_ref[0, 0, rows, cols] + e1 * tok_ref[0, p, rows, :] + e2 * tok_ref[1, p, rows, :])
            ob_ref[rows, cols] = (ob / (e0 + e1 + e2)).astype(BF16)

    xb_ref[...] = x_ref[0].astype(BF16)
    for c in range(D_MODEL // MERGE_COLS):
        cols = slice(c * MERGE_COLS, (c + 1) * MERGE_COLS)
        gcols = slice(D_MODEL + c * MERGE_COLS, D_MODEL + (c + 1) * MERGE_COLS)

        def gate(sel):
            z = jnp.dot(xb_ref[...], wg_ref[:, sel], preferred_element_type=F32) + bg_ref[:, sel]
            return 1.0 / (1.0 + jnp.exp(-z))

        pa = jnp.dot(oa_ref[0], wa_ref[:, cols], preferred_element_type=F32)
        pb = jnp.dot(ob_ref[...], wb_ref[:, cols], preferred_element_type=F32)
        mg_ref[:, cols] = (gate(cols) * pa + gate(gcols) * pb).astype(BF16)

    for c in range(TM_MERGE // MERGE_ROWS):
        rows = slice(c * MERGE_ROWS, (c + 1) * MERGE_ROWS)
        mixed = jnp.dot(mg_ref[rows, :], wo_ref[...], preferred_element_type=F32)
        y_ref[0, rows, :] = _layer_norm(alpha * x_ref[0, rows, :] + mixed, g_ref[...], b_ref[...])


def _merge(alpha, oa, obs, lses, x, w_gate, b_gate, wa, wb, wo, g, bb):
    b, s, _ = x.shape
    tm = TM_MERGE
    tok = lambda width: pl.BlockSpec((1, tm, width), lambda i, j: (i, j, 0))
    full = lambda a: pl.BlockSpec(a.shape, lambda i, j: (0, 0), pipeline_mode=pl.Buffered(1))
    streams = [pl.BlockSpec((1, a.shape[1], tm // a.shape[1], B_OUT_WIDTH), lambda i, j: (i, 0, j, 0))
               for a in list(obs) + list(lses)]
    return pl.pallas_call(
        functools.partial(_merge_kernel, alpha),
        grid=(b, s // tm),
        in_specs=[tok(A_WIDTH)] + streams + [tok(D_MODEL), full(w_gate),
                  full(b_gate), full(wa), full(wb), full(wo), full(g), full(bb)],
        out_specs=tok(D_MODEL),
        out_shape=jax.ShapeDtypeStruct((b, s, D_MODEL), F32),
        scratch_shapes=[pltpu.VMEM((4, B_OUT_WIDTH // LANES, tm, LANES), F32),
                        pltpu.VMEM((tm, B_OUT_WIDTH), BF16),
                        pltpu.VMEM((tm, D_MODEL), BF16),
                        pltpu.VMEM((tm, D_MODEL), BF16)],
        compiler_params=_params(("parallel", "parallel")),
        name="merge_outproj_norm",
    )(oa, *obs, *lses, x, w_gate, b_gate, wa, wb, wo, g, bb)


def _ffn_kernel(alpha, x_ref, wg_ref, wu_ref, wd_ref, g_ref, b_ref, y_ref):
    for r in range(TM_FFN // TM):
        rows = slice(r * TM, (r + 1) * TM)
        x = x_ref[rows, :]
        xb = x.astype(BF16)
        acc = jnp.zeros((TM, D_MODEL), F32)
        for c in range(FFN_HIDDEN // FFN_CHUNK):
            cols = slice(c * FFN_CHUNK, (c + 1) * FFN_CHUNK)
            gate = jnp.dot(xb, wg_ref[:, cols], preferred_element_type=F32)
            up = jnp.dot(xb, wu_ref[:, cols], preferred_element_type=F32)
            h = gate / (1.0 + jnp.exp(-gate)) * up
            acc = acc + jnp.dot(h.astype(BF16), wd_ref[cols, :], preferred_element_type=F32)
        y_ref[rows, :] = _layer_norm(alpha * x + acc, g_ref[...], b_ref[...])


def _ffn(alpha, x2, wg, wu, wd, g, bb):
    m = x2.shape[0]
    tok = pl.BlockSpec((TM_FFN, D_MODEL), lambda i: (i, 0))
    full = lambda a: pl.BlockSpec(a.shape, lambda i: (0, 0), pipeline_mode=pl.Buffered(1))
    return pl.pallas_call(
        functools.partial(_ffn_kernel, alpha),
        grid=(m // TM_FFN,),
        in_specs=[tok, full(wg), full(wu), full(wd), full(g), full(bb)],
        out_specs=tok,
        out_shape=jax.ShapeDtypeStruct((m, D_MODEL), F32),
        compiler_params=_params(("parallel",)),
        name="swiglu_norm",
    )(x2, wg, wu, wd, g, bb)


def _split_w_in(w):
    a, bw, hi = A_WIDTH, B_WIDTH, IDX_HEADS * IDX_DIM
    o = 0
    qa, o = w[:, o:o + a], o + a
    ka, o = w[:, o:o + a], o + a
    va, o = w[:, o:o + a], o + a
    qb, o = w[:, o:o + bw], o + bw
    kb, o = w[:, o:o + bw], o + bw
    vb, o = w[:, o:o + bw], o + bw
    qi, o = w[:, o:o + hi], o + hi
    ki, o = w[:, o:o + IDX_DIM], o + IDX_DIM
    wi, o = w[:, o:o + IDX_HEADS], o + IDX_HEADS
    w_gate = w[:, o:].astype(BF16)
    w_nat = jnp.concatenate([ka, qb, kb, vb, ki, ki], axis=1).astype(BF16)
    pad = jnp.zeros((w.shape[0], _T_ROWS - _T_WI - IDX_HEADS), w.dtype)
    w_t = jnp.concatenate([qa, va, qi, wi, pad], axis=1).T.astype(BF16)
    return w_nat, w_t, w_gate


def kernel(x, positions, w_in, b_gate, w_branch_a, w_branch_b, w_out, ln1_g, ln1_b,
           w_ffn_gate, w_ffn_up, w_ffn_down, ln2_g, ln2_b):
    b, s, d = x.shape
    depth = w_in.shape[0]
    assert d == D_MODEL and s % (max(B_DILATIONS) * TB) == 0 and s % max(TM, TM_MERGE) == 0
    assert (b * s) % TM_FFN == 0 and B_DILATIONS[0] == 1
    alpha = (2 * depth) ** 0.25
    cos, sin, cos_t, sin_t = _rope_tables(positions)
    row = lambda v: v.reshape(1, -1)
    for layer in range(depth):
        w_nat, w_t, w_gate = _split_w_in(w_in[layer])
        ka6, ki4, *streams = _proj_nat(x, cos, sin, w_nat)
        qt, vt5, qi4t, wt = _proj_t(x, cos_t, sin_t, w_t)
        oa = _dsa_attention(qt, qi4t, wt, ka6, vt5, ki4)
        groups = [_window_attention(*streams[3 * g:3 * g + 3]) for g in range(len(B_DILATIONS))]
        x1 = _merge(alpha, oa, [o for o, _ in groups], [l for _, l in groups], x, w_gate, row(b_gate[layer]),
                    w_branch_a[layer].astype(BF16), w_branch_b[layer].astype(BF16),
                    w_out[layer].astype(BF16), row(ln1_g[layer]), row(ln1_b[layer]))
        x2 = _ffn(alpha, x1.reshape(b * s, d), w_ffn_gate[layer].astype(BF16), w_ffn_up[layer].astype(BF16),
                  w_ffn_down[layer].astype(BF16), row(ln2_g[layer]), row(ln2_b[layer]))
        x = x2.reshape(b, s, d)
    return x
```

```python
import functools

import jax
import jax.numpy as jnp
from jax import lax
from jax.experimental import pallas as pl
from jax.experimental.pallas import tpu as pltpu

F32 = jnp.float32
BF16 = jnp.bfloat16

D_MODEL = 1024
HEAD_DIM = 64
ROT_HALF = 8
ROPE_THETA = 500000.0
ATTN_SCALE = HEAD_DIM ** -0.5
LOG2E = 1.4426950408889634
V_ROWS = HEAD_DIM + 16
A_HEADS = 12
A_WIDTH = A_HEADS * HEAD_DIM
IDX_HEADS = 8
IDX_DIM = 64
IDX_SCALE = (IDX_HEADS ** -0.5) * (IDX_DIM ** -0.5)
TOPK_MAX = 256
B_DILATIONS = (1, 4, 16)
B_WINDOW_STEPS = 128
B_HEADS_PER_GROUP = 4
B_WIDTH = 3 * B_HEADS_PER_GROUP * HEAD_DIM
B_OUT_WIDTH = B_HEADS_PER_GROUP * HEAD_DIM
FFN_HIDDEN = 2816
LN_EPS = 1e-5
NEG = -1e30
MASK_NEG = -(2.0 ** 100)

LANES = 128
VMEM_LIMIT = 56 * 1024 * 1024
TM = 512
TM_FFN = 1024
TM_MERGE = 1024
TQ = 256
TK = 256
TB = 256
FFN_CHUNK = 256
KEY_TILES_PER_STEP = 8
BISECT_STEPS_PER_CHECK = 2
BISECT_BLIND_STEPS = 16
WINDOW_TILES_PER_STEP = 8
MERGE_ROWS = 256
MERGE_COLS = 256
ROPE_TILE = 2048

_N_KA, _N_QB, _N_KB, _N_VB = 0, 768, 1536, 2304
_N_KI, _N_NAT = 3072, 3200
_T_QA, _T_VA, _T_QI, _T_WI, _T_ROWS = 0, 768, 1536, 2048, 2064


def _params(sem):
    return pltpu.CompilerParams(dimension_semantics=sem, vmem_limit_bytes=VMEM_LIMIT)


def _rope_kernel(pr_ref, fc_ref, c_ref, s_ref, ct_ref, st_ref):
    ang_t = fc_ref[...] * pr_ref[...].astype(F32)
    cos_t, sin_t = jnp.cos(ang_t), jnp.sin(ang_t)
    ct_ref[...] = cos_t
    st_ref[...] = sin_t
    reps = LANES // ROT_HALF
    cos = jnp.concatenate([cos_t] * reps, axis=0).T
    sin = jnp.concatenate([sin_t] * reps, axis=0).T
    d = lax.broadcasted_iota(jnp.int32, cos.shape, 1) & (HEAD_DIM - 1)
    c_ref[...] = jnp.where(d < 2 * ROT_HALF, cos, 1.0)
    s_ref[...] = jnp.where(d < ROT_HALF, -sin, jnp.where(d < 2 * ROT_HALF, sin, 0.0))


def _rope_tables(positions):
    m = positions.size
    inv_freq = ROPE_THETA ** (-jnp.arange(0, 2 * ROT_HALF, 2, dtype=F32) / (2 * ROT_HALF))
    t = min(ROPE_TILE, m)
    return pl.pallas_call(
        _rope_kernel,
        grid=(m // t,),
        in_specs=[pl.BlockSpec((1, t), lambda i: (0, i)), pl.BlockSpec((ROT_HALF, 1), lambda i: (0, 0))],
        out_specs=[pl.BlockSpec((t, LANES), lambda i: (i, 0)), pl.BlockSpec((t, LANES), lambda i: (i, 0)),
                   pl.BlockSpec((ROT_HALF, t), lambda i: (0, i)), pl.BlockSpec((ROT_HALF, t), lambda i: (0, i))],
        out_shape=[jax.ShapeDtypeStruct((m, LANES), F32), jax.ShapeDtypeStruct((m, LANES), F32),
                   jax.ShapeDtypeStruct((ROT_HALF, m), F32), jax.ShapeDtypeStruct((ROT_HALF, m), F32)],
        compiler_params=_params(("parallel",)),
        name="rope_tables",
    )(positions.reshape(1, m), inv_freq.reshape(ROT_HALF, 1))


def _proj_nat_kernel(x_ref, c_ref, s_ref, w_ref, ka_ref, ki_ref, *rest):
    streams, y_ref = rest[:-1], rest[-1]
    xb = x_ref[0].astype(BF16)
    cos, sin = c_ref[...], s_ref[...]
    lane = lax.broadcasted_iota(jnp.int32, cos.shape, 1)
    first = (lane & (HEAD_DIM - 1)) < ROT_HALF

    def rope(y):
        partner = jnp.where(first, pltpu.roll(y, LANES - ROT_HALF, 1), pltpu.roll(y, ROT_HALF, 1))
        return y * cos + partner * sin

    def proj(lo, hi):
        return jnp.dot(xb, w_ref[:, lo:hi], preferred_element_type=F32)

    blocks = A_WIDTH // LANES
    y = proj(_N_KA, _N_KA + A_WIDTH)
    for p in range(blocks):
        ka_ref[0, p] = rope(y[:, p * LANES:(p + 1) * LANES]).astype(BF16)

    def scatter_streams(which):
        per_group = B_OUT_WIDTH // LANES
        for g, d in enumerate(B_DILATIONS):
            out = streams[3 * g + which]
            for r in range(d):
                for p in range(per_group):
                    rows = y_ref[g * per_group + p, pl.ds(r, TM // d, stride=d), :]
                    out[0, r, :, p * LANES:(p + 1) * LANES] = rows.astype(BF16)

    y = proj(_N_QB, _N_QB + B_WIDTH)
    for p in range(blocks):
        y_ref[p] = rope(y[:, p * LANES:(p + 1) * LANES]) * (ATTN_SCALE * LOG2E)
    scatter_streams(0)
    y = proj(_N_KB, _N_KB + B_WIDTH)
    for p in range(blocks):
        y_ref[p] = rope(y[:, p * LANES:(p + 1) * LANES])
    scatter_streams(1)
    y = proj(_N_VB, _N_VB + B_WIDTH)
    for p in range(blocks):
        y_ref[p] = y[:, p * LANES:(p + 1) * LANES]
    scatter_streams(2)
    r = rope(proj(_N_KI, _N_KI + LANES))
    hi = r.astype(BF16).astype(F32)
    hl = jnp.where(lane < IDX_DIM, hi, r - hi).astype(BF16)
    ki_ref[0, :, 0:LANES] = hl
    ki_ref[0, :, LANES:2 * LANES] = hl


def _proj_nat(x, cos, sin, w_nat):
    b, s, d = x.shape
    nt = s // TM
    tok = lambda width: pl.BlockSpec((1, TM, width), lambda i, j: (i, j, 0))
    stream_specs, stream_shapes = [], []
    for dil in B_DILATIONS:
        for _ in range(3):
            stream_specs.append(pl.BlockSpec((1, dil, TM // dil, B_OUT_WIDTH), lambda i, j: (i, 0, j, 0)))
            stream_shapes.append(jax.ShapeDtypeStruct((b, dil, s // dil, B_OUT_WIDTH), BF16))
    return pl.pallas_call(
        _proj_nat_kernel,
        grid=(b, nt),
        in_specs=[tok(d),
                  pl.BlockSpec((TM, LANES), lambda i, j: (i * nt + j, 0)),
                  pl.BlockSpec((TM, LANES), lambda i, j: (i * nt + j, 0)),
                  pl.BlockSpec((d, _N_NAT), lambda i, j: (0, 0))],
        out_specs=[pl.BlockSpec((1, A_WIDTH // LANES, TM, LANES), lambda i, j: (i, 0, j, 0)),
                   tok(2 * LANES)] + stream_specs,
        out_shape=[jax.ShapeDtypeStruct((b, A_WIDTH // LANES, s, LANES), BF16),
                   jax.ShapeDtypeStruct((b, s, 2 * LANES), BF16)] + stream_shapes,
        scratch_shapes=[pltpu.VMEM((B_WIDTH // LANES, TM, LANES), F32)],
        compiler_params=_params(("parallel", "parallel")),
        name="proj_token_major",
    )(x, cos, sin, w_nat)


def _proj_t_kernel(x_ref, ct_ref, st_ref, w_ref, qt_ref, vt_ref, qi_ref, wt_ref):
    xb = x_ref[0].astype(BF16)
    cos, sin = ct_ref[...], st_ref[...]

    def proj(lo, hi):
        return lax.dot_general(w_ref[lo:hi, :], xb, (((1,), (1,)), ((), ())), preferred_element_type=F32)

    def rope_head(y):
        x1, x2 = y[0:ROT_HALF], y[ROT_HALF:2 * ROT_HALF]
        return jnp.concatenate([x1 * cos - x2 * sin, x2 * cos + x1 * sin, y[2 * ROT_HALF:]], axis=0)

    y = proj(_T_QA, _T_QA + A_WIDTH)
    for h in range(A_HEADS):
        r = rope_head(y[h * HEAD_DIM:(h + 1) * HEAD_DIM])
        qt_ref[0, h * HEAD_DIM:(h + 1) * HEAD_DIM, :] = (r * (ATTN_SCALE * LOG2E)).astype(BF16)
    y = proj(_T_VA, _T_VA + A_WIDTH)
    for h in range(A_HEADS):
        for c in range(TM // TK):
            vt_ref[0, h, c, 0:HEAD_DIM, :] = y[h * HEAD_DIM:(h + 1) * HEAD_DIM, c * TK:(c + 1) * TK].astype(BF16)
            vt_ref[0, h, c, HEAD_DIM:V_ROWS, :] = jnp.ones((V_ROWS - HEAD_DIM, TK), BF16)
    y = proj(_T_QI, _T_QI + IDX_HEADS * IDX_DIM)
    for h in range(IDX_HEADS):
        r = rope_head(y[h * IDX_DIM:(h + 1) * IDX_DIM])
        hi = r.astype(BF16)
        lo = (r - hi.astype(F32)).astype(BF16)
        base = 4 * h * IDX_DIM
        qi_ref[0, base:base + IDX_DIM, :] = hi
        qi_ref[0, base + IDX_DIM:base + 2 * IDX_DIM, :] = hi
        qi_ref[0, base + 2 * IDX_DIM:base + 3 * IDX_DIM, :] = lo
        qi_ref[0, base + 3 * IDX_DIM:base + 4 * IDX_DIM, :] = lo
    wt_ref[0] = proj(_T_WI, _T_ROWS)[0:IDX_HEADS] * IDX_SCALE


def _proj_t(x, cos_t, sin_t, w_t):
    b, s, d = x.shape
    nt = s // TM
    return pl.pallas_call(
        _proj_t_kernel,
        grid=(b, nt),
        in_specs=[pl.BlockSpec((1, TM, d), lambda i, j: (i, j, 0)),
                  pl.BlockSpec((ROT_HALF, TM), lambda i, j: (0, i * nt + j)),
                  pl.BlockSpec((ROT_HALF, TM), lambda i, j: (0, i * nt + j)),
                  pl.BlockSpec((_T_ROWS, d), lambda i, j: (0, 0))],
        out_specs=[pl.BlockSpec((1, A_WIDTH, TM), lambda i, j: (i, 0, j)),
                   pl.BlockSpec((1, A_HEADS, TM // TK, V_ROWS, TK), lambda i, j: (i, 0, j, 0, 0)),
                   pl.BlockSpec((1, 4 * IDX_HEADS * IDX_DIM, TM), lambda i, j: (i, 0, j)),
                   pl.BlockSpec((1, IDX_HEADS, TM), lambda i, j: (i, 0, j))],
        out_shape=[jax.ShapeDtypeStruct((b, A_WIDTH, s), BF16),
                   jax.ShapeDtypeStruct((b, A_HEADS, s // TK, V_ROWS, TK), BF16),
                   jax.ShapeDtypeStruct((b, 4 * IDX_HEADS * IDX_DIM, s), BF16),
                   jax.ShapeDtypeStruct((b, IDX_HEADS, s), F32)],
        compiler_params=_params(("parallel", "parallel")),
        name="proj_feature_major",
    )(x, cos_t, sin_t, w_t)


def _dsa_kernel(qt_ref, qi_ref, wt_ref, k_ref, vt_ref, ki_ref, o_ref,
                sc_ref, qpad_ref, ot_ref, m_ref, st_ref, stat_ref, sa_ref, sb_ref):
    j = pl.program_id(1)
    nkt = j + 1

    for h in range(A_HEADS):
        off = (h % 2) * HEAD_DIM
        qpad_ref[h] = jnp.zeros((2 * HEAD_DIM, TQ), BF16)
        qpad_ref[h, off:off + HEAD_DIM, :] = qt_ref[0, h * HEAD_DIM:(h + 1) * HEAD_DIM, :]

    qpos = j * TQ + lax.broadcasted_iota(jnp.int32, (1, TQ), 1)
    row_iota = lax.broadcasted_iota(jnp.int32, (TK, TQ), 0)
    w = wt_ref[0]

    def fold(t):
        return t.reshape(TK // 8, 8, TQ)

    def score_tile(kt, diagonal, carry):
        mn, mx, pos, nn = carry
        ki = ki_ref[0, pl.ds(pl.multiple_of(kt * TK, TK), TK), :]
        acc = jnp.zeros((TK, TQ), F32)
        for h in range(IDX_HEADS):
            s = jnp.dot(ki, qi_ref[0, 4 * h * IDX_DIM:4 * (h + 1) * IDX_DIM, :], preferred_element_type=F32)
            acc = acc + w[h:h + 1, :] * jnp.maximum(s, 0.0)
        if diagonal:
            causal = (kt * TK + row_iota) <= qpos
            val = jnp.where(causal, acc, -jnp.inf)
            low = jnp.where(causal, acc, jnp.inf)
        else:
            val = low = acc
        sc_ref[kt] = val
        return (jnp.minimum(mn, fold(low).min(axis=0)),
                jnp.maximum(mx, fold(val).max(axis=0)),
                pos + fold(jnp.where(val > 0.0, 1.0, 0.0)).sum(axis=0),
                nn + fold(jnp.where(val >= 0.0, 1.0, 0.0)).sum(axis=0))

    def load_stats():
        return tuple(stat_ref[n] for n in range(4))

    def store_stats(stats):
        for n, v in enumerate(stats):
            stat_ref[n] = v

    def score_block(i, carry):
        for u in range(KEY_TILES_PER_STEP):
            carry = score_tile(KEY_TILES_PER_STEP * i + u, False, carry)
        return carry

    zeros8 = jnp.zeros((8, TQ), F32)
    store_stats(lax.fori_loop(
        0, j // KEY_TILES_PER_STEP, score_block,
        (jnp.full((8, TQ), jnp.inf, F32), jnp.full((8, TQ), -jnp.inf, F32), zeros8, zeros8)))

    for left in range(KEY_TILES_PER_STEP):
        @pl.when(j % KEY_TILES_PER_STEP == left)
        def _(left=left):
            stats = load_stats()
            for u in range(left):
                stats = score_tile(j - left + u, False, stats)
            store_stats(score_tile(j, True, stats))

    mn8, mx8, pos8, nn8 = load_stats()
    mn = mn8.min(axis=0, keepdims=True)
    mx = mx8.max(axis=0, keepdims=True)
    c_pos = pos8.sum(axis=0, keepdims=True)
    c_nn = nn8.sum(axis=0, keepdims=True)

    sc_ref[nkt] = jnp.full((TK, TQ), -jnp.inf, F32)

    def count(preds):
        def part(p, kt):
            return fold(p(sc_ref[kt])).sum(axis=0)

        def body(i, cnts):
            return tuple(c + part(p, 2 * i) + part(p, 2 * i + 1) for c, p in zip(cnts, preds))

        cnts = lax.fori_loop(0, (nkt + 1) // 2, body, tuple(jnp.zeros((8, TQ), F32) for _ in preds))
        return [c.sum(axis=0, keepdims=True) for c in cnts]

    def ge(cand):
        return lambda t: jnp.where(t >= cand, 1.0, 0.0)

    n_causal = (qpos + 1).astype(F32)
    k_q = jnp.minimum(n_causal, float(TOPK_MAX))
    c_mx, = count([ge(mx)])
    select = n_causal > k_q
    at_max = select & (c_mx >= k_q)
    search = select & (c_mx < k_q)
    at_zero = search & (c_pos < k_q) & (c_nn >= k_q)
    above = search & (c_pos >= k_q)
    below = search & (c_nn < k_q)
    lo0 = jnp.where(at_max, mx, jnp.where(at_zero | above, 0.0, mn))
    c_lo0 = jnp.where(at_max, c_mx, jnp.where(at_zero | above, c_nn, n_causal))
    st_ref[0:1, :] = lo0
    st_ref[1:2, :] = jnp.where(below, 0.0, mx)
    st_ref[2:3, :] = c_lo0
    st_ref[3:4, :] = jnp.where(at_max, 0.0, jnp.where(at_zero, c_pos, jnp.where(below, c_nn, c_mx)))
    act0 = jnp.where((above | below) & (c_lo0 > k_q), 1.0, 0.0)
    st_ref[4:5, :] = act0

    def bisect_once():
        lo, hi, c_lo, c_hi = st_ref[0:1, :], st_ref[1:2, :], st_ref[2:3, :], st_ref[3:4, :]
        act = st_ref[4:5, :] > 0.0
        mid = 0.5 * lo + 0.5 * hi
        live = act & (mid > lo) & (mid < hi)
        c, = count([ge(mid)])
        up = live & (c >= k_q)
        dn = live & (c < k_q)
        c_lo = jnp.where(up, c, c_lo)
        st_ref[0:1, :] = jnp.where(up, mid, lo)
        st_ref[1:2, :] = jnp.where(dn, mid, hi)
        st_ref[2:3, :] = c_lo
        st_ref[3:4, :] = jnp.where(dn, c, c_hi)
        act_new = jnp.where(live & (c_lo > k_q), 1.0, 0.0)
        st_ref[4:5, :] = act_new
        return act_new

    def bisect(go):
        for _ in range(BISECT_STEPS_PER_CHECK - 1):
            bisect_once()
        return (jnp.max(bisect_once()) > 0.0).astype(jnp.int32)

    @pl.when(jnp.max(act0) > 0.0)
    def _():
        def blind(_, carry):
            bisect_once()
            bisect_once()
            return carry

        lax.fori_loop(0, BISECT_BLIND_STEPS // 2, blind, 0)

    lax.while_loop(lambda go: go > 0, bisect, (jnp.max(st_ref[4:5, :]) > 0.0).astype(jnp.int32))

    lo = st_ref[0:1, :]
    tie = st_ref[2:3, :] > k_q

    @pl.when(jnp.max(jnp.where(tie, 1.0, 0.0)) > 0.0)
    def _():
        need = jnp.where(tie, k_q - st_ref[3:4, :], jnp.inf)
        tri = jnp.where(lax.broadcasted_iota(jnp.int32, (TK, TK), 0) >= lax.broadcasted_iota(jnp.int32, (TK, TK), 1),
                        1.0, 0.0).astype(BF16)

        def drop(kt, seen):
            t = sc_ref[kt]
            eq = jnp.where(t == lo, 1.0, 0.0)
            rank = jnp.dot(tri, eq.astype(BF16), preferred_element_type=F32) + seen
            sc_ref[kt] = jnp.where(t == lo, jnp.where(rank > need, -jnp.inf, t), t)
            return seen + fold(eq).sum(axis=0).sum(axis=0, keepdims=True)

        def drop_pair(i, seen):
            return drop(2 * i + 1, drop(2 * i, seen))

        lax.fori_loop(0, (nkt + 1) // 2, drop_pair, jnp.zeros((1, TQ), F32))

    m_ref[...] = jnp.full(m_ref.shape, MASK_NEG, F32)
    ot_ref[...] = jnp.zeros(ot_ref.shape, F32)

    def mask_bias(kt):
        return jnp.where(sc_ref[kt] >= lo, 0.0, MASK_NEG).astype(BF16)

    def logits(h, kt, bias):
        rows = pl.ds(pl.multiple_of(kt * TK, TK), TK)
        return jnp.dot(k_ref[0, h // 2, rows, :], qpad_ref[h], preferred_element_type=F32).astype(BF16) + bias

    def half_step(kt, cur_ref, nxt_ref):
        if nxt_ref is not None:
            bias_n = mask_bias(kt + 1)
        for h in range(A_HEADS):
            s = cur_ref[h]
            if nxt_ref is not None:
                nxt_ref[h] = logits(h, kt + 1, bias_n)
            m = m_ref[h]
            m_tile = s.reshape(TK // 16, 16, TQ).max(axis=0).astype(F32).max(axis=0, keepdims=True)
            m_new = jnp.maximum(m, m_tile)
            p = jnp.exp2(s - m_new.astype(BF16))
            corr = jnp.exp2(m - m_new)
            m_ref[h] = m_new
            ot_ref[h] = ot_ref[h] * corr + jnp.dot(vt_ref[0, h, kt], p, preferred_element_type=F32)

    bias0 = mask_bias(0)
    for h in range(A_HEADS):
        sa_ref[h] = logits(h, 0, bias0)

    def buffers(u):
        return (sa_ref, sb_ref) if u % 2 == 0 else (sb_ref, sa_ref)

    def kv_block(i, _):
        for u in range(KEY_TILES_PER_STEP):
            half_step(KEY_TILES_PER_STEP * i + u, *buffers(u))
        return 0

    n_blocks = (nkt - 1) // KEY_TILES_PER_STEP
    lax.fori_loop(0, n_blocks, kv_block, 0)

    for left in range(1, KEY_TILES_PER_STEP + 1):
        @pl.when(nkt - KEY_TILES_PER_STEP * n_blocks == left)
        def _(left=left):
            for u in range(left):
                cur_ref, nxt_ref = buffers(u)
                half_step(nkt - left + u, cur_ref, nxt_ref if u < left - 1 else None)

    for p in range(A_HEADS // 2):
        both = jnp.concatenate([ot_ref[h, 0:HEAD_DIM, :] / ot_ref[h, HEAD_DIM:HEAD_DIM + 1, :]
                                for h in (2 * p, 2 * p + 1)], axis=0)
        o_ref[0, :, p * LANES:(p + 1) * LANES] = both.T.astype(BF16)


def _dsa_attention(qt, qi4t, wt, k6, vt5, ki4):
    b, _, s = qt.shape
    nq = s // TQ
    return pl.pallas_call(
        _dsa_kernel,
        grid=(b, nq),
        in_specs=[pl.BlockSpec((1, A_WIDTH, TQ), lambda i, j: (i, 0, j)),
                  pl.BlockSpec((1, 4 * IDX_HEADS * IDX_DIM, TQ), lambda i, j: (i, 0, j)),
                  pl.BlockSpec((1, IDX_HEADS, TQ), lambda i, j: (i, 0, j)),
                  pl.BlockSpec((1, A_WIDTH // LANES, s, LANES), lambda i, j: (i, 0, 0, 0)),
                  pl.BlockSpec((1, A_HEADS, s // TK, V_ROWS, TK), lambda i, j: (i, 0, 0, 0, 0)),
                  pl.BlockSpec((1, s, 4 * IDX_DIM), lambda i, j: (i, 0, 0))],
        out_specs=pl.BlockSpec((1, TQ, A_WIDTH), lambda i, j: (i, j, 0)),
        out_shape=jax.ShapeDtypeStruct((b, s, A_WIDTH), BF16),
        scratch_shapes=[pltpu.VMEM((s // TK + 1, TK, TQ), F32),
                        pltpu.VMEM((A_HEADS, 2 * HEAD_DIM, TQ), BF16),
                        pltpu.VMEM((A_HEADS, V_ROWS, TQ), F32),
                        pltpu.VMEM((A_HEADS, 1, TQ), F32),
                        pltpu.VMEM((8, TQ), F32),
                        pltpu.VMEM((4, 8, TQ), F32),
                        pltpu.VMEM((A_HEADS, TK, TQ), BF16),
                        pltpu.VMEM((A_HEADS, TK, TQ), BF16)],
        compiler_params=_params(("parallel", "arbitrary")),
        name="dsa_attention",
    )(qt, qi4t, wt, k6, vt5, ki4)


def _window_kernel(q_ref, kp_ref, kc_ref, vp_ref, vc_ref, o_ref, lse_ref):
    hb = B_WINDOW_STEPS
    n_streams, n_rows = q_ref.shape[1], q_ref.shape[2]
    r = lax.broadcasted_iota(jnp.int32, (hb, 2 * hb), 0)
    c = lax.broadcasted_iota(jnp.int32, (hb, 2 * hb), 1)
    dist = r + hb - c
    band = jnp.where(dist >= 0, jnp.where(dist <= B_WINDOW_STEPS, 0.0, NEG), NEG)
    first_col = jnp.where(pl.program_id(2) > 0, 0, hb)
    band_first = jnp.where(c >= first_col, band, NEG)
    lane = lax.broadcasted_iota(jnp.int32, (hb, LANES), 1)
    left = lane < HEAD_DIM
    ones = jnp.ones((2 * hb, LANES), BF16)
    for sb in range(n_streams):
        for t in range(n_rows // TB):
            for p in range(B_HEADS_PER_GROUP // 2):
                cols = slice(p * LANES, (p + 1) * LANES)
                tile = slice(t * TB, (t + 1) * TB)
                q2, kc, vc = q_ref[0, sb, tile, cols], kc_ref[0, sb, tile, cols], vc_ref[0, sb, tile, cols]
                if t == 0:
                    k_before, v_before = kp_ref[0, sb, :, cols], vp_ref[0, sb, :, cols]
                else:
                    before = slice(t * TB - hb, t * TB)
                    k_before, v_before = kc_ref[0, sb, before, cols], vc_ref[0, sb, before, cols]
                windows = ((jnp.concatenate([k_before, kc[:hb]], axis=0),
                            jnp.concatenate([v_before, vc[:hb]], axis=0)), (kc, vc))
                for half, ((kw, vw), bias) in enumerate(zip(windows, (band_first if t == 0 else band, band))):
                    rows = slice(t * TB + half * hb, t * TB + (half + 1) * hb)
                    v1 = jnp.concatenate([vw, ones], axis=1)
                    outs, lses = [], []
                    for side in (left, ~left):
                        qh = jnp.where(side, q2[half * hb:(half + 1) * hb], jnp.zeros((hb, LANES), BF16))
                        s = lax.dot_general(qh, kw, (((1,), (1,)), ((), ())), preferred_element_type=F32) + bias
                        m = s.max(axis=-1, keepdims=True)
                        e = jnp.exp2((s - m).astype(BF16))
                        ol = jnp.dot(e, v1, preferred_element_type=F32)
                        l = ol[:, LANES:]
                        outs.append(ol[:, :LANES] / l)
                        lses.append(m + jnp.log2(l))
                    o_ref[0, sb, rows, cols] = jnp.where(left, outs[0], outs[1])
                    lse_ref[0, sb, rows, cols] = jnp.where(left, lses[0], lses[1])


def _window_attention(q, k, v):
    b, d, n, _ = q.shape
    assert TB == 2 * B_WINDOW_STEPS
    rows = min(n, WINDOW_TILES_PER_STEP * TB)
    streams = min(d, WINDOW_TILES_PER_STEP * TB // rows)
    halves = rows // (TB // 2)
    cur = pl.BlockSpec((1, streams, rows, B_OUT_WIDTH), lambda bi, ri, ti: (bi, ri, ti, 0))
    prev = pl.BlockSpec((1, streams, TB // 2, B_OUT_WIDTH),
                        lambda bi, ri, ti: (bi, ri, jnp.maximum(halves * ti - 1, 0), 0))
    return pl.pallas_call(
        _window_kernel,
        grid=(b, d // streams, n // rows),
        in_specs=[cur, prev, cur, prev, cur],
        out_specs=[cur, cur],
        out_shape=[jax.ShapeDtypeStruct((b, d, n, B_OUT_WIDTH), F32)] * 2,
        compiler_params=_params(("parallel", "parallel", "arbitrary")),
        name=f"window_attention_d{d}",
    )(q, k, k, v, v)


def _layer_norm(y, g, b):
    mu = y.mean(axis=-1, keepdims=True)
    yc = y - mu
    var = (yc * yc).mean(axis=-1, keepdims=True)
    return yc * lax.rsqrt(var + LN_EPS) * g + b


def _merge_kernel(alpha, oa_ref, o0_ref, o1_ref, o2_ref, l0_ref, l1_ref, l2_ref, x_ref,
                  wg_ref, bg_ref, wa_ref, wb_ref, wo_ref, g_ref, b_ref, y_ref,
                  tok_ref, ob_ref, xb_ref, mg_ref):
    halves = B_OUT_WIDTH // LANES

    for n, ref in enumerate((o1_ref, o2_ref, l1_ref, l2_ref)):
        d = ref.shape[1]
        for r in range(d):
            for p in range(halves):
                tok_ref[n, p, pl.ds(r, TM_MERGE // d, stride=d), :] = ref[0, r, :, p * LANES:(p + 1) * LANES]
    for c in range(TM_MERGE // MERGE_ROWS):
        rows = slice(c * MERGE_ROWS, (c + 1) * MERGE_ROWS)
        for p in range(halves):
            cols = slice(p * LANES, (p + 1) * LANES)
            l0, l1, l2 = l0_ref[0, 0, rows, cols], tok_ref[2, p, rows, :], tok_ref[3, p, rows, :]
            lm = jnp.maximum(jnp.maximum(l0, l1), l2)
            e0, e1, e2 = jnp.exp2(l0 - lm), jnp.exp2(l1 - lm), jnp.exp2(l2 - lm)
            ob = (e0 * o0_ref[0, 0, rows, cols] + e1 * tok_ref[0, p, rows, :] + e2 * tok_ref[1, p, rows, :])
            ob_ref[rows, cols] = (ob / (e0 + e1 + e2)).astype(BF16)

    xb_ref[...] = x_ref[0].astype(BF16)
    for c in range(D_MODEL // MERGE_COLS):
        cols = slice(c * MERGE_COLS, (c + 1) * MERGE_COLS)
        gcols = slice(D_MODEL + c * MERGE_COLS, D_MODEL + (c + 1) * MERGE_COLS)

        def gate(sel):
            z = jnp.dot(xb_ref[...], wg_ref[:, sel], preferred_element_type=F32) + bg_ref[:, sel]
            return 1.0 / (1.0 + jnp.exp(-z))

        pa = jnp.dot(oa_ref[0], wa_ref[:, cols], preferred_element_type=F32)
        pb = jnp.dot(ob_ref[...], wb_ref[:, cols], preferred_element_type=F32)
        mg_ref[:, cols] = (gate(cols) * pa + gate(gcols) * pb).astype(BF16)

    for c in range(TM_MERGE // MERGE_ROWS):
        rows = slice(c * MERGE_ROWS, (c + 1) * MERGE_ROWS)
        mixed = jnp.dot(mg_ref[rows, :], wo_ref[...], preferred_element_type=F32)
        y_ref[0, rows, :] = _layer_norm(alpha * x_ref[0, rows, :] + mixed, g_ref[...], b_ref[...])


def _merge(alpha, oa, obs, lses, x, w_gate, b_gate, wa, wb, wo, g, bb):
    b, s, _ = x.shape
    tm = TM_MERGE
    tok = lambda width: pl.BlockSpec((1, tm, width), lambda i, j: (i, j, 0))
    full = lambda a: pl.BlockSpec(a.shape, lambda i, j: (0, 0), pipeline_mode=pl.Buffered(1))
    streams = [pl.BlockSpec((1, a.shape[1], tm // a.shape[1], B_OUT_WIDTH), lambda i, j: (i, 0, j, 0))
               for a in list(obs) + list(lses)]
    return pl.pallas_call(
        functools.partial(_merge_kernel, alpha),
        grid=(b, s // tm),
        in_specs=[tok(A_WIDTH)] + streams + [tok(D_MODEL), full(w_gate),
                  full(b_gate), full(wa), full(wb), full(wo), full(g), full(bb)],
        out_specs=tok(D_MODEL),
        out_shape=jax.ShapeDtypeStruct((b, s, D_MODEL), F32),
        scratch_shapes=[pltpu.VMEM((4, B_OUT_WIDTH // LANES, tm, LANES), F32),
                        pltpu.VMEM((tm, B_OUT_WIDTH), BF16),
                        pltpu.VMEM((tm, D_MODEL), BF16),
                        pltpu.VMEM((tm, D_MODEL), BF16)],
        compiler_params=_params(("parallel", "parallel")),
        name="merge_outproj_norm",
    )(oa, *obs, *lses, x, w_gate, b_gate, wa, wb, wo, g, bb)


def _ffn_kernel(alpha, x_ref, wg_ref, wu_ref, wd_ref, g_ref, b_ref, y_ref):
    for r in range(TM_FFN // TM):
        rows = slice(r * TM, (r + 1) * TM)
        x = x_ref[rows, :]
        xb = x.astype(BF16)
        acc = jnp.zeros((TM, D_MODEL), F32)
        for c in range(FFN_HIDDEN // FFN_CHUNK):
            cols = slice(c * FFN_CHUNK, (c + 1) * FFN_CHUNK)
            gate = jnp.dot(xb, wg_ref[:, cols], preferred_element_type=F32)
            up = jnp.dot(xb, wu_ref[:, cols], preferred_element_type=F32)
            h = gate / (1.0 + jnp.exp(-gate)) * up
            acc = acc + jnp.dot(h.astype(BF16), wd_ref[cols, :], preferred_element_type=F32)
        y_ref[rows, :] = _layer_norm(alpha * x + acc, g_ref[...], b_ref[...])


def _ffn(alpha, x2, wg, wu, wd, g, bb):
    m = x2.shape[0]
    tok = pl.BlockSpec((TM_FFN, D_MODEL), lambda i: (i, 0))
    full = lambda a: pl.BlockSpec(a.shape, lambda i: (0, 0), pipeline_mode=pl.Buffered(1))
    return pl.pallas_call(
        functools.partial(_ffn_kernel, alpha),
        grid=(m // TM_FFN,),
        in_specs=[tok, full(wg), full(wu), full(wd), full(g), full(bb)],
        out_specs=tok,
        out_shape=jax.ShapeDtypeStruct((m, D_MODEL), F32),
        compiler_params=_params(("parallel",)),
        name="swiglu_norm",
    )(x2, wg, wu, wd, g, bb)


def _split_w_in(w):
    a, bw, hi = A_WIDTH, B_WIDTH, IDX_HEADS * IDX_DIM
    o = 0
    qa, o = w[:, o:o + a], o + a
    ka, o = w[:, o:o + a], o + a
    va, o = w[:, o:o + a], o + a
    qb, o = w[:, o:o + bw], o + bw
    kb, o = w[:, o:o + bw], o + bw
    vb, o = w[:, o:o + bw], o + bw
    qi, o = w[:, o:o + hi], o + hi
    ki, o = w[:, o:o + IDX_DIM], o + IDX_DIM
    wi, o = w[:, o:o + IDX_HEADS], o + IDX_HEADS
    w_gate = w[:, o:].astype(BF16)
    w_nat = jnp.concatenate([ka, qb, kb, vb, ki, ki], axis=1).astype(BF16)
    pad = jnp.zeros((w.shape[0], _T_ROWS - _T_WI - IDX_HEADS), w.dtype)
    w_t = jnp.concatenate([qa, va, qi, wi, pad], axis=1).T.astype(BF16)
    return w_nat, w_t, w_gate


def kernel(x, positions, w_in, b_gate, w_branch_a, w_branch_b, w_out, ln1_g, ln1_b,
           w_ffn_gate, w_ffn_up, w_ffn_down, ln2_g, ln2_b):
    b, s, d = x.shape
    depth = w_in.shape[0]
    assert d == D_MODEL and s % (max(B_DILATIONS) * TB) == 0 and s % max(TM, TM_MERGE) == 0
    assert (b * s) % TM_FFN == 0 and B_DILATIONS[0] == 1
    alpha = (2 * depth) ** 0.25
    cos, sin, cos_t, sin_t = _rope_tables(positions)
    row = lambda v: v.reshape(1, -1)
    for layer in range(depth):
        w_nat, w_t, w_gate = _split_w_in(w_in[layer])
        ka6, ki4, *streams = _proj_nat(x, cos, sin, w_nat)
        qt, vt5, qi4t, wt = _proj_t(x, cos_t, sin_t, w_t)
        oa = _dsa_attention(qt, qi4t, wt, ka6, vt5, ki4)
        groups = [_window_attention(*streams[3 * g:3 * g + 3]) for g in range(len(B_DILATIONS))]
        x1 = _merge(alpha, oa, [o for o, _ in groups], [l for _, l in groups], x, w_gate, row(b_gate[layer]),
                    w_branch_a[layer].astype(BF16), w_branch_b[layer].astype(BF16),
                    w_out[layer].astype(BF16), row(ln1_g[layer]), row(ln1_b[layer]))
        x2 = _ffn(alpha, x1.reshape(b * s, d), w_ffn_gate[layer].astype(BF16), w_ffn_up[layer].astype(BF16),
                  w_ffn_down[layer].astype(BF16), row(ln2_g[layer]), row(ln2_b[layer]))
        x = x2.reshape(b, s, d)
    return x
```

```python
import functools

import jax
import jax.numpy as jnp
from jax import lax
from jax.experimental import pallas as pl
from jax.experimental.pallas import tpu as pltpu

F32 = jnp.float32
BF16 = jnp.bfloat16

D_MODEL = 1024
HEAD_DIM = 64
ROT_HALF = 8
ROPE_THETA = 500000.0
ATTN_SCALE = HEAD_DIM ** -0.5
LOG2E = 1.4426950408889634
V_ROWS = HEAD_DIM + 16
A_HEADS = 12
A_WIDTH = A_HEADS * HEAD_DIM
IDX_HEADS = 8
IDX_DIM = 64
IDX_SCALE = (IDX_HEADS ** -0.5) * (IDX_DIM ** -0.5)
TOPK_MAX = 256
B_DILATIONS = (1, 4, 16)
B_WINDOW_STEPS = 128
B_HEADS_PER_GROUP = 4
B_WIDTH = 3 * B_HEADS_PER_GROUP * HEAD_DIM
B_OUT_WIDTH = B_HEADS_PER_GROUP * HEAD_DIM
FFN_HIDDEN = 2816
LN_EPS = 1e-5
NEG = -1e30
MASK_NEG = -(2.0 ** 100)

LANES = 128
VMEM_LIMIT = 56 * 1024 * 1024
TM = 512
TM_FFN = 1024
TM_MERGE = 1024
TQ = 256
TK = 256
TB = 256
FFN_CHUNK = 256
KEY_TILES_PER_STEP = 4
BISECT_STEPS_PER_CHECK = 2
BISECT_BLIND_STEPS = 16
WINDOW_TILES_PER_STEP = 8
MERGE_ROWS = 256
MERGE_COLS = 256
ROPE_TILE = 2048

_N_KA, _N_QB, _N_KB, _N_VB = 0, 768, 1536, 2304
_N_KI, _N_NAT = 3072, 3200
_T_QA, _T_VA, _T_QI, _T_WI, _T_ROWS = 0, 768, 1536, 2048, 2064


def _params(sem):
    return pltpu.CompilerParams(dimension_semantics=sem, vmem_limit_bytes=VMEM_LIMIT)


def _rope_kernel(pr_ref, fc_ref, c_ref, s_ref, ct_ref, st_ref):
    ang_t = fc_ref[...] * pr_ref[...].astype(F32)
    cos_t, sin_t = jnp.cos(ang_t), jnp.sin(ang_t)
    ct_ref[...] = cos_t
    st_ref[...] = sin_t
    reps = LANES // ROT_HALF
    cos = jnp.concatenate([cos_t] * reps, axis=0).T
    sin = jnp.concatenate([sin_t] * reps, axis=0).T
    d = lax.broadcasted_iota(jnp.int32, cos.shape, 1) & (HEAD_DIM - 1)
    c_ref[...] = jnp.where(d < 2 * ROT_HALF, cos, 1.0)
    s_ref[...] = jnp.where(d < ROT_HALF, -sin, jnp.where(d < 2 * ROT_HALF, sin, 0.0))


def _rope_tables(positions):
    m = positions.size
    inv_freq = ROPE_THETA ** (-jnp.arange(0, 2 * ROT_HALF, 2, dtype=F32) / (2 * ROT_HALF))
    t = min(ROPE_TILE, m)
    return pl.pallas_call(
        _rope_kernel,
        grid=(m // t,),
        in_specs=[pl.BlockSpec((1, t), lambda i: (0, i)), pl.BlockSpec((ROT_HALF, 1), lambda i: (0, 0))],
        out_specs=[pl.BlockSpec((t, LANES), lambda i: (i, 0)), pl.BlockSpec((t, LANES), lambda i: (i, 0)),
                   pl.BlockSpec((ROT_HALF, t), lambda i: (0, i)), pl.BlockSpec((ROT_HALF, t), lambda i: (0, i))],
        out_shape=[jax.ShapeDtypeStruct((m, LANES), F32), jax.ShapeDtypeStruct((m, LANES), F32),
                   jax.ShapeDtypeStruct((ROT_HALF, m), F32), jax.ShapeDtypeStruct((ROT_HALF, m), F32)],
        compiler_params=_params(("parallel",)),
        name="rope_tables",
    )(positions.reshape(1, m), inv_freq.reshape(ROT_HALF, 1))


def _proj_nat_kernel(x_ref, c_ref, s_ref, w_ref, ka_ref, ki_ref, *rest):
    streams, y_ref = rest[:-1], rest[-1]
    xb = x_ref[0].astype(BF16)
    cos, sin = c_ref[...], s_ref[...]
    lane = lax.broadcasted_iota(jnp.int32, cos.shape, 1)
    first = (lane & (HEAD_DIM - 1)) < ROT_HALF

    def rope(y):
        partner = jnp.where(first, pltpu.roll(y, LANES - ROT_HALF, 1), pltpu.roll(y, ROT_HALF, 1))
        return y * cos + partner * sin

    def proj(lo, hi):
        return jnp.dot(xb, w_ref[:, lo:hi], preferred_element_type=F32)

    blocks = A_WIDTH // LANES
    y = proj(_N_KA, _N_KA + A_WIDTH)
    for p in range(blocks):
        ka_ref[0, p] = rope(y[:, p * LANES:(p + 1) * LANES]).astype(BF16)

    def scatter_streams(which):
        per_group = B_OUT_WIDTH // LANES
        for g, d in enumerate(B_DILATIONS):
            out = streams[3 * g + which]
            for r in range(d):
                for p in range(per_group):
                    rows = y_ref[g * per_group + p, pl.ds(r, TM // d, stride=d), :]
                    out[0, r, :, p * LANES:(p + 1) * LANES] = rows.astype(BF16)

    y = proj(_N_QB, _N_QB + B_WIDTH)
    for p in range(blocks):
        y_ref[p] = rope(y[:, p * LANES:(p + 1) * LANES]) * (ATTN_SCALE * LOG2E)
    scatter_streams(0)
    y = proj(_N_KB, _N_KB + B_WIDTH)
    for p in range(blocks):
        y_ref[p] = rope(y[:, p * LANES:(p + 1) * LANES])
    scatter_streams(1)
    y = proj(_N_VB, _N_VB + B_WIDTH)
    for p in range(blocks):
        y_ref[p] = y[:, p * LANES:(p + 1) * LANES]
    scatter_streams(2)
    r = rope(proj(_N_KI, _N_KI + LANES))
    hi = r.astype(BF16).astype(F32)
    hl = jnp.where(lane < IDX_DIM, hi, r - hi).astype(BF16)
    ki_ref[0, :, 0:LANES] = hl
    ki_ref[0, :, LANES:2 * LANES] = hl


def _proj_nat(x, cos, sin, w_nat):
    b, s, d = x.shape
    nt = s // TM
    tok = lambda width: pl.BlockSpec((1, TM, width), lambda i, j: (i, j, 0))
    stream_specs, stream_shapes = [], []
    for dil in B_DILATIONS:
        for _ in range(3):
            stream_specs.append(pl.BlockSpec((1, dil, TM // dil, B_OUT_WIDTH), lambda i, j: (i, 0, j, 0)))
            stream_shapes.append(jax.ShapeDtypeStruct((b, dil, s // dil, B_OUT_WIDTH), BF16))
    return pl.pallas_call(
        _proj_nat_kernel,
        grid=(b, nt),
        in_specs=[tok(d),
                  pl.BlockSpec((TM, LANES), lambda i, j: (i * nt + j, 0)),
                  pl.BlockSpec((TM, LANES), lambda i, j: (i * nt + j, 0)),
                  pl.BlockSpec((d, _N_NAT), lambda i, j: (0, 0))],
        out_specs=[pl.BlockSpec((1, A_WIDTH // LANES, TM, LANES), lambda i, j: (i, 0, j, 0)),
                   tok(2 * LANES)] + stream_specs,
        out_shape=[jax.ShapeDtypeStruct((b, A_WIDTH // LANES, s, LANES), BF16),
                   jax.ShapeDtypeStruct((b, s, 2 * LANES), BF16)] + stream_shapes,
        scratch_shapes=[pltpu.VMEM((B_WIDTH // LANES, TM, LANES), F32)],
        compiler_params=_params(("parallel", "parallel")),
        name="proj_token_major",
    )(x, cos, sin, w_nat)


def _proj_t_kernel(x_ref, ct_ref, st_ref, w_ref, qt_ref, vt_ref, qi_ref, wt_ref):
    xb = x_ref[0].astype(BF16)
    cos, sin = ct_ref[...], st_ref[...]

    def proj(lo, hi):
        return lax.dot_general(w_ref[lo:hi, :], xb, (((1,), (1,)), ((), ())), preferred_element_type=F32)

    def rope_head(y):
        x1, x2 = y[0:ROT_HALF], y[ROT_HALF:2 * ROT_HALF]
        return jnp.concatenate([x1 * cos - x2 * sin, x2 * cos + x1 * sin, y[2 * ROT_HALF:]], axis=0)

    y = proj(_T_QA, _T_QA + A_WIDTH)
    for h in range(A_HEADS):
        r = rope_head(y[h * HEAD_DIM:(h + 1) * HEAD_DIM])
        qt_ref[0, h * HEAD_DIM:(h + 1) * HEAD_DIM, :] = (r * (ATTN_SCALE * LOG2E)).astype(BF16)
    y = proj(_T_VA, _T_VA + A_WIDTH)
    for h in range(A_HEADS):
        for c in range(TM // TK):
            vt_ref[0, h, c, 0:HEAD_DIM, :] = y[h * HEAD_DIM:(h + 1) * HEAD_DIM, c * TK:(c + 1) * TK].astype(BF16)
            vt_ref[0, h, c, HEAD_DIM:V_ROWS, :] = jnp.ones((V_ROWS - HEAD_DIM, TK), BF16)
    y = proj(_T_QI, _T_QI + IDX_HEADS * IDX_DIM)
    for h in range(IDX_HEADS):
        r = rope_head(y[h * IDX_DIM:(h + 1) * IDX_DIM])
        hi = r.astype(BF16)
        lo = (r - hi.astype(F32)).astype(BF16)
        base = 4 * h * IDX_DIM
        qi_ref[0, base:base + IDX_DIM, :] = hi
        qi_ref[0, base + IDX_DIM:base + 2 * IDX_DIM, :] = hi
        qi_ref[0, base + 2 * IDX_DIM:base + 3 * IDX_DIM, :] = lo
        qi_ref[0, base + 3 * IDX_DIM:base + 4 * IDX_DIM, :] = lo
    wt_ref[0] = proj(_T_WI, _T_ROWS)[0:IDX_HEADS] * IDX_SCALE


def _proj_t(x, cos_t, sin_t, w_t):
    b, s, d = x.shape
    nt = s // TM
    return pl.pallas_call(
        _proj_t_kernel,
        grid=(b, nt),
        in_specs=[pl.BlockSpec((1, TM, d), lambda i, j: (i, j, 0)),
                  pl.BlockSpec((ROT_HALF, TM), lambda i, j: (0, i * nt + j)),
                  pl.BlockSpec((ROT_HALF, TM), lambda i, j: (0, i * nt + j)),
                  pl.BlockSpec((_T_ROWS, d), lambda i, j: (0, 0))],
        out_specs=[pl.BlockSpec((1, A_WIDTH, TM), lambda i, j: (i, 0, j)),
                   pl.BlockSpec((1, A_HEADS, TM // TK, V_ROWS, TK), lambda i, j: (i, 0, j, 0, 0)),
                   pl.BlockSpec((1, 4 * IDX_HEADS * IDX_DIM, TM), lambda i, j: (i, 0, j)),
                   pl.BlockSpec((1, IDX_HEADS, TM), lambda i, j: (i, 0, j))],
        out_shape=[jax.ShapeDtypeStruct((b, A_WIDTH, s), BF16),
                   jax.ShapeDtypeStruct((b, A_HEADS, s // TK, V_ROWS, TK), BF16),
                   jax.ShapeDtypeStruct((b, 4 * IDX_HEADS * IDX_DIM, s), BF16),
                   jax.ShapeDtypeStruct((b, IDX_HEADS, s), F32)],
        compiler_params=_params(("parallel", "parallel")),
        name="proj_feature_major",
    )(x, cos_t, sin_t, w_t)


def _dsa_kernel(qt_ref, qi_ref, wt_ref, k_ref, vt_ref, ki_ref, o_ref,
                sc_ref, qpad_ref, ot_ref, m_ref, st_ref, stat_ref, sa_ref, sb_ref):
    j = pl.program_id(1)
    nkt = j + 1

    for h in range(A_HEADS):
        off = (h % 2) * HEAD_DIM
        qpad_ref[h] = jnp.zeros((2 * HEAD_DIM, TQ), BF16)
        qpad_ref[h, off:off + HEAD_DIM, :] = qt_ref[0, h * HEAD_DIM:(h + 1) * HEAD_DIM, :]

    qpos = j * TQ + lax.broadcasted_iota(jnp.int32, (1, TQ), 1)
    row_iota = lax.broadcasted_iota(jnp.int32, (TK, TQ), 0)
    w = wt_ref[0]

    def fold(t):
        return t.reshape(TK // 8, 8, TQ)

    def score_tile(kt, diagonal, carry):
        mn, mx, pos, nn = carry
        ki = ki_ref[0, pl.ds(pl.multiple_of(kt * TK, TK), TK), :]
        acc = jnp.zeros((TK, TQ), F32)
        for h in range(IDX_HEADS):
            s = jnp.dot(ki, qi_ref[0, 4 * h * IDX_DIM:4 * (h + 1) * IDX_DIM, :], preferred_element_type=F32)
            acc = acc + w[h:h + 1, :] * jnp.maximum(s, 0.0)
        if diagonal:
            causal = (kt * TK + row_iota) <= qpos
            val = jnp.where(causal, acc, -jnp.inf)
            low = jnp.where(causal, acc, jnp.inf)
        else:
            val = low = acc
        sc_ref[kt] = val
        return (jnp.minimum(mn, fold(low).min(axis=0)),
                jnp.maximum(mx, fold(val).max(axis=0)),
                pos + fold(jnp.where(val > 0.0, 1.0, 0.0)).sum(axis=0),
                nn + fold(jnp.where(val >= 0.0, 1.0, 0.0)).sum(axis=0))

    def load_stats():
        return tuple(stat_ref[n] for n in range(4))

    def store_stats(stats):
        for n, v in enumerate(stats):
            stat_ref[n] = v

    def score_block(i, carry):
        for u in range(KEY_TILES_PER_STEP):
            carry = score_tile(KEY_TILES_PER_STEP * i + u, False, carry)
        return carry

    zeros8 = jnp.zeros((8, TQ), F32)
    store_stats(lax.fori_loop(
        0, j // KEY_TILES_PER_STEP, score_block,
        (jnp.full((8, TQ), jnp.inf, F32), jnp.full((8, TQ), -jnp.inf, F32), zeros8, zeros8)))

    for left in range(KEY_TILES_PER_STEP):
        @pl.when(j % KEY_TILES_PER_STEP == left)
        def _(left=left):
            stats = load_stats()
            for u in range(left):
                stats = score_tile(j - left + u, False, stats)
            store_stats(score_tile(j, True, stats))

    mn8, mx8, pos8, nn8 = load_stats()
    mn = mn8.min(axis=0, keepdims=True)
    mx = mx8.max(axis=0, keepdims=True)
    c_pos = pos8.sum(axis=0, keepdims=True)
    c_nn = nn8.sum(axis=0, keepdims=True)

    sc_ref[nkt] = jnp.full((TK, TQ), -jnp.inf, F32)

    def count(preds):
        def part(p, kt):
            return fold(p(sc_ref[kt])).sum(axis=0)

        def body(i, cnts):
            return tuple(c + part(p, 2 * i) + part(p, 2 * i + 1) for c, p in zip(cnts, preds))

        cnts = lax.fori_loop(0, (nkt + 1) // 2, body, tuple(jnp.zeros((8, TQ), F32) for _ in preds))
        return [c.sum(axis=0, keepdims=True) for c in cnts]

    def ge(cand):
        return lambda t: jnp.where(t >= cand, 1.0, 0.0)

    n_causal = (qpos + 1).astype(F32)
    k_q = jnp.minimum(n_causal, float(TOPK_MAX))
    c_mx, = count([ge(mx)])
    select = n_causal > k_q
    at_max = select & (c_mx >= k_q)
    search = select & (c_mx < k_q)
    at_zero = search & (c_pos < k_q) & (c_nn >= k_q)
    above = search & (c_pos >= k_q)
    below = search & (c_nn < k_q)
    lo0 = jnp.where(at_max, mx, jnp.where(at_zero | above, 0.0, mn))
    c_lo0 = jnp.where(at_max, c_mx, jnp.where(at_zero | above, c_nn, n_causal))
    st_ref[0:1, :] = lo0
    st_ref[1:2, :] = jnp.where(below, 0.0, mx)
    st_ref[2:3, :] = c_lo0
    st_ref[3:4, :] = jnp.where(at_max, 0.0, jnp.where(at_zero, c_pos, jnp.where(below, c_nn, c_mx)))
    act0 = jnp.where((above | below) & (c_lo0 > k_q), 1.0, 0.0)
    st_ref[4:5, :] = act0

    def bisect_once():
        lo, hi, c_lo, c_hi = st_ref[0:1, :], st_ref[1:2, :], st_ref[2:3, :], st_ref[3:4, :]
        act = st_ref[4:5, :] > 0.0
        mid = 0.5 * lo + 0.5 * hi
        live = act & (mid > lo) & (mid < hi)
        c, = count([ge(mid)])
        up = live & (c >= k_q)
        dn = live & (c < k_q)
        c_lo = jnp.where(up, c, c_lo)
        st_ref[0:1, :] = jnp.where(up, mid, lo)
        st_ref[1:2, :] = jnp.where(dn, mid, hi)
        st_ref[2:3, :] = c_lo
        st_ref[3:4, :] = jnp.where(dn, c, c_hi)
        act_new = jnp.where(live & (c_lo > k_q), 1.0, 0.0)
        st_ref[4:5, :] = act_new
        return act_new

    def bisect(go):
        for _ in range(BISECT_STEPS_PER_CHECK - 1):
            bisect_once()
        return (jnp.max(bisect_once()) > 0.0).astype(jnp.int32)

    @pl.when(jnp.max(act0) > 0.0)
    def _():
        def blind(_, carry):
            bisect_once()
            bisect_once()
            return carry

        lax.fori_loop(0, BISECT_BLIND_STEPS // 2, blind, 0)

    lax.while_loop(lambda go: go > 0, bisect, (jnp.max(st_ref[4:5, :]) > 0.0).astype(jnp.int32))

    lo = st_ref[0:1, :]
    tie = st_ref[2:3, :] > k_q

    @pl.when(jnp.max(jnp.where(tie, 1.0, 0.0)) > 0.0)
    def _():
        need = jnp.where(tie, k_q - st_ref[3:4, :], jnp.inf)
        tri = jnp.where(lax.broadcasted_iota(jnp.int32, (TK, TK), 0) >= lax.broadcasted_iota(jnp.int32, (TK, TK), 1),
                        1.0, 0.0).astype(BF16)

        def drop(kt, seen):
            t = sc_ref[kt]
            eq = jnp.where(t == lo, 1.0, 0.0)
            rank = jnp.dot(tri, eq.astype(BF16), preferred_element_type=F32) + seen
            sc_ref[kt] = jnp.where(t == lo, jnp.where(rank > need, -jnp.inf, t), t)
            return seen + fold(eq).sum(axis=0).sum(axis=0, keepdims=True)

        def drop_pair(i, seen):
            return drop(2 * i + 1, drop(2 * i, seen))

        lax.fori_loop(0, (nkt + 1) // 2, drop_pair, jnp.zeros((1, TQ), F32))

    m_ref[...] = jnp.full(m_ref.shape, MASK_NEG, F32)
    ot_ref[...] = jnp.zeros(ot_ref.shape, F32)

    def mask_bias(kt):
        return jnp.where(sc_ref[kt] >= lo, 0.0, MASK_NEG).astype(BF16)

    def logits(h, kt, bias):
        rows = pl.ds(pl.multiple_of(kt * TK, TK), TK)
        return jnp.dot(k_ref[0, h // 2, rows, :], qpad_ref[h], preferred_element_type=F32).astype(BF16) + bias

    def half_step(kt, cur_ref, nxt_ref):
        if nxt_ref is not None:
            bias_n = mask_bias(kt + 1)
        for h in range(A_HEADS):
            s = cur_ref[h]
            if nxt_ref is not None:
                nxt_ref[h] = logits(h, kt + 1, bias_n)
            m = m_ref[h]
            m_tile = s.reshape(TK // 16, 16, TQ).max(axis=0).astype(F32).max(axis=0, keepdims=True)
            m_new = jnp.maximum(m, m_tile)
            p = jnp.exp2(s - m_new.astype(BF16))
            corr = jnp.exp2(m - m_new)
            m_ref[h] = m_new
            ot_ref[h] = ot_ref[h] * corr + jnp.dot(vt_ref[0, h, kt], p, preferred_element_type=F32)

    bias0 = mask_bias(0)
    for h in range(A_HEADS):
        sa_ref[h] = logits(h, 0, bias0)

    def buffers(u):
        return (sa_ref, sb_ref) if u % 2 == 0 else (sb_ref, sa_ref)

    def kv_block(i, _):
        for u in range(KEY_TILES_PER_STEP):
            half_step(KEY_TILES_PER_STEP * i + u, *buffers(u))
        return 0

    n_blocks = (nkt - 1) // KEY_TILES_PER_STEP
    lax.fori_loop(0, n_blocks, kv_block, 0)

    for left in range(1, KEY_TILES_PER_STEP + 1):
        @pl.when(nkt - KEY_TILES_PER_STEP * n_blocks == left)
        def _(left=left):
            for u in range(left):
                cur_ref, nxt_ref = buffers(u)
                half_step(nkt - left + u, cur_ref, nxt_ref if u < left - 1 else None)

    for p in range(A_HEADS // 2):
        both = jnp.concatenate([ot_ref[h, 0:HEAD_DIM, :] / ot_ref[h, HEAD_DIM:HEAD_DIM + 1, :]
                                for h in (2 * p, 2 * p + 1)], axis=0)
        o_ref[0, :, p * LANES:(p + 1) * LANES] = both.T.astype(BF16)


def _dsa_attention(qt, qi4t, wt, k6, vt5, ki4):
    b, _, s = qt.shape
    nq = s // TQ
    return pl.pallas_call(
        _dsa_kernel,
        grid=(b, nq),
        in_specs=[pl.BlockSpec((1, A_WIDTH, TQ), lambda i, j: (i, 0, j)),
                  pl.BlockSpec((1, 4 * IDX_HEADS * IDX_DIM, TQ), lambda i, j: (i, 0, j)),
                  pl.BlockSpec((1, IDX_HEADS, TQ), lambda i, j: (i, 0, j)),
                  pl.BlockSpec((1, A_WIDTH // LANES, s, LANES), lambda i, j: (i, 0, 0, 0)),
                  pl.BlockSpec((1, A_HEADS, s // TK, V_ROWS, TK), lambda i, j: (i, 0, 0, 0, 0)),
                  pl.BlockSpec((1, s, 4 * IDX_DIM), lambda i, j: (i, 0, 0))],
        out_specs=pl.BlockSpec((1, TQ, A_WIDTH), lambda i, j: (i, j, 0)),
        out_shape=jax.ShapeDtypeStruct((b, s, A_WIDTH), BF16),
        scratch_shapes=[pltpu.VMEM((s // TK + 1, TK, TQ), F32),
                        pltpu.VMEM((A_HEADS, 2 * HEAD_DIM, TQ), BF16),
                        pltpu.VMEM((A_HEADS, V_ROWS, TQ), F32),
                        pltpu.VMEM((A_HEADS, 1, TQ), F32),
                        pltpu.VMEM((8, TQ), F32),
                        pltpu.VMEM((4, 8, TQ), F32),
                        pltpu.VMEM((A_HEADS, TK, TQ), BF16),
                        pltpu.VMEM((A_HEADS, TK, TQ), BF16)],
        compiler_params=_params(("parallel", "arbitrary")),
        name="dsa_attention",
    )(qt, qi4t, wt, k6, vt5, ki4)


def _window_kernel(q_ref, kp_ref, kc_ref, vp_ref, vc_ref, o_ref, lse_ref):
    hb = B_WINDOW_STEPS
    n_streams, n_rows = q_ref.shape[1], q_ref.shape[2]
    r = lax.broadcasted_iota(jnp.int32, (hb, 2 * hb), 0)
    c = lax.broadcasted_iota(jnp.int32, (hb, 2 * hb), 1)
    dist = r + hb - c
    band = jnp.where(dist >= 0, jnp.where(dist <= B_WINDOW_STEPS, 0.0, NEG), NEG)
    first_col = jnp.where(pl.program_id(2) > 0, 0, hb)
    band_first = jnp.where(c >= first_col, band, NEG)
    lane = lax.broadcasted_iota(jnp.int32, (hb, LANES), 1)
    left = lane < HEAD_DIM
    ones = jnp.ones((2 * hb, LANES), BF16)
    for sb in range(n_streams):
        for t in range(n_rows // TB):
            for p in range(B_HEADS_PER_GROUP // 2):
                cols = slice(p * LANES, (p + 1) * LANES)
                tile = slice(t * TB, (t + 1) * TB)
                q2, kc, vc = q_ref[0, sb, tile, cols], kc_ref[0, sb, tile, cols], vc_ref[0, sb, tile, cols]
                if t == 0:
                    k_before, v_before = kp_ref[0, sb, :, cols], vp_ref[0, sb, :, cols]
                else:
                    before = slice(t * TB - hb, t * TB)
                    k_before, v_before = kc_ref[0, sb, before, cols], vc_ref[0, sb, before, cols]
                windows = ((jnp.concatenate([k_before, kc[:hb]], axis=0),
                            jnp.concatenate([v_before, vc[:hb]], axis=0)), (kc, vc))
                for half, ((kw, vw), bias) in enumerate(zip(windows, (band_first if t == 0 else band, band))):
                    rows = slice(t * TB + half * hb, t * TB + (half + 1) * hb)
                    v1 = jnp.concatenate([vw, ones], axis=1)
                    outs, lses = [], []
                    for side in (left, ~left):
                        qh = jnp.where(side, q2[half * hb:(half + 1) * hb], jnp.zeros((hb, LANES), BF16))
                        s = lax.dot_general(qh, kw, (((1,), (1,)), ((), ())), preferred_element_type=F32) + bias
                        m = s.max(axis=-1, keepdims=True)
                        e = jnp.exp2((s - m).astype(BF16))
                        ol = jnp.dot(e, v1, preferred_element_type=F32)
                        l = ol[:, LANES:]
                        outs.append(ol[:, :LANES] / l)
                        lses.append(m + jnp.log2(l))
                    o_ref[0, sb, rows, cols] = jnp.where(left, outs[0], outs[1])
                    lse_ref[0, sb, rows, cols] = jnp.where(left, lses[0], lses[1])


def _window_attention(q, k, v):
    b, d, n, _ = q.shape
    assert TB == 2 * B_WINDOW_STEPS
    rows = min(n, WINDOW_TILES_PER_STEP * TB)
    streams = min(d, WINDOW_TILES_PER_STEP * TB // rows)
    halves = rows // (TB // 2)
    cur = pl.BlockSpec((1, streams, rows, B_OUT_WIDTH), lambda bi, ri, ti: (bi, ri, ti, 0))
    prev = pl.BlockSpec((1, streams, TB // 2, B_OUT_WIDTH),
                        lambda bi, ri, ti: (bi, ri, jnp.maximum(halves * ti - 1, 0), 0))
    return pl.pallas_call(
        _window_kernel,
        grid=(b, d // streams, n // rows),
        in_specs=[cur, prev, cur, prev, cur],
        out_specs=[cur, cur],
        out_shape=[jax.ShapeDtypeStruct((b, d, n, B_OUT_WIDTH), F32)] * 2,
        compiler_params=_params(("parallel", "parallel", "arbitrary")),
        name=f"window_attention_d{d}",
    )(q, k, k, v, v)


def _layer_norm(y, g, b):
    mu = y.mean(axis=-1, keepdims=True)
    yc = y - mu
    var = (yc * yc).mean(axis=-1, keepdims=True)
    return yc * lax.rsqrt(var + LN_EPS) * g + b


def _merge_kernel(alpha, oa_ref, o0_ref, o1_ref, o2_ref, l0_ref, l1_ref, l2_ref, x_ref,
                  wg_ref, bg_ref, wa_ref, wb_ref, wo_ref, g_ref, b_ref, y_ref,
                  tok_ref, ob_ref, xb_ref, mg_ref):
    halves = B_OUT_WIDTH // LANES

    for n, ref in enumerate((o1_ref, o2_ref, l1_ref, l2_ref)):
        d = ref.shape[1]
        for r in range(d):
            for p in range(halves):
                tok_ref[n, p, pl.ds(r, TM_MERGE // d, stride=d), :] = ref[0, r, :, p * LANES:(p + 1) * LANES]
    for c in range(TM_MERGE // MERGE_ROWS):
        rows = slice(c * MERGE_ROWS, (c + 1) * MERGE_ROWS)
        for p in range(halves):
            cols = slice(p * LANES, (p + 1) * LANES)
            l0, l1, l2 = l0_ref[0, 0, rows, cols], tok_ref[2, p, rows, :], tok_ref[3, p, rows, :]
            lm = jnp.maximum(jnp.maximum(l0, l1), l2)
            e0, e1, e2 = jnp.exp2(l0 - lm), jnp.exp2(l1 - lm), jnp.exp2(l2 - lm)
            ob = (e0 * o0_ref[0, 0, rows, cols] + e1 * tok_ref[0, p, rows, :] + e2 * tok_ref[1, p, rows, :])
            ob_ref[rows, cols] = (ob / (e0 + e1 + e2)).astype(BF16)

    xb_ref[...] = x_ref[0].astype(BF16)
    for c in range(D_MODEL // MERGE_COLS):
        cols = slice(c * MERGE_COLS, (c + 1) * MERGE_COLS)
        gcols = slice(D_MODEL + c * MERGE_COLS, D_MODEL + (c + 1) * MERGE_COLS)

        def gate(sel):
            z = jnp.dot(xb_ref[...], wg_ref[:, sel], preferred_element_type=F32) + bg_ref[:, sel]
            return 1.0 / (1.0 + jnp.exp(-z))

        pa = jnp.dot(oa_ref[0], wa_ref[:, cols], preferred_element_type=F32)
        pb = jnp.dot(ob_ref[...], wb_ref[:, cols], preferred_element_type=F32)
        mg_ref[:, cols] = (gate(cols) * pa + gate(gcols) * pb).astype(BF16)

    for c in range(TM_MERGE // MERGE_ROWS):
        rows = slice(c * MERGE_ROWS, (c + 1) * MERGE_ROWS)
        mixed = jnp.dot(mg_ref[rows, :], wo_ref[...], preferred_element_type=F32)
        y_ref[0, rows, :] = _layer_norm(alpha * x_ref[0, rows, :] + mixed, g_ref[...], b_ref[...])


def _merge(alpha, oa, obs, lses, x, w_gate, b_gate, wa, wb, wo, g, bb):
    b, s, _ = x.shape
    tm = TM_MERGE
    tok = lambda width: pl.BlockSpec((1, tm, width), lambda i, j: (i, j, 0))
    full = lambda a: pl.BlockSpec(a.shape, lambda i, j: (0, 0), pipeline_mode=pl.Buffered(1))
    streams = [pl.BlockSpec((1, a.shape[1], tm // a.shape[1], B_OUT_WIDTH), lambda i, j: (i, 0, j, 0))
               for a in list(obs) + list(lses)]
    return pl.pallas_call(
        functools.partial(_merge_kernel, alpha),
        grid=(b, s // tm),
        in_specs=[tok(A_WIDTH)] + streams + [tok(D_MODEL), full(w_gate),
                  full(b_gate), full(wa), full(wb), full(wo), full(g), full(bb)],
        out_specs=tok(D_MODEL),
        out_shape=jax.ShapeDtypeStruct((b, s, D_MODEL), F32),
        scratch_shapes=[pltpu.VMEM((4, B_OUT_WIDTH // LANES, tm, LANES), F32),
                        pltpu.VMEM((tm, B_OUT_WIDTH), BF16),
                        pltpu.VMEM((tm, D_MODEL), BF16),
                        pltpu.VMEM((tm, D_MODEL), BF16)],
        compiler_params=_params(("parallel", "parallel")),
        name="merge_outproj_norm",
    )(oa, *obs, *lses, x, w_gate, b_gate, wa, wb, wo, g, bb)


def _ffn_kernel(alpha, x_ref, wg_ref, wu_ref, wd_ref, g_ref, b_ref, y_ref):
    for r in range(TM_FFN // TM):
        rows = slice(r * TM, (r + 1) * TM)
        x = x_ref[rows, :]
        xb = x.astype(BF16)
        acc = jnp.zeros((TM, D_MODEL), F32)
        for c in range(FFN_HIDDEN // FFN_CHUNK):
            cols = slice(c * FFN_CHUNK, (c + 1) * FFN_CHUNK)
            gate = jnp.dot(xb, wg_ref[:, cols], preferred_element_type=F32)
            up = jnp.dot(xb, wu_ref[:, cols], preferred_element_type=F32)
            h = gate / (1.0 + jnp.exp(-gate)) * up
            acc = acc + jnp.dot(h.astype(BF16), wd_ref[cols, :], preferred_element_type=F32)
        y_ref[rows, :] = _layer_norm(alpha * x + acc, g_ref[...], b_ref[...])


def _ffn(alpha, x2, wg, wu, wd, g, bb):
    m = x2.shape[0]
    tok = pl.BlockSpec((TM_FFN, D_MODEL), lambda i: (i, 0))
    full = lambda a: pl.BlockSpec(a.shape, lambda i: (0, 0), pipeline_mode=pl.Buffered(1))
    return pl.pallas_call(
        functools.partial(_ffn_kernel, alpha),
        grid=(m // TM_FFN,),
        in_specs=[tok, full(wg), full(wu), full(wd), full(g), full(bb)],
        out_specs=tok,
        out_shape=jax.ShapeDtypeStruct((m, D_MODEL), F32),
        compiler_params=_params(("parallel",)),
        name="swiglu_norm",
    )(x2, wg, wu, wd, g, bb)


def _split_w_in(w):
    a, bw, hi = A_WIDTH, B_WIDTH, IDX_HEADS * IDX_DIM
    o = 0
    qa, o = w[:, o:o + a], o + a
    ka, o = w[:, o:o + a], o + a
    va, o = w[:, o:o + a], o + a
    qb, o = w[:, o:o + bw], o + bw
    kb, o = w[:, o:o + bw], o + bw
    vb, o = w[:, o:o + bw], o + bw
    qi, o = w[:, o:o + hi], o + hi
    ki, o = w[:, o:o + IDX_DIM], o + IDX_DIM
    wi, o = w[:, o:o + IDX_HEADS], o + IDX_HEADS
    w_gate = w[:, o:].astype(BF16)
    w_nat = jnp.concatenate([ka, qb, kb, vb, ki, ki], axis=1).astype(BF16)
    pad = jnp.zeros((w.shape[0], _T_ROWS - _T_WI - IDX_HEADS), w.dtype)
    w_t = jnp.concatenate([qa, va, qi, wi, pad], axis=1).T.astype(BF16)
    return w_nat, w_t, w_gate


def kernel(x, positions, w_in, b_gate, w_branch_a, w_branch_b, w_out, ln1_g, ln1_b,
           w_ffn_gate, w_ffn_up, w_ffn_down, ln2_g, ln2_b):
    b, s, d = x.shape
    depth = w_in.shape[0]
    assert d == D_MODEL and s % (max(B_DILATIONS) * TB) == 0 and s % max(TM, TM_MERGE) == 0
    assert (b * s) % TM_FFN == 0 and B_DILATIONS[0] == 1
    alpha = (2 * depth) ** 0.25
    cos, sin, cos_t, sin_t = _rope_tables(positions)
    row = lambda v: v.reshape(1, -1)
    for layer in range(depth):
        w_nat, w_t, w_gate = _split_w_in(w_in[layer])
        ka6, ki4, *streams = _proj_nat(x, cos, sin, w_nat)
        qt, vt5, qi4t, wt = _proj_t(x, cos_t, sin_t, w_t)
        oa = _dsa_attention(qt, qi4t, wt, ka6, vt5, ki4)
        groups = [_window_attention(*streams[3 * g:3 * g + 3]) for g in range(len(B_DILATIONS))]
        x1 = _merge(alpha, oa, [o for o, _ in groups], [l for _, l in groups], x, w_gate, row(b_gate[layer]),
                    w_branch_a[layer].astype(BF16), w_branch_b[layer].astype(BF16),
                    w_out[layer].astype(BF16), row(ln1_g[layer]), row(ln1_b[layer]))
        x2 = _ffn(alpha, x1.reshape(b * s, d), w_ffn_gate[layer].astype(BF16), w_ffn_up[layer].astype(BF16),
                  w_ffn_down[layer].astype(BF16), row(ln2_g[layer]), row(ln2_b[layer]))
        x = x2.reshape(b, s, d)
    return x
```

```python
import functools

import jax
import jax.numpy as jnp
from jax import lax
from jax.experimental import pallas as pl
from jax.experimental.pallas import tpu as pltpu

F32 = jnp.float32
BF16 = jnp.bfloat16

D_MODEL = 1024
HEAD_DIM = 64
ROT_HALF = 8
ROPE_THETA = 500000.0
ATTN_SCALE = HEAD_DIM ** -0.5
LOG2E = 1.4426950408889634
A_HEADS = 12
A_WIDTH = A_HEADS * HEAD_DIM
IDX_HEADS = 8
IDX_DIM = 64
IDX_SCALE = (IDX_HEADS ** -0.5) * (IDX_DIM ** -0.5)
TOPK_MAX = 256
B_DILATIONS = (1, 4, 16)
B_WINDOW_STEPS = 128
B_HEADS_PER_GROUP = 4
B_WIDTH = 3 * B_HEADS_PER_GROUP * HEAD_DIM
B_OUT_WIDTH = B_HEADS_PER_GROUP * HEAD_DIM
FFN_HIDDEN = 2816
LN_EPS = 1e-5
NEG = -1e30
MASK_NEG = -(2.0 ** 100)

LANES = 128
SUBLANES = 8
V_ROWS = HEAD_DIM + 2 * SUBLANES
VMEM_LIMIT = 56 * 1024 * 1024
TM = 512
TM_FFN = 1024
TM_MERGE = 1024
TQ = 256
TK = 256
TB = 256
FFN_CHUNK = 256
KEY_TILES_PER_STEP = 4
BISECT_STEPS_PER_CHECK = 2
BISECT_BLIND_STEPS = 16
WINDOW_TILES_PER_STEP = 16
MERGE_ROWS = 256
MERGE_COLS = 256
ROPE_TILE = 2048

_N_KA, _N_QB, _N_KB, _N_VB = 0, A_WIDTH, A_WIDTH + B_WIDTH, A_WIDTH + 2 * B_WIDTH
_N_KI = A_WIDTH + 3 * B_WIDTH
_N_NAT = _N_KI + LANES
_T_QA, _T_VA, _T_QI = 0, A_WIDTH, 2 * A_WIDTH
_T_WI = _T_QI + IDX_HEADS * IDX_DIM
_T_ROWS = _T_WI + 2 * SUBLANES


def _params(sem):
    return pltpu.CompilerParams(dimension_semantics=sem, vmem_limit_bytes=VMEM_LIMIT)


def _rope_kernel(pr_ref, fc_ref, c_ref, s_ref, ct_ref, st_ref):
    ang_t = fc_ref[...] * pr_ref[...].astype(F32)
    cos_t, sin_t = jnp.cos(ang_t), jnp.sin(ang_t)
    ct_ref[...] = cos_t
    st_ref[...] = sin_t
    reps = LANES // ROT_HALF
    cos = jnp.concatenate([cos_t] * reps, axis=0).T
    sin = jnp.concatenate([sin_t] * reps, axis=0).T
    d = lax.broadcasted_iota(jnp.int32, cos.shape, 1) & (HEAD_DIM - 1)
    c_ref[...] = jnp.where(d < 2 * ROT_HALF, cos, 1.0)
    s_ref[...] = jnp.where(d < ROT_HALF, -sin, jnp.where(d < 2 * ROT_HALF, sin, 0.0))


def _rope_tables(positions):
    m = positions.size
    inv_freq = ROPE_THETA ** (-jnp.arange(0, 2 * ROT_HALF, 2, dtype=F32) / (2 * ROT_HALF))
    t = min(ROPE_TILE, m)
    return pl.pallas_call(
        _rope_kernel,
        grid=(m // t,),
        in_specs=[pl.BlockSpec((1, t), lambda i: (0, i)), pl.BlockSpec((ROT_HALF, 1), lambda i: (0, 0))],
        out_specs=[pl.BlockSpec((t, LANES), lambda i: (i, 0)), pl.BlockSpec((t, LANES), lambda i: (i, 0)),
                   pl.BlockSpec((ROT_HALF, t), lambda i: (0, i)), pl.BlockSpec((ROT_HALF, t), lambda i: (0, i))],
        out_shape=[jax.ShapeDtypeStruct((m, LANES), F32), jax.ShapeDtypeStruct((m, LANES), F32),
                   jax.ShapeDtypeStruct((ROT_HALF, m), F32), jax.ShapeDtypeStruct((ROT_HALF, m), F32)],
        compiler_params=_params(("parallel",)),
        name="rope_tables",
    )(positions.reshape(1, m), inv_freq.reshape(ROT_HALF, 1))


def _proj_nat_kernel(x_ref, c_ref, s_ref, w_ref, ka_ref, ki_ref, *rest):
    streams, y_ref = rest[:-1], rest[-1]
    xb = x_ref[0].astype(BF16)
    cos, sin = c_ref[...], s_ref[...]
    lane = lax.broadcasted_iota(jnp.int32, cos.shape, 1)
    first = (lane & (HEAD_DIM - 1)) < ROT_HALF

    def rope(y):
        partner = jnp.where(first, pltpu.roll(y, LANES - ROT_HALF, 1), pltpu.roll(y, ROT_HALF, 1))
        return y * cos + partner * sin

    def proj(lo, hi):
        return jnp.dot(xb, w_ref[:, lo:hi], preferred_element_type=F32)

    blocks = A_WIDTH // LANES
    y = proj(_N_KA, _N_KA + A_WIDTH)
    for p in range(blocks):
        ka_ref[0, p] = rope(y[:, p * LANES:(p + 1) * LANES]).astype(BF16)

    def scatter_streams(which):
        per_group = B_OUT_WIDTH // LANES
        for g, d in enumerate(B_DILATIONS):
            out = streams[3 * g + which]
            for r in range(d):
                for p in range(per_group):
                    rows = y_ref[g * per_group + p, pl.ds(r, TM // d, stride=d), :]
                    out[0, r, :, p * LANES:(p + 1) * LANES] = rows.astype(BF16)

    y = proj(_N_QB, _N_QB + B_WIDTH)
    for p in range(blocks):
        y_ref[p] = rope(y[:, p * LANES:(p + 1) * LANES]) * (ATTN_SCALE * LOG2E)
    scatter_streams(0)
    y = proj(_N_KB, _N_KB + B_WIDTH)
    for p in range(blocks):
        y_ref[p] = rope(y[:, p * LANES:(p + 1) * LANES])
    scatter_streams(1)
    y = proj(_N_VB, _N_VB + B_WIDTH)
    for p in range(blocks):
        y_ref[p] = y[:, p * LANES:(p + 1) * LANES]
    scatter_streams(2)
    r = rope(proj(_N_KI, _N_KI + LANES))
    hi = r.astype(BF16).astype(F32)
    hl = jnp.where(lane < IDX_DIM, hi, r - hi).astype(BF16)
    ki_ref[0, :, 0:LANES] = hl
    ki_ref[0, :, LANES:2 * LANES] = hl


def _proj_nat(x, cos, sin, w_nat):
    b, s, d = x.shape
    nt = s // TM
    tok = lambda width: pl.BlockSpec((1, TM, width), lambda i, j: (i, j, 0))
    stream_specs, stream_shapes = [], []
    for dil in B_DILATIONS:
        for _ in range(3):
            stream_specs.append(pl.BlockSpec((1, dil, TM // dil, B_OUT_WIDTH), lambda i, j: (i, 0, j, 0)))
            stream_shapes.append(jax.ShapeDtypeStruct((b, dil, s // dil, B_OUT_WIDTH), BF16))
    return pl.pallas_call(
        _proj_nat_kernel,
        grid=(b, nt),
        in_specs=[tok(d),
                  pl.BlockSpec((TM, LANES), lambda i, j: (i * nt + j, 0)),
                  pl.BlockSpec((TM, LANES), lambda i, j: (i * nt + j, 0)),
                  pl.BlockSpec((d, _N_NAT), lambda i, j: (0, 0))],
        out_specs=[pl.BlockSpec((1, A_WIDTH // LANES, TM, LANES), lambda i, j: (i, 0, j, 0)),
                   tok(2 * LANES)] + stream_specs,
        out_shape=[jax.ShapeDtypeStruct((b, A_WIDTH // LANES, s, LANES), BF16),
                   jax.ShapeDtypeStruct((b, s, 2 * LANES), BF16)] + stream_shapes,
        scratch_shapes=[pltpu.VMEM((B_WIDTH // LANES, TM, LANES), F32)],
        compiler_params=_params(("parallel", "parallel")),
        name="proj_token_major",
    )(x, cos, sin, w_nat)


def _proj_t_kernel(x_ref, ct_ref, st_ref, w_ref, qt_ref, vt_ref, qi_ref, wt_ref):
    xb = x_ref[0].astype(BF16)
    cos, sin = ct_ref[...], st_ref[...]

    def proj(lo, hi):
        return lax.dot_general(w_ref[lo:hi, :], xb, (((1,), (1,)), ((), ())), preferred_element_type=F32)

    def rope_head(y):
        x1, x2 = y[0:ROT_HALF], y[ROT_HALF:2 * ROT_HALF]
        return jnp.concatenate([x1 * cos - x2 * sin, x2 * cos + x1 * sin, y[2 * ROT_HALF:]], axis=0)

    y = proj(_T_QA, _T_QA + A_WIDTH)
    for h in range(A_HEADS):
        r = rope_head(y[h * HEAD_DIM:(h + 1) * HEAD_DIM])
        qt_ref[0, h * HEAD_DIM:(h + 1) * HEAD_DIM, :] = (r * (ATTN_SCALE * LOG2E)).astype(BF16)
    y = proj(_T_VA, _T_VA + A_WIDTH)
    for h in range(A_HEADS):
        for c in range(TM // TK):
            vt_ref[0, h, c, 0:HEAD_DIM, :] = y[h * HEAD_DIM:(h + 1) * HEAD_DIM, c * TK:(c + 1) * TK].astype(BF16)
            vt_ref[0, h, c, HEAD_DIM:V_ROWS, :] = jnp.ones((V_ROWS - HEAD_DIM, TK), BF16)
    y = proj(_T_QI, _T_QI + IDX_HEADS * IDX_DIM)
    for h in range(IDX_HEADS):
        r = rope_head(y[h * IDX_DIM:(h + 1) * IDX_DIM])
        hi = r.astype(BF16)
        lo = (r - hi.astype(F32)).astype(BF16)
        base = 4 * h * IDX_DIM
        qi_ref[0, base:base + IDX_DIM, :] = hi
        qi_ref[0, base + IDX_DIM:base + 2 * IDX_DIM, :] = hi
        qi_ref[0, base + 2 * IDX_DIM:base + 3 * IDX_DIM, :] = lo
        qi_ref[0, base + 3 * IDX_DIM:base + 4 * IDX_DIM, :] = lo
    wt_ref[0] = proj(_T_WI, _T_ROWS)[0:IDX_HEADS] * IDX_SCALE


def _proj_t(x, cos_t, sin_t, w_t):
    b, s, d = x.shape
    nt = s // TM
    return pl.pallas_call(
        _proj_t_kernel,
        grid=(b, nt),
        in_specs=[pl.BlockSpec((1, TM, d), lambda i, j: (i, j, 0)),
                  pl.BlockSpec((ROT_HALF, TM), lambda i, j: (0, i * nt + j)),
                  pl.BlockSpec((ROT_HALF, TM), lambda i, j: (0, i * nt + j)),
                  pl.BlockSpec((_T_ROWS, d), lambda i, j: (0, 0))],
        out_specs=[pl.BlockSpec((1, A_WIDTH, TM), lambda i, j: (i, 0, j)),
                   pl.BlockSpec((1, A_HEADS, TM // TK, V_ROWS, TK), lambda i, j: (i, 0, j, 0, 0)),
                   pl.BlockSpec((1, 4 * IDX_HEADS * IDX_DIM, TM), lambda i, j: (i, 0, j)),
                   pl.BlockSpec((1, IDX_HEADS, TM), lambda i, j: (i, 0, j))],
        out_shape=[jax.ShapeDtypeStruct((b, A_WIDTH, s), BF16),
                   jax.ShapeDtypeStruct((b, A_HEADS, s // TK, V_ROWS, TK), BF16),
                   jax.ShapeDtypeStruct((b, 4 * IDX_HEADS * IDX_DIM, s), BF16),
                   jax.ShapeDtypeStruct((b, IDX_HEADS, s), F32)],
        compiler_params=_params(("parallel", "parallel")),
        name="proj_feature_major",
    )(x, cos_t, sin_t, w_t)


def _dsa_kernel(qt_ref, qi_ref, wt_ref, k_ref, vt_ref, ki_ref, o_ref,
                sc_ref, qpad_ref, ot_ref, m_ref, st_ref, stat_ref, sa_ref, sb_ref):
    j = pl.program_id(1)
    nkt = j + 1

    for h in range(A_HEADS):
        off = (h % 2) * HEAD_DIM
        qpad_ref[h] = jnp.zeros((2 * HEAD_DIM, TQ), BF16)
        qpad_ref[h, off:off + HEAD_DIM, :] = qt_ref[0, h * HEAD_DIM:(h + 1) * HEAD_DIM, :]

    qpos = j * TQ + lax.broadcasted_iota(jnp.int32, (1, TQ), 1)
    row_iota = lax.broadcasted_iota(jnp.int32, (TK, TQ), 0)
    w = wt_ref[0]

    def fold(t):
        return t.reshape(TK // SUBLANES, SUBLANES, TQ)

    def score_tile(kt, diagonal, carry):
        mn, mx, pos, nn = carry
        ki = ki_ref[0, pl.ds(pl.multiple_of(kt * TK, TK), TK), :]
        acc = jnp.zeros((TK, TQ), F32)
        for h in range(IDX_HEADS):
            s = jnp.dot(ki, qi_ref[0, 4 * h * IDX_DIM:4 * (h + 1) * IDX_DIM, :], preferred_element_type=F32)
            acc = acc + w[h:h + 1, :] * jnp.maximum(s, 0.0)
        if diagonal:
            causal = (kt * TK + row_iota) <= qpos
            val = jnp.where(causal, acc, -jnp.inf)
            low = jnp.where(causal, acc, jnp.inf)
        else:
            val = low = acc
        sc_ref[kt] = val
        return (jnp.minimum(mn, fold(low).min(axis=0)),
                jnp.maximum(mx, fold(val).max(axis=0)),
                pos + fold(jnp.where(val > 0.0, 1.0, 0.0)).sum(axis=0),
                nn + fold(jnp.where(val >= 0.0, 1.0, 0.0)).sum(axis=0))

    def load_stats():
        return tuple(stat_ref[n] for n in range(4))

    def store_stats(stats):
        for n, v in enumerate(stats):
            stat_ref[n] = v

    def score_block(i, carry):
        for u in range(KEY_TILES_PER_STEP):
            carry = score_tile(KEY_TILES_PER_STEP * i + u, False, carry)
        return carry

    zeros8 = jnp.zeros((SUBLANES, TQ), F32)
    store_stats(lax.fori_loop(
        0, j // KEY_TILES_PER_STEP, score_block,
        (jnp.full((SUBLANES, TQ), jnp.inf, F32), jnp.full((SUBLANES, TQ), -jnp.inf, F32), zeros8, zeros8)))

    for left in range(KEY_TILES_PER_STEP):
        @pl.when(j % KEY_TILES_PER_STEP == left)
        def _(left=left):
            stats = load_stats()
            for u in range(left):
                stats = score_tile(j - left + u, False, stats)
            store_stats(score_tile(j, True, stats))

    mn8, mx8, pos8, nn8 = load_stats()
    mn = mn8.min(axis=0, keepdims=True)
    mx = mx8.max(axis=0, keepdims=True)
    c_pos = pos8.sum(axis=0, keepdims=True)
    c_nn = nn8.sum(axis=0, keepdims=True)

    sc_ref[nkt] = jnp.full((TK, TQ), -jnp.inf, F32)

    def count(preds):
        def part(p, kt):
            return fold(p(sc_ref[kt])).sum(axis=0)

        def body(i, cnts):
            return tuple(c + part(p, 2 * i) + part(p, 2 * i + 1) for c, p in zip(cnts, preds))

        cnts = lax.fori_loop(0, (nkt + 1) // 2, body, tuple(zeros8 for _ in preds))
        return [c.sum(axis=0, keepdims=True) for c in cnts]

    def ge(cand):
        return lambda t: jnp.where(t >= cand, 1.0, 0.0)

    n_causal = (qpos + 1).astype(F32)
    k_q = jnp.minimum(n_causal, float(TOPK_MAX))
    c_mx, = count([ge(mx)])
    select = n_causal > k_q
    at_max = select & (c_mx >= k_q)
    search = select & (c_mx < k_q)
    at_zero = search & (c_pos < k_q) & (c_nn >= k_q)
    above = search & (c_pos >= k_q)
    below = search & (c_nn < k_q)
    lo0 = jnp.where(at_max, mx, jnp.where(at_zero | above, 0.0, mn))
    c_lo0 = jnp.where(at_max, c_mx, jnp.where(at_zero | above, c_nn, n_causal))
    st_ref[0:1, :] = lo0
    st_ref[1:2, :] = jnp.where(below, 0.0, mx)
    st_ref[2:3, :] = c_lo0
    st_ref[3:4, :] = jnp.where(at_max, 0.0, jnp.where(at_zero, c_pos, jnp.where(below, c_nn, c_mx)))
    act0 = jnp.where((above | below) & (c_lo0 > k_q), 1.0, 0.0)
    st_ref[4:5, :] = act0

    def bisect_once():
        lo, hi, c_lo, c_hi = st_ref[0:1, :], st_ref[1:2, :], st_ref[2:3, :], st_ref[3:4, :]
        act = st_ref[4:5, :] > 0.0
        mid = 0.5 * lo + 0.5 * hi
        live = act & (mid > lo) & (mid < hi)
        c, = count([ge(mid)])
        up = live & (c >= k_q)
        dn = live & (c < k_q)
        c_lo = jnp.where(up, c, c_lo)
        st_ref[0:1, :] = jnp.where(up, mid, lo)
        st_ref[1:2, :] = jnp.where(dn, mid, hi)
        st_ref[2:3, :] = c_lo
        st_ref[3:4, :] = jnp.where(dn, c, c_hi)
        act_new = jnp.where(live & (c_lo > k_q), 1.0, 0.0)
        st_ref[4:5, :] = act_new
        return act_new

    def bisect(go):
        for _ in range(BISECT_STEPS_PER_CHECK - 1):
            bisect_once()
        return (jnp.max(bisect_once()) > 0.0).astype(jnp.int32)

    @pl.when(jnp.max(act0) > 0.0)
    def _():
        def blind(_, carry):
            bisect_once()
            bisect_once()
            return carry

        lax.fori_loop(0, BISECT_BLIND_STEPS // 2, blind, 0)

    lax.while_loop(lambda go: go > 0, bisect, (jnp.max(st_ref[4:5, :]) > 0.0).astype(jnp.int32))

    lo = st_ref[0:1, :]
    tie = st_ref[2:3, :] > k_q

    @pl.when(jnp.max(jnp.where(tie, 1.0, 0.0)) > 0.0)
    def _():
        need = jnp.where(tie, k_q - st_ref[3:4, :], jnp.inf)
        tri = jnp.where(lax.broadcasted_iota(jnp.int32, (TK, TK), 0) >= lax.broadcasted_iota(jnp.int32, (TK, TK), 1),
                        1.0, 0.0).astype(BF16)

        def drop(kt, seen):
            t = sc_ref[kt]
            eq = jnp.where(t == lo, 1.0, 0.0)
            rank = jnp.dot(tri, eq.astype(BF16), preferred_element_type=F32) + seen
            sc_ref[kt] = jnp.where(t == lo, jnp.where(rank > need, -jnp.inf, t), t)
            return seen + fold(eq).sum(axis=0).sum(axis=0, keepdims=True)

        def drop_pair(i, seen):
            return drop(2 * i + 1, drop(2 * i, seen))

        lax.fori_loop(0, (nkt + 1) // 2, drop_pair, jnp.zeros((1, TQ), F32))

    m_ref[...] = jnp.full(m_ref.shape, MASK_NEG, F32)
    ot_ref[...] = jnp.zeros(ot_ref.shape, F32)

    def mask_bias(kt):
        return jnp.where(sc_ref[kt] >= lo, 0.0, MASK_NEG).astype(BF16)

    def logits(h, kt, bias):
        rows = pl.ds(pl.multiple_of(kt * TK, TK), TK)
        return jnp.dot(k_ref[0, h // 2, rows, :], qpad_ref[h], preferred_element_type=F32).astype(BF16) + bias

    def half_step(kt, cur_ref, nxt_ref):
        if nxt_ref is not None:
            bias_n = mask_bias(kt + 1)
        for h in range(A_HEADS):
            s = cur_ref[h]
            if nxt_ref is not None:
                nxt_ref[h] = logits(h, kt + 1, bias_n)
            m = m_ref[h]
            m_tile = (s.reshape(TK // (2 * SUBLANES), 2 * SUBLANES, TQ).max(axis=0)
                      .astype(F32).max(axis=0, keepdims=True))
            m_new = jnp.maximum(m, m_tile)
            p = jnp.exp2(s - m_new.astype(BF16))
            corr = jnp.exp2(m - m_new)
            m_ref[h] = m_new
            ot_ref[h] = ot_ref[h] * corr + jnp.dot(vt_ref[0, h, kt], p, preferred_element_type=F32)

    bias0 = mask_bias(0)
    for h in range(A_HEADS):
        sa_ref[h] = logits(h, 0, bias0)

    def buffers(u):
        return (sa_ref, sb_ref) if u % 2 == 0 else (sb_ref, sa_ref)

    def kv_block(i, _):
        for u in range(KEY_TILES_PER_STEP):
            half_step(KEY_TILES_PER_STEP * i + u, *buffers(u))
        return 0

    n_blocks = (nkt - 1) // KEY_TILES_PER_STEP
    lax.fori_loop(0, n_blocks, kv_block, 0)

    for left in range(1, KEY_TILES_PER_STEP + 1):
        @pl.when(nkt - KEY_TILES_PER_STEP * n_blocks == left)
        def _(left=left):
            for u in range(left):
                cur_ref, nxt_ref = buffers(u)
                half_step(nkt - left + u, cur_ref, nxt_ref if u < left - 1 else None)

    for p in range(A_HEADS // 2):
        both = jnp.concatenate([ot_ref[h, 0:HEAD_DIM, :] / ot_ref[h, HEAD_DIM:HEAD_DIM + 1, :]
                                for h in (2 * p, 2 * p + 1)], axis=0)
        o_ref[0, :, p * LANES:(p + 1) * LANES] = both.T.astype(BF16)


def _dsa_attention(qt, qi4t, wt, k6, vt5, ki4):
    b, _, s = qt.shape
    nq = s // TQ
    return pl.pallas_call(
        _dsa_kernel,
        grid=(b, nq),
        in_specs=[pl.BlockSpec((1, A_WIDTH, TQ), lambda i, j: (i, 0, j)),
                  pl.BlockSpec((1, 4 * IDX_HEADS * IDX_DIM, TQ), lambda i, j: (i, 0, j)),
                  pl.BlockSpec((1, IDX_HEADS, TQ), lambda i, j: (i, 0, j)),
                  pl.BlockSpec((1, A_WIDTH // LANES, s, LANES), lambda i, j: (i, 0, 0, 0)),
                  pl.BlockSpec((1, A_HEADS, s // TK, V_ROWS, TK), lambda i, j: (i, 0, 0, 0, 0)),
                  pl.BlockSpec((1, s, 4 * IDX_DIM), lambda i, j: (i, 0, 0))],
        out_specs=pl.BlockSpec((1, TQ, A_WIDTH), lambda i, j: (i, j, 0)),
        out_shape=jax.ShapeDtypeStruct((b, s, A_WIDTH), BF16),
        scratch_shapes=[pltpu.VMEM((s // TK + 1, TK, TQ), F32),
                        pltpu.VMEM((A_HEADS, 2 * HEAD_DIM, TQ), BF16),
                        pltpu.VMEM((A_HEADS, V_ROWS, TQ), F32),
                        pltpu.VMEM((A_HEADS, 1, TQ), F32),
                        pltpu.VMEM((SUBLANES, TQ), F32),
                        pltpu.VMEM((4, SUBLANES, TQ), F32),
                        pltpu.VMEM((A_HEADS, TK, TQ), BF16),
                        pltpu.VMEM((A_HEADS, TK, TQ), BF16)],
        compiler_params=_params(("parallel", "arbitrary")),
        name="dsa_attention",
    )(qt, qi4t, wt, k6, vt5, ki4)


def _window_kernel(q_ref, kp_ref, kc_ref, vp_ref, vc_ref, o_ref, lse_ref):
    hb = B_WINDOW_STEPS
    n_streams, n_rows = q_ref.shape[1], q_ref.shape[2]
    r = lax.broadcasted_iota(jnp.int32, (hb, 2 * hb), 0)
    c = lax.broadcasted_iota(jnp.int32, (hb, 2 * hb), 1)
    dist = r + hb - c
    band = jnp.where(dist >= 0, jnp.where(dist <= B_WINDOW_STEPS, 0.0, NEG), NEG)
    first_col = jnp.where(pl.program_id(2) > 0, 0, hb)
    band_first = jnp.where(c >= first_col, band, NEG)
    lane = lax.broadcasted_iota(jnp.int32, (hb, LANES), 1)
    left = lane < HEAD_DIM
    ones = jnp.ones((2 * hb, LANES), BF16)
    for sb in range(n_streams):
        for t in range(n_rows // TB):
            for p in range(B_HEADS_PER_GROUP // 2):
                cols = slice(p * LANES, (p + 1) * LANES)
                tile = slice(t * TB, (t + 1) * TB)
                q2, kc, vc = q_ref[0, sb, tile, cols], kc_ref[0, sb, tile, cols], vc_ref[0, sb, tile, cols]
                if t == 0:
                    k_before, v_before = kp_ref[0, sb, :, cols], vp_ref[0, sb, :, cols]
                else:
                    before = slice(t * TB - hb, t * TB)
                    k_before, v_before = kc_ref[0, sb, before, cols], vc_ref[0, sb, before, cols]
                windows = ((jnp.concatenate([k_before, kc[:hb]], axis=0),
                            jnp.concatenate([v_before, vc[:hb]], axis=0)), (kc, vc))
                for half, ((kw, vw), bias) in enumerate(zip(windows, (band_first if t == 0 else band, band))):
                    rows = slice(t * TB + half * hb, t * TB + (half + 1) * hb)
                    v1 = jnp.concatenate([vw, ones], axis=1)
                    outs, lses = [], []
                    for side in (left, ~left):
                        qh = jnp.where(side, q2[half * hb:(half + 1) * hb], jnp.zeros((hb, LANES), BF16))
                        s = lax.dot_general(qh, kw, (((1,), (1,)), ((), ())), preferred_element_type=F32) + bias
                        m = s.max(axis=-1, keepdims=True)
                        e = jnp.exp2((s - m).astype(BF16))
                        ol = jnp.dot(e, v1, preferred_element_type=F32)
                        l = ol[:, LANES:]
                        outs.append(ol[:, :LANES] / l)
                        lses.append(m + jnp.log2(l))
                    o_ref[0, sb, rows, cols] = jnp.where(left, outs[0], outs[1])
                    lse_ref[0, sb, rows, cols] = jnp.where(left, lses[0], lses[1])


def _window_attention(q, k, v):
    b, d, n, _ = q.shape
    assert TB == 2 * B_WINDOW_STEPS
    rows = min(n, WINDOW_TILES_PER_STEP * TB)
    streams = min(d, WINDOW_TILES_PER_STEP * TB // rows)
    halves = rows // (TB // 2)
    cur = pl.BlockSpec((1, streams, rows, B_OUT_WIDTH), lambda bi, ri, ti: (bi, ri, ti, 0))
    prev = pl.BlockSpec((1, streams, TB // 2, B_OUT_WIDTH),
                        lambda bi, ri, ti: (bi, ri, jnp.maximum(halves * ti - 1, 0), 0))
    return pl.pallas_call(
        _window_kernel,
        grid=(b, d // streams, n // rows),
        in_specs=[cur, prev, cur, prev, cur],
        out_specs=[cur, cur],
        out_shape=[jax.ShapeDtypeStruct((b, d, n, B_OUT_WIDTH), F32)] * 2,
        compiler_params=_params(("parallel", "parallel", "arbitrary")),
        name=f"window_attention_d{d}",
    )(q, k, k, v, v)


def _layer_norm(y, g, b):
    mu = y.mean(axis=-1, keepdims=True)
    yc = y - mu
    var = (yc * yc).mean(axis=-1, keepdims=True)
    return yc * lax.rsqrt(var + LN_EPS) * g + b


def _merge_kernel(alpha, oa_ref, o0_ref, o1_ref, o2_ref, l0_ref, l1_ref, l2_ref, x_ref,
                  wg_ref, bg_ref, wa_ref, wb_ref, wo_ref, g_ref, b_ref, y_ref,
                  tok_ref, ob_ref, xb_ref, mg_ref):
    halves = B_OUT_WIDTH // LANES

    for n, ref in enumerate((o1_ref, o2_ref, l1_ref, l2_ref)):
        d = ref.shape[1]
        for r in range(d):
            for p in range(halves):
                tok_ref[n, p, pl.ds(r, TM_MERGE // d, stride=d), :] = ref[0, r, :, p * LANES:(p + 1) * LANES]
    for c in range(TM_MERGE // MERGE_ROWS):
        rows = slice(c * MERGE_ROWS, (c + 1) * MERGE_ROWS)
        for p in range(halves):
            cols = slice(p * LANES, (p + 1) * LANES)
            l0, l1, l2 = l0_ref[0, 0, rows, cols], tok_ref[2, p, rows, :], tok_ref[3, p, rows, :]
            lm = jnp.maximum(jnp.maximum(l0, l1), l2)
            e0, e1, e2 = jnp.exp2(l0 - lm), jnp.exp2(l1 - lm), jnp.exp2(l2 - lm)
            ob = (e0 * o0_ref[0, 0, rows, cols] + e1 * tok_ref[0, p, rows, :] + e2 * tok_ref[1, p, rows, :])
            ob_ref[rows, cols] = (ob / (e0 + e1 + e2)).astype(BF16)

    xb_ref[...] = x_ref[0].astype(BF16)
    for c in range(D_MODEL // MERGE_COLS):
        cols = slice(c * MERGE_COLS, (c + 1) * MERGE_COLS)
        gcols = slice(D_MODEL + c * MERGE_COLS, D_MODEL + (c + 1) * MERGE_COLS)

        def gate(sel):
            z = jnp.dot(xb_ref[...], wg_ref[:, sel], preferred_element_type=F32) + bg_ref[:, sel]
            return 1.0 / (1.0 + jnp.exp(-z))

        pa = jnp.dot(oa_ref[0], wa_ref[:, cols], preferred_element_type=F32)
        pb = jnp.dot(ob_ref[...], wb_ref[:, cols], preferred_element_type=F32)
        mg_ref[:, cols] = (gate(cols) * pa + gate(gcols) * pb).astype(BF16)

    for c in range(TM_MERGE // MERGE_ROWS):
        rows = slice(c * MERGE_ROWS, (c + 1) * MERGE_ROWS)
        mixed = jnp.dot(mg_ref[rows, :], wo_ref[...], preferred_element_type=F32)
        y_ref[0, rows, :] = _layer_norm(alpha * x_ref[0, rows, :] + mixed, g_ref[...], b_ref[...])


def _merge(alpha, oa, obs, lses, x, w_gate, b_gate, wa, wb, wo, g, bb):
    b, s, _ = x.shape
    tm = TM_MERGE
    tok = lambda width: pl.BlockSpec((1, tm, width), lambda i, j: (i, j, 0))
    full = lambda a: pl.BlockSpec(a.shape, lambda i, j: (0, 0), pipeline_mode=pl.Buffered(1))
    streams = [pl.BlockSpec((1, a.shape[1], tm // a.shape[1], B_OUT_WIDTH), lambda i, j: (i, 0, j, 0))
               for a in list(obs) + list(lses)]
    return pl.pallas_call(
        functools.partial(_merge_kernel, alpha),
        grid=(b, s // tm),
        in_specs=[tok(A_WIDTH)] + streams + [tok(D_MODEL), full(w_gate),
                  full(b_gate), full(wa), full(wb), full(wo), full(g), full(bb)],
        out_specs=tok(D_MODEL),
        out_shape=jax.ShapeDtypeStruct((b, s, D_MODEL), F32),
        scratch_shapes=[pltpu.VMEM((4, B_OUT_WIDTH // LANES, tm, LANES), F32),
                        pltpu.VMEM((tm, B_OUT_WIDTH), BF16),
                        pltpu.VMEM((tm, D_MODEL), BF16),
                        pltpu.VMEM((tm, D_MODEL), BF16)],
        compiler_params=_params(("parallel", "parallel")),
        name="merge_outproj_norm",
    )(oa, *obs, *lses, x, w_gate, b_gate, wa, wb, wo, g, bb)


def _ffn_kernel(alpha, x_ref, wg_ref, wu_ref, wd_ref, g_ref, b_ref, y_ref):
    for r in range(TM_FFN // TM):
        rows = slice(r * TM, (r + 1) * TM)
        x = x_ref[rows, :]
        xb = x.astype(BF16)
        acc = jnp.zeros((TM, D_MODEL), F32)
        for c in range(FFN_HIDDEN // FFN_CHUNK):
            cols = slice(c * FFN_CHUNK, (c + 1) * FFN_CHUNK)
            gate = jnp.dot(xb, wg_ref[:, cols], preferred_element_type=F32)
            up = jnp.dot(xb, wu_ref[:, cols], preferred_element_type=F32)
            h = gate / (1.0 + jnp.exp(-gate)) * up
            acc = acc + jnp.dot(h.astype(BF16), wd_ref[cols, :], preferred_element_type=F32)
        y_ref[rows, :] = _layer_norm(alpha * x + acc, g_ref[...], b_ref[...])


def _ffn(alpha, x2, wg, wu, wd, g, bb):
    m = x2.shape[0]
    tok = pl.BlockSpec((TM_FFN, D_MODEL), lambda i: (i, 0))
    full = lambda a: pl.BlockSpec(a.shape, lambda i: (0, 0), pipeline_mode=pl.Buffered(1))
    return pl.pallas_call(
        functools.partial(_ffn_kernel, alpha),
        grid=(m // TM_FFN,),
        in_specs=[tok, full(wg), full(wu), full(wd), full(g), full(bb)],
        out_specs=tok,
        out_shape=jax.ShapeDtypeStruct((m, D_MODEL), F32),
        compiler_params=_params(("parallel",)),
        name="swiglu_norm",
    )(x2, wg, wu, wd, g, bb)


def _split_w_in(w):
    a, bw, hi = A_WIDTH, B_WIDTH, IDX_HEADS * IDX_DIM
    o = 0
    qa, o = w[:, o:o + a], o + a
    ka, o = w[:, o:o + a], o + a
    va, o = w[:, o:o + a], o + a
    qb, o = w[:, o:o + bw], o + bw
    kb, o = w[:, o:o + bw], o + bw
    vb, o = w[:, o:o + bw], o + bw
    qi, o = w[:, o:o + hi], o + hi
    ki, o = w[:, o:o + IDX_DIM], o + IDX_DIM
    wi, o = w[:, o:o + IDX_HEADS], o + IDX_HEADS
    w_gate = w[:, o:].astype(BF16)
    w_nat = jnp.concatenate([ka, qb, kb, vb, ki, ki], axis=1).astype(BF16)
    pad = jnp.zeros((w.shape[0], _T_ROWS - _T_WI - IDX_HEADS), w.dtype)
    w_t = jnp.concatenate([qa, va, qi, wi, pad], axis=1).T.astype(BF16)
    return w_nat, w_t, w_gate


def kernel(x, positions, w_in, b_gate, w_branch_a, w_branch_b, w_out, ln1_g, ln1_b,
           w_ffn_gate, w_ffn_up, w_ffn_down, ln2_g, ln2_b):
    b, s, d = x.shape
    depth = w_in.shape[0]
    assert d == D_MODEL and s % (max(B_DILATIONS) * TB) == 0 and s % max(TM, TM_MERGE) == 0
    assert (b * s) % TM_FFN == 0 and B_DILATIONS[0] == 1
    alpha = (2 * depth) ** 0.25
    cos, sin, cos_t, sin_t = _rope_tables(positions)
    row = lambda v: v.reshape(1, -1)
    for layer in range(depth):
        w_nat, w_t, w_gate = _split_w_in(w_in[layer])
        ka6, ki4, *streams = _proj_nat(x, cos, sin, w_nat)
        qt, vt5, qi4t, wt = _proj_t(x, cos_t, sin_t, w_t)
        oa = _dsa_attention(qt, qi4t, wt, ka6, vt5, ki4)
        groups = [_window_attention(*streams[3 * g:3 * g + 3]) for g in range(len(B_DILATIONS))]
        x1 = _merge(alpha, oa, [o for o, _ in groups], [l for _, l in groups], x, w_gate, row(b_gate[layer]),
                    w_branch_a[layer].astype(BF16), w_branch_b[layer].astype(BF16),
                    w_out[layer].astype(BF16), row(ln1_g[layer]), row(ln1_b[layer]))
        x2 = _ffn(alpha, x1.reshape(b * s, d), w_ffn_gate[layer].astype(BF16), w_ffn_up[layer].astype(BF16),
                  w_ffn_down[layer].astype(BF16), row(ln2_g[layer]), row(ln2_b[layer]))
        x = x2.reshape(b, s, d)
    return x
```

```python
import functools

import jax
import jax.numpy as jnp
from jax import lax
from jax.experimental import pallas as pl
from jax.experimental.pallas import tpu as pltpu

F32 = jnp.float32
BF16 = jnp.bfloat16

D_MODEL = 1024
HEAD_DIM = 64
ROT_HALF = 8
ROPE_THETA = 500000.0
ATTN_SCALE = HEAD_DIM ** -0.5
LOG2E = 1.4426950408889634
A_HEADS = 12
A_WIDTH = A_HEADS * HEAD_DIM
IDX_HEADS = 8
IDX_DIM = 64
IDX_SCALE = (IDX_HEADS ** -0.5) * (IDX_DIM ** -0.5)
TOPK_MAX = 256
B_DILATIONS = (1, 4, 16)
B_WINDOW_STEPS = 128
B_HEADS_PER_GROUP = 4
B_WIDTH = 3 * B_HEADS_PER_GROUP * HEAD_DIM
B_OUT_WIDTH = B_HEADS_PER_GROUP * HEAD_DIM
FFN_HIDDEN = 2816
LN_EPS = 1e-5
NEG = -1e30
MASK_NEG = -(2.0 ** 100)

LANES = 128
SUBLANES = 8
V_ROWS = HEAD_DIM + 2 * SUBLANES
VMEM_LIMIT = 56 * 1024 * 1024
TM = 512
TM_FFN = 1024
TM_MERGE = 1024
TQ = 256
TK = 256
TB = 256
FFN_CHUNK = 256
KEY_TILES_PER_STEP = 6
BISECT_STEPS_PER_CHECK = 2
BISECT_BLIND_STEPS = 16
WINDOW_TILES_PER_STEP = 16
MERGE_ROWS = 256
MERGE_COLS = 256
ROPE_TILE = 2048

_N_KA, _N_QB, _N_KB, _N_VB = 0, A_WIDTH, A_WIDTH + B_WIDTH, A_WIDTH + 2 * B_WIDTH
_N_KI = A_WIDTH + 3 * B_WIDTH
_N_NAT = _N_KI + LANES
_T_QA, _T_VA, _T_QI = 0, A_WIDTH, 2 * A_WIDTH
_T_WI = _T_QI + IDX_HEADS * IDX_DIM
_T_ROWS = _T_WI + 2 * SUBLANES


def _params(sem):
    return pltpu.CompilerParams(dimension_semantics=sem, vmem_limit_bytes=VMEM_LIMIT)


def _rope_kernel(pr_ref, fc_ref, c_ref, s_ref, ct_ref, st_ref):
    ang_t = fc_ref[...] * pr_ref[...].astype(F32)
    cos_t, sin_t = jnp.cos(ang_t), jnp.sin(ang_t)
    ct_ref[...] = cos_t
    st_ref[...] = sin_t
    reps = LANES // ROT_HALF
    cos = jnp.concatenate([cos_t] * reps, axis=0).T
    sin = jnp.concatenate([sin_t] * reps, axis=0).T
    d = lax.broadcasted_iota(jnp.int32, cos.shape, 1) & (HEAD_DIM - 1)
    c_ref[...] = jnp.where(d < 2 * ROT_HALF, cos, 1.0)
    s_ref[...] = jnp.where(d < ROT_HALF, -sin, jnp.where(d < 2 * ROT_HALF, sin, 0.0))


def _rope_tables(positions):
    m = positions.size
    inv_freq = ROPE_THETA ** (-jnp.arange(0, 2 * ROT_HALF, 2, dtype=F32) / (2 * ROT_HALF))
    t = min(ROPE_TILE, m)
    return pl.pallas_call(
        _rope_kernel,
        grid=(m // t,),
        in_specs=[pl.BlockSpec((1, t), lambda i: (0, i)), pl.BlockSpec((ROT_HALF, 1), lambda i: (0, 0))],
        out_specs=[pl.BlockSpec((t, LANES), lambda i: (i, 0)), pl.BlockSpec((t, LANES), lambda i: (i, 0)),
                   pl.BlockSpec((ROT_HALF, t), lambda i: (0, i)), pl.BlockSpec((ROT_HALF, t), lambda i: (0, i))],
        out_shape=[jax.ShapeDtypeStruct((m, LANES), F32), jax.ShapeDtypeStruct((m, LANES), F32),
                   jax.ShapeDtypeStruct((ROT_HALF, m), F32), jax.ShapeDtypeStruct((ROT_HALF, m), F32)],
        compiler_params=_params(("parallel",)),
        name="rope_tables",
    )(positions.reshape(1, m), inv_freq.reshape(ROT_HALF, 1))


def _proj_nat_kernel(x_ref, c_ref, s_ref, w_ref, ka_ref, ki_ref, *rest):
    streams, y_ref = rest[:-1], rest[-1]
    xb = x_ref[0].astype(BF16)
    cos, sin = c_ref[...], s_ref[...]
    lane = lax.broadcasted_iota(jnp.int32, cos.shape, 1)
    first = (lane & (HEAD_DIM - 1)) < ROT_HALF

    def rope(y):
        partner = jnp.where(first, pltpu.roll(y, LANES - ROT_HALF, 1), pltpu.roll(y, ROT_HALF, 1))
        return y * cos + partner * sin

    def proj(lo, hi):
        return jnp.dot(xb, w_ref[:, lo:hi], preferred_element_type=F32)

    blocks = A_WIDTH // LANES
    y = proj(_N_KA, _N_KA + A_WIDTH)
    for p in range(blocks):
        ka_ref[0, p] = rope(y[:, p * LANES:(p + 1) * LANES]).astype(BF16)

    def scatter_streams(which):
        per_group = B_OUT_WIDTH // LANES
        for g, d in enumerate(B_DILATIONS):
            out = streams[3 * g + which]
            for r in range(d):
                for p in range(per_group):
                    rows = y_ref[g * per_group + p, pl.ds(r, TM // d, stride=d), :]
                    out[0, r, :, p * LANES:(p + 1) * LANES] = rows.astype(BF16)

    y = proj(_N_QB, _N_QB + B_WIDTH)
    for p in range(blocks):
        y_ref[p] = rope(y[:, p * LANES:(p + 1) * LANES]) * (ATTN_SCALE * LOG2E)
    scatter_streams(0)
    y = proj(_N_KB, _N_KB + B_WIDTH)
    for p in range(blocks):
        y_ref[p] = rope(y[:, p * LANES:(p + 1) * LANES])
    scatter_streams(1)
    y = proj(_N_VB, _N_VB + B_WIDTH)
    for p in range(blocks):
        y_ref[p] = y[:, p * LANES:(p + 1) * LANES]
    scatter_streams(2)
    r = rope(proj(_N_KI, _N_KI + LANES))
    hi = r.astype(BF16).astype(F32)
    hl = jnp.where(lane < IDX_DIM, hi, r - hi).astype(BF16)
    ki_ref[0, :, 0:LANES] = hl
    ki_ref[0, :, LANES:2 * LANES] = hl


def _proj_nat(x, cos, sin, w_nat):
    b, s, d = x.shape
    nt = s // TM
    tok = lambda width: pl.BlockSpec((1, TM, width), lambda i, j: (i, j, 0))
    stream_specs, stream_shapes = [], []
    for dil in B_DILATIONS:
        for _ in range(3):
            stream_specs.append(pl.BlockSpec((1, dil, TM // dil, B_OUT_WIDTH), lambda i, j: (i, 0, j, 0)))
            stream_shapes.append(jax.ShapeDtypeStruct((b, dil, s // dil, B_OUT_WIDTH), BF16))
    return pl.pallas_call(
        _proj_nat_kernel,
        grid=(b, nt),
        in_specs=[tok(d),
                  pl.BlockSpec((TM, LANES), lambda i, j: (i * nt + j, 0)),
                  pl.BlockSpec((TM, LANES), lambda i, j: (i * nt + j, 0)),
                  pl.BlockSpec((d, _N_NAT), lambda i, j: (0, 0))],
        out_specs=[pl.BlockSpec((1, A_WIDTH // LANES, TM, LANES), lambda i, j: (i, 0, j, 0)),
                   tok(2 * LANES)] + stream_specs,
        out_shape=[jax.ShapeDtypeStruct((b, A_WIDTH // LANES, s, LANES), BF16),
                   jax.ShapeDtypeStruct((b, s, 2 * LANES), BF16)] + stream_shapes,
        scratch_shapes=[pltpu.VMEM((B_WIDTH // LANES, TM, LANES), F32)],
        compiler_params=_params(("parallel", "parallel")),
        name="proj_token_major",
    )(x, cos, sin, w_nat)


def _proj_t_kernel(x_ref, ct_ref, st_ref, w_ref, qt_ref, vt_ref, qi_ref, wt_ref):
    xb = x_ref[0].astype(BF16)
    cos, sin = ct_ref[...], st_ref[...]

    def proj(lo, hi):
        return lax.dot_general(w_ref[lo:hi, :], xb, (((1,), (1,)), ((), ())), preferred_element_type=F32)

    def rope_head(y):
        x1, x2 = y[0:ROT_HALF], y[ROT_HALF:2 * ROT_HALF]
        return jnp.concatenate([x1 * cos - x2 * sin, x2 * cos + x1 * sin, y[2 * ROT_HALF:]], axis=0)

    y = proj(_T_QA, _T_QA + A_WIDTH)
    for h in range(A_HEADS):
        r = rope_head(y[h * HEAD_DIM:(h + 1) * HEAD_DIM])
        qt_ref[0, h * HEAD_DIM:(h + 1) * HEAD_DIM, :] = (r * (ATTN_SCALE * LOG2E)).astype(BF16)
    y = proj(_T_VA, _T_VA + A_WIDTH)
    for h in range(A_HEADS):
        for c in range(TM // TK):
            vt_ref[0, h, c, 0:HEAD_DIM, :] = y[h * HEAD_DIM:(h + 1) * HEAD_DIM, c * TK:(c + 1) * TK].astype(BF16)
            vt_ref[0, h, c, HEAD_DIM:V_ROWS, :] = jnp.ones((V_ROWS - HEAD_DIM, TK), BF16)
    y = proj(_T_QI, _T_QI + IDX_HEADS * IDX_DIM)
    for h in range(IDX_HEADS):
        r = rope_head(y[h * IDX_DIM:(h + 1) * IDX_DIM])
        hi = r.astype(BF16)
        lo = (r - hi.astype(F32)).astype(BF16)
        base = 4 * h * IDX_DIM
        qi_ref[0, base:base + IDX_DIM, :] = hi
        qi_ref[0, base + IDX_DIM:base + 2 * IDX_DIM, :] = hi
        qi_ref[0, base + 2 * IDX_DIM:base + 3 * IDX_DIM, :] = lo
        qi_ref[0, base + 3 * IDX_DIM:base + 4 * IDX_DIM, :] = lo
    wt_ref[0] = proj(_T_WI, _T_ROWS)[0:IDX_HEADS] * IDX_SCALE


def _proj_t(x, cos_t, sin_t, w_t):
    b, s, d = x.shape
    nt = s // TM
    return pl.pallas_call(
        _proj_t_kernel,
        grid=(b, nt),
        in_specs=[pl.BlockSpec((1, TM, d), lambda i, j: (i, j, 0)),
                  pl.BlockSpec((ROT_HALF, TM), lambda i, j: (0, i * nt + j)),
                  pl.BlockSpec((ROT_HALF, TM), lambda i, j: (0, i * nt + j)),
                  pl.BlockSpec((_T_ROWS, d), lambda i, j: (0, 0))],
        out_specs=[pl.BlockSpec((1, A_WIDTH, TM), lambda i, j: (i, 0, j)),
                   pl.BlockSpec((1, A_HEADS, TM // TK, V_ROWS, TK), lambda i, j: (i, 0, j, 0, 0)),
                   pl.BlockSpec((1, 4 * IDX_HEADS * IDX_DIM, TM), lambda i, j: (i, 0, j)),
                   pl.BlockSpec((1, IDX_HEADS, TM), lambda i, j: (i, 0, j))],
        out_shape=[jax.ShapeDtypeStruct((b, A_WIDTH, s), BF16),
                   jax.ShapeDtypeStruct((b, A_HEADS, s // TK, V_ROWS, TK), BF16),
                   jax.ShapeDtypeStruct((b, 4 * IDX_HEADS * IDX_DIM, s), BF16),
                   jax.ShapeDtypeStruct((b, IDX_HEADS, s), F32)],
        compiler_params=_params(("parallel", "parallel")),
        name="proj_feature_major",
    )(x, cos_t, sin_t, w_t)


def _dsa_kernel(qt_ref, qi_ref, wt_ref, k_ref, vt_ref, ki_ref, o_ref,
                sc_ref, qpad_ref, ot_ref, m_ref, st_ref, stat_ref, sa_ref, sb_ref):
    j = pl.program_id(1)
    nkt = j + 1

    for h in range(A_HEADS):
        off = (h % 2) * HEAD_DIM
        qpad_ref[h] = jnp.zeros((2 * HEAD_DIM, TQ), BF16)
        qpad_ref[h, off:off + HEAD_DIM, :] = qt_ref[0, h * HEAD_DIM:(h + 1) * HEAD_DIM, :]

    qpos = j * TQ + lax.broadcasted_iota(jnp.int32, (1, TQ), 1)
    row_iota = lax.broadcasted_iota(jnp.int32, (TK, TQ), 0)
    w = wt_ref[0]

    def fold(t):
        return t.reshape(TK // SUBLANES, SUBLANES, TQ)

    def score_tile(kt, diagonal, carry):
        mn, mx, pos, nn = carry
        ki = ki_ref[0, pl.ds(pl.multiple_of(kt * TK, TK), TK), :]
        acc = jnp.zeros((TK, TQ), F32)
        for h in range(IDX_HEADS):
            s = jnp.dot(ki, qi_ref[0, 4 * h * IDX_DIM:4 * (h + 1) * IDX_DIM, :], preferred_element_type=F32)
            acc = acc + w[h:h + 1, :] * jnp.maximum(s, 0.0)
        if diagonal:
            causal = (kt * TK + row_iota) <= qpos
            val = jnp.where(causal, acc, -jnp.inf)
            low = jnp.where(causal, acc, jnp.inf)
        else:
            val = low = acc
        sc_ref[kt] = val
        return (jnp.minimum(mn, fold(low).min(axis=0)),
                jnp.maximum(mx, fold(val).max(axis=0)),
                pos + fold(jnp.where(val > 0.0, 1.0, 0.0)).sum(axis=0),
                nn + fold(jnp.where(val >= 0.0, 1.0, 0.0)).sum(axis=0))

    def load_stats():
        return tuple(stat_ref[n] for n in range(4))

    def store_stats(stats):
        for n, v in enumerate(stats):
            stat_ref[n] = v

    def score_block(i, carry):
        for u in range(KEY_TILES_PER_STEP):
            carry = score_tile(KEY_TILES_PER_STEP * i + u, False, carry)
        return carry

    zeros8 = jnp.zeros((SUBLANES, TQ), F32)
    store_stats(lax.fori_loop(
        0, j // KEY_TILES_PER_STEP, score_block,
        (jnp.full((SUBLANES, TQ), jnp.inf, F32), jnp.full((SUBLANES, TQ), -jnp.inf, F32), zeros8, zeros8)))

    for left in range(KEY_TILES_PER_STEP):
        @pl.when(j % KEY_TILES_PER_STEP == left)
        def _(left=left):
            stats = load_stats()
            for u in range(left):
                stats = score_tile(j - left + u, False, stats)
            store_stats(score_tile(j, True, stats))

    mn8, mx8, pos8, nn8 = load_stats()
    mn = mn8.min(axis=0, keepdims=True)
    mx = mx8.max(axis=0, keepdims=True)
    c_pos = pos8.sum(axis=0, keepdims=True)
    c_nn = nn8.sum(axis=0, keepdims=True)

    sc_ref[nkt] = jnp.full((TK, TQ), -jnp.inf, F32)

    def count(preds):
        def part(p, kt):
            return fold(p(sc_ref[kt])).sum(axis=0)

        def body(i, cnts):
            return tuple(c + part(p, 2 * i) + part(p, 2 * i + 1) for c, p in zip(cnts, preds))

        cnts = lax.fori_loop(0, (nkt + 1) // 2, body, tuple(zeros8 for _ in preds))
        return [c.sum(axis=0, keepdims=True) for c in cnts]

    def ge(cand):
        return lambda t: jnp.where(t >= cand, 1.0, 0.0)

    n_causal = (qpos + 1).astype(F32)
    k_q = jnp.minimum(n_causal, float(TOPK_MAX))
    c_mx, = count([ge(mx)])
    select = n_causal > k_q
    at_max = select & (c_mx >= k_q)
    search = select & (c_mx < k_q)
    at_zero = search & (c_pos < k_q) & (c_nn >= k_q)
    above = search & (c_pos >= k_q)
    below = search & (c_nn < k_q)
    lo0 = jnp.where(at_max, mx, jnp.where(at_zero | above, 0.0, mn))
    c_lo0 = jnp.where(at_max, c_mx, jnp.where(at_zero | above, c_nn, n_causal))
    st_ref[0:1, :] = lo0
    st_ref[1:2, :] = jnp.where(below, 0.0, mx)
    st_ref[2:3, :] = c_lo0
    st_ref[3:4, :] = jnp.where(at_max, 0.0, jnp.where(at_zero, c_pos, jnp.where(below, c_nn, c_mx)))
    act0 = jnp.where((above | below) & (c_lo0 > k_q), 1.0, 0.0)
    st_ref[4:5, :] = act0

    def bisect_once():
        lo, hi, c_lo, c_hi = st_ref[0:1, :], st_ref[1:2, :], st_ref[2:3, :], st_ref[3:4, :]
        act = st_ref[4:5, :] > 0.0
        mid = 0.5 * lo + 0.5 * hi
        live = act & (mid > lo) & (mid < hi)
        c, = count([ge(mid)])
        up = live & (c >= k_q)
        dn = live & (c < k_q)
        c_lo = jnp.where(up, c, c_lo)
        st_ref[0:1, :] = jnp.where(up, mid, lo)
        st_ref[1:2, :] = jnp.where(dn, mid, hi)
        st_ref[2:3, :] = c_lo
        st_ref[3:4, :] = jnp.where(dn, c, c_hi)
        act_new = jnp.where(live & (c_lo > k_q), 1.0, 0.0)
        st_ref[4:5, :] = act_new
        return act_new

    def bisect(go):
        for _ in range(BISECT_STEPS_PER_CHECK - 1):
            bisect_once()
        return (jnp.max(bisect_once()) > 0.0).astype(jnp.int32)

    @pl.when(jnp.max(act0) > 0.0)
    def _():
        def blind(_, carry):
            bisect_once()
            bisect_once()
            return carry

        lax.fori_loop(0, BISECT_BLIND_STEPS // 2, blind, 0)

    lax.while_loop(lambda go: go > 0, bisect, (jnp.max(st_ref[4:5, :]) > 0.0).astype(jnp.int32))

    lo = st_ref[0:1, :]
    tie = st_ref[2:3, :] > k_q

    @pl.when(jnp.max(jnp.where(tie, 1.0, 0.0)) > 0.0)
    def _():
        need = jnp.where(tie, k_q - st_ref[3:4, :], jnp.inf)
        tri = jnp.where(lax.broadcasted_iota(jnp.int32, (TK, TK), 0) >= lax.broadcasted_iota(jnp.int32, (TK, TK), 1),
                        1.0, 0.0).astype(BF16)

        def drop(kt, seen):
            t = sc_ref[kt]
            eq = jnp.where(t == lo, 1.0, 0.0)
            rank = jnp.dot(tri, eq.astype(BF16), preferred_element_type=F32) + seen
            sc_ref[kt] = jnp.where(t == lo, jnp.where(rank > need, -jnp.inf, t), t)
            return seen + fold(eq).sum(axis=0).sum(axis=0, keepdims=True)

        def drop_pair(i, seen):
            return drop(2 * i + 1, drop(2 * i, seen))

        lax.fori_loop(0, (nkt + 1) // 2, drop_pair, jnp.zeros((1, TQ), F32))

    m_ref[...] = jnp.full(m_ref.shape, MASK_NEG, F32)
    ot_ref[...] = jnp.zeros(ot_ref.shape, F32)

    def mask_bias(kt):
        return jnp.where(sc_ref[kt] >= lo, 0.0, MASK_NEG).astype(BF16)

    def logits(h, kt, bias):
        rows = pl.ds(pl.multiple_of(kt * TK, TK), TK)
        return jnp.dot(k_ref[0, h // 2, rows, :], qpad_ref[h], preferred_element_type=F32).astype(BF16) + bias

    def half_step(kt, cur_ref, nxt_ref):
        if nxt_ref is not None:
            bias_n = mask_bias(kt + 1)
        for h in range(A_HEADS):
            s = cur_ref[h]
            if nxt_ref is not None:
                nxt_ref[h] = logits(h, kt + 1, bias_n)
            m = m_ref[h]
            m_tile = (s.reshape(TK // (2 * SUBLANES), 2 * SUBLANES, TQ).max(axis=0)
                      .astype(F32).max(axis=0, keepdims=True))
            m_new = jnp.maximum(m, m_tile)
            p = jnp.exp2(s - m_new.astype(BF16))
            corr = jnp.exp2(m - m_new)
            m_ref[h] = m_new
            ot_ref[h] = ot_ref[h] * corr + jnp.dot(vt_ref[0, h, kt], p, preferred_element_type=F32)

    bias0 = mask_bias(0)
    for h in range(A_HEADS):
        sa_ref[h] = logits(h, 0, bias0)

    def buffers(u):
        return (sa_ref, sb_ref) if u % 2 == 0 else (sb_ref, sa_ref)

    def kv_block(i, _):
        for u in range(KEY_TILES_PER_STEP):
            half_step(KEY_TILES_PER_STEP * i + u, *buffers(u))
        return 0

    n_blocks = (nkt - 1) // KEY_TILES_PER_STEP
    lax.fori_loop(0, n_blocks, kv_block, 0)

    for left in range(1, KEY_TILES_PER_STEP + 1):
        @pl.when(nkt - KEY_TILES_PER_STEP * n_blocks == left)
        def _(left=left):
            for u in range(left):
                cur_ref, nxt_ref = buffers(u)
                half_step(nkt - left + u, cur_ref, nxt_ref if u < left - 1 else None)

    for p in range(A_HEADS // 2):
        both = jnp.concatenate([ot_ref[h, 0:HEAD_DIM, :] / ot_ref[h, HEAD_DIM:HEAD_DIM + 1, :]
                                for h in (2 * p, 2 * p + 1)], axis=0)
        o_ref[0, :, p * LANES:(p + 1) * LANES] = both.T.astype(BF16)


def _dsa_attention(qt, qi4t, wt, k6, vt5, ki4):
    b, _, s = qt.shape
    nq = s // TQ
    return pl.pallas_call(
        _dsa_kernel,
        grid=(b, nq),
        in_specs=[pl.BlockSpec((1, A_WIDTH, TQ), lambda i, j: (i, 0, j)),
                  pl.BlockSpec((1, 4 * IDX_HEADS * IDX_DIM, TQ), lambda i, j: (i, 0, j)),
                  pl.BlockSpec((1, IDX_HEADS, TQ), lambda i, j: (i, 0, j)),
                  pl.BlockSpec((1, A_WIDTH // LANES, s, LANES), lambda i, j: (i, 0, 0, 0)),
                  pl.BlockSpec((1, A_HEADS, s // TK, V_ROWS, TK), lambda i, j: (i, 0, 0, 0, 0)),
                  pl.BlockSpec((1, s, 4 * IDX_DIM), lambda i, j: (i, 0, 0))],
        out_specs=pl.BlockSpec((1, TQ, A_WIDTH), lambda i, j: (i, j, 0)),
        out_shape=jax.ShapeDtypeStruct((b, s, A_WIDTH), BF16),
        scratch_shapes=[pltpu.VMEM((s // TK + 1, TK, TQ), F32),
                        pltpu.VMEM((A_HEADS, 2 * HEAD_DIM, TQ), BF16),
                        pltpu.VMEM((A_HEADS, V_ROWS, TQ), F32),
                        pltpu.VMEM((A_HEADS, 1, TQ), F32),
                        pltpu.VMEM((SUBLANES, TQ), F32),
                        pltpu.VMEM((4, SUBLANES, TQ), F32),
                        pltpu.VMEM((A_HEADS, TK, TQ), BF16),
                        pltpu.VMEM((A_HEADS, TK, TQ), BF16)],
        compiler_params=_params(("parallel", "arbitrary")),
        name="dsa_attention",
    )(qt, qi4t, wt, k6, vt5, ki4)


def _window_kernel(q_ref, kp_ref, kc_ref, vp_ref, vc_ref, o_ref, lse_ref):
    hb = B_WINDOW_STEPS
    n_streams, n_rows = q_ref.shape[1], q_ref.shape[2]
    r = lax.broadcasted_iota(jnp.int32, (hb, 2 * hb), 0)
    c = lax.broadcasted_iota(jnp.int32, (hb, 2 * hb), 1)
    dist = r + hb - c
    band = jnp.where(dist >= 0, jnp.where(dist <= B_WINDOW_STEPS, 0.0, NEG), NEG)
    first_col = jnp.where(pl.program_id(2) > 0, 0, hb)
    band_first = jnp.where(c >= first_col, band, NEG)
    lane = lax.broadcasted_iota(jnp.int32, (hb, LANES), 1)
    left = lane < HEAD_DIM
    ones = jnp.ones((2 * hb, LANES), BF16)
    for sb in range(n_streams):
        for t in range(n_rows // TB):
            for p in range(B_HEADS_PER_GROUP // 2):
                cols = slice(p * LANES, (p + 1) * LANES)
                tile = slice(t * TB, (t + 1) * TB)
                q2, kc, vc = q_ref[0, sb, tile, cols], kc_ref[0, sb, tile, cols], vc_ref[0, sb, tile, cols]
                if t == 0:
                    k_before, v_before = kp_ref[0, sb, :, cols], vp_ref[0, sb, :, cols]
                else:
                    before = slice(t * TB - hb, t * TB)
                    k_before, v_before = kc_ref[0, sb, before, cols], vc_ref[0, sb, before, cols]
                windows = ((jnp.concatenate([k_before, kc[:hb]], axis=0),
                            jnp.concatenate([v_before, vc[:hb]], axis=0)), (kc, vc))
                for half, ((kw, vw), bias) in enumerate(zip(windows, (band_first if t == 0 else band, band))):
                    rows = slice(t * TB + half * hb, t * TB + (half + 1) * hb)
                    v1 = jnp.concatenate([vw, ones], axis=1)
                    outs, lses = [], []
                    for side in (left, ~left):
                        qh = jnp.where(side, q2[half * hb:(half + 1) * hb], jnp.zeros((hb, LANES), BF16))
                        s = lax.dot_general(qh, kw, (((1,), (1,)), ((), ())), preferred_element_type=F32) + bias
                        m = s.max(axis=-1, keepdims=True)
                        e = jnp.exp2((s - m).astype(BF16))
                        ol = jnp.dot(e, v1, preferred_element_type=F32)
                        l = ol[:, LANES:]
                        outs.append(ol[:, :LANES] / l)
                        lses.append(m + jnp.log2(l))
                    o_ref[0, sb, rows, cols] = jnp.where(left, outs[0], outs[1])
                    lse_ref[0, sb, rows, cols] = jnp.where(left, lses[0], lses[1])


def _window_attention(q, k, v):
    b, d, n, _ = q.shape
    assert TB == 2 * B_WINDOW_STEPS
    rows = min(n, WINDOW_TILES_PER_STEP * TB)
    streams = min(d, WINDOW_TILES_PER_STEP * TB // rows)
    halves = rows // (TB // 2)
    cur = pl.BlockSpec((1, streams, rows, B_OUT_WIDTH), lambda bi, ri, ti: (bi, ri, ti, 0))
    prev = pl.BlockSpec((1, streams, TB // 2, B_OUT_WIDTH),
                        lambda bi, ri, ti: (bi, ri, jnp.maximum(halves * ti - 1, 0), 0))
    return pl.pallas_call(
        _window_kernel,
        grid=(b, d // streams, n // rows),
        in_specs=[cur, prev, cur, prev, cur],
        out_specs=[cur, cur],
        out_shape=[jax.ShapeDtypeStruct((b, d, n, B_OUT_WIDTH), F32)] * 2,
        compiler_params=_params(("parallel", "parallel", "arbitrary")),
        name=f"window_attention_d{d}",
    )(q, k, k, v, v)


def _layer_norm(y, g, b):
    mu = y.mean(axis=-1, keepdims=True)
    yc = y - mu
    var = (yc * yc).mean(axis=-1, keepdims=True)
    return yc * lax.rsqrt(var + LN_EPS) * g + b


def _merge_kernel(alpha, oa_ref, o0_ref, o1_ref, o2_ref, l0_ref, l1_ref, l2_ref, x_ref,
                  wg_ref, bg_ref, wa_ref, wb_ref, wo_ref, g_ref, b_ref, y_ref,
                  tok_ref, ob_ref, xb_ref, mg_ref):
    halves = B_OUT_WIDTH // LANES

    for n, ref in enumerate((o1_ref, o2_ref, l1_ref, l2_ref)):
        d = ref.shape[1]
        for r in range(d):
            for p in range(halves):
                tok_ref[n, p, pl.ds(r, TM_MERGE // d, stride=d), :] = ref[0, r, :, p * LANES:(p + 1) * LANES]
    for c in range(TM_MERGE // MERGE_ROWS):
        rows = slice(c * MERGE_ROWS, (c + 1) * MERGE_ROWS)
        for p in range(halves):
            cols = slice(p * LANES, (p + 1) * LANES)
            l0, l1, l2 = l0_ref[0, 0, rows, cols], tok_ref[2, p, rows, :], tok_ref[3, p, rows, :]
            lm = jnp.maximum(jnp.maximum(l0, l1), l2)
            e0, e1, e2 = jnp.exp2(l0 - lm), jnp.exp2(l1 - lm), jnp.exp2(l2 - lm)
            ob = (e0 * o0_ref[0, 0, rows, cols] + e1 * tok_ref[0, p, rows, :] + e2 * tok_ref[1, p, rows, :])
            ob_ref[rows, cols] = (ob / (e0 + e1 + e2)).astype(BF16)

    xb_ref[...] = x_ref[0].astype(BF16)
    for c in range(D_MODEL // MERGE_COLS):
        cols = slice(c * MERGE_COLS, (c + 1) * MERGE_COLS)
        gcols = slice(D_MODEL + c * MERGE_COLS, D_MODEL + (c + 1) * MERGE_COLS)

        def gate(sel):
            z = jnp.dot(xb_ref[...], wg_ref[:, sel], preferred_element_type=F32) + bg_ref[:, sel]
            return 1.0 / (1.0 + jnp.exp(-z))

        pa = jnp.dot(oa_ref[0], wa_ref[:, cols], preferred_element_type=F32)
        pb = jnp.dot(ob_ref[...], wb_ref[:, cols], preferred_element_type=F32)
        mg_ref[:, cols] = (gate(cols) * pa + gate(gcols) * pb).astype(BF16)

    for c in range(TM_MERGE // MERGE_ROWS):
        rows = slice(c * MERGE_ROWS, (c + 1) * MERGE_ROWS)
        mixed = jnp.dot(mg_ref[rows, :], wo_ref[...], preferred_element_type=F32)
        y_ref[0, rows, :] = _layer_norm(alpha * x_ref[0, rows, :] + mixed, g_ref[...], b_ref[...])


def _merge(alpha, oa, obs, lses, x, w_gate, b_gate, wa, wb, wo, g, bb):
    b, s, _ = x.shape
    tm = TM_MERGE
    tok = lambda width: pl.BlockSpec((1, tm, width), lambda i, j: (i, j, 0))
    full = lambda a: pl.BlockSpec(a.shape, lambda i, j: (0, 0), pipeline_mode=pl.Buffered(1))
    streams = [pl.BlockSpec((1, a.shape[1], tm // a.shape[1], B_OUT_WIDTH), lambda i, j: (i, 0, j, 0))
               for a in list(obs) + list(lses)]
    return pl.pallas_call(
        functools.partial(_merge_kernel, alpha),
        grid=(b, s // tm),
        in_specs=[tok(A_WIDTH)] + streams + [tok(D_MODEL), full(w_gate),
                  full(b_gate), full(wa), full(wb), full(wo), full(g), full(bb)],
        out_specs=tok(D_MODEL),
        out_shape=jax.ShapeDtypeStruct((b, s, D_MODEL), F32),
        scratch_shapes=[pltpu.VMEM((4, B_OUT_WIDTH // LANES, tm, LANES), F32),
                        pltpu.VMEM((tm, B_OUT_WIDTH), BF16),
                        pltpu.VMEM((tm, D_MODEL), BF16),
                        pltpu.VMEM((tm, D_MODEL), BF16)],
        compiler_params=_params(("parallel", "parallel")),
        name="merge_outproj_norm",
    )(oa, *obs, *lses, x, w_gate, b_gate, wa, wb, wo, g, bb)


def _ffn_kernel(alpha, x_ref, wg_ref, wu_ref, wd_ref, g_ref, b_ref, y_ref):
    for r in range(TM_FFN // TM):
        rows = slice(r * TM, (r + 1) * TM)
        x = x_ref[rows, :]
        xb = x.astype(BF16)
        acc = jnp.zeros((TM, D_MODEL), F32)
        for c in range(FFN_HIDDEN // FFN_CHUNK):
            cols = slice(c * FFN_CHUNK, (c + 1) * FFN_CHUNK)
            gate = jnp.dot(xb, wg_ref[:, cols], preferred_element_type=F32)
            up = jnp.dot(xb, wu_ref[:, cols], preferred_element_type=F32)
            h = gate / (1.0 + jnp.exp(-gate)) * up
            acc = acc + jnp.dot(h.astype(BF16), wd_ref[cols, :], preferred_element_type=F32)
        y_ref[rows, :] = _layer_norm(alpha * x + acc, g_ref[...], b_ref[...])


def _ffn(alpha, x2, wg, wu, wd, g, bb):
    m = x2.shape[0]
    tok = pl.BlockSpec((TM_FFN, D_MODEL), lambda i: (i, 0))
    full = lambda a: pl.BlockSpec(a.shape, lambda i: (0, 0), pipeline_mode=pl.Buffered(1))
    return pl.pallas_call(
        functools.partial(_ffn_kernel, alpha),
        grid=(m // TM_FFN,),
        in_specs=[tok, full(wg), full(wu), full(wd), full(g), full(bb)],
        out_specs=tok,
        out_shape=jax.ShapeDtypeStruct((m, D_MODEL), F32),
        compiler_params=_params(("parallel",)),
        name="swiglu_norm",
    )(x2, wg, wu, wd, g, bb)


def _split_w_in(w):
    a, bw, hi = A_WIDTH, B_WIDTH, IDX_HEADS * IDX_DIM
    o = 0
    qa, o = w[:, o:o + a], o + a
    ka, o = w[:, o:o + a], o + a
    va, o = w[:, o:o + a], o + a
    qb, o = w[:, o:o + bw], o + bw
    kb, o = w[:, o:o + bw], o + bw
    vb, o = w[:, o:o + bw], o + bw
    qi, o = w[:, o:o + hi], o + hi
    ki, o = w[:, o:o + IDX_DIM], o + IDX_DIM
    wi, o = w[:, o:o + IDX_HEADS], o + IDX_HEADS
    w_gate = w[:, o:].astype(BF16)
    w_nat = jnp.concatenate([ka, qb, kb, vb, ki, ki], axis=1).astype(BF16)
    pad = jnp.zeros((w.shape[0], _T_ROWS - _T_WI - IDX_HEADS), w.dtype)
    w_t = jnp.concatenate([qa, va, qi, wi, pad], axis=1).T.astype(BF16)
    return w_nat, w_t, w_gate


def kernel(x, positions, w_in, b_gate, w_branch_a, w_branch_b, w_out, ln1_g, ln1_b,
           w_ffn_gate, w_ffn_up, w_ffn_down, ln2_g, ln2_b):
    b, s, d = x.shape
    depth = w_in.shape[0]
    assert d == D_MODEL and s % (max(B_DILATIONS) * TB) == 0 and s % max(TM, TM_MERGE) == 0
    assert (b * s) % TM_FFN == 0 and B_DILATIONS[0] == 1
    alpha = (2 * depth) ** 0.25
    cos, sin, cos_t, sin_t = _rope_tables(positions)
    row = lambda v: v.reshape(1, -1)
    for layer in range(depth):
        w_nat, w_t, w_gate = _split_w_in(w_in[layer])
        ka6, ki4, *streams = _proj_nat(x, cos, sin, w_nat)
        qt, vt5, qi4t, wt = _proj_t(x, cos_t, sin_t, w_t)
        oa = _dsa_attention(qt, qi4t, wt, ka6, vt5, ki4)
        groups = [_window_attention(*streams[3 * g:3 * g + 3]) for g in range(len(B_DILATIONS))]
        x1 = _merge(alpha, oa, [o for o, _ in groups], [l for _, l in groups], x, w_gate, row(b_gate[layer]),
                    w_branch_a[layer].astype(BF16), w_branch_b[layer].astype(BF16),
                    w_out[layer].astype(BF16), row(ln1_g[layer]), row(ln1_b[layer]))
        x2 = _ffn(alpha, x1.reshape(b * s, d), w_ffn_gate[layer].astype(BF16), w_ffn_up[layer].astype(BF16),
                  w_ffn_down[layer].astype(BF16), row(ln2_g[layer]), row(ln2_b[layer]))
        x = x2.reshape(b, s, d)
    return x
```

```python
import functools

import jax
import jax.numpy as jnp
from jax import lax
from jax.experimental import pallas as pl
from jax.experimental.pallas import tpu as pltpu

F32 = jnp.float32
BF16 = jnp.bfloat16

D_MODEL = 1024
HEAD_DIM = 64
ROT_HALF = 8
ROPE_THETA = 500000.0
ATTN_SCALE = HEAD_DIM ** -0.5
LOG2E = 1.4426950408889634
A_HEADS = 12
A_WIDTH = A_HEADS * HEAD_DIM
IDX_HEADS = 8
IDX_DIM = 64
IDX_SCALE = (IDX_HEADS ** -0.5) * (IDX_DIM ** -0.5)
TOPK_MAX = 256
B_DILATIONS = (1, 4, 16)
B_WINDOW_STEPS = 128
B_HEADS_PER_GROUP = 4
B_WIDTH = 3 * B_HEADS_PER_GROUP * HEAD_DIM
B_OUT_WIDTH = B_HEADS_PER_GROUP * HEAD_DIM
FFN_HIDDEN = 2816
LN_EPS = 1e-5
NEG = -1e30
MASK_NEG = -(2.0 ** 100)

LANES = 128
SUBLANES = 8
V_ROWS = HEAD_DIM + 2 * SUBLANES
VMEM_LIMIT = 56 * 1024 * 1024
TM = 512
TM_FFN = 1024
TM_MERGE = 1024
TQ = 256
TK = 256
TB = 256
FFN_CHUNK = 256
KEY_TILES_PER_STEP = 6
BISECT_STEPS_PER_CHECK = 2
BISECT_BLIND_STEPS = 16
WINDOW_TILES_PER_STEP = 16
MERGE_ROWS = 256
MERGE_COLS = 256
ROPE_TILE = 2048

_N_KA, _N_QB, _N_KB, _N_VB = 0, A_WIDTH, A_WIDTH + B_WIDTH, A_WIDTH + 2 * B_WIDTH
_N_KI = A_WIDTH + 3 * B_WIDTH
_N_NAT = _N_KI + LANES
_T_QA, _T_VA, _T_QI = 0, A_WIDTH, 2 * A_WIDTH
_T_WI = _T_QI + IDX_HEADS * IDX_DIM
_T_ROWS = _T_WI + 2 * SUBLANES


def _params(sem):
    return pltpu.CompilerParams(dimension_semantics=sem, vmem_limit_bytes=VMEM_LIMIT)


def _rope_kernel(pr_ref, fc_ref, c_ref, s_ref, ct_ref, st_ref):
    ang_t = fc_ref[...] * pr_ref[...].astype(F32)
    cos_t, sin_t = jnp.cos(ang_t), jnp.sin(ang_t)
    ct_ref[...] = cos_t
    st_ref[...] = sin_t
    reps = LANES // ROT_HALF
    cos = jnp.concatenate([cos_t] * reps, axis=0).T
    sin = jnp.concatenate([sin_t] * reps, axis=0).T
    d = lax.broadcasted_iota(jnp.int32, cos.shape, 1) & (HEAD_DIM - 1)
    c_ref[...] = jnp.where(d < 2 * ROT_HALF, cos, 1.0)
    s_ref[...] = jnp.where(d < ROT_HALF, -sin, jnp.where(d < 2 * ROT_HALF, sin, 0.0))


def _rope_tables(positions):
    m = positions.size
    inv_freq = ROPE_THETA ** (-jnp.arange(0, 2 * ROT_HALF, 2, dtype=F32) / (2 * ROT_HALF))
    t = min(ROPE_TILE, m)
    return pl.pallas_call(
        _rope_kernel,
        grid=(m // t,),
        in_specs=[pl.BlockSpec((1, t), lambda i: (0, i)), pl.BlockSpec((ROT_HALF, 1), lambda i: (0, 0))],
        out_specs=[pl.BlockSpec((t, LANES), lambda i: (i, 0)), pl.BlockSpec((t, LANES), lambda i: (i, 0)),
                   pl.BlockSpec((ROT_HALF, t), lambda i: (0, i)), pl.BlockSpec((ROT_HALF, t), lambda i: (0, i))],
        out_shape=[jax.ShapeDtypeStruct((m, LANES), F32), jax.ShapeDtypeStruct((m, LANES), F32),
                   jax.ShapeDtypeStruct((ROT_HALF, m), F32), jax.ShapeDtypeStruct((ROT_HALF, m), F32)],
        compiler_params=_params(("parallel",)),
        name="rope_tables",
    )(positions.reshape(1, m), inv_freq.reshape(ROT_HALF, 1))


def _proj_nat_kernel(x_ref, c_ref, s_ref, w_ref, ka_ref, ki_ref, *rest):
    streams, y_ref = rest[:-1], rest[-1]
    xb = x_ref[0].astype(BF16)
    cos, sin = c_ref[...], s_ref[...]
    lane = lax.broadcasted_iota(jnp.int32, cos.shape, 1)
    first = (lane & (HEAD_DIM - 1)) < ROT_HALF

    def rope(y):
        partner = jnp.where(first, pltpu.roll(y, LANES - ROT_HALF, 1), pltpu.roll(y, ROT_HALF, 1))
        return y * cos + partner * sin

    def proj(lo, hi):
        return jnp.dot(xb, w_ref[:, lo:hi], preferred_element_type=F32)

    blocks = A_WIDTH // LANES
    y = proj(_N_KA, _N_KA + A_WIDTH)
    for p in range(blocks):
        ka_ref[0, p] = rope(y[:, p * LANES:(p + 1) * LANES]).astype(BF16)

    def scatter_streams(which):
        per_group = B_OUT_WIDTH // LANES
        for g, d in enumerate(B_DILATIONS):
            out = streams[3 * g + which]
            for r in range(d):
                for p in range(per_group):
                    rows = y_ref[g * per_group + p, pl.ds(r, TM // d, stride=d), :]
                    out[0, r, :, p * LANES:(p + 1) * LANES] = rows.astype(BF16)

    y = proj(_N_QB, _N_QB + B_WIDTH)
    for p in range(blocks):
        y_ref[p] = rope(y[:, p * LANES:(p + 1) * LANES]) * (ATTN_SCALE * LOG2E)
    scatter_streams(0)
    y = proj(_N_KB, _N_KB + B_WIDTH)
    for p in range(blocks):
        y_ref[p] = rope(y[:, p * LANES:(p + 1) * LANES])
    scatter_streams(1)
    y = proj(_N_VB, _N_VB + B_WIDTH)
    for p in range(blocks):
        y_ref[p] = y[:, p * LANES:(p + 1) * LANES]
    scatter_streams(2)
    r = rope(proj(_N_KI, _N_KI + LANES))
    hi = r.astype(BF16).astype(F32)
    hl = jnp.where(lane < IDX_DIM, hi, r - hi).astype(BF16)
    ki_ref[0, :, 0:LANES] = hl
    ki_ref[0, :, LANES:2 * LANES] = hl


N_TOKEN_MAJOR_OUTS = 2 + 3 * len(B_DILATIONS)


def _proj_kernel(x_ref, c_ref, s_ref, ct_ref, st_ref, wn_ref, wt_ref, *rest):
    tok_outs, feat_outs, y_ref = rest[:N_TOKEN_MAJOR_OUTS], rest[N_TOKEN_MAJOR_OUTS:-1], rest[-1]
    _proj_nat_kernel(x_ref, c_ref, s_ref, wn_ref, *tok_outs, y_ref)
    _proj_t_kernel(x_ref, ct_ref, st_ref, wt_ref, *feat_outs)


def _project(x, cos, sin, cos_t, sin_t, w_nat, w_t):
    b, s, d = x.shape
    nt = s // TM
    tok = lambda width: pl.BlockSpec((1, TM, width), lambda i, j: (i, j, 0))
    once = lambda a: pl.BlockSpec(a.shape, lambda i, j: (0, 0), pipeline_mode=pl.Buffered(1))
    stream_specs, stream_shapes = [], []
    for dil in B_DILATIONS:
        for _ in range(3):
            stream_specs.append(pl.BlockSpec((1, dil, TM // dil, B_OUT_WIDTH), lambda i, j: (i, 0, j, 0)))
            stream_shapes.append(jax.ShapeDtypeStruct((b, dil, s // dil, B_OUT_WIDTH), BF16))
    return pl.pallas_call(
        _proj_kernel,
        grid=(b, nt),
        in_specs=[tok(d),
                  pl.BlockSpec((TM, LANES), lambda i, j: (i * nt + j, 0)),
                  pl.BlockSpec((TM, LANES), lambda i, j: (i * nt + j, 0)),
                  pl.BlockSpec((ROT_HALF, TM), lambda i, j: (0, i * nt + j)),
                  pl.BlockSpec((ROT_HALF, TM), lambda i, j: (0, i * nt + j)),
                  once(w_nat), once(w_t)],
        out_specs=[pl.BlockSpec((1, A_WIDTH // LANES, TM, LANES), lambda i, j: (i, 0, j, 0)),
                   tok(2 * LANES)] + stream_specs +
                  [pl.BlockSpec((1, A_WIDTH, TM), lambda i, j: (i, 0, j)),
                   pl.BlockSpec((1, A_HEADS, TM // TK, V_ROWS, TK), lambda i, j: (i, 0, j, 0, 0)),
                   pl.BlockSpec((1, 4 * IDX_HEADS * IDX_DIM, TM), lambda i, j: (i, 0, j)),
                   pl.BlockSpec((1, IDX_HEADS, TM), lambda i, j: (i, 0, j))],
        out_shape=[jax.ShapeDtypeStruct((b, A_WIDTH // LANES, s, LANES), BF16),
                   jax.ShapeDtypeStruct((b, s, 2 * LANES), BF16)] + stream_shapes +
                  [jax.ShapeDtypeStruct((b, A_WIDTH, s), BF16),
                   jax.ShapeDtypeStruct((b, A_HEADS, s // TK, V_ROWS, TK), BF16),
                   jax.ShapeDtypeStruct((b, 4 * IDX_HEADS * IDX_DIM, s), BF16),
                   jax.ShapeDtypeStruct((b, IDX_HEADS, s), F32)],
        scratch_shapes=[pltpu.VMEM((B_WIDTH // LANES, TM, LANES), F32)],
        compiler_params=_params(("parallel", "parallel")),
        name="input_projection",
    )(x, cos, sin, cos_t, sin_t, w_nat, w_t)


def _proj_t_kernel(x_ref, ct_ref, st_ref, w_ref, qt_ref, vt_ref, qi_ref, wt_ref):
    xb = x_ref[0].astype(BF16)
    cos, sin = ct_ref[...], st_ref[...]

    def proj(lo, hi):
        return lax.dot_general(w_ref[lo:hi, :], xb, (((1,), (1,)), ((), ())), preferred_element_type=F32)

    def rope_head(y):
        x1, x2 = y[0:ROT_HALF], y[ROT_HALF:2 * ROT_HALF]
        return jnp.concatenate([x1 * cos - x2 * sin, x2 * cos + x1 * sin, y[2 * ROT_HALF:]], axis=0)

    y = proj(_T_QA, _T_QA + A_WIDTH)
    for h in range(A_HEADS):
        r = rope_head(y[h * HEAD_DIM:(h + 1) * HEAD_DIM])
        qt_ref[0, h * HEAD_DIM:(h + 1) * HEAD_DIM, :] = (r * (ATTN_SCALE * LOG2E)).astype(BF16)
    y = proj(_T_VA, _T_VA + A_WIDTH)
    for h in range(A_HEADS):
        for c in range(TM // TK):
            vt_ref[0, h, c, 0:HEAD_DIM, :] = y[h * HEAD_DIM:(h + 1) * HEAD_DIM, c * TK:(c + 1) * TK].astype(BF16)
            vt_ref[0, h, c, HEAD_DIM:V_ROWS, :] = jnp.ones((V_ROWS - HEAD_DIM, TK), BF16)
    y = proj(_T_QI, _T_QI + IDX_HEADS * IDX_DIM)
    for h in range(IDX_HEADS):
        r = rope_head(y[h * IDX_DIM:(h + 1) * IDX_DIM])
        hi = r.astype(BF16)
        lo = (r - hi.astype(F32)).astype(BF16)
        base = 4 * h * IDX_DIM
        qi_ref[0, base:base + IDX_DIM, :] = hi
        qi_ref[0, base + IDX_DIM:base + 2 * IDX_DIM, :] = hi
        qi_ref[0, base + 2 * IDX_DIM:base + 3 * IDX_DIM, :] = lo
        qi_ref[0, base + 3 * IDX_DIM:base + 4 * IDX_DIM, :] = lo
    wt_ref[0] = proj(_T_WI, _T_ROWS)[0:IDX_HEADS] * IDX_SCALE


def _dsa_kernel(qt_ref, qi_ref, wt_ref, k_ref, vt_ref, ki_ref, o_ref,
                sc_ref, qpad_ref, ot_ref, m_ref, st_ref, stat_ref, sa_ref, sb_ref):
    j = pl.program_id(1)
    nkt = j + 1

    for h in range(A_HEADS):
        off = (h % 2) * HEAD_DIM
        qpad_ref[h] = jnp.zeros((2 * HEAD_DIM, TQ), BF16)
        qpad_ref[h, off:off + HEAD_DIM, :] = qt_ref[0, h * HEAD_DIM:(h + 1) * HEAD_DIM, :]

    qpos = j * TQ + lax.broadcasted_iota(jnp.int32, (1, TQ), 1)
    row_iota = lax.broadcasted_iota(jnp.int32, (TK, TQ), 0)
    w = wt_ref[0]

    def fold(t):
        return t.reshape(TK // SUBLANES, SUBLANES, TQ)

    def score_tile(kt, diagonal, carry):
        mn, mx, pos, nn = carry
        ki = ki_ref[0, pl.ds(pl.multiple_of(kt * TK, TK), TK), :]
        acc = jnp.zeros((TK, TQ), F32)
        for h in range(IDX_HEADS):
            s = jnp.dot(ki, qi_ref[0, 4 * h * IDX_DIM:4 * (h + 1) * IDX_DIM, :], preferred_element_type=F32)
            acc = acc + w[h:h + 1, :] * jnp.maximum(s, 0.0)
        if diagonal:
            causal = (kt * TK + row_iota) <= qpos
            val = jnp.where(causal, acc, -jnp.inf)
            low = jnp.where(causal, acc, jnp.inf)
        else:
            val = low = acc
        sc_ref[kt] = val
        return (jnp.minimum(mn, fold(low).min(axis=0)),
                jnp.maximum(mx, fold(val).max(axis=0)),
                pos + fold(jnp.where(val > 0.0, 1.0, 0.0)).sum(axis=0),
                nn + fold(jnp.where(val >= 0.0, 1.0, 0.0)).sum(axis=0))

    def load_stats():
        return tuple(stat_ref[n] for n in range(4))

    def store_stats(stats):
        for n, v in enumerate(stats):
            stat_ref[n] = v

    def score_block(i, carry):
        for u in range(KEY_TILES_PER_STEP):
            carry = score_tile(KEY_TILES_PER_STEP * i + u, False, carry)
        return carry

    zeros8 = jnp.zeros((SUBLANES, TQ), F32)
    store_stats(lax.fori_loop(
        0, j // KEY_TILES_PER_STEP, score_block,
        (jnp.full((SUBLANES, TQ), jnp.inf, F32), jnp.full((SUBLANES, TQ), -jnp.inf, F32), zeros8, zeros8)))

    for left in range(KEY_TILES_PER_STEP):
        @pl.when(j % KEY_TILES_PER_STEP == left)
        def _(left=left):
            stats = load_stats()
            for u in range(left):
                stats = score_tile(j - left + u, False, stats)
            store_stats(score_tile(j, True, stats))

    mn8, mx8, pos8, nn8 = load_stats()
    mn = mn8.min(axis=0, keepdims=True)
    mx = mx8.max(axis=0, keepdims=True)
    c_pos = pos8.sum(axis=0, keepdims=True)
    c_nn = nn8.sum(axis=0, keepdims=True)

    sc_ref[nkt] = jnp.full((TK, TQ), -jnp.inf, F32)

    def count(preds):
        def part(p, kt):
            return fold(p(sc_ref[kt])).sum(axis=0)

        def body(i, cnts):
            return tuple(c + part(p, 2 * i) + part(p, 2 * i + 1) for c, p in zip(cnts, preds))

        cnts = lax.fori_loop(0, (nkt + 1) // 2, body, tuple(zeros8 for _ in preds))
        return [c.sum(axis=0, keepdims=True) for c in cnts]

    def ge(cand):
        return lambda t: jnp.where(t >= cand, 1.0, 0.0)

    n_causal = (qpos + 1).astype(F32)
    k_q = jnp.minimum(n_causal, float(TOPK_MAX))
    c_mx, = count([ge(mx)])
    select = n_causal > k_q
    at_max = select & (c_mx >= k_q)
    search = select & (c_mx < k_q)
    at_zero = search & (c_pos < k_q) & (c_nn >= k_q)
    above = search & (c_pos >= k_q)
    below = search & (c_nn < k_q)
    lo0 = jnp.where(at_max, mx, jnp.where(at_zero | above, 0.0, mn))
    c_lo0 = jnp.where(at_max, c_mx, jnp.where(at_zero | above, c_nn, n_causal))
    st_ref[0:1, :] = lo0
    st_ref[1:2, :] = jnp.where(below, 0.0, mx)
    st_ref[2:3, :] = c_lo0
    st_ref[3:4, :] = jnp.where(at_max, 0.0, jnp.where(at_zero, c_pos, jnp.where(below, c_nn, c_mx)))
    act0 = jnp.where((above | below) & (c_lo0 > k_q), 1.0, 0.0)
    st_ref[4:5, :] = act0

    def bisect_once():
        lo, hi, c_lo, c_hi = st_ref[0:1, :], st_ref[1:2, :], st_ref[2:3, :], st_ref[3:4, :]
        act = st_ref[4:5, :] > 0.0
        mid = 0.5 * lo + 0.5 * hi
        live = act & (mid > lo) & (mid < hi)
        c, = count([ge(mid)])
        up = live & (c >= k_q)
        dn = live & (c < k_q)
        c_lo = jnp.where(up, c, c_lo)
        st_ref[0:1, :] = jnp.where(up, mid, lo)
        st_ref[1:2, :] = jnp.where(dn, mid, hi)
        st_ref[2:3, :] = c_lo
        st_ref[3:4, :] = jnp.where(dn, c, c_hi)
        act_new = jnp.where(live & (c_lo > k_q), 1.0, 0.0)
        st_ref[4:5, :] = act_new
        return act_new

    def bisect(go):
        for _ in range(BISECT_STEPS_PER_CHECK - 1):
            bisect_once()
        return (jnp.max(bisect_once()) > 0.0).astype(jnp.int32)

    @pl.when(jnp.max(act0) > 0.0)
    def _():
        def blind(_, carry):
            bisect_once()
            bisect_once()
            return carry

        lax.fori_loop(0, BISECT_BLIND_STEPS // 2, blind, 0)

    lax.while_loop(lambda go: go > 0, bisect, (jnp.max(st_ref[4:5, :]) > 0.0).astype(jnp.int32))

    lo = st_ref[0:1, :]
    tie = st_ref[2:3, :] > k_q

    @pl.when(jnp.max(jnp.where(tie, 1.0, 0.0)) > 0.0)
    def _():
        need = jnp.where(tie, k_q - st_ref[3:4, :], jnp.inf)
        tri = jnp.where(lax.broadcasted_iota(jnp.int32, (TK, TK), 0) >= lax.broadcasted_iota(jnp.int32, (TK, TK), 1),
                        1.0, 0.0).astype(BF16)

        def drop(kt, seen):
            t = sc_ref[kt]
            eq = jnp.where(t == lo, 1.0, 0.0)
            rank = jnp.dot(tri, eq.astype(BF16), preferred_element_type=F32) + seen
            sc_ref[kt] = jnp.where(t == lo, jnp.where(rank > need, -jnp.inf, t), t)
            return seen + fold(eq).sum(axis=0).sum(axis=0, keepdims=True)

        def drop_pair(i, seen):
            return drop(2 * i + 1, drop(2 * i, seen))

        lax.fori_loop(0, (nkt + 1) // 2, drop_pair, jnp.zeros((1, TQ), F32))

    m_ref[...] = jnp.full(m_ref.shape, MASK_NEG, F32)
    ot_ref[...] = jnp.zeros(ot_ref.shape, F32)

    def mask_bias(kt):
        return jnp.where(sc_ref[kt] >= lo, 0.0, MASK_NEG).astype(BF16)

    def logits(h, kt, bias):
        rows = pl.ds(pl.multiple_of(kt * TK, TK), TK)
        return jnp.dot(k_ref[0, h // 2, rows, :], qpad_ref[h], preferred_element_type=F32).astype(BF16) + bias

    def half_step(kt, cur_ref, nxt_ref):
        if nxt_ref is not None:
            bias_n = mask_bias(kt + 1)
        for h in range(A_HEADS):
            s = cur_ref[h]
            if nxt_ref is not None:
                nxt_ref[h] = logits(h, kt + 1, bias_n)
            m = m_ref[h]
            m_tile = (s.reshape(TK // (2 * SUBLANES), 2 * SUBLANES, TQ).max(axis=0)
                      .astype(F32).max(axis=0, keepdims=True))
            m_new = jnp.maximum(m, m_tile)
            p = jnp.exp2(s - m_new.astype(BF16))
            corr = jnp.exp2(m - m_new)
            m_ref[h] = m_new
            ot_ref[h] = ot_ref[h] * corr + jnp.dot(vt_ref[0, h, kt], p, preferred_element_type=F32)

    bias0 = mask_bias(0)
    for h in range(A_HEADS):
        sa_ref[h] = logits(h, 0, bias0)

    def buffers(u):
        return (sa_ref, sb_ref) if u % 2 == 0 else (sb_ref, sa_ref)

    def kv_block(i, _):
        for u in range(KEY_TILES_PER_STEP):
            half_step(KEY_TILES_PER_STEP * i + u, *buffers(u))
        return 0

    n_blocks = (nkt - 1) // KEY_TILES_PER_STEP
    lax.fori_loop(0, n_blocks, kv_block, 0)

    for left in range(1, KEY_TILES_PER_STEP + 1):
        @pl.when(nkt - KEY_TILES_PER_STEP * n_blocks == left)
        def _(left=left):
            for u in range(left):
                cur_ref, nxt_ref = buffers(u)
                half_step(nkt - left + u, cur_ref, nxt_ref if u < left - 1 else None)

    for p in range(A_HEADS // 2):
        both = jnp.concatenate([ot_ref[h, 0:HEAD_DIM, :] / ot_ref[h, HEAD_DIM:HEAD_DIM + 1, :]
                                for h in (2 * p, 2 * p + 1)], axis=0)
        o_ref[0, :, p * LANES:(p + 1) * LANES] = both.T.astype(BF16)


def _dsa_attention(qt, qi4t, wt, k6, vt5, ki4):
    b, _, s = qt.shape
    nq = s // TQ
    return pl.pallas_call(
        _dsa_kernel,
        grid=(b, nq),
        in_specs=[pl.BlockSpec((1, A_WIDTH, TQ), lambda i, j: (i, 0, j)),
                  pl.BlockSpec((1, 4 * IDX_HEADS * IDX_DIM, TQ), lambda i, j: (i, 0, j)),
                  pl.BlockSpec((1, IDX_HEADS, TQ), lambda i, j: (i, 0, j)),
                  pl.BlockSpec((1, A_WIDTH // LANES, s, LANES), lambda i, j: (i, 0, 0, 0)),
                  pl.BlockSpec((1, A_HEADS, s // TK, V_ROWS, TK), lambda i, j: (i, 0, 0, 0, 0)),
                  pl.BlockSpec((1, s, 4 * IDX_DIM), lambda i, j: (i, 0, 0))],
        out_specs=pl.BlockSpec((1, TQ, A_WIDTH), lambda i, j: (i, j, 0)),
        out_shape=jax.ShapeDtypeStruct((b, s, A_WIDTH), BF16),
        scratch_shapes=[pltpu.VMEM((s // TK + 1, TK, TQ), F32),
                        pltpu.VMEM((A_HEADS, 2 * HEAD_DIM, TQ), BF16),
                        pltpu.VMEM((A_HEADS, V_ROWS, TQ), F32),
                        pltpu.VMEM((A_HEADS, 1, TQ), F32),
                        pltpu.VMEM((SUBLANES, TQ), F32),
                        pltpu.VMEM((4, SUBLANES, TQ), F32),
                        pltpu.VMEM((A_HEADS, TK, TQ), BF16),
                        pltpu.VMEM((A_HEADS, TK, TQ), BF16)],
        compiler_params=_params(("parallel", "arbitrary")),
        name="dsa_attention",
    )(qt, qi4t, wt, k6, vt5, ki4)


def _window_kernel(q_ref, kp_ref, kc_ref, vp_ref, vc_ref, o_ref, lse_ref):
    hb = B_WINDOW_STEPS
    n_streams, n_rows = q_ref.shape[1], q_ref.shape[2]
    r = lax.broadcasted_iota(jnp.int32, (hb, 2 * hb), 0)
    c = lax.broadcasted_iota(jnp.int32, (hb, 2 * hb), 1)
    dist = r + hb - c
    band = jnp.where(dist >= 0, jnp.where(dist <= B_WINDOW_STEPS, 0.0, NEG), NEG)
    first_col = jnp.where(pl.program_id(2) > 0, 0, hb)
    band_first = jnp.where(c >= first_col, band, NEG)
    lane = lax.broadcasted_iota(jnp.int32, (hb, LANES), 1)
    left = lane < HEAD_DIM
    ones = jnp.ones((2 * hb, LANES), BF16)
    for sb in range(n_streams):
        for t in range(n_rows // TB):
            for p in range(B_HEADS_PER_GROUP // 2):
                cols = slice(p * LANES, (p + 1) * LANES)
                tile = slice(t * TB, (t + 1) * TB)
                q2, kc, vc = q_ref[0, sb, tile, cols], kc_ref[0, sb, tile, cols], vc_ref[0, sb, tile, cols]
                if t == 0:
                    k_before, v_before = kp_ref[0, sb, :, cols], vp_ref[0, sb, :, cols]
                else:
                    before = slice(t * TB - hb, t * TB)
                    k_before, v_before = kc_ref[0, sb, before, cols], vc_ref[0, sb, before, cols]
                windows = ((jnp.concatenate([k_before, kc[:hb]], axis=0),
                            jnp.concatenate([v_before, vc[:hb]], axis=0)), (kc, vc))
                for half, ((kw, vw), bias) in enumerate(zip(windows, (band_first if t == 0 else band, band))):
                    rows = slice(t * TB + half * hb, t * TB + (half + 1) * hb)
                    v1 = jnp.concatenate([vw, ones], axis=1)
                    outs, lses = [], []
                    for side in (left, ~left):
                        qh = jnp.where(side, q2[half * hb:(half + 1) * hb], jnp.zeros((hb, LANES), BF16))
                        s = lax.dot_general(qh, kw, (((1,), (1,)), ((), ())), preferred_element_type=F32) + bias
                        m = s.max(axis=-1, keepdims=True)
                        e = jnp.exp2((s - m).astype(BF16))
                        ol = jnp.dot(e, v1, preferred_element_type=F32)
                        l = ol[:, LANES:]
                        outs.append(ol[:, :LANES] / l)
                        lses.append(m + jnp.log2(l))
                    o_ref[0, sb, rows, cols] = jnp.where(left, outs[0], outs[1])
                    lse_ref[0, sb, rows, cols] = jnp.where(left, lses[0], lses[1])


def _window_attention(q, k, v):
    b, d, n, _ = q.shape
    assert TB == 2 * B_WINDOW_STEPS
    rows = min(n, WINDOW_TILES_PER_STEP * TB)
    streams = min(d, WINDOW_TILES_PER_STEP * TB // rows)
    halves = rows // (TB // 2)
    cur = pl.BlockSpec((1, streams, rows, B_OUT_WIDTH), lambda bi, ri, ti: (bi, ri, ti, 0))
    prev = pl.BlockSpec((1, streams, TB // 2, B_OUT_WIDTH),
                        lambda bi, ri, ti: (bi, ri, jnp.maximum(halves * ti - 1, 0), 0))
    return pl.pallas_call(
        _window_kernel,
        grid=(b, d // streams, n // rows),
        in_specs=[cur, prev, cur, prev, cur],
        out_specs=[cur, cur],
        out_shape=[jax.ShapeDtypeStruct((b, d, n, B_OUT_WIDTH), F32)] * 2,
        compiler_params=_params(("parallel", "parallel", "arbitrary")),
        name=f"window_attention_d{d}",
    )(q, k, k, v, v)


def _layer_norm(y, g, b):
    mu = y.mean(axis=-1, keepdims=True)
    yc = y - mu
    var = (yc * yc).mean(axis=-1, keepdims=True)
    return yc * lax.rsqrt(var + LN_EPS) * g + b


def _merge_kernel(alpha, oa_ref, o0_ref, o1_ref, o2_ref, l0_ref, l1_ref, l2_ref, x_ref,
                  wg_ref, bg_ref, wa_ref, wb_ref, wo_ref, g_ref, b_ref, y_ref,
                  tok_ref, ob_ref, xb_ref, mg_ref):
    halves = B_OUT_WIDTH // LANES

    for n, ref in enumerate((o1_ref, o2_ref, l1_ref, l2_ref)):
        d = ref.shape[1]
        for r in range(d):
            for p in range(halves):
                tok_ref[n, p, pl.ds(r, TM_MERGE // d, stride=d), :] = ref[0, r, :, p * LANES:(p + 1) * LANES]
    for c in range(TM_MERGE // MERGE_ROWS):
        rows = slice(c * MERGE_ROWS, (c + 1) * MERGE_ROWS)
        for p in range(halves):
            cols = slice(p * LANES, (p + 1) * LANES)
            l0, l1, l2 = l0_ref[0, 0, rows, cols], tok_ref[2, p, rows, :], tok_ref[3, p, rows, :]
            lm = jnp.maximum(jnp.maximum(l0, l1), l2)
            e0, e1, e2 = jnp.exp2(l0 - lm), jnp.exp2(l1 - lm), jnp.exp2(l2 - lm)
            ob = (e0 * o0_ref[0, 0, rows, cols] + e1 * tok_ref[0, p, rows, :] + e2 * tok_ref[1, p, rows, :])
            ob_ref[rows, cols] = (ob / (e0 + e1 + e2)).astype(BF16)

    xb_ref[...] = x_ref[0].astype(BF16)
    for c in range(D_MODEL // MERGE_COLS):
        cols = slice(c * MERGE_COLS, (c + 1) * MERGE_COLS)
        gcols = slice(D_MODEL + c * MERGE_COLS, D_MODEL + (c + 1) * MERGE_COLS)

        def gate(sel):
            z = jnp.dot(xb_ref[...], wg_ref[:, sel], preferred_element_type=F32) + bg_ref[:, sel]
            return 1.0 / (1.0 + jnp.exp(-z))

        pa = jnp.dot(oa_ref[0], wa_ref[:, cols], preferred_element_type=F32)
        pb = jnp.dot(ob_ref[...], wb_ref[:, cols], preferred_element_type=F32)
        mg_ref[:, cols] = (gate(cols) * pa + gate(gcols) * pb).astype(BF16)

    for c in range(TM_MERGE // MERGE_ROWS):
        rows = slice(c * MERGE_ROWS, (c + 1) * MERGE_ROWS)
        mixed = jnp.dot(mg_ref[rows, :], wo_ref[...], preferred_element_type=F32)
        y_ref[0, rows, :] = _layer_norm(alpha * x_ref[0, rows, :] + mixed, g_ref[...], b_ref[...])


def _merge(alpha, oa, obs, lses, x, w_gate, b_gate, wa, wb, wo, g, bb):
    b, s, _ = x.shape
    tm = TM_MERGE
    tok = lambda width: pl.BlockSpec((1, tm, width), lambda i, j: (i, j, 0))
    full = lambda a: pl.BlockSpec(a.shape, lambda i, j: (0, 0), pipeline_mode=pl.Buffered(1))
    streams = [pl.BlockSpec((1, a.shape[1], tm // a.shape[1], B_OUT_WIDTH), lambda i, j: (i, 0, j, 0))
               for a in list(obs) + list(lses)]
    return pl.pallas_call(
        functools.partial(_merge_kernel, alpha),
        grid=(b, s // tm),
        in_specs=[tok(A_WIDTH)] + streams + [tok(D_MODEL), full(w_gate),
                  full(b_gate), full(wa), full(wb), full(wo), full(g), full(bb)],
        out_specs=tok(D_MODEL),
        out_shape=jax.ShapeDtypeStruct((b, s, D_MODEL), F32),
        scratch_shapes=[pltpu.VMEM((4, B_OUT_WIDTH // LANES, tm, LANES), F32),
                        pltpu.VMEM((tm, B_OUT_WIDTH), BF16),
                        pltpu.VMEM((tm, D_MODEL), BF16),
                        pltpu.VMEM((tm, D_MODEL), BF16)],
        compiler_params=_params(("parallel", "parallel")),
        name="merge_outproj_norm",
    )(oa, *obs, *lses, x, w_gate, b_gate, wa, wb, wo, g, bb)


def _ffn_kernel(alpha, x_ref, wg_ref, wu_ref, wd_ref, g_ref, b_ref, y_ref):
    for r in range(TM_FFN // TM):
        rows = slice(r * TM, (r + 1) * TM)
        x = x_ref[rows, :]
        xb = x.astype(BF16)
        acc = jnp.zeros((TM, D_MODEL), F32)
        for c in range(FFN_HIDDEN // FFN_CHUNK):
            cols = slice(c * FFN_CHUNK, (c + 1) * FFN_CHUNK)
            gate = jnp.dot(xb, wg_ref[:, cols], preferred_element_type=F32)
            up = jnp.dot(xb, wu_ref[:, cols], preferred_element_type=F32)
            h = gate / (1.0 + jnp.exp(-gate)) * up
            acc = acc + jnp.dot(h.astype(BF16), wd_ref[cols, :], preferred_element_type=F32)
        y_ref[rows, :] = _layer_norm(alpha * x + acc, g_ref[...], b_ref[...])


def _ffn(alpha, x2, wg, wu, wd, g, bb):
    m = x2.shape[0]
    tok = pl.BlockSpec((TM_FFN, D_MODEL), lambda i: (i, 0))
    full = lambda a: pl.BlockSpec(a.shape, lambda i: (0, 0), pipeline_mode=pl.Buffered(1))
    return pl.pallas_call(
        functools.partial(_ffn_kernel, alpha),
        grid=(m // TM_FFN,),
        in_specs=[tok, full(wg), full(wu), full(wd), full(g), full(bb)],
        out_specs=tok,
        out_shape=jax.ShapeDtypeStruct((m, D_MODEL), F32),
        compiler_params=_params(("parallel",)),
        name="swiglu_norm",
    )(x2, wg, wu, wd, g, bb)


def _split_w_in(w):
    a, bw, hi = A_WIDTH, B_WIDTH, IDX_HEADS * IDX_DIM
    o = 0
    qa, o = w[:, o:o + a], o + a
    ka, o = w[:, o:o + a], o + a
    va, o = w[:, o:o + a], o + a
    qb, o = w[:, o:o + bw], o + bw
    kb, o = w[:, o:o + bw], o + bw
    vb, o = w[:, o:o + bw], o + bw
    qi, o = w[:, o:o + hi], o + hi
    ki, o = w[:, o:o + IDX_DIM], o + IDX_DIM
    wi, o = w[:, o:o + IDX_HEADS], o + IDX_HEADS
    w_gate = w[:, o:].astype(BF16)
    w_nat = jnp.concatenate([ka, qb, kb, vb, ki, ki], axis=1).astype(BF16)
    pad = jnp.zeros((w.shape[0], _T_ROWS - _T_WI - IDX_HEADS), w.dtype)
    w_t = jnp.concatenate([qa, va, qi, wi, pad], axis=1).T.astype(BF16)
    return w_nat, w_t, w_gate


def kernel(x, positions, w_in, b_gate, w_branch_a, w_branch_b, w_out, ln1_g, ln1_b,
           w_ffn_gate, w_ffn_up, w_ffn_down, ln2_g, ln2_b):
    b, s, d = x.shape
    depth = w_in.shape[0]
    assert d == D_MODEL and s % (max(B_DILATIONS) * TB) == 0 and s % max(TM, TM_MERGE) == 0
    assert (b * s) % TM_FFN == 0 and B_DILATIONS[0] == 1
    alpha = (2 * depth) ** 0.25
    cos, sin, cos_t, sin_t = _rope_tables(positions)
    row = lambda v: v.reshape(1, -1)
    for layer in range(depth):
        w_nat, w_t, w_gate = _split_w_in(w_in[layer])
        ka6, ki4, *rest = _project(x, cos, sin, cos_t, sin_t, w_nat, w_t)
        streams, (qt, vt5, qi4t, wt) = rest[:-4], rest[-4:]
        oa = _dsa_attention(qt, qi4t, wt, ka6, vt5, ki4)
        groups = [_window_attention(*streams[3 * g:3 * g + 3]) for g in range(len(B_DILATIONS))]
        x1 = _merge(alpha, oa, [o for o, _ in groups], [l for _, l in groups], x, w_gate, row(b_gate[layer]),
                    w_branch_a[layer].astype(BF16), w_branch_b[layer].astype(BF16),
                    w_out[layer].astype(BF16), row(ln1_g[layer]), row(ln1_b[layer]))
        x2 = _ffn(alpha, x1.reshape(b * s, d), w_ffn_gate[layer].astype(BF16), w_ffn_up[layer].astype(BF16),
                  w_ffn_down[layer].astype(BF16), row(ln2_g[layer]), row(ln2_b[layer]))
        x = x2.reshape(b, s, d)
    return x
```

```python
import functools

import jax
import jax.numpy as jnp
from jax import lax
from jax.experimental import pallas as pl
from jax.experimental.pallas import tpu as pltpu

F32 = jnp.float32
BF16 = jnp.bfloat16

D_MODEL = 1024
HEAD_DIM = 64
ROT_HALF = 8
ROPE_THETA = 500000.0
ATTN_SCALE = HEAD_DIM ** -0.5
LOG2E = 1.4426950408889634
A_HEADS = 12
A_WIDTH = A_HEADS * HEAD_DIM
IDX_HEADS = 8
IDX_DIM = 64
IDX_SCALE = (IDX_HEADS ** -0.5) * (IDX_DIM ** -0.5)
TOPK_MAX = 256
B_DILATIONS = (1, 4, 16)
B_WINDOW_STEPS = 128
B_HEADS_PER_GROUP = 4
B_WIDTH = 3 * B_HEADS_PER_GROUP * HEAD_DIM
B_OUT_WIDTH = B_HEADS_PER_GROUP * HEAD_DIM
FFN_HIDDEN = 2816
LN_EPS = 1e-5
NEG = -1e30
MASK_NEG = -(2.0 ** 100)

LANES = 128
SUBLANES = 8
V_ROWS = HEAD_DIM + 2 * SUBLANES
VMEM_LIMIT = 56 * 1024 * 1024
TM = 512
TM_FFN = 1024
TM_MERGE = 1024
TQ = 256
TK = 256
TB = 256
FFN_CHUNK = 256
KEY_TILES_PER_STEP = 6
BISECT_STEPS_PER_CHECK = 2
BISECT_BLIND_STEPS = 16
WINDOW_TILES_PER_STEP = 16
MERGE_ROWS = 256
MERGE_COLS = 256
ROPE_TILE = 2048

_N_KA, _N_QB, _N_KB, _N_VB = 0, A_WIDTH, A_WIDTH + B_WIDTH, A_WIDTH + 2 * B_WIDTH
_N_KI = A_WIDTH + 3 * B_WIDTH
_N_NAT = _N_KI + LANES
_T_QA, _T_VA, _T_QI = 0, A_WIDTH, 2 * A_WIDTH
_T_WI = _T_QI + IDX_HEADS * IDX_DIM
_T_ROWS = _T_WI + 2 * SUBLANES


def _params(sem):
    return pltpu.CompilerParams(dimension_semantics=sem, vmem_limit_bytes=VMEM_LIMIT)


def _rope_kernel(pr_ref, fc_ref, c_ref, s_ref, ct_ref, st_ref):
    ang_t = fc_ref[...] * pr_ref[...].astype(F32)
    cos_t, sin_t = jnp.cos(ang_t), jnp.sin(ang_t)
    ct_ref[...] = cos_t
    st_ref[...] = sin_t
    reps = LANES // ROT_HALF
    cos = jnp.concatenate([cos_t] * reps, axis=0).T
    sin = jnp.concatenate([sin_t] * reps, axis=0).T
    d = lax.broadcasted_iota(jnp.int32, cos.shape, 1) & (HEAD_DIM - 1)
    c_ref[...] = jnp.where(d < 2 * ROT_HALF, cos, 1.0)
    s_ref[...] = jnp.where(d < ROT_HALF, -sin, jnp.where(d < 2 * ROT_HALF, sin, 0.0))


def _proj_nat_kernel(x_ref, c_ref, s_ref, w_ref, ka_ref, ki_ref, *rest):
    streams, y_ref = rest[:-1], rest[-1]
    xb = x_ref[0].astype(BF16)
    cos, sin = c_ref[...], s_ref[...]
    lane = lax.broadcasted_iota(jnp.int32, cos.shape, 1)
    first = (lane & (HEAD_DIM - 1)) < ROT_HALF

    def rope(y):
        partner = jnp.where(first, pltpu.roll(y, LANES - ROT_HALF, 1), pltpu.roll(y, ROT_HALF, 1))
        return y * cos + partner * sin

    def proj(lo, hi):
        return jnp.dot(xb, w_ref[:, lo:hi], preferred_element_type=F32)

    blocks = A_WIDTH // LANES
    y = proj(_N_KA, _N_KA + A_WIDTH)
    for p in range(blocks):
        ka_ref[0, p] = rope(y[:, p * LANES:(p + 1) * LANES]).astype(BF16)

    def scatter_streams(which):
        per_group = B_OUT_WIDTH // LANES
        for g, d in enumerate(B_DILATIONS):
            out = streams[3 * g + which]
            for r in range(d):
                for p in range(per_group):
                    rows = y_ref[g * per_group + p, pl.ds(r, TM // d, stride=d), :]
                    out[0, r, :, p * LANES:(p + 1) * LANES] = rows.astype(BF16)

    y = proj(_N_QB, _N_QB + B_WIDTH)
    for p in range(blocks):
        y_ref[p] = rope(y[:, p * LANES:(p + 1) * LANES]) * (ATTN_SCALE * LOG2E)
    scatter_streams(0)
    y = proj(_N_KB, _N_KB + B_WIDTH)
    for p in range(blocks):
        y_ref[p] = rope(y[:, p * LANES:(p + 1) * LANES])
    scatter_streams(1)
    y = proj(_N_VB, _N_VB + B_WIDTH)
    for p in range(blocks):
        y_ref[p] = y[:, p * LANES:(p + 1) * LANES]
    scatter_streams(2)
    r = rope(proj(_N_KI, _N_KI + LANES))
    hi = r.astype(BF16).astype(F32)
    hl = jnp.where(lane < IDX_DIM, hi, r - hi).astype(BF16)
    ki_ref[0, :, 0:LANES] = hl
    ki_ref[0, :, LANES:2 * LANES] = hl


N_TOKEN_MAJOR_OUTS = 2 + 3 * len(B_DILATIONS)


def _proj_kernel(pos_ref, freq_ref, x_ref, wn_ref, wt_ref, *rest):
    outs, (y_ref, c_ref, s_ref, ct_ref, st_ref) = rest[:-5], rest[-5:]
    _rope_kernel(pos_ref, freq_ref, c_ref, s_ref, ct_ref, st_ref)
    _proj_nat_kernel(x_ref, c_ref, s_ref, wn_ref, *outs[:N_TOKEN_MAJOR_OUTS], y_ref)
    _proj_t_kernel(x_ref, ct_ref, st_ref, wt_ref, *outs[N_TOKEN_MAJOR_OUTS:])


def _project(x, positions, w_nat, w_t):
    b, s, d = x.shape
    inv_freq = ROPE_THETA ** (-jnp.arange(0, 2 * ROT_HALF, 2, dtype=F32) / (2 * ROT_HALF))
    nt = s // TM
    tok = lambda width: pl.BlockSpec((1, TM, width), lambda i, j: (i, j, 0))
    once = lambda a: pl.BlockSpec(a.shape, lambda i, j: (0, 0), pipeline_mode=pl.Buffered(1))
    stream_specs, stream_shapes = [], []
    for dil in B_DILATIONS:
        for _ in range(3):
            stream_specs.append(pl.BlockSpec((1, dil, TM // dil, B_OUT_WIDTH), lambda i, j: (i, 0, j, 0)))
            stream_shapes.append(jax.ShapeDtypeStruct((b, dil, s // dil, B_OUT_WIDTH), BF16))
    return pl.pallas_call(
        _proj_kernel,
        grid=(b, nt),
        in_specs=[pl.BlockSpec((1, TM), lambda i, j: (0, i * nt + j)),
                  pl.BlockSpec((ROT_HALF, 1), lambda i, j: (0, 0)),
                  tok(d), once(w_nat), once(w_t)],
        out_specs=[pl.BlockSpec((1, A_WIDTH // LANES, TM, LANES), lambda i, j: (i, 0, j, 0)),
                   tok(2 * LANES)] + stream_specs +
                  [pl.BlockSpec((1, A_WIDTH, TM), lambda i, j: (i, 0, j)),
                   pl.BlockSpec((1, A_HEADS, TM // TK, V_ROWS, TK), lambda i, j: (i, 0, j, 0, 0)),
                   pl.BlockSpec((1, 4 * IDX_HEADS * IDX_DIM, TM), lambda i, j: (i, 0, j)),
                   pl.BlockSpec((1, IDX_HEADS, TM), lambda i, j: (i, 0, j))],
        out_shape=[jax.ShapeDtypeStruct((b, A_WIDTH // LANES, s, LANES), BF16),
                   jax.ShapeDtypeStruct((b, s, 2 * LANES), BF16)] + stream_shapes +
                  [jax.ShapeDtypeStruct((b, A_WIDTH, s), BF16),
                   jax.ShapeDtypeStruct((b, A_HEADS, s // TK, V_ROWS, TK), BF16),
                   jax.ShapeDtypeStruct((b, 4 * IDX_HEADS * IDX_DIM, s), BF16),
                   jax.ShapeDtypeStruct((b, IDX_HEADS, s), F32)],
        scratch_shapes=[pltpu.VMEM((B_WIDTH // LANES, TM, LANES), F32),
                        pltpu.VMEM((TM, LANES), F32), pltpu.VMEM((TM, LANES), F32),
                        pltpu.VMEM((ROT_HALF, TM), F32), pltpu.VMEM((ROT_HALF, TM), F32)],
        compiler_params=_params(("parallel", "parallel")),
        name="input_projection",
    )(positions.reshape(1, b * s), inv_freq.reshape(ROT_HALF, 1), x, w_nat, w_t)


def _proj_t_kernel(x_ref, ct_ref, st_ref, w_ref, qt_ref, vt_ref, qi_ref, wt_ref):
    xb = x_ref[0].astype(BF16)
    cos, sin = ct_ref[...], st_ref[...]

    def proj(lo, hi):
        return lax.dot_general(w_ref[lo:hi, :], xb, (((1,), (1,)), ((), ())), preferred_element_type=F32)

    def rope_head(y):
        x1, x2 = y[0:ROT_HALF], y[ROT_HALF:2 * ROT_HALF]
        return jnp.concatenate([x1 * cos - x2 * sin, x2 * cos + x1 * sin, y[2 * ROT_HALF:]], axis=0)

    y = proj(_T_QA, _T_QA + A_WIDTH)
    for h in range(A_HEADS):
        r = rope_head(y[h * HEAD_DIM:(h + 1) * HEAD_DIM])
        qt_ref[0, h * HEAD_DIM:(h + 1) * HEAD_DIM, :] = (r * (ATTN_SCALE * LOG2E)).astype(BF16)
    y = proj(_T_VA, _T_VA + A_WIDTH)
    for h in range(A_HEADS):
        for c in range(TM // TK):
            vt_ref[0, h, c, 0:HEAD_DIM, :] = y[h * HEAD_DIM:(h + 1) * HEAD_DIM, c * TK:(c + 1) * TK].astype(BF16)
            vt_ref[0, h, c, HEAD_DIM:V_ROWS, :] = jnp.ones((V_ROWS - HEAD_DIM, TK), BF16)
    y = proj(_T_QI, _T_QI + IDX_HEADS * IDX_DIM)
    for h in range(IDX_HEADS):
        r = rope_head(y[h * IDX_DIM:(h + 1) * IDX_DIM])
        hi = r.astype(BF16)
        lo = (r - hi.astype(F32)).astype(BF16)
        base = 4 * h * IDX_DIM
        qi_ref[0, base:base + IDX_DIM, :] = hi
        qi_ref[0, base + IDX_DIM:base + 2 * IDX_DIM, :] = hi
        qi_ref[0, base + 2 * IDX_DIM:base + 3 * IDX_DIM, :] = lo
        qi_ref[0, base + 3 * IDX_DIM:base + 4 * IDX_DIM, :] = lo
    wt_ref[0] = proj(_T_WI, _T_ROWS)[0:IDX_HEADS] * IDX_SCALE


def _dsa_kernel(qt_ref, qi_ref, wt_ref, k_ref, vt_ref, ki_ref, o_ref,
                sc_ref, qpad_ref, ot_ref, m_ref, st_ref, stat_ref, sa_ref, sb_ref):
    j = pl.program_id(1)
    nkt = j + 1

    for h in range(A_HEADS):
        off = (h % 2) * HEAD_DIM
        qpad_ref[h] = jnp.zeros((2 * HEAD_DIM, TQ), BF16)
        qpad_ref[h, off:off + HEAD_DIM, :] = qt_ref[0, h * HEAD_DIM:(h + 1) * HEAD_DIM, :]

    qpos = j * TQ + lax.broadcasted_iota(jnp.int32, (1, TQ), 1)
    row_iota = lax.broadcasted_iota(jnp.int32, (TK, TQ), 0)
    w = wt_ref[0]

    def fold(t):
        return t.reshape(TK // SUBLANES, SUBLANES, TQ)

    def score_tile(kt, diagonal, carry):
        mn, mx, pos, nn = carry
        ki = ki_ref[0, pl.ds(pl.multiple_of(kt * TK, TK), TK), :]
        acc = jnp.zeros((TK, TQ), F32)
        for h in range(IDX_HEADS):
            s = jnp.dot(ki, qi_ref[0, 4 * h * IDX_DIM:4 * (h + 1) * IDX_DIM, :], preferred_element_type=F32)
            acc = acc + w[h:h + 1, :] * jnp.maximum(s, 0.0)
        if diagonal:
            causal = (kt * TK + row_iota) <= qpos
            val = jnp.where(causal, acc, -jnp.inf)
            low = jnp.where(causal, acc, jnp.inf)
        else:
            val = low = acc
        sc_ref[kt] = val
        return (jnp.minimum(mn, fold(low).min(axis=0)),
                jnp.maximum(mx, fold(val).max(axis=0)),
                pos + fold(jnp.where(val > 0.0, 1.0, 0.0)).sum(axis=0),
                nn + fold(jnp.where(val >= 0.0, 1.0, 0.0)).sum(axis=0))

    def load_stats():
        return tuple(stat_ref[n] for n in range(4))

    def store_stats(stats):
        for n, v in enumerate(stats):
            stat_ref[n] = v

    def score_block(i, carry):
        for u in range(KEY_TILES_PER_STEP):
            carry = score_tile(KEY_TILES_PER_STEP * i + u, False, carry)
        return carry

    zeros8 = jnp.zeros((SUBLANES, TQ), F32)
    store_stats(lax.fori_loop(
        0, j // KEY_TILES_PER_STEP, score_block,
        (jnp.full((SUBLANES, TQ), jnp.inf, F32), jnp.full((SUBLANES, TQ), -jnp.inf, F32), zeros8, zeros8)))

    for left in range(KEY_TILES_PER_STEP):
        @pl.when(j % KEY_TILES_PER_STEP == left)
        def _(left=left):
            stats = load_stats()
            for u in range(left):
                stats = score_tile(j - left + u, False, stats)
            store_stats(score_tile(j, True, stats))

    mn8, mx8, pos8, nn8 = load_stats()
    mn = mn8.min(axis=0, keepdims=True)
    mx = mx8.max(axis=0, keepdims=True)
    c_pos = pos8.sum(axis=0, keepdims=True)
    c_nn = nn8.sum(axis=0, keepdims=True)

    sc_ref[nkt] = jnp.full((TK, TQ), -jnp.inf, F32)

    def count(preds):
        def part(p, kt):
            return fold(p(sc_ref[kt])).sum(axis=0)

        def body(i, cnts):
            return tuple(c + part(p, 2 * i) + part(p, 2 * i + 1) for c, p in zip(cnts, preds))

        cnts = lax.fori_loop(0, (nkt + 1) // 2, body, tuple(zeros8 for _ in preds))
        return [c.sum(axis=0, keepdims=True) for c in cnts]

    def ge(cand):
        return lambda t: jnp.where(t >= cand, 1.0, 0.0)

    n_causal = (qpos + 1).astype(F32)
    k_q = jnp.minimum(n_causal, float(TOPK_MAX))
    c_mx, = count([ge(mx)])
    select = n_causal > k_q
    at_max = select & (c_mx >= k_q)
    search = select & (c_mx < k_q)
    at_zero = search & (c_pos < k_q) & (c_nn >= k_q)
    above = search & (c_pos >= k_q)
    below = search & (c_nn < k_q)
    lo0 = jnp.where(at_max, mx, jnp.where(at_zero | above, 0.0, mn))
    c_lo0 = jnp.where(at_max, c_mx, jnp.where(at_zero | above, c_nn, n_causal))
    st_ref[0:1, :] = lo0
    st_ref[1:2, :] = jnp.where(below, 0.0, mx)
    st_ref[2:3, :] = c_lo0
    st_ref[3:4, :] = jnp.where(at_max, 0.0, jnp.where(at_zero, c_pos, jnp.where(below, c_nn, c_mx)))
    act0 = jnp.where((above | below) & (c_lo0 > k_q), 1.0, 0.0)
    st_ref[4:5, :] = act0

    def bisect_once():
        lo, hi, c_lo, c_hi = st_ref[0:1, :], st_ref[1:2, :], st_ref[2:3, :], st_ref[3:4, :]
        act = st_ref[4:5, :] > 0.0
        mid = 0.5 * lo + 0.5 * hi
        live = act & (mid > lo) & (mid < hi)
        c, = count([ge(mid)])
        up = live & (c >= k_q)
        dn = live & (c < k_q)
        c_lo = jnp.where(up, c, c_lo)
        st_ref[0:1, :] = jnp.where(up, mid, lo)
        st_ref[1:2, :] = jnp.where(dn, mid, hi)
        st_ref[2:3, :] = c_lo
        st_ref[3:4, :] = jnp.where(dn, c, c_hi)
        act_new = jnp.where(live & (c_lo > k_q), 1.0, 0.0)
        st_ref[4:5, :] = act_new
        return act_new

    def bisect(go):
        for _ in range(BISECT_STEPS_PER_CHECK - 1):
            bisect_once()
        return (jnp.max(bisect_once()) > 0.0).astype(jnp.int32)

    @pl.when(jnp.max(act0) > 0.0)
    def _():
        def blind(_, carry):
            bisect_once()
            bisect_once()
            return carry

        lax.fori_loop(0, BISECT_BLIND_STEPS // 2, blind, 0)

    lax.while_loop(lambda go: go > 0, bisect, (jnp.max(st_ref[4:5, :]) > 0.0).astype(jnp.int32))

    lo = st_ref[0:1, :]
    tie = st_ref[2:3, :] > k_q

    @pl.when(jnp.max(jnp.where(tie, 1.0, 0.0)) > 0.0)
    def _():
        need = jnp.where(tie, k_q - st_ref[3:4, :], jnp.inf)
        tri = jnp.where(lax.broadcasted_iota(jnp.int32, (TK, TK), 0) >= lax.broadcasted_iota(jnp.int32, (TK, TK), 1),
                        1.0, 0.0).astype(BF16)

        def drop(kt, seen):
            t = sc_ref[kt]
            eq = jnp.where(t == lo, 1.0, 0.0)
            rank = jnp.dot(tri, eq.astype(BF16), preferred_element_type=F32) + seen
            sc_ref[kt] = jnp.where(t == lo, jnp.where(rank > need, -jnp.inf, t), t)
            return seen + fold(eq).sum(axis=0).sum(axis=0, keepdims=True)

        def drop_pair(i, seen):
            return drop(2 * i + 1, drop(2 * i, seen))

        lax.fori_loop(0, (nkt + 1) // 2, drop_pair, jnp.zeros((1, TQ), F32))

    m_ref[...] = jnp.full(m_ref.shape, MASK_NEG, F32)
    ot_ref[...] = jnp.zeros(ot_ref.shape, F32)

    def mask_bias(kt):
        return jnp.where(sc_ref[kt] >= lo, 0.0, MASK_NEG).astype(BF16)

    def logits(h, kt, bias):
        rows = pl.ds(pl.multiple_of(kt * TK, TK), TK)
        return jnp.dot(k_ref[0, h // 2, rows, :], qpad_ref[h], preferred_element_type=F32).astype(BF16) + bias

    def half_step(kt, cur_ref, nxt_ref):
        if nxt_ref is not None:
            bias_n = mask_bias(kt + 1)
        for h in range(A_HEADS):
            s = cur_ref[h]
            if nxt_ref is not None:
                nxt_ref[h] = logits(h, kt + 1, bias_n)
            m = m_ref[h]
            m_tile = (s.reshape(TK // (2 * SUBLANES), 2 * SUBLANES, TQ).max(axis=0)
                      .astype(F32).max(axis=0, keepdims=True))
            m_new = jnp.maximum(m, m_tile)
            p = jnp.exp2(s - m_new.astype(BF16))
            corr = jnp.exp2(m - m_new)
            m_ref[h] = m_new
            ot_ref[h] = ot_ref[h] * corr + jnp.dot(vt_ref[0, h, kt], p, preferred_element_type=F32)

    bias0 = mask_bias(0)
    for h in range(A_HEADS):
        sa_ref[h] = logits(h, 0, bias0)

    def buffers(u):
        return (sa_ref, sb_ref) if u % 2 == 0 else (sb_ref, sa_ref)

    def kv_block(i, _):
        for u in range(KEY_TILES_PER_STEP):
            half_step(KEY_TILES_PER_STEP * i + u, *buffers(u))
        return 0

    n_blocks = (nkt - 1) // KEY_TILES_PER_STEP
    lax.fori_loop(0, n_blocks, kv_block, 0)

    for left in range(1, KEY_TILES_PER_STEP + 1):
        @pl.when(nkt - KEY_TILES_PER_STEP * n_blocks == left)
        def _(left=left):
            for u in range(left):
                cur_ref, nxt_ref = buffers(u)
                half_step(nkt - left + u, cur_ref, nxt_ref if u < left - 1 else None)

    for p in range(A_HEADS // 2):
        both = jnp.concatenate([ot_ref[h, 0:HEAD_DIM, :] / ot_ref[h, HEAD_DIM:HEAD_DIM + 1, :]
                                for h in (2 * p, 2 * p + 1)], axis=0)
        o_ref[0, :, p * LANES:(p + 1) * LANES] = both.T.astype(BF16)


def _dsa_attention(qt, qi4t, wt, k6, vt5, ki4):
    b, _, s = qt.shape
    nq = s // TQ
    return pl.pallas_call(
        _dsa_kernel,
        grid=(b, nq),
        in_specs=[pl.BlockSpec((1, A_WIDTH, TQ), lambda i, j: (i, 0, j)),
                  pl.BlockSpec((1, 4 * IDX_HEADS * IDX_DIM, TQ), lambda i, j: (i, 0, j)),
                  pl.BlockSpec((1, IDX_HEADS, TQ), lambda i, j: (i, 0, j)),
                  pl.BlockSpec((1, A_WIDTH // LANES, s, LANES), lambda i, j: (i, 0, 0, 0)),
                  pl.BlockSpec((1, A_HEADS, s // TK, V_ROWS, TK), lambda i, j: (i, 0, 0, 0, 0)),
                  pl.BlockSpec((1, s, 4 * IDX_DIM), lambda i, j: (i, 0, 0))],
        out_specs=pl.BlockSpec((1, TQ, A_WIDTH), lambda i, j: (i, j, 0)),
        out_shape=jax.ShapeDtypeStruct((b, s, A_WIDTH), BF16),
        scratch_shapes=[pltpu.VMEM((s // TK + 1, TK, TQ), F32),
                        pltpu.VMEM((A_HEADS, 2 * HEAD_DIM, TQ), BF16),
                        pltpu.VMEM((A_HEADS, V_ROWS, TQ), F32),
                        pltpu.VMEM((A_HEADS, 1, TQ), F32),
                        pltpu.VMEM((SUBLANES, TQ), F32),
                        pltpu.VMEM((4, SUBLANES, TQ), F32),
                        pltpu.VMEM((A_HEADS, TK, TQ), BF16),
                        pltpu.VMEM((A_HEADS, TK, TQ), BF16)],
        compiler_params=_params(("parallel", "arbitrary")),
        name="dsa_attention",
    )(qt, qi4t, wt, k6, vt5, ki4)


def _window_kernel(q_ref, kp_ref, kc_ref, vp_ref, vc_ref, o_ref, lse_ref):
    hb = B_WINDOW_STEPS
    n_streams, n_rows = q_ref.shape[1], q_ref.shape[2]
    r = lax.broadcasted_iota(jnp.int32, (hb, 2 * hb), 0)
    c = lax.broadcasted_iota(jnp.int32, (hb, 2 * hb), 1)
    dist = r + hb - c
    band = jnp.where(dist >= 0, jnp.where(dist <= B_WINDOW_STEPS, 0.0, NEG), NEG)
    first_col = jnp.where(pl.program_id(2) > 0, 0, hb)
    band_first = jnp.where(c >= first_col, band, NEG)
    lane = lax.broadcasted_iota(jnp.int32, (hb, LANES), 1)
    left = lane < HEAD_DIM
    ones = jnp.ones((2 * hb, LANES), BF16)
    for sb in range(n_streams):
        for t in range(n_rows // TB):
            for p in range(B_HEADS_PER_GROUP // 2):
                cols = slice(p * LANES, (p + 1) * LANES)
                tile = slice(t * TB, (t + 1) * TB)
                q2, kc, vc = q_ref[0, sb, tile, cols], kc_ref[0, sb, tile, cols], vc_ref[0, sb, tile, cols]
                if t == 0:
                    k_before, v_before = kp_ref[0, sb, :, cols], vp_ref[0, sb, :, cols]
                else:
                    before = slice(t * TB - hb, t * TB)
                    k_before, v_before = kc_ref[0, sb, before, cols], vc_ref[0, sb, before, cols]
                windows = ((jnp.concatenate([k_before, kc[:hb]], axis=0),
                            jnp.concatenate([v_before, vc[:hb]], axis=0)), (kc, vc))
                for half, ((kw, vw), bias) in enumerate(zip(windows, (band_first if t == 0 else band, band))):
                    rows = slice(t * TB + half * hb, t * TB + (half + 1) * hb)
                    v1 = jnp.concatenate([vw, ones], axis=1)
                    outs, lses = [], []
                    for side in (left, ~left):
                        qh = jnp.where(side, q2[half * hb:(half + 1) * hb], jnp.zeros((hb, LANES), BF16))
                        s = lax.dot_general(qh, kw, (((1,), (1,)), ((), ())), preferred_element_type=F32) + bias
                        m = s.max(axis=-1, keepdims=True)
                        e = jnp.exp2((s - m).astype(BF16))
                        ol = jnp.dot(e, v1, preferred_element_type=F32)
                        l = ol[:, LANES:]
                        outs.append(ol[:, :LANES] / l)
                        lses.append(m + jnp.log2(l))
                    o_ref[0, sb, rows, cols] = jnp.where(left, outs[0], outs[1])
                    lse_ref[0, sb, rows, cols] = jnp.where(left, lses[0], lses[1])


def _window_attention(q, k, v):
    b, d, n, _ = q.shape
    assert TB == 2 * B_WINDOW_STEPS
    rows = min(n, WINDOW_TILES_PER_STEP * TB)
    streams = min(d, WINDOW_TILES_PER_STEP * TB // rows)
    halves = rows // (TB // 2)
    cur = pl.BlockSpec((1, streams, rows, B_OUT_WIDTH), lambda bi, ri, ti: (bi, ri, ti, 0))
    prev = pl.BlockSpec((1, streams, TB // 2, B_OUT_WIDTH),
                        lambda bi, ri, ti: (bi, ri, jnp.maximum(halves * ti - 1, 0), 0))
    return pl.pallas_call(
        _window_kernel,
        grid=(b, d // streams, n // rows),
        in_specs=[cur, prev, cur, prev, cur],
        out_specs=[cur, cur],
        out_shape=[jax.ShapeDtypeStruct((b, d, n, B_OUT_WIDTH), F32)] * 2,
        compiler_params=_params(("parallel", "parallel", "arbitrary")),
        name=f"window_attention_d{d}",
    )(q, k, k, v, v)


def _layer_norm(y, g, b):
    mu = y.mean(axis=-1, keepdims=True)
    yc = y - mu
    var = (yc * yc).mean(axis=-1, keepdims=True)
    return yc * lax.rsqrt(var + LN_EPS) * g + b


def _merge_kernel(alpha, oa_ref, o0_ref, o1_ref, o2_ref, l0_ref, l1_ref, l2_ref, x_ref,
                  wg_ref, bg_ref, wa_ref, wb_ref, wo_ref, g_ref, b_ref, y_ref,
                  tok_ref, ob_ref, xb_ref, mg_ref):
    halves = B_OUT_WIDTH // LANES

    for n, ref in enumerate((o1_ref, o2_ref, l1_ref, l2_ref)):
        d = ref.shape[1]
        for r in range(d):
            for p in range(halves):
                tok_ref[n, p, pl.ds(r, TM_MERGE // d, stride=d), :] = ref[0, r, :, p * LANES:(p + 1) * LANES]
    for c in range(TM_MERGE // MERGE_ROWS):
        rows = slice(c * MERGE_ROWS, (c + 1) * MERGE_ROWS)
        for p in range(halves):
            cols = slice(p * LANES, (p + 1) * LANES)
            l0, l1, l2 = l0_ref[0, 0, rows, cols], tok_ref[2, p, rows, :], tok_ref[3, p, rows, :]
            lm = jnp.maximum(jnp.maximum(l0, l1), l2)
            e0, e1, e2 = jnp.exp2(l0 - lm), jnp.exp2(l1 - lm), jnp.exp2(l2 - lm)
            ob = (e0 * o0_ref[0, 0, rows, cols] + e1 * tok_ref[0, p, rows, :] + e2 * tok_ref[1, p, rows, :])
            ob_ref[rows, cols] = (ob / (e0 + e1 + e2)).astype(BF16)

    xb_ref[...] = x_ref[0].astype(BF16)
    for c in range(D_MODEL // MERGE_COLS):
        cols = slice(c * MERGE_COLS, (c + 1) * MERGE_COLS)
        gcols = slice(D_MODEL + c * MERGE_COLS, D_MODEL + (c + 1) * MERGE_COLS)

        def gate(sel):
            z = jnp.dot(xb_ref[...], wg_ref[:, sel], preferred_element_type=F32) + bg_ref[:, sel]
            return 1.0 / (1.0 + jnp.exp(-z))

        pa = jnp.dot(oa_ref[0], wa_ref[:, cols], preferred_element_type=F32)
        pb = jnp.dot(ob_ref[...], wb_ref[:, cols], preferred_element_type=F32)
        mg_ref[:, cols] = (gate(cols) * pa + gate(gcols) * pb).astype(BF16)

    for c in range(TM_MERGE // MERGE_ROWS):
        rows = slice(c * MERGE_ROWS, (c + 1) * MERGE_ROWS)
        mixed = jnp.dot(mg_ref[rows, :], wo_ref[...], preferred_element_type=F32)
        y_ref[0, rows, :] = _layer_norm(alpha * x_ref[0, rows, :] + mixed, g_ref[...], b_ref[...])


def _merge(alpha, oa, obs, lses, x, w_gate, b_gate, wa, wb, wo, g, bb):
    b, s, _ = x.shape
    tm = TM_MERGE
    tok = lambda width: pl.BlockSpec((1, tm, width), lambda i, j: (i, j, 0))
    full = lambda a: pl.BlockSpec(a.shape, lambda i, j: (0, 0), pipeline_mode=pl.Buffered(1))
    streams = [pl.BlockSpec((1, a.shape[1], tm // a.shape[1], B_OUT_WIDTH), lambda i, j: (i, 0, j, 0))
               for a in list(obs) + list(lses)]
    return pl.pallas_call(
        functools.partial(_merge_kernel, alpha),
        grid=(b, s // tm),
        in_specs=[tok(A_WIDTH)] + streams + [tok(D_MODEL), full(w_gate),
                  full(b_gate), full(wa), full(wb), full(wo), full(g), full(bb)],
        out_specs=tok(D_MODEL),
        out_shape=jax.ShapeDtypeStruct((b, s, D_MODEL), F32),
        scratch_shapes=[pltpu.VMEM((4, B_OUT_WIDTH // LANES, tm, LANES), F32),
                        pltpu.VMEM((tm, B_OUT_WIDTH), BF16),
                        pltpu.VMEM((tm, D_MODEL), BF16),
                        pltpu.VMEM((tm, D_MODEL), BF16)],
        compiler_params=_params(("parallel", "parallel")),
        name="merge_outproj_norm",
    )(oa, *obs, *lses, x, w_gate, b_gate, wa, wb, wo, g, bb)


def _ffn_kernel(alpha, x_ref, wg_ref, wu_ref, wd_ref, g_ref, b_ref, y_ref):
    for r in range(TM_FFN // TM):
        rows = slice(r * TM, (r + 1) * TM)
        x = x_ref[rows, :]
        xb = x.astype(BF16)
        acc = jnp.zeros((TM, D_MODEL), F32)
        for c in range(FFN_HIDDEN // FFN_CHUNK):
            cols = slice(c * FFN_CHUNK, (c + 1) * FFN_CHUNK)
            gate = jnp.dot(xb, wg_ref[:, cols], preferred_element_type=F32)
            up = jnp.dot(xb, wu_ref[:, cols], preferred_element_type=F32)
            h = gate / (1.0 + jnp.exp(-gate)) * up
            acc = acc + jnp.dot(h.astype(BF16), wd_ref[cols, :], preferred_element_type=F32)
        y_ref[rows, :] = _layer_norm(alpha * x + acc, g_ref[...], b_ref[...])


def _ffn(alpha, x2, wg, wu, wd, g, bb):
    m = x2.shape[0]
    tok = pl.BlockSpec((TM_FFN, D_MODEL), lambda i: (i, 0))
    full = lambda a: pl.BlockSpec(a.shape, lambda i: (0, 0), pipeline_mode=pl.Buffered(1))
    return pl.pallas_call(
        functools.partial(_ffn_kernel, alpha),
        grid=(m // TM_FFN,),
        in_specs=[tok, full(wg), full(wu), full(wd), full(g), full(bb)],
        out_specs=tok,
        out_shape=jax.ShapeDtypeStruct((m, D_MODEL), F32),
        compiler_params=_params(("parallel",)),
        name="swiglu_norm",
    )(x2, wg, wu, wd, g, bb)


def _split_w_in(w):
    a, bw, hi = A_WIDTH, B_WIDTH, IDX_HEADS * IDX_DIM
    o = 0
    qa, o = w[:, o:o + a], o + a
    ka, o = w[:, o:o + a], o + a
    va, o = w[:, o:o + a], o + a
    qb, o = w[:, o:o + bw], o + bw
    kb, o = w[:, o:o + bw], o + bw
    vb, o = w[:, o:o + bw], o + bw
    qi, o = w[:, o:o + hi], o + hi
    ki, o = w[:, o:o + IDX_DIM], o + IDX_DIM
    wi, o = w[:, o:o + IDX_HEADS], o + IDX_HEADS
    w_gate = w[:, o:].astype(BF16)
    w_nat = jnp.concatenate([ka, qb, kb, vb, ki, ki], axis=1).astype(BF16)
    pad = jnp.zeros((w.shape[0], _T_ROWS - _T_WI - IDX_HEADS), w.dtype)
    w_t = jnp.concatenate([qa, va, qi, wi, pad], axis=1).T.astype(BF16)
    return w_nat, w_t, w_gate


def kernel(x, positions, w_in, b_gate, w_branch_a, w_branch_b, w_out, ln1_g, ln1_b,
           w_ffn_gate, w_ffn_up, w_ffn_down, ln2_g, ln2_b):
    b, s, d = x.shape
    depth = w_in.shape[0]
    assert d == D_MODEL and s % (max(B_DILATIONS) * TB) == 0 and s % max(TM, TM_MERGE) == 0
    assert (b * s) % TM_FFN == 0 and B_DILATIONS[0] == 1
    alpha = (2 * depth) ** 0.25
    row = lambda v: v.reshape(1, -1)
    for layer in range(depth):
        w_nat, w_t, w_gate = _split_w_in(w_in[layer])
        ka6, ki4, *rest = _project(x, positions, w_nat, w_t)
        streams, (qt, vt5, qi4t, wt) = rest[:-4], rest[-4:]
        oa = _dsa_attention(qt, qi4t, wt, ka6, vt5, ki4)
        groups = [_window_attention(*streams[3 * g:3 * g + 3]) for g in range(len(B_DILATIONS))]
        x1 = _merge(alpha, oa, [o for o, _ in groups], [l for _, l in groups], x, w_gate, row(b_gate[layer]),
                    w_branch_a[layer].astype(BF16), w_branch_b[layer].astype(BF16),
                    w_out[layer].astype(BF16), row(ln1_g[layer]), row(ln1_b[layer]))
        x2 = _ffn(alpha, x1.reshape(b * s, d), w_ffn_gate[layer].astype(BF16), w_ffn_up[layer].astype(BF16),
                  w_ffn_down[layer].astype(BF16), row(ln2_g[layer]), row(ln2_b[layer]))
        x = x2.reshape(b, s, d)
    return x
```
